```python
import jax, jax.numpy as jnp
from jax import lax
import numpy as np

D_MODEL = 1024
BATCH = 4
SEQ = 4096
DEPTH = 4

GRID_W = 64
CTX_LEN = 256
N_MIXERS = 4
HEAD_DIM = 64
N_HEADS = 16
N_KV_HEADS = 4
KV_GROUP = N_HEADS // N_KV_HEADS
Q_BLOCK = 128
WINDOW = 128
MLA_HEADS = 16
MLA_Q_LORA = 384
MLA_KV_LORA = 256
MLA_NOPE = 64
MLA_ROPE = 32
MLA_V = 64
NA_HEADS = 16
NA_ROWS_MAX = 8
NA_COLS = 16
D_FF = 2816
N_EXPERTS = 8
TOP_K = 2
D_FF_EXPERT = 1408
ROPE_THETA = 10000.0
NORM_EPS = 1e-6
NEG_INF = -1e30

kernel_name = 'hybrid_interleaved_dit_block'


def rms_norm(x, g):
    xf = x.astype(jnp.float32)
    y = xf * lax.rsqrt(jnp.mean(xf * xf, axis=-1, keepdims=True) + NORM_EPS)
    return (y * g.astype(jnp.float32)).astype(x.dtype)


def rope_tables(n_tokens, rot_dim):
    t = jnp.arange(n_tokens, dtype=jnp.int32)
    row = (t // GRID_W).astype(jnp.float32)
    col = (t % GRID_W).astype(jnp.float32)
    quarter = rot_dim // 4
    inv = ROPE_THETA ** (-jnp.arange(quarter, dtype=jnp.float32) / quarter)
    ang = jnp.concatenate([row[:, None] * inv, col[:, None] * inv], axis=-1)
    return jnp.cos(ang), jnp.sin(ang)


def apply_rope(x, cos, sin):
    shape = (cos.shape[0],) + (1,) * (x.ndim - 3) + (cos.shape[1],)
    cos = cos.reshape(shape)
    sin = sin.reshape(shape)
    xf = x.astype(jnp.float32)
    x1, x2 = jnp.split(xf, 2, axis=-1)
    return jnp.concatenate([x1 * cos - x2 * sin, x2 * cos + x1 * sin], axis=-1).astype(x.dtype)


def to_blocks(a, blk):
    B, T = a.shape[:2]
    return jnp.moveaxis(a.reshape((B, T // blk, blk) + a.shape[2:]), 1, 0)


def from_blocks(a):
    a = jnp.moveaxis(a, 0, 1)
    return a.reshape((a.shape[0], a.shape[1] * a.shape[2]) + a.shape[3:])


def gqa_split(t, w_qkv):
    B, T, _ = t.shape
    qkv = t @ w_qkv
    nq = N_HEADS * HEAD_DIM
    nk = N_KV_HEADS * HEAD_DIM
    q = qkv[..., :nq].reshape(B, T, N_KV_HEADS, KV_GROUP, HEAD_DIM)
    k = qkv[..., nq:nq + nk].reshape(B, T, N_KV_HEADS, HEAD_DIM)
    v = qkv[..., nq + nk:].reshape(B, T, N_KV_HEADS, HEAD_DIM)
    return q, k, v


def gqa_attend(q, k, v, scale):
    s = jnp.einsum('btkgd,bskd->bkgts', q, k).astype(jnp.float32) * scale
    p = jax.nn.softmax(s, axis=-1).astype(v.dtype)
    return jnp.einsum('bkgts,bskd->btkgd', p, v)


def mha_attend(q, k, v, scale):
    s = jnp.einsum('bthd,bshd->bhts', q, k).astype(jnp.float32) * scale
    p = jax.nn.softmax(s, axis=-1).astype(v.dtype)
    return jnp.einsum('bhts,bshd->bthd', p, v)


def mixer_global_gqa(h, hc, params, with_ctx):
    w_qkv, q_norm, k_norm, w_o = params
    B, S, _ = h.shape
    scale = HEAD_DIM ** -0.5
    cos, sin = rope_tables(S, HEAD_DIM)
    q, k, v = gqa_split(h, w_qkv)
    qc, kc, vc = gqa_split(hc, w_qkv)
    q = apply_rope(rms_norm(q, q_norm), cos, sin)
    k = apply_rope(rms_norm(k, k_norm), cos, sin)
    qc = rms_norm(qc, q_norm)
    kc = rms_norm(kc, k_norm)
    keys = jnp.concatenate([kc, k], axis=1)
    vals = jnp.concatenate([vc, v], axis=1)
    o = lax.map(lambda qb: gqa_attend(qb, keys, vals, scale), to_blocks(q, Q_BLOCK))
    y = from_blocks(o).reshape(B, S, N_HEADS * HEAD_DIM) @ w_o
    yc = None
    if with_ctx:
        oc = gqa_attend(qc, kc, vc, scale)
        yc = oc.reshape(B, oc.shape[1], N_HEADS * HEAD_DIM) @ w_o
    return y, yc


def mixer_window_gqa(h, hc, params, with_ctx):
    w_qkv, sink, w_o = params
    B, S, _ = h.shape
    scale = HEAD_DIM ** -0.5
    nb = S // Q_BLOCK
    span = Q_BLOCK + 2 * WINDOW
    cos, sin = rope_tables(S, HEAD_DIM)
    q, k, v = gqa_split(h, w_qkv)
    qc, kc, vc = gqa_split(hc, w_qkv)
    q = apply_rope(q, cos, sin)
    k = apply_rope(k, cos, sin)
    pad = ((0, 0), (WINDOW, WINDOW), (0, 0), (0, 0))
    k_pad = jnp.pad(k, pad)
    v_pad = jnp.pad(v, pad)
    valid = jnp.pad(jnp.ones((S,), dtype=bool), (WINDOW, WINDOW))
    tq = jnp.arange(Q_BLOCK)[:, None]
    uk = jnp.arange(span)[None, :]
    band = (uk >= tq) & (uk <= tq + 2 * WINDOW)
    sink_l = sink.astype(jnp.float32).reshape(1, N_KV_HEADS, KV_GROUP, 1, 1)
    n_ctx = kc.shape[1]

    def attend(args):
        qb, b = args
        start = b * Q_BLOCK
        kw = lax.dynamic_slice_in_dim(k_pad, start, span, axis=1)
        vw = lax.dynamic_slice_in_dim(v_pad, start, span, axis=1)
        mask = band & lax.dynamic_slice_in_dim(valid, start, span)[None, :]
        s_ctx = jnp.einsum('btkgd,bskd->bkgts', qb, kc).astype(jnp.float32) * scale
        s_win = jnp.einsum('btkgd,bskd->bkgts', qb, kw).astype(jnp.float32) * scale
        s_win = jnp.where(mask, s_win, NEG_INF)
        s_sink = jnp.broadcast_to(sink_l, s_ctx.shape[:-1] + (1,))
        p = jax.nn.softmax(jnp.concatenate([s_ctx, s_win, s_sink], axis=-1), axis=-1).astype(vw.dtype)
        return (jnp.einsum('bkgts,bskd->btkgd', p[..., :n_ctx], vc)
                + jnp.einsum('bkgts,bskd->btkgd', p[..., n_ctx:n_ctx + span], vw))

    o = lax.map(attend, (to_blocks(q, Q_BLOCK), jnp.arange(nb)))
    y = from_blocks(o).reshape(B, S, N_HEADS * HEAD_DIM) @ w_o
    yc = None
    if with_ctx:
        s_cc = jnp.einsum('btkgd,bskd->bkgts', qc, kc).astype(jnp.float32) * scale
        s_sink = jnp.broadcast_to(sink_l, s_cc.shape[:-1] + (1,))
        p = jax.nn.softmax(jnp.concatenate([s_cc, s_sink], axis=-1), axis=-1).astype(vc.dtype)
        oc = jnp.einsum('bkgts,bskd->btkgd', p[..., :n_ctx], vc)
        yc = oc.reshape(B, n_ctx, N_HEADS * HEAD_DIM) @ w_o
    return y, yc


def mla_project(t, w_dkv, q_a_norm, w_q_b, kv_a_norm, w_kv_b):
    B, T, _ = t.shape
    lat = t @ w_dkv
    c_q = rms_norm(lat[..., :MLA_Q_LORA], q_a_norm)
    c_kv = rms_norm(lat[..., MLA_Q_LORA:MLA_Q_LORA + MLA_KV_LORA], kv_a_norm)
    k_rope = lat[..., MLA_Q_LORA + MLA_KV_LORA:]
    q = (c_q @ w_q_b).reshape(B, T, MLA_HEADS, MLA_NOPE + MLA_ROPE)
    kv = (c_kv @ w_kv_b).reshape(B, T, MLA_HEADS, MLA_NOPE + MLA_V)
    return q[..., :MLA_NOPE], q[..., MLA_NOPE:], kv[..., :MLA_NOPE], k_rope, kv[..., MLA_NOPE:]


def mla_attend(q_nope, q_rope, k_nope, k_rope, v, scale):
    s = (jnp.einsum('bthd,bshd->bhts', q_nope, k_nope)
         + jnp.einsum('bthr,bsr->bhts', q_rope, k_rope)).astype(jnp.float32) * scale
    p = jax.nn.softmax(s, axis=-1).astype(v.dtype)
    return jnp.einsum('bhts,bshd->bthd', p, v)


def mixer_latent_attention(h, hc, params, with_ctx):
    w_dkv, q_a_norm, w_q_b, kv_a_norm, w_kv_b, w_o = params
    B, S, _ = h.shape
    scale = (MLA_NOPE + MLA_ROPE) ** -0.5
    cos, sin = rope_tables(S, MLA_ROPE)
    q_nope, q_rope, k_nope, k_rope, v = mla_project(h, w_dkv, q_a_norm, w_q_b, kv_a_norm, w_kv_b)
    q_rope = apply_rope(q_rope, cos, sin)
    k_rope = apply_rope(k_rope, cos, sin)
    cq_nope, cq_rope, ck_nope, ck_rope, cv = mla_project(hc, w_dkv, q_a_norm, w_q_b, kv_a_norm, w_kv_b)
    keys_nope = jnp.concatenate([ck_nope, k_nope], axis=1)
    keys_rope = jnp.concatenate([ck_rope, k_rope], axis=1)
    vals = jnp.concatenate([cv, v], axis=1)
    o = lax.map(lambda qs: mla_attend(qs[0], qs[1], keys_nope, keys_rope, vals, scale),
                (to_blocks(q_nope, Q_BLOCK), to_blocks(q_rope, Q_BLOCK)))
    y = from_blocks(o).reshape(B, S, MLA_HEADS * MLA_V) @ w_o
    yc = None
    if with_ctx:
        oc = mla_attend(cq_nope, cq_rope, ck_nope, ck_rope, cv, scale)
        yc = oc.reshape(B, oc.shape[1], MLA_HEADS * MLA_V) @ w_o
    return y, yc


def mixer_neighbourhood(h, hc, params, with_ctx):
    w_qkv, rpb, w_o = params
    B, S, _ = h.shape
    rows = S // GRID_W
    kh = min(NA_ROWS_MAX, rows)
    kw = NA_COLS
    scale = HEAD_DIM ** -0.5

    def split(t):
        qkv = (t @ w_qkv).reshape(t.shape[0], t.shape[1], 3, NA_HEADS, HEAD_DIM)
        return qkv[:, :, 0], qkv[:, :, 1], qkv[:, :, 2]

    q, k, v = split(h)
    qc, kc, vc = split(hc)
    k_grid = k.reshape(B, rows, GRID_W, NA_HEADS, HEAD_DIM)
    v_grid = v.reshape(B, rows, GRID_W, NA_HEADS, HEAD_DIM)
    cols = jnp.arange(GRID_W)
    col_start = jnp.clip(cols - kw // 2, 0, GRID_W - kw)
    col_idx = col_start[:, None] + jnp.arange(kw)[None, :]
    dc = col_idx - cols[:, None] + (kw - 1)
    n_ctx = kc.shape[1]

    def attend(args):
        qr, r = args
        rs = jnp.clip(r - kh // 2, 0, rows - kh)
        dr = rs + jnp.arange(kh) - r + (NA_ROWS_MAX - 1)
        bias = rpb[:, dr[None, :, None], dc[:, None, :]].reshape(NA_HEADS, GRID_W, kh * kw)

        def gather(grid):
            band = lax.dynamic_slice_in_dim(grid, rs, kh, axis=1)
            g = band[:, :, col_idx]
            return jnp.moveaxis(g, 2, 1).reshape(B, GRID_W, kh * kw, NA_HEADS, HEAD_DIM)

        kn = gather(k_grid)
        vn = gather(v_grid)
        s_nb = jnp.einsum('bqhd,bqnhd->bhqn', qr, kn).astype(jnp.float32) * scale + bias.astype(jnp.float32)
        s_ctx = jnp.einsum('bqhd,bchd->bhqc', qr, kc).astype(jnp.float32) * scale
        p = jax.nn.softmax(jnp.concatenate([s_ctx, s_nb], axis=-1), axis=-1).astype(vn.dtype)
        return (jnp.einsum('bhqc,bchd->bqhd', p[..., :n_ctx], vc)
                + jnp.einsum('bhqn,bqnhd->bqhd', p[..., n_ctx:], vn))

    o = lax.map(attend, (to_blocks(q, GRID_W), jnp.arange(rows)))
    y = from_blocks(o).reshape(B, S, NA_HEADS * HEAD_DIM) @ w_o
    yc = None
    if with_ctx:
        oc = mha_attend(qc, kc, vc, scale)
        yc = oc.reshape(B, n_ctx, NA_HEADS * HEAD_DIM) @ w_o
    return y, yc


def swiglu(h, w_gu, w_down):
    g, u = jnp.split(h @ w_gu, 2, axis=-1)
    return (jax.nn.silu(g) * u) @ w_down


def moe_ffn(h, w_router, w_gu, w_down):
    shape = h.shape
    t = h.reshape(-1, shape[-1])
    logits = (t @ w_router).astype(jnp.float32)
    top_vals, top_idx = lax.top_k(logits, TOP_K)
    top_w = jax.nn.softmax(top_vals, axis=-1)
    combine = jnp.sum(jax.nn.one_hot(top_idx, N_EXPERTS, dtype=jnp.float32) * top_w[..., None], axis=1)
    y = jnp.zeros_like(t)
    for e in range(N_EXPERTS):
        y = y + combine[:, e:e + 1].astype(t.dtype) * swiglu(t, w_gu[e], w_down[e])
    return y.reshape(shape)


def setup_inputs(seed: int = 0) -> dict:
    key = jax.random.key(seed)
    keys = iter(jax.random.split(key, 128))
    D = D_MODEL

    def nrm(shape, s):
        return jax.random.normal(next(keys), shape, jnp.float32) * s

    def lin(fan_in, shape):
        return nrm(shape, fan_in ** -0.5)

    def gain(n):
        return 1.0 + nrm((n,), 0.1)

    inp = {}
    inp['x'] = nrm((BATCH, SEQ, D), 1.0)
    inp['c'] = nrm((BATCH, D), 1.0)
    inp['ctx'] = nrm((BATCH, CTX_LEN, D), 1.0)
    inp['c_ctx'] = nrm((D,), 1.0)
    gqa_width = (N_HEADS + 2 * N_KV_HEADS) * HEAD_DIM
    for i in range(DEPTH):
        p = 'l%d_' % i
        inp[p + 'ada_w'] = nrm((D, 6 * D), 0.5 * D ** -0.5)
        inp[p + 'ada_b'] = nrm((6 * D,), 0.02)
        for n in ('mix_pre', 'mix_post', 'ffn_pre', 'ffn_post'):
            inp[p + n] = gain(D)
        kind = i % N_MIXERS
        if kind == 0:
            inp[p + 'w_qkv'] = lin(D, (D, gqa_width))
            inp[p + 'q_norm'] = gain(HEAD_DIM)
            inp[p + 'k_norm'] = gain(HEAD_DIM)
            inp[p + 'w_o'] = lin(N_HEADS * HEAD_DIM, (N_HEADS * HEAD_DIM, D))
        elif kind == 1:
            inp[p + 'w_qkv'] = lin(D, (D, gqa_width))
            inp[p + 'sink'] = nrm((N_HEADS,), 1.0)
            inp[p + 'w_o'] = lin(N_HEADS * HEAD_DIM, (N_HEADS * HEAD_DIM, D))
        elif kind == 2:
            inp[p + 'w_dkv'] = lin(D, (D, MLA_Q_LORA + MLA_KV_LORA + MLA_ROPE))
            inp[p + 'q_a_norm'] = gain(MLA_Q_LORA)
            inp[p + 'w_q_b'] = lin(MLA_Q_LORA, (MLA_Q_LORA, MLA_HEADS * (MLA_NOPE + MLA_ROPE)))
            inp[p + 'kv_a_norm'] = gain(MLA_KV_LORA)
            inp[p + 'w_kv_b'] = lin(MLA_KV_LORA, (MLA_KV_LORA, MLA_HEADS * (MLA_NOPE + MLA_V)))
            inp[p + 'w_o'] = lin(MLA_HEADS * MLA_V, (MLA_HEADS * MLA_V, D))
        else:
            inp[p + 'w_qkv'] = lin(D, (D, 3 * NA_HEADS * HEAD_DIM))
            inp[p + 'rpb'] = nrm((NA_HEADS, 2 * NA_ROWS_MAX - 1, 2 * NA_COLS - 1), 0.5)
            inp[p + 'w_o'] = lin(NA_HEADS * HEAD_DIM, (NA_HEADS * HEAD_DIM, D))
        if i % 2 == 0:
            inp[p + 'ffn_w_gu'] = lin(D, (D, 2 * D_FF))
            inp[p + 'ffn_w_down'] = lin(D_FF, (D_FF, D))
        else:
            inp[p + 'router'] = lin(D, (D, N_EXPERTS))
            inp[p + 'moe_w_gu'] = lin(D, (N_EXPERTS, D, 2 * D_FF_EXPERT))
            inp[p + 'moe_w_down'] = lin(D_FF_EXPERT, (N_EXPERTS, D_FF_EXPERT, D))
    return inp


def reference(x, c, ctx, c_ctx,
              l0_ada_w, l0_ada_b, l0_mix_pre, l0_mix_post, l0_ffn_pre, l0_ffn_post,
              l0_w_qkv, l0_q_norm, l0_k_norm, l0_w_o, l0_ffn_w_gu, l0_ffn_w_down,
              l1_ada_w, l1_ada_b, l1_mix_pre, l1_mix_post, l1_ffn_pre, l1_ffn_post,
              l1_w_qkv, l1_sink, l1_w_o, l1_router, l1_moe_w_gu, l1_moe_w_down,
              l2_ada_w, l2_ada_b, l2_mix_pre, l2_mix_post, l2_ffn_pre, l2_ffn_post,
              l2_w_dkv, l2_q_a_norm, l2_w_q_b, l2_kv_a_norm, l2_w_kv_b, l2_w_o, l2_ffn_w_gu, l2_ffn_w_down,
              l3_ada_w, l3_ada_b, l3_mix_pre, l3_mix_post, l3_ffn_pre, l3_ffn_post,
              l3_w_qkv, l3_rpb, l3_w_o, l3_router, l3_moe_w_gu, l3_moe_w_down):
    mixers = (mixer_global_gqa, mixer_window_gqa, mixer_latent_attention, mixer_neighbourhood)
    mixer_params = ((l0_w_qkv, l0_q_norm, l0_k_norm, l0_w_o),
                    (l1_w_qkv, l1_sink, l1_w_o),
                    (l2_w_dkv, l2_q_a_norm, l2_w_q_b, l2_kv_a_norm, l2_w_kv_b, l2_w_o),
                    (l3_w_qkv, l3_rpb, l3_w_o))
    ffn_params = ((l0_ffn_w_gu, l0_ffn_w_down),
                  (l1_router, l1_moe_w_gu, l1_moe_w_down),
                  (l2_ffn_w_gu, l2_ffn_w_down),
                  (l3_router, l3_moe_w_gu, l3_moe_w_down))
    ada = ((l0_ada_w, l0_ada_b), (l1_ada_w, l1_ada_b), (l2_ada_w, l2_ada_b), (l3_ada_w, l3_ada_b))
    norms = ((l0_mix_pre, l0_mix_post, l0_ffn_pre, l0_ffn_post),
             (l1_mix_pre, l1_mix_post, l1_ffn_pre, l1_ffn_post),
             (l2_mix_pre, l2_mix_post, l2_ffn_pre, l2_ffn_post),
             (l3_mix_pre, l3_mix_post, l3_ffn_pre, l3_ffn_post))
    xc = ctx
    for i in range(DEPTH):
        last = i == DEPTH - 1
        ada_w, ada_b = ada[i]
        mix_pre, mix_post, ffn_pre, ffn_post = norms[i]
        sh_m, sc_m, g_m, sh_f, sc_f, g_f = [m[:, None, :] for m in
                                            jnp.split(jax.nn.silu(c) @ ada_w + ada_b, 6, axis=-1)]
        csh_m, csc_m, cg_m, csh_f, csc_f, cg_f = jnp.split(jax.nn.silu(c_ctx) @ ada_w + ada_b, 6, axis=-1)
        ffn = swiglu if i % 2 == 0 else moe_ffn

        h = rms_norm(x, mix_pre) * (1 + sc_m) + sh_m
        hc = rms_norm(xc, mix_pre) * (1 + csc_m) + csh_m
        y, yc = mixers[i % N_MIXERS](h, hc, mixer_params[i], not last)
        x = x + g_m * rms_norm(y, mix_post)

        h = rms_norm(x, ffn_pre) * (1 + sc_f) + sh_f
        x = x + g_f * rms_norm(ffn(h, *ffn_params[i]), ffn_post)

        if not last:
            xc = xc + cg_m * rms_norm(yc, mix_post)
            hc = rms_norm(xc, ffn_pre) * (1 + csc_f) + csh_f
            xc = xc + cg_f * rms_norm(ffn(hc, *ffn_params[i]), ffn_post)
    return x
```

```python
import functools

import numpy as np
import jax
import jax.numpy as jnp
from jax import lax
from jax.experimental import pallas as pl
from jax.experimental.pallas import tpu as pltpu

D = 1024
B = 4
S = 4096
CTX = 256
GRID_W = 64
N_LAT = B * S
N_CTX = B * CTX
N_TOK = N_LAT + N_CTX
HEAD_DIM = 64
N_HEADS = 16
N_KV_HEADS = 4
WINDOW = 128
MLA_Q_LORA = 384
MLA_KV_LORA = 256
MLA_NOPE = 64
MLA_ROPE = 32
NA_ROWS = 8
NA_COLS = 16
D_FF = 2816
N_EXPERTS = 8
D_FF_EXPERT = 1408
ROPE_THETA = 10000.0
EPS = 1e-6
NEG = -1e30

LANES = 128
HALF = LANES // 2
TM = 512
TQ = 256
TK = 512
TF = 1408
NA_QROWS = 2
NA_BAND = NA_QROWS + NA_ROWS - 1
VMEM_LIMIT = 56 * 1024 * 1024

F32 = jnp.float32
BF16 = jnp.bfloat16

SH_M, SC_M, G_M, SH_F, SC_F, G_F = (slice(i * D, (i + 1) * D) for i in range(6))


def _params(*sem):
    return pltpu.CompilerParams(dimension_semantics=sem, vmem_limit_bytes=VMEM_LIMIT)


def _mod_row(i):
    r0 = i * TM
    return jnp.where(r0 >= N_LAT, B, r0 // S)


def _rms(x, gain):
    return x * lax.rsqrt(jnp.mean(x * x, axis=-1, keepdims=True) + EPS) * gain


def _lane(shape):
    return lax.broadcasted_iota(jnp.int32, shape, len(shape) - 1)


def _ada_kernel(c_ref, w_ref, b_ref, o_ref):
    a = jax.nn.silu(c_ref[...])
    o_ref[...] = jnp.dot(a, w_ref[...], preferred_element_type=F32,
                         precision=lax.Precision.HIGHEST) + b_ref[...]


def _ada(c_all, w, b):
    tn = 1536
    return pl.pallas_call(
        _ada_kernel,
        out_shape=jax.ShapeDtypeStruct((8, 6 * D), F32),
        grid=(6 * D // tn,),
        in_specs=[pl.BlockSpec((8, D), lambda j: (0, 0)),
                  pl.BlockSpec((D, tn), lambda j: (0, j)),
                  pl.BlockSpec((1, tn), lambda j: (0, j))],
        out_specs=pl.BlockSpec((8, tn), lambda j: (0, j)),
        compiler_params=_params("arbitrary"),
        name="ada",
    )(c_all, w, b.reshape(1, 6 * D))


def _rope_chunk(t, cos, sin, half):
    lane = _lane(t.shape)
    rot = jnp.where((lane % (2 * half)) < half,
                    pltpu.roll(t, LANES - half, 1), pltpu.roll(t, half, 1))
    return t * cos + rot * sin


def _head_rms(t, gain, bd):
    t2 = t * t
    hi = t2.astype(BF16)
    lo = (t2 - hi.astype(F32)).astype(BF16)
    ss = (jnp.dot(hi, bd, preferred_element_type=F32)
          + jnp.dot(lo, bd, preferred_element_type=F32))
    return t * lax.rsqrt(ss * (1.0 / HEAD_DIM) + EPS) * gain


def _with_ones(v, lane, parity):
    if parity == 0:
        return jnp.where(lane < HALF, v, jnp.where(lane == LANES - 1, 1.0, 0.0))
    return jnp.where(lane >= HALF, v, jnp.where(lane == 0, 1.0, 0.0))


def _half_only(k, lane, parity):
    return jnp.where(lane < HALF, k, 0.0) if parity == 0 else jnp.where(lane >= HALF, k, 0.0)


def _qkv_kernel(x_ref, mod_ref, g_ref, w_ref, cos_ref, sin_ref, qg_ref, kg_ref, bd_ref,
                q_ref, k_ref, v_ref, *, shared_kv, head_norm, rope):
    h = _rms(x_ref[...], g_ref[...]) * (1.0 + mod_ref[:, SC_M]) + mod_ref[:, SH_M]
    acc = jnp.dot(h.astype(BF16), w_ref[...], preferred_element_type=F32)
    lane = _lane((TM, LANES))
    n_q = N_HEADS * HEAD_DIM // LANES
    n_kv = (N_KV_HEADS if shared_kv else N_HEADS) * HEAD_DIM // LANES

    def finish(t, gain_ref):
        if head_norm:
            t = _head_rms(t, gain_ref[...], bd_ref[...])
        if rope:
            t = _rope_chunk(t, cos_ref[...], sin_ref[...], HEAD_DIM // 2)
        return t

    for c in range(n_q):
        t = finish(acc[:, c * LANES:(c + 1) * LANES], qg_ref)
        q_ref[:, c * LANES:(c + 1) * LANES] = (t * HEAD_DIM ** -0.5).astype(BF16)
    for c in range(n_kv):
        k = finish(acc[:, (n_q + c) * LANES:(n_q + c + 1) * LANES], kg_ref)
        v = acc[:, (n_q + n_kv + c) * LANES:(n_q + n_kv + c + 1) * LANES]
        if shared_kv:
            k_sw = pltpu.roll(k, HALF, 1)
            v_sw = pltpu.roll(v, HALF, 1)
            for half in range(2):
                j = 2 * c + half
                for parity in range(2):
                    ksrc = k if parity == half else k_sw
                    vsrc = v if parity == half else v_sw
                    k_ref[j, parity] = _half_only(ksrc, lane, parity).astype(BF16)
                    v_ref[j, parity] = _with_ones(vsrc, lane, parity).astype(BF16)
        else:
            for parity in range(2):
                k_ref[c, parity] = _half_only(k, lane, parity).astype(BF16)
                v_ref[c, parity] = _with_ones(v, lane, parity).astype(BF16)


def _qkv(x, mod, gain, w, cos, sin, qg, kg, bd, *, shared_kv, head_norm, rope):
    n_out = w.shape[1]
    groups = N_KV_HEADS if shared_kv else N_HEADS // 2
    row = lambda i: (i, 0)
    const = lambda i: (0, 0)
    kv_spec = pl.BlockSpec((groups, 2, TM, LANES), lambda i: (0, 0, i, 0))
    kv_shape = jax.ShapeDtypeStruct((groups, 2, N_TOK, LANES), BF16)
    return pl.pallas_call(
        functools.partial(_qkv_kernel, shared_kv=shared_kv, head_norm=head_norm, rope=rope),
        out_shape=(jax.ShapeDtypeStruct((N_TOK, N_HEADS * HEAD_DIM), BF16), kv_shape, kv_shape),
        grid=(N_TOK // TM,),
        in_specs=[pl.BlockSpec((TM, D), row),
                  pl.BlockSpec((None, 1, 6 * D), lambda i: (_mod_row(i), 0, 0)),
                  pl.BlockSpec((1, D), const),
                  pl.BlockSpec((D, n_out), const),
                  pl.BlockSpec((TM, LANES), row),
                  pl.BlockSpec((TM, LANES), row),
                  pl.BlockSpec((1, LANES), const),
                  pl.BlockSpec((1, LANES), const),
                  pl.BlockSpec((LANES, LANES), const)],
        out_specs=(pl.BlockSpec((TM, N_HEADS * HEAD_DIM), row), kv_spec, kv_spec),
        compiler_params=_params("arbitrary"),
        name="qkv",
    )(x, mod, gain, w, cos, sin, qg, kg, bd)


def _mla_kernel(x_ref, mod_ref, g_ref, wd_ref, qan_ref, kvan_ref, wq_ref, wkv_ref,
                cos_ref, sin_ref, q_ref, k_ref, v_ref):
    h = _rms(x_ref[...], g_ref[...]) * (1.0 + mod_ref[:, SC_M]) + mod_ref[:, SH_M]
    lat = jnp.dot(h.astype(BF16), wd_ref[...], preferred_element_type=F32)
    c_q = _rms(lat[:, :MLA_Q_LORA], qan_ref[...])
    c_kv = _rms(lat[:, MLA_Q_LORA:MLA_Q_LORA + MLA_KV_LORA], kvan_ref[...])
    cos = cos_ref[...]
    sin = sin_ref[...]
    k_rope = _rope_chunk(lat[:, MLA_Q_LORA + MLA_KV_LORA:], cos, sin, MLA_ROPE // 2)
    q = jnp.dot(c_q.astype(BF16), wq_ref[...], preferred_element_type=F32)
    kv = jnp.dot(c_kv.astype(BF16), wkv_ref[...], preferred_element_type=F32)
    scale = (MLA_NOPE + MLA_ROPE) ** -0.5
    lane = _lane((TM, LANES))
    for hd in range(N_HEADS):
        sl = slice(hd * LANES, (hd + 1) * LANES)
        q_ref[:, sl] = (_rope_chunk(q[:, sl], cos, sin, MLA_ROPE // 2) * scale).astype(BF16)
        k_ref[hd // 2, hd % 2] = (kv[:, sl] + k_rope).astype(BF16)
        v = kv[:, N_HEADS * LANES + hd * LANES:N_HEADS * LANES + (hd + 1) * LANES]
        v_ref[hd // 2, hd % 2] = _with_ones(v, lane, hd % 2).astype(BF16)


def _mla_proj(x, mod, gain, wd, qan, kvan, wq, wkv, cos, sin):
    row = lambda i: (i, 0)
    const = lambda i: (0, 0)
    kv_spec = pl.BlockSpec((N_HEADS // 2, 2, TM, LANES), lambda i: (0, 0, i, 0))
    kv_shape = jax.ShapeDtypeStruct((N_HEADS // 2, 2, N_TOK, LANES), BF16)
    return pl.pallas_call(
        _mla_kernel,
        out_shape=(jax.ShapeDtypeStruct((N_TOK, N_HEADS * LANES), BF16), kv_shape, kv_shape),
        grid=(N_TOK // TM,),
        in_specs=[pl.BlockSpec((TM, D), row),
                  pl.BlockSpec((None, 1, 6 * D), lambda i: (_mod_row(i), 0, 0)),
                  pl.BlockSpec((1, D), const),
                  pl.BlockSpec(wd.shape, const),
                  pl.BlockSpec((1, MLA_Q_LORA), const),
                  pl.BlockSpec((1, MLA_KV_LORA), const),
                  pl.BlockSpec(wq.shape, const),
                  pl.BlockSpec(wkv.shape, const),
                  pl.BlockSpec((TM, LANES), row),
                  pl.BlockSpec((TM, LANES), row)],
        out_specs=(pl.BlockSpec((TM, N_HEADS * LANES), row), kv_spec, kv_spec),
        compiler_params=_params("arbitrary"),
        name="mla_proj",
    )(x, mod, gain, wd, qan, kvan, wq, wkv, cos, sin)


def _softmax_step(q, k, v, m, acc, bias=None, mask=None):
    s = lax.dot_general(q, k, (((1,), (1,)), ((), ())), preferred_element_type=F32)
    if bias is not None:
        s = s + bias
    if mask is not None:
        s = jnp.where(mask, s, NEG)
    m_new = jnp.maximum(m, jnp.max(s, axis=-1, keepdims=True))
    p = jnp.exp(s - m_new)
    acc = jnp.exp(m - m_new) * acc + jnp.dot(p.astype(BF16), v, preferred_element_type=F32)
    return m_new, acc


def _finish_heads(heads, accs, o_ref):
    lane = _lane(accs[0].shape)
    chunks = {}
    for (_, kv, oc), acc in zip(heads, accs):
        if kv == 0:
            o = jnp.where(lane < HALF, acc / acc[:, LANES - 1:LANES], 0.0)
        else:
            o = jnp.where(lane >= HALF, acc / acc[:, 0:1], 0.0)
        chunks[oc] = o if oc not in chunks else chunks[oc] + o
    for oc, o in chunks.items():
        o_ref[:, oc * LANES:(oc + 1) * LANES] = o.astype(o_ref.dtype)


def _attn_kernel(*refs, heads, mode):
    if mode == "window":
        sink_ref, q_ref, kc_ref, vc_ref, kl_ref, vl_ref, o_ref = refs
    else:
        q_ref, kc_ref, vc_ref, kl_ref, vl_ref, o_ref = refs
    g = pl.program_id(1)
    t = pl.program_id(2)
    is_lat = t < S // TQ
    qs = [q_ref[:, qc * LANES:(qc + 1) * LANES] for qc, _, _ in heads]
    lane = _lane((TQ, LANES))
    ms, accs = [], []
    for hi, (qc, kv, _) in enumerate(heads):
        if mode == "window":
            m0 = jnp.full((TQ, 1), sink_ref[g * len(heads) + 2 * qc + kv], F32)
            den = LANES - 1 if kv == 0 else 0
            a0 = jnp.where(lane == den, 1.0, 0.0).astype(F32)
        else:
            m0 = jnp.full((TQ, 1), -jnp.inf, F32)
            a0 = jnp.zeros((TQ, LANES), F32)
        m1, a1 = _softmax_step(qs[hi], kc_ref[kv], vc_ref[kv], m0, a0)
        ms.append(m1)
        accs.append(a1)

    if mode == "global":
        def body(c, carry):
            ms, accs = carry
            off = pl.multiple_of(c * TK, TK)
            out_m, out_a = [], []
            for hi, (qc, kv, _) in enumerate(heads):
                m1, a1 = _softmax_step(qs[hi], kl_ref[kv, pl.ds(off, TK), :],
                                       vl_ref[kv, pl.ds(off, TK), :], ms[hi], accs[hi])
                out_m.append(m1)
                out_a.append(a1)
            return tuple(out_m), tuple(out_a)
        n_steps = jnp.where(is_lat, S // TK, 0)
    else:
        span = TQ + 2 * WINDOW

        def body(c, carry):
            ms, accs = carry
            q0 = t * TQ
            start = pl.multiple_of(jnp.clip(q0 - WINDOW, 0, S - span), WINDOW)
            qi = q0 + lax.broadcasted_iota(jnp.int32, (TQ, span), 0)
            ki = start + lax.broadcasted_iota(jnp.int32, (TQ, span), 1)
            mask = jnp.abs(qi - ki) <= WINDOW
            out_m, out_a = [], []
            for hi, (qc, kv, _) in enumerate(heads):
                m1, a1 = _softmax_step(qs[hi], kl_ref[kv, pl.ds(start, span), :],
                                       vl_ref[kv, pl.ds(start, span), :], ms[hi], accs[hi],
                                       mask=mask)
                out_m.append(m1)
                out_a.append(a1)
            return tuple(out_m), tuple(out_a)
        n_steps = jnp.where(is_lat, 1, 0)

    ms, accs = lax.fori_loop(0, n_steps, body, (tuple(ms), tuple(accs)))
    _finish_heads(heads, accs, o_ref)


def _attention(q, k4, v4, *, heads, mode, sink=None):
    groups = k4.shape[0]
    qw = q.shape[1] // groups
    ow = N_HEADS * HEAD_DIM // groups
    nq = S // TQ

    def q_map(b, g, t):
        return (jnp.where(t < nq, b * nq + t, N_LAT // TQ + b), g)

    ctx_spec = pl.BlockSpec((None, 2, CTX, LANES), lambda b, g, t: (g, 0, N_LAT // CTX + b, 0))
    lat_spec = pl.BlockSpec((None, 2, S, LANES), lambda b, g, t: (g, 0, b, 0))
    in_specs = [pl.BlockSpec((TQ, qw), q_map), ctx_spec, ctx_spec, lat_spec, lat_spec]
    args = [q, k4, v4, k4, v4]
    if mode == "window":
        in_specs = [pl.BlockSpec(memory_space=pltpu.SMEM)] + in_specs
        args = [sink] + args
    return pl.pallas_call(
        functools.partial(_attn_kernel, heads=heads, mode=mode),
        out_shape=jax.ShapeDtypeStruct((N_TOK, N_HEADS * HEAD_DIM), BF16),
        grid=(B, groups, nq + 1),
        in_specs=in_specs,
        out_specs=pl.BlockSpec((TQ, ow), q_map),
        compiler_params=_params("arbitrary", "arbitrary", "arbitrary"),
        name="attn_" + mode,
    )(*args)


def _na_band_start(t):
    return jnp.clip(t * NA_QROWS - NA_ROWS // 2, 0, S // GRID_W - NA_BAND)


def _na_variant(t):
    last = S // GRID_W // NA_QROWS - 1
    return jnp.where(t < 2, t, jnp.where(t > last - 2, t - (last - 4), 2))


def _na_kernel(q_ref, kc_ref, vc_ref, kl_ref, vl_ref, tab_ref, o_ref):
    t = pl.program_id(2)
    off = pl.multiple_of(_na_band_start(t) * GRID_W, GRID_W)
    q = q_ref[...]
    heads = ((0, 0, 0), (0, 1, 0))
    accs = []
    for _, kv, _ in heads:
        m0 = jnp.full((q.shape[0], 1), -jnp.inf, F32)
        a0 = jnp.zeros((q.shape[0], LANES), F32)
        m1, a1 = _softmax_step(q, kc_ref[kv], vc_ref[kv], m0, a0)
        _, a2 = _softmax_step(q, kl_ref[kv, pl.ds(off, NA_BAND * GRID_W), :],
                              vl_ref[kv, pl.ds(off, NA_BAND * GRID_W), :], m1, a1,
                              bias=tab_ref[kv])
        accs.append(a2)
    _finish_heads(heads, accs, o_ref)


def _na_attention(q, k4, v4, table):
    tq = NA_QROWS * GRID_W
    nq = S // tq
    ctx_spec = pl.BlockSpec((None, 2, CTX, LANES), lambda b, g, t: (g, 0, N_LAT // CTX + b, 0))
    lat_spec = pl.BlockSpec((None, 2, S, LANES), lambda b, g, t: (g, 0, b, 0))
    q_spec = pl.BlockSpec((tq, LANES), lambda b, g, t: (b * nq + t, g))
    return pl.pallas_call(
        _na_kernel,
        out_shape=jax.ShapeDtypeStruct((N_LAT, N_HEADS * HEAD_DIM), BF16),
        grid=(B, N_HEADS // 2, nq),
        in_specs=[q_spec, ctx_spec, ctx_spec, lat_spec, lat_spec,
                  pl.BlockSpec((None, 2, tq, NA_BAND * GRID_W),
                               lambda b, g, t: (_na_variant(t), g, 0, 0))],
        out_specs=q_spec,
        compiler_params=_params("arbitrary", "arbitrary", "arbitrary"),
        name="attn_neighbourhood",
    )(q, k4, v4, k4, v4, table)


def _na_table(rpb):
    tq = NA_QROWS * GRID_W
    rows = S // GRID_W
    qi = np.arange(tq)
    ki = np.arange(NA_BAND * GRID_W)
    drs, dcs, valids = [], [], []
    for r0, bs in ((0, 0), (2, 0), (4, 0), (rows - 4, rows - NA_BAND), (rows - 2, rows - NA_BAND)):
        r, c = r0 + qi // GRID_W, qi % GRID_W
        kr, kc = bs + ki // GRID_W, ki % GRID_W
        rs = np.clip(r - NA_ROWS // 2, 0, rows - NA_ROWS)
        cs = np.clip(c - NA_COLS // 2, 0, GRID_W - NA_COLS)
        valid = ((kr[None] >= rs[:, None]) & (kr[None] < rs[:, None] + NA_ROWS)
                 & (kc[None] >= cs[:, None]) & (kc[None] < cs[:, None] + NA_COLS))
        drs.append(np.where(valid, kr[None] - r[:, None] + NA_ROWS - 1, 0))
        dcs.append(np.where(valid, kc[None] - c[:, None] + NA_COLS - 1, 0))
        valids.append(valid)
    dr, dc, valid = (np.stack(a) for a in (drs, dcs, valids))
    tab = jnp.where(valid[None], rpb[:, dr, dc], NEG)
    return jnp.swapaxes(tab, 0, 1)


def _router_combine(h, wr_ref):
    logits = jnp.dot(h, wr_ref[...], preferred_element_type=F32, precision=lax.Precision.HIGHEST)
    lane = _lane(logits.shape).astype(F32)
    lg = jnp.where(lane < N_EXPERTS, logits, -jnp.inf)
    m1 = jnp.max(lg, axis=-1, keepdims=True)
    i1 = jnp.min(jnp.where(lg == m1, lane, float(LANES)), axis=-1, keepdims=True)
    lg2 = jnp.where(lane == i1, -jnp.inf, lg)
    m2 = jnp.max(lg2, axis=-1, keepdims=True)
    i2 = jnp.min(jnp.where(lg2 == m2, lane, float(LANES)), axis=-1, keepdims=True)
    e = jnp.exp(m2 - m1)
    return jnp.where(lane == i1, 1.0 / (1.0 + e), 0.0) + jnp.where(lane == i2, e / (1.0 + e), 0.0)


def _oproj_kernel(*refs, moe):
    if moe:
        o_ref, wo_ref, x_ref, mod_ref, post_ref, pre_ref, wr_ref, xo_ref, h_ref, comb_ref = refs
    else:
        o_ref, wo_ref, x_ref, mod_ref, post_ref, pre_ref, xo_ref, h_ref = refs
    y = jnp.dot(o_ref[...], wo_ref[...], preferred_element_type=F32)
    x1 = x_ref[...] + mod_ref[:, G_M] * _rms(y, post_ref[...])
    xo_ref[...] = x1
    h = _rms(x1, pre_ref[...]) * (1.0 + mod_ref[:, SC_F]) + mod_ref[:, SH_F]
    h_ref[...] = h.astype(BF16)
    if moe:
        comb_ref[...] = _router_combine(h, wr_ref)


def _oproj(o, wo, x, mod, post, pre, w_router, n_rows):
    moe = w_router is not None
    row = lambda i: (i, 0)
    const = lambda i: (0, 0)
    in_specs = [pl.BlockSpec((TM, D), row), pl.BlockSpec((D, D), const), pl.BlockSpec((TM, D), row),
                pl.BlockSpec((None, 1, 6 * D), lambda i: (_mod_row(i), 0, 0)),
                pl.BlockSpec((1, D), const), pl.BlockSpec((1, D), const)]
    out_shape = [jax.ShapeDtypeStruct((n_rows, D), F32), jax.ShapeDtypeStruct((n_rows, D), BF16)]
    out_specs = [pl.BlockSpec((TM, D), row), pl.BlockSpec((TM, D), row)]
    args = [o, wo, x, mod, post, pre]
    if moe:
        in_specs.append(pl.BlockSpec((D, LANES), const))
        out_shape.append(jax.ShapeDtypeStruct((n_rows, LANES), F32))
        out_specs.append(pl.BlockSpec((TM, LANES), row))
        args.append(w_router)
    return pl.pallas_call(
        functools.partial(_oproj_kernel, moe=moe),
        out_shape=tuple(out_shape),
        grid=(n_rows // TM,),
        in_specs=in_specs,
        out_specs=tuple(out_specs),
        compiler_params=_params("arbitrary"),
        name="oproj",
    )(*args)


def _ffn_kernel(*refs, moe):
    if moe:
        h_ref, gw_ref, uw_ref, dw_ref, x_ref, mod_ref, post_ref, comb_ref, o_ref, acc_ref = refs
    else:
        h_ref, gw_ref, uw_ref, dw_ref, x_ref, mod_ref, post_ref, o_ref, acc_ref = refs
    j = pl.program_id(1)

    @pl.when(j == 0)
    def _():
        acc_ref[...] = jnp.zeros_like(acc_ref)

    h = h_ref[...]
    gate = jnp.dot(h, gw_ref[...], preferred_element_type=F32)
    up = jnp.dot(h, uw_ref[...], preferred_element_type=F32)
    y = jnp.dot((jax.nn.silu(gate) * up).astype(BF16), dw_ref[...], preferred_element_type=F32)
    if moe:
        comb = comb_ref[...]
        y = y * jnp.sum(jnp.where(_lane(comb.shape) == j, comb, 0.0), axis=-1, keepdims=True)
    acc_ref[...] += y

    @pl.when(j == pl.num_programs(1) - 1)
    def _():
        o_ref[...] = x_ref[...] + mod_ref[:, G_F] * _rms(acc_ref[...], post_ref[...])


def _ffn(h, w_gu, w_down, x, mod, post, comb, n_rows):
    moe = comb is not None
    row = lambda i, j: (i, 0)
    if moe:
        n_steps = N_EXPERTS
        w_specs = [pl.BlockSpec((None, D, TF), lambda i, j: (j, 0, 0)),
                   pl.BlockSpec((None, D, TF), lambda i, j: (j, 0, 1)),
                   pl.BlockSpec((None, TF, D), lambda i, j: (j, 0, 0))]
    else:
        n_steps = D_FF // TF
        w_specs = [pl.BlockSpec((D, TF), lambda i, j: (0, j)),
                   pl.BlockSpec((D, TF), lambda i, j: (0, n_steps + j)),
                   pl.BlockSpec((TF, D), lambda i, j: (j, 0))]
    in_specs = [pl.BlockSpec((TM, D), row)] + w_specs + [
        pl.BlockSpec((TM, D), row),
        pl.BlockSpec((None, 1, 6 * D), lambda i, j: (_mod_row(i), 0, 0)),
        pl.BlockSpec((1, D), lambda i, j: (0, 0))]
    args = [h, w_gu, w_gu, w_down, x, mod, post]
    if moe:
        in_specs.append(pl.BlockSpec((TM, LANES), row))
        args.append(comb)
    return pl.pallas_call(
        functools.partial(_ffn_kernel, moe=moe),
        out_shape=jax.ShapeDtypeStruct((n_rows, D), F32),
        grid=(n_rows // TM, n_steps),
        in_specs=in_specs,
        out_specs=pl.BlockSpec((TM, D), row),
        scratch_shapes=[pltpu.VMEM((TM, D), F32)],
        compiler_params=_params("arbitrary", "arbitrary"),
        name="ffn_moe" if moe else "ffn_dense",
    )(*args)


def _rope_tables(rot_dim, lead, tail):
    t = jnp.arange(S, dtype=jnp.int32)
    row = (t // GRID_W).astype(F32)
    col = (t % GRID_W).astype(F32)
    quarter = rot_dim // 4
    inv = ROPE_THETA ** (-jnp.arange(quarter, dtype=F32) / quarter)
    ang = jnp.concatenate([row[:, None] * inv, col[:, None] * inv], axis=-1)
    cos, sin = jnp.cos(ang), jnp.sin(ang)
    reps = (LANES - lead - tail) // rot_dim
    cos_l = jnp.concatenate([jnp.ones((S, lead), F32)] + [cos, cos] * reps + [jnp.ones((S, tail), F32)], axis=1)
    sin_l = jnp.concatenate([jnp.zeros((S, lead), F32)] + [-sin, sin] * reps + [jnp.zeros((S, tail), F32)], axis=1)
    cos_all = jnp.concatenate([jnp.tile(cos_l, (B, 1)), jnp.ones((N_CTX, LANES), F32)], axis=0)
    sin_all = jnp.concatenate([jnp.tile(sin_l, (B, 1)), jnp.zeros((N_CTX, LANES), F32)], axis=0)
    return cos_all, sin_all


def _mla_weights(w_dkv, w_q_b, w_kv_b):
    lat = MLA_Q_LORA + MLA_KV_LORA
    pad = LANES - MLA_NOPE - MLA_ROPE
    wd = jnp.concatenate([w_dkv[:, :lat], jnp.zeros((D, MLA_NOPE), F32), w_dkv[:, lat:],
                          jnp.zeros((D, pad), F32)], axis=1)
    wq = jnp.pad(w_q_b.reshape(MLA_Q_LORA, N_HEADS, MLA_NOPE + MLA_ROPE), ((0, 0), (0, 0), (0, pad)))
    kvr = w_kv_b.reshape(MLA_KV_LORA, N_HEADS, 2 * HALF)
    k_part, v_part = kvr[..., :HALF], kvr[..., HALF:]
    zero = jnp.zeros_like(v_part)
    wk = jnp.concatenate([k_part, zero], axis=-1)
    even = (jnp.arange(N_HEADS) % 2 == 0)[None, :, None]
    wv = jnp.where(even, jnp.concatenate([v_part, zero], -1), jnp.concatenate([zero, v_part], -1))
    wkv = jnp.concatenate([wk.reshape(MLA_KV_LORA, -1), wv.reshape(MLA_KV_LORA, -1)], axis=1)
    return wd.astype(BF16), wq.reshape(MLA_Q_LORA, -1).astype(BF16), wkv.astype(BF16)


GQA_HEADS = ((0, 0, 0), (0, 1, 0), (1, 0, 1), (1, 1, 1))
PAIR_HEADS = ((0, 0, 0), (1, 1, 0))


def kernel(x, c, ctx, c_ctx,
           l0_ada_w, l0_ada_b, l0_mix_pre, l0_mix_post, l0_ffn_pre, l0_ffn_post,
           l0_w_qkv, l0_q_norm, l0_k_norm, l0_w_o, l0_ffn_w_gu, l0_ffn_w_down,
           l1_ada_w, l1_ada_b, l1_mix_pre, l1_mix_post, l1_ffn_pre, l1_ffn_post,
           l1_w_qkv, l1_sink, l1_w_o, l1_router, l1_moe_w_gu, l1_moe_w_down,
           l2_ada_w, l2_ada_b, l2_mix_pre, l2_mix_post, l2_ffn_pre, l2_ffn_post,
           l2_w_dkv, l2_q_a_norm, l2_w_q_b, l2_kv_a_norm, l2_w_kv_b, l2_w_o, l2_ffn_w_gu, l2_ffn_w_down,
           l3_ada_w, l3_ada_b, l3_mix_pre, l3_mix_post, l3_ffn_pre, l3_ffn_post,
           l3_w_qkv, l3_rpb, l3_w_o, l3_router, l3_moe_w_gu, l3_moe_w_down):
    vec = lambda a: a.reshape(1, -1)
    tile2 = lambda a: jnp.tile(a, 2).reshape(1, LANES)
    bf = lambda a: a.astype(BF16)
    pad_router = lambda w: jnp.pad(w, ((0, 0), (0, LANES - N_EXPERTS)))

    xs = jnp.concatenate([x.reshape(N_LAT, D), ctx.reshape(N_CTX, D)], axis=0)
    c_all = jnp.concatenate([c, c_ctx[None], jnp.zeros((8 - B - 1, D), F32)], axis=0)
    mods = [_ada(c_all, w, b).reshape(8, 1, 6 * D)
            for w, b in ((l0_ada_w, l0_ada_b), (l1_ada_w, l1_ada_b), (l2_ada_w, l2_ada_b), (l3_ada_w, l3_ada_b))]
    cos64, sin64 = _rope_tables(HEAD_DIM, 0, 0)
    cos32, sin32 = _rope_tables(MLA_ROPE, MLA_NOPE, LANES - MLA_NOPE - MLA_ROPE)
    ones = jnp.ones((1, LANES), F32)
    idx = np.arange(LANES)
    bd = jnp.asarray(idx[:, None] // HALF == idx[None, :] // HALF, BF16)

    q, k4, v4 = _qkv(xs, mods[0], vec(l0_mix_pre), bf(l0_w_qkv), cos64, sin64,
                     tile2(l0_q_norm), tile2(l0_k_norm), bd, shared_kv=True, head_norm=True, rope=True)
    o = _attention(q, k4, v4, heads=GQA_HEADS, mode="global")
    xs, h = _oproj(o, bf(l0_w_o), xs, mods[0], vec(l0_mix_post), vec(l0_ffn_pre), None, N_TOK)
    xs = _ffn(h, bf(l0_ffn_w_gu), bf(l0_ffn_w_down), xs, mods[0], vec(l0_ffn_post), None, N_TOK)

    q, k4, v4 = _qkv(xs, mods[1], vec(l1_mix_pre), bf(l1_w_qkv), cos64, sin64,
                     ones, ones, bd, shared_kv=True, head_norm=False, rope=True)
    o = _attention(q, k4, v4, heads=GQA_HEADS, mode="window", sink=l1_sink)
    xs, h, comb = _oproj(o, bf(l1_w_o), xs, mods[1], vec(l1_mix_post), vec(l1_ffn_pre),
                         pad_router(l1_router), N_TOK)
    xs = _ffn(h, bf(l1_moe_w_gu), bf(l1_moe_w_down), xs, mods[1], vec(l1_ffn_post), comb, N_TOK)

    wd, wq, wkv = _mla_weights(l2_w_dkv, l2_w_q_b, l2_w_kv_b)
    q, k4, v4 = _mla_proj(xs, mods[2], vec(l2_mix_pre), wd, vec(l2_q_a_norm), vec(l2_kv_a_norm),
                          wq, wkv, cos32, sin32)
    o = _attention(q, k4, v4, heads=PAIR_HEADS, mode="global")
    xs, h = _oproj(o, bf(l2_w_o), xs, mods[2], vec(l2_mix_post), vec(l2_ffn_pre), None, N_TOK)
    xs = _ffn(h, bf(l2_ffn_w_gu), bf(l2_ffn_w_down), xs, mods[2], vec(l2_ffn_post), None, N_TOK)

    q, k4, v4 = _qkv(xs, mods[3], vec(l3_mix_pre), bf(l3_w_qkv), cos64, sin64,
                     ones, ones, bd, shared_kv=False, head_norm=False, rope=False)
    o = _na_attention(q, k4, v4, _na_table(l3_rpb))
    xl, h, comb = _oproj(o, bf(l3_w_o), xs, mods[3], vec(l3_mix_post), vec(l3_ffn_pre),
                         pad_router(l3_router), N_LAT)
    xl = _ffn(h, bf(l3_moe_w_gu), bf(l3_moe_w_down), xl, mods[3], vec(l3_ffn_post), comb, N_LAT)
    return xl.reshape(B, S, D)
```

```python
import functools

import numpy as np
import jax
import jax.numpy as jnp
from jax import lax
from jax.experimental import pallas as pl
from jax.experimental.pallas import tpu as pltpu

D = 1024
B = 4
S = 4096
CTX = 256
GRID_W = 64
N_LAT = B * S
N_CTX = B * CTX
N_TOK = N_LAT + N_CTX
HEAD_DIM = 64
N_HEADS = 16
N_KV_HEADS = 4
WINDOW = 128
MLA_Q_LORA = 384
MLA_KV_LORA = 256
MLA_NOPE = 64
MLA_ROPE = 32
NA_ROWS = 8
NA_COLS = 16
D_FF = 2816
N_EXPERTS = 8
D_FF_EXPERT = 1408
ROPE_THETA = 10000.0
EPS = 1e-6
NEG = -1e30

LANES = 128
HALF = LANES // 2
TM = 512
TQ = 256
TK = 512
TF = 1408
NA_QROWS = 2
NA_BAND = NA_QROWS + NA_ROWS - 1
VMEM_LIMIT = 56 * 1024 * 1024

F32 = jnp.float32
BF16 = jnp.bfloat16

SH_M, SC_M, G_M, SH_F, SC_F, G_F = (slice(i * D, (i + 1) * D) for i in range(6))


def _params(*sem):
    return pltpu.CompilerParams(dimension_semantics=sem, vmem_limit_bytes=VMEM_LIMIT)


def _mod_row(i):
    r0 = i * TM
    return jnp.where(r0 >= N_LAT, B, r0 // S)


def _rms(x, gain):
    return x * lax.rsqrt(jnp.mean(x * x, axis=-1, keepdims=True) + EPS) * gain


def _lane(shape):
    return lax.broadcasted_iota(jnp.int32, shape, len(shape) - 1)


def _ada_kernel(c_ref, w_ref, b_ref, o_ref):
    a = jax.nn.silu(c_ref[...])
    o_ref[...] = jnp.dot(a, w_ref[...], preferred_element_type=F32,
                         precision=lax.Precision.HIGHEST) + b_ref[...]


def _ada(c_all, w, b):
    tn = 1536
    return pl.pallas_call(
        _ada_kernel,
        out_shape=jax.ShapeDtypeStruct((8, 6 * D), F32),
        grid=(6 * D // tn,),
        in_specs=[pl.BlockSpec((8, D), lambda j: (0, 0)),
                  pl.BlockSpec((D, tn), lambda j: (0, j)),
                  pl.BlockSpec((1, tn), lambda j: (0, j))],
        out_specs=pl.BlockSpec((8, tn), lambda j: (0, j)),
        compiler_params=_params("arbitrary"),
        name="ada",
    )(c_all, w, b.reshape(1, 6 * D))


def _rope_chunk(t, cos, sin, half):
    lane = _lane(t.shape)
    rot = jnp.where((lane % (2 * half)) < half,
                    pltpu.roll(t, LANES - half, 1), pltpu.roll(t, half, 1))
    return t * cos + rot * sin


def _head_rms(t, gain, bd):
    t2 = t * t
    hi = t2.astype(BF16)
    lo = (t2 - hi.astype(F32)).astype(BF16)
    ss = (jnp.dot(hi, bd, preferred_element_type=F32)
          + jnp.dot(lo, bd, preferred_element_type=F32))
    return t * lax.rsqrt(ss * (1.0 / HEAD_DIM) + EPS) * gain


def _with_ones(v, lane, parity):
    if parity == 0:
        return jnp.where(lane < HALF, v, jnp.where(lane == LANES - 1, 1.0, 0.0))
    return jnp.where(lane >= HALF, v, jnp.where(lane == 0, 1.0, 0.0))


def _half_only(k, lane, parity):
    return jnp.where(lane < HALF, k, 0.0) if parity == 0 else jnp.where(lane >= HALF, k, 0.0)


def _qkv_kernel(x_ref, mod_ref, g_ref, w_ref, cos_ref, sin_ref, qg_ref, kg_ref, bd_ref,
                q_ref, k_ref, v_ref, *, shared_kv, head_norm, rope):
    h = _rms(x_ref[...], g_ref[...]) * (1.0 + mod_ref[:, SC_M]) + mod_ref[:, SH_M]
    acc = jnp.dot(h.astype(BF16), w_ref[...], preferred_element_type=F32)
    lane = _lane((TM, LANES))
    n_q = N_HEADS * HEAD_DIM // LANES
    n_kv = (N_KV_HEADS if shared_kv else N_HEADS) * HEAD_DIM // LANES

    def finish(t, gain_ref):
        if head_norm:
            t = _head_rms(t, gain_ref[...], bd_ref[...])
        if rope:
            t = _rope_chunk(t, cos_ref[...], sin_ref[...], HEAD_DIM // 2)
        return t

    for c in range(n_q):
        t = finish(acc[:, c * LANES:(c + 1) * LANES], qg_ref)
        q_ref[:, c * LANES:(c + 1) * LANES] = (t * HEAD_DIM ** -0.5).astype(BF16)
    for c in range(n_kv):
        k = finish(acc[:, (n_q + c) * LANES:(n_q + c + 1) * LANES], kg_ref)
        v = acc[:, (n_q + n_kv + c) * LANES:(n_q + n_kv + c + 1) * LANES]
        if shared_kv:
            k_sw = pltpu.roll(k, HALF, 1)
            v_sw = pltpu.roll(v, HALF, 1)
            for half in range(2):
                j = 2 * c + half
                for parity in range(2):
                    ksrc = k if parity == half else k_sw
                    vsrc = v if parity == half else v_sw
                    k_ref[j, parity] = _half_only(ksrc, lane, parity).astype(BF16)
                    v_ref[j, parity] = _with_ones(vsrc, lane, parity).astype(BF16)
        else:
            for parity in range(2):
                k_ref[c, parity] = _half_only(k, lane, parity).astype(BF16)
                v_ref[c, parity] = _with_ones(v, lane, parity).astype(BF16)


def _qkv(x, mod, gain, w, cos, sin, qg, kg, bd, *, shared_kv, head_norm, rope):
    n_out = w.shape[1]
    groups = N_KV_HEADS if shared_kv else N_HEADS // 2
    row = lambda i: (i, 0)
    const = lambda i: (0, 0)
    kv_spec = pl.BlockSpec((groups, 2, TM, LANES), lambda i: (0, 0, i, 0))
    kv_shape = jax.ShapeDtypeStruct((groups, 2, N_TOK, LANES), BF16)
    return pl.pallas_call(
        functools.partial(_qkv_kernel, shared_kv=shared_kv, head_norm=head_norm, rope=rope),
        out_shape=(jax.ShapeDtypeStruct((N_TOK, N_HEADS * HEAD_DIM), BF16), kv_shape, kv_shape),
        grid=(N_TOK // TM,),
        in_specs=[pl.BlockSpec((TM, D), row),
                  pl.BlockSpec((None, 1, 6 * D), lambda i: (_mod_row(i), 0, 0)),
                  pl.BlockSpec((1, D), const),
                  pl.BlockSpec((D, n_out), const),
                  pl.BlockSpec((TM, LANES), row),
                  pl.BlockSpec((TM, LANES), row),
                  pl.BlockSpec((1, LANES), const),
                  pl.BlockSpec((1, LANES), const),
                  pl.BlockSpec((LANES, LANES), const)],
        out_specs=(pl.BlockSpec((TM, N_HEADS * HEAD_DIM), row), kv_spec, kv_spec),
        compiler_params=_params("arbitrary"),
        name="qkv",
    )(x, mod, gain, w, cos, sin, qg, kg, bd)


def _mla_kernel(x_ref, mod_ref, g_ref, wd_ref, qan_ref, kvan_ref, wq_ref, wkv_ref,
                cos_ref, sin_ref, q_ref, k_ref, v_ref):
    h = _rms(x_ref[...], g_ref[...]) * (1.0 + mod_ref[:, SC_M]) + mod_ref[:, SH_M]
    lat = jnp.dot(h.astype(BF16), wd_ref[...], preferred_element_type=F32)
    c_q = _rms(lat[:, :MLA_Q_LORA], qan_ref[...])
    c_kv = _rms(lat[:, MLA_Q_LORA:MLA_Q_LORA + MLA_KV_LORA], kvan_ref[...])
    cos = cos_ref[...]
    sin = sin_ref[...]
    k_rope = _rope_chunk(lat[:, MLA_Q_LORA + MLA_KV_LORA:], cos, sin, MLA_ROPE // 2)
    q = jnp.dot(c_q.astype(BF16), wq_ref[...], preferred_element_type=F32)
    kv = jnp.dot(c_kv.astype(BF16), wkv_ref[...], preferred_element_type=F32)
    scale = (MLA_NOPE + MLA_ROPE) ** -0.5
    lane = _lane((TM, LANES))
    for hd in range(N_HEADS):
        sl = slice(hd * LANES, (hd + 1) * LANES)
        q_ref[:, sl] = (_rope_chunk(q[:, sl], cos, sin, MLA_ROPE // 2) * scale).astype(BF16)
        k_ref[hd // 2, hd % 2] = (kv[:, sl] + k_rope).astype(BF16)
        v = kv[:, N_HEADS * LANES + hd * LANES:N_HEADS * LANES + (hd + 1) * LANES]
        v_ref[hd // 2, hd % 2] = _with_ones(v, lane, hd % 2).astype(BF16)


def _mla_proj(x, mod, gain, wd, qan, kvan, wq, wkv, cos, sin):
    row = lambda i: (i, 0)
    const = lambda i: (0, 0)
    kv_spec = pl.BlockSpec((N_HEADS // 2, 2, TM, LANES), lambda i: (0, 0, i, 0))
    kv_shape = jax.ShapeDtypeStruct((N_HEADS // 2, 2, N_TOK, LANES), BF16)
    return pl.pallas_call(
        _mla_kernel,
        out_shape=(jax.ShapeDtypeStruct((N_TOK, N_HEADS * LANES), BF16), kv_shape, kv_shape),
        grid=(N_TOK // TM,),
        in_specs=[pl.BlockSpec((TM, D), row),
                  pl.BlockSpec((None, 1, 6 * D), lambda i: (_mod_row(i), 0, 0)),
                  pl.BlockSpec((1, D), const),
                  pl.BlockSpec(wd.shape, const),
                  pl.BlockSpec((1, MLA_Q_LORA), const),
                  pl.BlockSpec((1, MLA_KV_LORA), const),
                  pl.BlockSpec(wq.shape, const),
                  pl.BlockSpec(wkv.shape, const),
                  pl.BlockSpec((TM, LANES), row),
                  pl.BlockSpec((TM, LANES), row)],
        out_specs=(pl.BlockSpec((TM, N_HEADS * LANES), row), kv_spec, kv_spec),
        compiler_params=_params("arbitrary"),
        name="mla_proj",
    )(x, mod, gain, wd, qan, kvan, wq, wkv, cos, sin)


def _attn_kernel(*refs, stacks, tq, mode, with_lat, aliased):
    refs = list(refs)
    sink_ref = refs.pop(0) if mode == "window" else None
    q_ref, kc_ref, vc_ref = refs[:3]
    refs = refs[3:]
    if with_lat:
        kl_ref, vl_ref = refs[:2]
        refs = refs[2:]
    tab_ref = refs.pop(0) if mode == "neighbourhood" else None
    if aliased:
        refs.pop(0)
    o_ref, s_ref = refs
    g = pl.program_id(1)
    t = pl.program_id(2)

    chunks = [("ctx", 0, CTX)]
    mask = None
    if with_lat and mode == "global":
        chunks += [("lat", c * TK, TK) for c in range(S // TK)]
    elif with_lat and mode == "window":
        span = tq + 2 * WINDOW
        q0 = t * tq
        start = pl.multiple_of(jnp.clip(q0 - WINDOW, 0, S - span), WINDOW)
        m_rows = tq * len(stacks[0][0])
        qi = q0 + lax.broadcasted_iota(jnp.int32, (m_rows, span), 0) % tq
        ki = start + lax.broadcasted_iota(jnp.int32, (m_rows, span), 1)
        mask = jnp.abs(qi - ki) <= WINDOW
        chunks += [("lat", start, span)]
    elif with_lat:
        start = pl.multiple_of(_na_band_start(t) * GRID_W, GRID_W)
        chunks += [("lat", start, NA_BAND * GRID_W)]

    outs = {}
    for qcs, kv in stacks:
        q = jnp.concatenate([q_ref[:, qc * LANES:(qc + 1) * LANES] for qc, _ in qcs], axis=0)
        rows = q.shape[0]
        sink_col = None
        if mode == "window":
            hd = [g * 2 * len(qcs) + 2 * qc + kv for qc, _ in qcs]
            row = lax.broadcasted_iota(jnp.int32, (rows, 1), 0)
            sink_col = jnp.where(row < tq, sink_ref[hd[0]], sink_ref[hd[-1]])

        col = 0
        m_fold, m_col = None, sink_col
        for kind, st, width in chunks:
            k = kc_ref[kv] if kind == "ctx" else kl_ref[kv, pl.ds(st, width), :]
            s = lax.dot_general(q, k, (((1,), (1,)), ((), ())), preferred_element_type=F32)
            if kind == "lat" and tab_ref is not None:
                s = s + tab_ref[kv]
            if kind == "lat" and mask is not None:
                s = jnp.where(mask, s, NEG)
            s_ref[0:rows, col:col + width] = s
            if width % LANES == 0:
                for j in range(width // LANES):
                    slab = s[:, j * LANES:(j + 1) * LANES]
                    m_fold = slab if m_fold is None else jnp.maximum(m_fold, slab)
            else:
                mc = jnp.max(s, axis=-1, keepdims=True)
                m_col = mc if m_col is None else jnp.maximum(m_col, mc)
            col += width
        m = jnp.max(m_fold, axis=-1, keepdims=True)
        if m_col is not None:
            m = jnp.maximum(m, m_col)

        col = 0
        acc = jnp.zeros((rows, LANES), F32)
        for kind, st, width in chunks:
            v = vc_ref[kv] if kind == "ctx" else vl_ref[kv, pl.ds(st, width), :]
            p = jnp.exp(s_ref[0:rows, col:col + width] - m).astype(BF16)
            acc = acc + jnp.dot(p, v, preferred_element_type=F32)
            col += width
        den = acc[:, LANES - 1:LANES] if kv == 0 else acc[:, 0:1]
        if sink_col is not None:
            den = den + jnp.exp(sink_col - m)
        lane = _lane(acc.shape)
        o = jnp.where(lane < HALF if kv == 0 else lane >= HALF, acc / den, 0.0)
        for i, (_, oc) in enumerate(qcs):
            part = o[i * tq:(i + 1) * tq]
            outs[oc] = part if oc not in outs else outs[oc] + part
    for oc, o in outs.items():
        o_ref[:, oc * LANES:(oc + 1) * LANES] = o.astype(o_ref.dtype)


def _attention(q, k4, v4, *, stacks, mode, tq, sink=None, table=None, ctx_into=None):
    groups = k4.shape[0]
    qw = q.shape[1] // groups
    ow = N_HEADS * HEAD_DIM // groups
    with_lat = ctx_into is None
    nq = S // tq if with_lat else 1
    row0 = 0 if with_lat else N_LAT // tq

    q_spec = pl.BlockSpec((tq, qw), lambda b, g, t: (row0 + b * nq + t, g))
    ctx_spec = pl.BlockSpec((None, 2, CTX, LANES), lambda b, g, t: (g, 0, N_LAT // CTX + b, 0))
    lat_spec = pl.BlockSpec((None, 2, S, LANES), lambda b, g, t: (g, 0, b, 0))
    in_specs = [q_spec, ctx_spec, ctx_spec]
    args = [q, k4, v4]
    n_keys = CTX
    if with_lat:
        in_specs += [lat_spec, lat_spec]
        args += [k4, v4]
        n_keys += {"global": S, "window": tq + 2 * WINDOW, "neighbourhood": NA_BAND * GRID_W}[mode]
    if mode == "window":
        in_specs = [pl.BlockSpec(memory_space=pltpu.SMEM)] + in_specs
        args = [sink] + args
    if mode == "neighbourhood":
        in_specs.append(pl.BlockSpec((None, 2, tq, NA_BAND * GRID_W),
                                     lambda b, g, t: (_na_variant(t), g, 0, 0)))
        args.append(table)
    aliases = {}
    if not with_lat:
        aliases = {len(args): 0}
        in_specs.append(pl.BlockSpec(memory_space=pl.ANY))
        args.append(ctx_into)
    n_rows = N_LAT if mode == "neighbourhood" else N_TOK
    m_rows = tq * max(len(qcs) for qcs, _ in stacks)
    return pl.pallas_call(
        functools.partial(_attn_kernel, stacks=stacks, tq=tq, mode=mode, with_lat=with_lat,
                          aliased=not with_lat),
        out_shape=jax.ShapeDtypeStruct((n_rows, N_HEADS * HEAD_DIM), BF16),
        grid=(B, groups, nq),
        in_specs=in_specs,
        out_specs=pl.BlockSpec((tq, ow), lambda b, g, t: (row0 + b * nq + t, g)),
        scratch_shapes=[pltpu.VMEM((m_rows, n_keys), F32)],
        input_output_aliases=aliases,
        compiler_params=_params("arbitrary", "arbitrary", "arbitrary"),
        name="attn_" + mode + ("" if with_lat else "_ctx"),
    )(*args)


def _na_band_start(t):
    return jnp.clip(t * NA_QROWS - NA_ROWS // 2, 0, S // GRID_W - NA_BAND)


def _na_variant(t):
    last = S // GRID_W // NA_QROWS - 1
    return jnp.where(t < 2, t, jnp.where(t > last - 2, t - (last - 4), 2))


def _bias_cols_kernel(rpb_ref, sel_ref, o_ref):
    o_ref[...] = jnp.dot(rpb_ref[...], sel_ref[...], preferred_element_type=F32,
                         precision=lax.Precision.HIGHEST)


def _na_table(rpb):
    rows = S // GRID_W
    n_dr, n_dc = 2 * NA_ROWS - 1, 2 * NA_COLS - 1
    c = np.arange(GRID_W)[:, None]
    kc = np.arange(GRID_W)[None, :]
    cs = np.clip(c - NA_COLS // 2, 0, GRID_W - NA_COLS)
    col_ok = (kc >= cs) & (kc < cs + NA_COLS)
    sel = (np.arange(LANES)[:, None, None] == (kc - c + NA_COLS - 1)[None]) & col_ok[None]
    sel = jnp.asarray(sel.reshape(LANES, GRID_W * GRID_W), F32)
    rpb2 = jnp.pad(rpb.reshape(N_HEADS * n_dr, n_dc), ((0, 256 - N_HEADS * n_dr), (0, LANES - n_dc)))
    cols = pl.pallas_call(
        _bias_cols_kernel,
        out_shape=jax.ShapeDtypeStruct((256, GRID_W * GRID_W), F32),
        name="na_bias_cols",
    )(rpb2, sel)
    cols = cols[:N_HEADS * n_dr].reshape(N_HEADS, n_dr, GRID_W, GRID_W)
    cols = jnp.where(col_ok[None, None], cols, NEG)
    masked = jnp.full((N_HEADS, GRID_W, GRID_W), NEG, F32)
    variants = []
    for r0, bs in ((0, 0), (2, 0), (4, 0), (rows - 4, rows - NA_BAND), (rows - 2, rows - NA_BAND)):
        q_rows = []
        for r in range(r0, r0 + NA_QROWS):
            rs = min(max(r - NA_ROWS // 2, 0), rows - NA_ROWS)
            q_rows.append(jnp.concatenate(
                [cols[:, kr - r + NA_ROWS - 1] if rs <= kr < rs + NA_ROWS else masked
                 for kr in range(bs, bs + NA_BAND)], axis=2))
        variants.append(jnp.concatenate(q_rows, axis=1))
    return jnp.stack(variants)


def _router_combine(h, wr_ref):
    logits = jnp.dot(h, wr_ref[...], preferred_element_type=F32, precision=lax.Precision.HIGHEST)
    lane = _lane(logits.shape).astype(F32)
    lg = jnp.where(lane < N_EXPERTS, logits, -jnp.inf)
    m1 = jnp.max(lg, axis=-1, keepdims=True)
    i1 = jnp.min(jnp.where(lg == m1, lane, float(LANES)), axis=-1, keepdims=True)
    lg2 = jnp.where(lane == i1, -jnp.inf, lg)
    m2 = jnp.max(lg2, axis=-1, keepdims=True)
    i2 = jnp.min(jnp.where(lg2 == m2, lane, float(LANES)), axis=-1, keepdims=True)
    e = jnp.exp(m2 - m1)
    return jnp.where(lane == i1, 1.0 / (1.0 + e), 0.0) + jnp.where(lane == i2, e / (1.0 + e), 0.0)


def _oproj_kernel(*refs, moe):
    if moe:
        o_ref, wo_ref, x_ref, mod_ref, post_ref, pre_ref, wr_ref, xo_ref, h_ref, comb_ref = refs
    else:
        o_ref, wo_ref, x_ref, mod_ref, post_ref, pre_ref, xo_ref, h_ref = refs
    y = jnp.dot(o_ref[...], wo_ref[...], preferred_element_type=F32)
    x1 = x_ref[...] + mod_ref[:, G_M] * _rms(y, post_ref[...])
    xo_ref[...] = x1
    h = _rms(x1, pre_ref[...]) * (1.0 + mod_ref[:, SC_F]) + mod_ref[:, SH_F]
    h_ref[...] = h.astype(BF16)
    if moe:
        comb_ref[...] = _router_combine(h, wr_ref)


def _oproj(o, wo, x, mod, post, pre, w_router, n_rows):
    moe = w_router is not None
    row = lambda i: (i, 0)
    const = lambda i: (0, 0)
    in_specs = [pl.BlockSpec((TM, D), row), pl.BlockSpec((D, D), const), pl.BlockSpec((TM, D), row),
                pl.BlockSpec((None, 1, 6 * D), lambda i: (_mod_row(i), 0, 0)),
                pl.BlockSpec((1, D), const), pl.BlockSpec((1, D), const)]
    out_shape = [jax.ShapeDtypeStruct((n_rows, D), F32), jax.ShapeDtypeStruct((n_rows, D), BF16)]
    out_specs = [pl.BlockSpec((TM, D), row), pl.BlockSpec((TM, D), row)]
    args = [o, wo, x, mod, post, pre]
    if moe:
        in_specs.append(pl.BlockSpec((D, LANES), const))
        out_shape.append(jax.ShapeDtypeStruct((n_rows, LANES), F32))
        out_specs.append(pl.BlockSpec((TM, LANES), row))
        args.append(w_router)
    return pl.pallas_call(
        functools.partial(_oproj_kernel, moe=moe),
        out_shape=tuple(out_shape),
        grid=(n_rows // TM,),
        in_specs=in_specs,
        out_specs=tuple(out_specs),
        compiler_params=_params("arbitrary"),
        name="oproj",
    )(*args)


def _ffn_kernel(*refs, moe):
    if moe:
        h_ref, gw_ref, uw_ref, dw_ref, x_ref, mod_ref, post_ref, comb_ref, o_ref, acc_ref = refs
    else:
        h_ref, gw_ref, uw_ref, dw_ref, x_ref, mod_ref, post_ref, o_ref, acc_ref = refs
    j = pl.program_id(1)

    @pl.when(j == 0)
    def _():
        acc_ref[...] = jnp.zeros_like(acc_ref)

    h = h_ref[...]
    gate = jnp.dot(h, gw_ref[...], preferred_element_type=F32)
    up = jnp.dot(h, uw_ref[...], preferred_element_type=F32)
    y = jnp.dot((jax.nn.silu(gate) * up).astype(BF16), dw_ref[...], preferred_element_type=F32)
    if moe:
        comb = comb_ref[...]
        y = y * jnp.sum(jnp.where(_lane(comb.shape) == j, comb, 0.0), axis=-1, keepdims=True)
    acc_ref[...] += y

    @pl.when(j == pl.num_programs(1) - 1)
    def _():
        o_ref[...] = x_ref[...] + mod_ref[:, G_F] * _rms(acc_ref[...], post_ref[...])


def _ffn(h, w_gu, w_down, x, mod, post, comb, n_rows):
    moe = comb is not None
    row = lambda i, j: (i, 0)
    if moe:
        n_steps = N_EXPERTS
        w_specs = [pl.BlockSpec((None, D, TF), lambda i, j: (j, 0, 0)),
                   pl.BlockSpec((None, D, TF), lambda i, j: (j, 0, 1)),
                   pl.BlockSpec((None, TF, D), lambda i, j: (j, 0, 0))]
    else:
        n_steps = D_FF // TF
        w_specs = [pl.BlockSpec((D, TF), lambda i, j: (0, j)),
                   pl.BlockSpec((D, TF), lambda i, j: (0, n_steps + j)),
                   pl.BlockSpec((TF, D), lambda i, j: (j, 0))]
    in_specs = [pl.BlockSpec((TM, D), row)] + w_specs + [
        pl.BlockSpec((TM, D), row),
        pl.BlockSpec((None, 1, 6 * D), lambda i, j: (_mod_row(i), 0, 0)),
        pl.BlockSpec((1, D), lambda i, j: (0, 0))]
    args = [h, w_gu, w_gu, w_down, x, mod, post]
    if moe:
        in_specs.append(pl.BlockSpec((TM, LANES), row))
        args.append(comb)
    return pl.pallas_call(
        functools.partial(_ffn_kernel, moe=moe),
        out_shape=jax.ShapeDtypeStruct((n_rows, D), F32),
        grid=(n_rows // TM, n_steps),
        in_specs=in_specs,
        out_specs=pl.BlockSpec((TM, D), row),
        scratch_shapes=[pltpu.VMEM((TM, D), F32)],
        compiler_params=_params("arbitrary", "arbitrary"),
        name="ffn_moe" if moe else "ffn_dense",
    )(*args)


def _rope_tables(rot_dim, lead, tail):
    t = jnp.arange(S, dtype=jnp.int32)
    row = (t // GRID_W).astype(F32)
    col = (t % GRID_W).astype(F32)
    quarter = rot_dim // 4
    inv = ROPE_THETA ** (-jnp.arange(quarter, dtype=F32) / quarter)
    ang = jnp.concatenate([row[:, None] * inv, col[:, None] * inv], axis=-1)
    cos, sin = jnp.cos(ang), jnp.sin(ang)
    reps = (LANES - lead - tail) // rot_dim
    cos_l = jnp.concatenate([jnp.ones((S, lead), F32)] + [cos, cos] * reps + [jnp.ones((S, tail), F32)], axis=1)
    sin_l = jnp.concatenate([jnp.zeros((S, lead), F32)] + [-sin, sin] * reps + [jnp.zeros((S, tail), F32)], axis=1)
    cos_all = jnp.concatenate([jnp.tile(cos_l, (B, 1)), jnp.ones((N_CTX, LANES), F32)], axis=0)
    sin_all = jnp.concatenate([jnp.tile(sin_l, (B, 1)), jnp.zeros((N_CTX, LANES), F32)], axis=0)
    return cos_all, sin_all


def _mla_weights(w_dkv, w_q_b, w_kv_b):
    lat = MLA_Q_LORA + MLA_KV_LORA
    pad = LANES - MLA_NOPE - MLA_ROPE
    wd = jnp.concatenate([w_dkv[:, :lat], jnp.zeros((D, MLA_NOPE), F32), w_dkv[:, lat:],
                          jnp.zeros((D, pad), F32)], axis=1)
    wq = jnp.pad(w_q_b.reshape(MLA_Q_LORA, N_HEADS, MLA_NOPE + MLA_ROPE), ((0, 0), (0, 0), (0, pad)))
    kvr = w_kv_b.reshape(MLA_KV_LORA, N_HEADS, 2 * HALF)
    k_part, v_part = kvr[..., :HALF], kvr[..., HALF:]
    zero = jnp.zeros_like(v_part)
    wk = jnp.concatenate([k_part, zero], axis=-1)
    even = (jnp.arange(N_HEADS) % 2 == 0)[None, :, None]
    wv = jnp.where(even, jnp.concatenate([v_part, zero], -1), jnp.concatenate([zero, v_part], -1))
    wkv = jnp.concatenate([wk.reshape(MLA_KV_LORA, -1), wv.reshape(MLA_KV_LORA, -1)], axis=1)
    return wd.astype(BF16), wq.reshape(MLA_Q_LORA, -1).astype(BF16), wkv.astype(BF16)


GQA_STACKS = ((((0, 0), (1, 1)), 0), (((0, 0), (1, 1)), 1))
MLA_STACKS = ((((0, 0),), 0), (((1, 0),), 1))
NA_STACKS = ((((0, 0),), 0), (((0, 0),), 1))


def kernel(x, c, ctx, c_ctx,
           l0_ada_w, l0_ada_b, l0_mix_pre, l0_mix_post, l0_ffn_pre, l0_ffn_post,
           l0_w_qkv, l0_q_norm, l0_k_norm, l0_w_o, l0_ffn_w_gu, l0_ffn_w_down,
           l1_ada_w, l1_ada_b, l1_mix_pre, l1_mix_post, l1_ffn_pre, l1_ffn_post,
           l1_w_qkv, l1_sink, l1_w_o, l1_router, l1_moe_w_gu, l1_moe_w_down,
           l2_ada_w, l2_ada_b, l2_mix_pre, l2_mix_post, l2_ffn_pre, l2_ffn_post,
           l2_w_dkv, l2_q_a_norm, l2_w_q_b, l2_kv_a_norm, l2_w_kv_b, l2_w_o, l2_ffn_w_gu, l2_ffn_w_down,
           l3_ada_w, l3_ada_b, l3_mix_pre, l3_mix_post, l3_ffn_pre, l3_ffn_post,
           l3_w_qkv, l3_rpb, l3_w_o, l3_router, l3_moe_w_gu, l3_moe_w_down):
    vec = lambda a: a.reshape(1, -1)
    tile2 = lambda a: jnp.tile(a, 2).reshape(1, LANES)
    bf = lambda a: a.astype(BF16)
    pad_router = lambda w: jnp.pad(w, ((0, 0), (0, LANES - N_EXPERTS)))

    xs = jnp.concatenate([x.reshape(N_LAT, D), ctx.reshape(N_CTX, D)], axis=0)
    c_all = jnp.concatenate([c, c_ctx[None], jnp.zeros((8 - B - 1, D), F32)], axis=0)
    mods = [_ada(c_all, w, b).reshape(8, 1, 6 * D)
            for w, b in ((l0_ada_w, l0_ada_b), (l1_ada_w, l1_ada_b), (l2_ada_w, l2_ada_b), (l3_ada_w, l3_ada_b))]
    cos64, sin64 = _rope_tables(HEAD_DIM, 0, 0)
    cos32, sin32 = _rope_tables(MLA_ROPE, MLA_NOPE, LANES - MLA_NOPE - MLA_ROPE)
    ones = jnp.ones((1, LANES), F32)
    idx = np.arange(LANES)
    bd = jnp.asarray(idx[:, None] // HALF == idx[None, :] // HALF, BF16)

    q, k4, v4 = _qkv(xs, mods[0], vec(l0_mix_pre), bf(l0_w_qkv), cos64, sin64,
                     tile2(l0_q_norm), tile2(l0_k_norm), bd, shared_kv=True, head_norm=True, rope=True)
    o = _attention(q, k4, v4, stacks=GQA_STACKS, mode="global", tq=TQ)
    o = _attention(q, k4, v4, stacks=GQA_STACKS, mode="global", tq=CTX, ctx_into=o)
    xs, h = _oproj(o, bf(l0_w_o), xs, mods[0], vec(l0_mix_post), vec(l0_ffn_pre), None, N_TOK)
    xs = _ffn(h, bf(l0_ffn_w_gu), bf(l0_ffn_w_down), xs, mods[0], vec(l0_ffn_post), None, N_TOK)

    q, k4, v4 = _qkv(xs, mods[1], vec(l1_mix_pre), bf(l1_w_qkv), cos64, sin64,
                     ones, ones, bd, shared_kv=True, head_norm=False, rope=True)
    o = _attention(q, k4, v4, stacks=GQA_STACKS, mode="window", tq=TQ, sink=l1_sink)
    o = _attention(q, k4, v4, stacks=GQA_STACKS, mode="window", tq=CTX, sink=l1_sink, ctx_into=o)
    xs, h, comb = _oproj(o, bf(l1_w_o), xs, mods[1], vec(l1_mix_post), vec(l1_ffn_pre),
                         pad_router(l1_router), N_TOK)
    xs = _ffn(h, bf(l1_moe_w_gu), bf(l1_moe_w_down), xs, mods[1], vec(l1_ffn_post), comb, N_TOK)

    wd, wq, wkv = _mla_weights(l2_w_dkv, l2_w_q_b, l2_w_kv_b)
    q, k4, v4 = _mla_proj(xs, mods[2], vec(l2_mix_pre), wd, vec(l2_q_a_norm), vec(l2_kv_a_norm),
                          wq, wkv, cos32, sin32)
    o = _attention(q, k4, v4, stacks=MLA_STACKS, mode="global", tq=2 * TQ)
    o = _attention(q, k4, v4, stacks=MLA_STACKS, mode="global", tq=CTX, ctx_into=o)
    xs, h = _oproj(o, bf(l2_w_o), xs, mods[2], vec(l2_mix_post), vec(l2_ffn_pre), None, N_TOK)
    xs = _ffn(h, bf(l2_ffn_w_gu), bf(l2_ffn_w_down), xs, mods[2], vec(l2_ffn_post), None, N_TOK)

    q, k4, v4 = _qkv(xs, mods[3], vec(l3_mix_pre), bf(l3_w_qkv), cos64, sin64,
                     ones, ones, bd, shared_kv=False, head_norm=False, rope=False)
    o = _attention(q, k4, v4, stacks=NA_STACKS, mode="neighbourhood", tq=NA_QROWS * GRID_W,
                   table=_na_table(l3_rpb))
    xl, h, comb = _oproj(o, bf(l3_w_o), xs, mods[3], vec(l3_mix_post), vec(l3_ffn_pre),
                         pad_router(l3_router), N_LAT)
    xl = _ffn(h, bf(l3_moe_w_gu), bf(l3_moe_w_down), xl, mods[3], vec(l3_ffn_post), comb, N_LAT)
    return xl.reshape(B, S, D)
```

```python
import functools

import numpy as np
import jax
import jax.numpy as jnp
from jax import lax
from jax.experimental import pallas as pl
from jax.experimental.pallas import tpu as pltpu

D = 1024
B = 4
S = 4096
CTX = 256
GRID_W = 64
N_LAT = B * S
N_CTX = B * CTX
N_TOK = N_LAT + N_CTX
HEAD_DIM = 64
N_HEADS = 16
N_KV_HEADS = 4
WINDOW = 128
MLA_Q_LORA = 384
MLA_KV_LORA = 256
MLA_NOPE = 64
MLA_ROPE = 32
NA_ROWS = 8
NA_COLS = 16
D_FF = 2816
N_EXPERTS = 8
D_FF_EXPERT = 1408
ROPE_THETA = 10000.0
EPS = 1e-6
NEG = -1e30

LANES = 128
HALF = LANES // 2
TM = 512
TQ = 256
TK = 512
TF = 1408
TME = 256
TMR = 256
NA_QROWS = 2
NA_BAND = NA_QROWS + NA_ROWS - 1
VMEM_LIMIT = 56 * 1024 * 1024

F32 = jnp.float32
BF16 = jnp.bfloat16

SH_M, SC_M, G_M, SH_F, SC_F, G_F = (slice(i * D, (i + 1) * D) for i in range(6))


def _params(*sem):
    return pltpu.CompilerParams(dimension_semantics=sem, vmem_limit_bytes=VMEM_LIMIT)


def _mod_row(i):
    r0 = i * TM
    return jnp.where(r0 >= N_LAT, B, r0 // S)


def _rms(x, gain):
    return x * lax.rsqrt(jnp.mean(x * x, axis=-1, keepdims=True) + EPS) * gain


def _lane(shape):
    return lax.broadcasted_iota(jnp.int32, shape, len(shape) - 1)


def _ada_kernel(c_ref, w_ref, b_ref, o_ref):
    a = jax.nn.silu(c_ref[...])
    o_ref[...] = jnp.dot(a, w_ref[...], preferred_element_type=F32,
                         precision=lax.Precision.HIGHEST) + b_ref[...]


def _ada(c_all, w, b):
    tn = 1536
    return pl.pallas_call(
        _ada_kernel,
        out_shape=jax.ShapeDtypeStruct((8, 6 * D), F32),
        grid=(6 * D // tn,),
        in_specs=[pl.BlockSpec((8, D), lambda j: (0, 0)),
                  pl.BlockSpec((D, tn), lambda j: (0, j)),
                  pl.BlockSpec((1, tn), lambda j: (0, j))],
        out_specs=pl.BlockSpec((8, tn), lambda j: (0, j)),
        compiler_params=_params("arbitrary"),
        name="ada",
    )(c_all, w, b.reshape(1, 6 * D))


def _rope_chunk(t, cos, sin, half):
    lane = _lane(t.shape)
    rot = jnp.where((lane % (2 * half)) < half,
                    pltpu.roll(t, LANES - half, 1), pltpu.roll(t, half, 1))
    return t * cos + rot * sin


def _head_rms(t, gain, bd):
    t2 = t * t
    hi = t2.astype(BF16)
    lo = (t2 - hi.astype(F32)).astype(BF16)
    ss = (jnp.dot(hi, bd, preferred_element_type=F32)
          + jnp.dot(lo, bd, preferred_element_type=F32))
    return t * lax.rsqrt(ss * (1.0 / HEAD_DIM) + EPS) * gain


def _with_ones(v, lane, parity):
    if parity == 0:
        return jnp.where(lane < HALF, v, jnp.where(lane == LANES - 1, 1.0, 0.0))
    return jnp.where(lane >= HALF, v, jnp.where(lane == 0, 1.0, 0.0))


def _half_only(k, lane, parity):
    return jnp.where(lane < HALF, k, 0.0) if parity == 0 else jnp.where(lane >= HALF, k, 0.0)


def _qkv_kernel(x_ref, mod_ref, g_ref, w_ref, cos_ref, sin_ref, qg_ref, kg_ref, bd_ref,
                q_ref, k_ref, v_ref, *, shared_kv, head_norm, rope):
    h = _rms(x_ref[...], g_ref[...]) * (1.0 + mod_ref[:, SC_M]) + mod_ref[:, SH_M]
    acc = jnp.dot(h.astype(BF16), w_ref[...], preferred_element_type=F32)
    lane = _lane((TM, LANES))
    n_q = N_HEADS * HEAD_DIM // LANES
    n_kv = (N_KV_HEADS if shared_kv else N_HEADS) * HEAD_DIM // LANES

    def finish(t, gain_ref):
        if head_norm:
            t = _head_rms(t, gain_ref[...], bd_ref[...])
        if rope:
            t = _rope_chunk(t, cos_ref[...], sin_ref[...], HEAD_DIM // 2)
        return t

    for c in range(n_q):
        t = finish(acc[:, c * LANES:(c + 1) * LANES], qg_ref)
        q_ref[:, c * LANES:(c + 1) * LANES] = (t * HEAD_DIM ** -0.5).astype(BF16)
    for c in range(n_kv):
        k = finish(acc[:, (n_q + c) * LANES:(n_q + c + 1) * LANES], kg_ref)
        v = acc[:, (n_q + n_kv + c) * LANES:(n_q + n_kv + c + 1) * LANES]
        if shared_kv:
            k_sw = pltpu.roll(k, HALF, 1)
            v_sw = pltpu.roll(v, HALF, 1)
            for half in range(2):
                j = 2 * c + half
                for parity in range(2):
                    ksrc = k if parity == half else k_sw
                    vsrc = v if parity == half else v_sw
                    k_ref[j, parity] = _half_only(ksrc, lane, parity).astype(BF16)
                    v_ref[j, parity] = _with_ones(vsrc, lane, parity).astype(BF16)
        else:
            for parity in range(2):
                k_ref[c, parity] = _half_only(k, lane, parity).astype(BF16)
                v_ref[c, parity] = _with_ones(v, lane, parity).astype(BF16)


def _qkv(x, mod, gain, w, cos, sin, qg, kg, bd, *, shared_kv, head_norm, rope):
    n_out = w.shape[1]
    groups = N_KV_HEADS if shared_kv else N_HEADS // 2
    row = lambda i: (i, 0)
    const = lambda i: (0, 0)
    kv_spec = pl.BlockSpec((groups, 2, TM, LANES), lambda i: (0, 0, i, 0))
    kv_shape = jax.ShapeDtypeStruct((groups, 2, N_TOK, LANES), BF16)
    return pl.pallas_call(
        functools.partial(_qkv_kernel, shared_kv=shared_kv, head_norm=head_norm, rope=rope),
        out_shape=(jax.ShapeDtypeStruct((N_TOK, N_HEADS * HEAD_DIM), BF16), kv_shape, kv_shape),
        grid=(N_TOK // TM,),
        in_specs=[pl.BlockSpec((TM, D), row),
                  pl.BlockSpec((None, 1, 6 * D), lambda i: (_mod_row(i), 0, 0)),
                  pl.BlockSpec((1, D), const),
                  pl.BlockSpec((D, n_out), const),
                  pl.BlockSpec((TM, LANES), row),
                  pl.BlockSpec((TM, LANES), row),
                  pl.BlockSpec((1, LANES), const),
                  pl.BlockSpec((1, LANES), const),
                  pl.BlockSpec((LANES, LANES), const)],
        out_specs=(pl.BlockSpec((TM, N_HEADS * HEAD_DIM), row), kv_spec, kv_spec),
        compiler_params=_params("arbitrary"),
        name="qkv",
    )(x, mod, gain, w, cos, sin, qg, kg, bd)


def _mla_kernel(x_ref, mod_ref, g_ref, wd_ref, qan_ref, kvan_ref, wq_ref, wkv_ref,
                cos_ref, sin_ref, q_ref, k_ref, v_ref):
    h = _rms(x_ref[...], g_ref[...]) * (1.0 + mod_ref[:, SC_M]) + mod_ref[:, SH_M]
    lat = jnp.dot(h.astype(BF16), wd_ref[...], preferred_element_type=F32)
    c_q = _rms(lat[:, :MLA_Q_LORA], qan_ref[...])
    c_kv = _rms(lat[:, MLA_Q_LORA:MLA_Q_LORA + MLA_KV_LORA], kvan_ref[...])
    cos = cos_ref[...]
    sin = sin_ref[...]
    k_rope = _rope_chunk(lat[:, MLA_Q_LORA + MLA_KV_LORA:], cos, sin, MLA_ROPE // 2)
    q = jnp.dot(c_q.astype(BF16), wq_ref[...], preferred_element_type=F32)
    kv = jnp.dot(c_kv.astype(BF16), wkv_ref[...], preferred_element_type=F32)
    scale = (MLA_NOPE + MLA_ROPE) ** -0.5
    lane = _lane((TM, LANES))
    for hd in range(N_HEADS):
        sl = slice(hd * LANES, (hd + 1) * LANES)
        q_ref[:, sl] = (_rope_chunk(q[:, sl], cos, sin, MLA_ROPE // 2) * scale).astype(BF16)
        k_ref[hd // 2, hd % 2] = (kv[:, sl] + k_rope).astype(BF16)
        v = kv[:, N_HEADS * LANES + hd * LANES:N_HEADS * LANES + (hd + 1) * LANES]
        v_ref[hd // 2, hd % 2] = _with_ones(v, lane, hd % 2).astype(BF16)


def _mla_proj(x, mod, gain, wd, qan, kvan, wq, wkv, cos, sin):
    row = lambda i: (i, 0)
    const = lambda i: (0, 0)
    kv_spec = pl.BlockSpec((N_HEADS // 2, 2, TM, LANES), lambda i: (0, 0, i, 0))
    kv_shape = jax.ShapeDtypeStruct((N_HEADS // 2, 2, N_TOK, LANES), BF16)
    return pl.pallas_call(
        _mla_kernel,
        out_shape=(jax.ShapeDtypeStruct((N_TOK, N_HEADS * LANES), BF16), kv_shape, kv_shape),
        grid=(N_TOK // TM,),
        in_specs=[pl.BlockSpec((TM, D), row),
                  pl.BlockSpec((None, 1, 6 * D), lambda i: (_mod_row(i), 0, 0)),
                  pl.BlockSpec((1, D), const),
                  pl.BlockSpec(wd.shape, const),
                  pl.BlockSpec((1, MLA_Q_LORA), const),
                  pl.BlockSpec((1, MLA_KV_LORA), const),
                  pl.BlockSpec(wq.shape, const),
                  pl.BlockSpec(wkv.shape, const),
                  pl.BlockSpec((TM, LANES), row),
                  pl.BlockSpec((TM, LANES), row)],
        out_specs=(pl.BlockSpec((TM, N_HEADS * LANES), row), kv_spec, kv_spec),
        compiler_params=_params("arbitrary"),
        name="mla_proj",
    )(x, mod, gain, wd, qan, kvan, wq, wkv, cos, sin)


def _attn_kernel(*refs, stacks, tq, mode, with_lat):
    refs = list(refs)
    sink_ref = refs.pop(0) if mode == "window" else None
    q_ref, kc_ref, vc_ref = refs[:3]
    refs = refs[3:]
    if with_lat:
        kl_ref, vl_ref = refs[:2]
        refs = refs[2:]
    tab_ref = refs.pop(0) if mode == "neighbourhood" else None
    o_ref, s_ref = refs
    g = pl.program_id(1)
    t = pl.program_id(2)

    chunks = [("ctx", 0, CTX)]
    mask = None
    if with_lat and mode == "global":
        chunks += [("lat", c * TK, TK) for c in range(S // TK)]
    elif with_lat and mode == "window":
        span = tq + 2 * WINDOW
        q0 = t * tq
        start = pl.multiple_of(jnp.clip(q0 - WINDOW, 0, S - span), WINDOW)
        m_rows = tq * len(stacks[0][0])
        qi = q0 + lax.broadcasted_iota(jnp.int32, (m_rows, span), 0) % tq
        ki = start + lax.broadcasted_iota(jnp.int32, (m_rows, span), 1)
        mask = jnp.abs(qi - ki) <= WINDOW
        chunks += [("lat", start, span)]
    elif with_lat:
        start = pl.multiple_of(_na_band_start(t) * GRID_W, GRID_W)
        chunks += [("lat", start, NA_BAND * GRID_W)]

    outs = {}
    for qcs, kv in stacks:
        q = jnp.concatenate([q_ref[:, qc * LANES:(qc + 1) * LANES] for qc, _ in qcs], axis=0)
        rows = q.shape[0]
        sink_col = None
        if mode == "window":
            hd = [g * 2 * len(qcs) + 2 * qc + kv for qc, _ in qcs]
            row = lax.broadcasted_iota(jnp.int32, (rows, 1), 0)
            sink_col = jnp.where(row < tq, sink_ref[hd[0]], sink_ref[hd[-1]])

        col = 0
        m_fold, m_col = None, sink_col
        for kind, st, width in chunks:
            k = kc_ref[kv] if kind == "ctx" else kl_ref[kv, pl.ds(st, width), :]
            s = lax.dot_general(q, k, (((1,), (1,)), ((), ())), preferred_element_type=F32)
            if kind == "lat" and tab_ref is not None:
                s = s + tab_ref[kv]
            if kind == "lat" and mask is not None:
                s = jnp.where(mask, s, NEG)
            s_ref[0:rows, col:col + width] = s
            if width % LANES == 0:
                for j in range(width // LANES):
                    slab = s[:, j * LANES:(j + 1) * LANES]
                    m_fold = slab if m_fold is None else jnp.maximum(m_fold, slab)
            else:
                mc = jnp.max(s, axis=-1, keepdims=True)
                m_col = mc if m_col is None else jnp.maximum(m_col, mc)
            col += width
        m = jnp.max(m_fold, axis=-1, keepdims=True)
        if m_col is not None:
            m = jnp.maximum(m, m_col)

        col = 0
        acc = jnp.zeros((rows, LANES), F32)
        for kind, st, width in chunks:
            v = vc_ref[kv] if kind == "ctx" else vl_ref[kv, pl.ds(st, width), :]
            p = jnp.exp(s_ref[0:rows, col:col + width] - m).astype(BF16)
            acc = acc + jnp.dot(p, v, preferred_element_type=F32)
            col += width
        den = acc[:, LANES - 1:LANES] if kv == 0 else acc[:, 0:1]
        if sink_col is not None:
            den = den + jnp.exp(sink_col - m)
        lane = _lane(acc.shape)
        o = jnp.where(lane < HALF if kv == 0 else lane >= HALF, acc / den, 0.0)
        for i, (_, oc) in enumerate(qcs):
            part = o[i * tq:(i + 1) * tq]
            outs[oc] = part if oc not in outs else outs[oc] + part
    for oc, o in outs.items():
        o_ref[:, oc * LANES:(oc + 1) * LANES] = o.astype(o_ref.dtype)


def _attention(q, k4, v4, *, stacks, mode, tq, sink=None, table=None, ctx_queries=False):
    groups = k4.shape[0]
    qw = q.shape[1] // groups
    ow = N_HEADS * HEAD_DIM // groups
    with_lat = not ctx_queries
    nq = S // tq if with_lat else CTX // tq
    row0 = 0 if with_lat else N_LAT // tq

    q_spec = pl.BlockSpec((tq, qw), lambda b, g, t: (row0 + b * nq + t, g))
    ctx_spec = pl.BlockSpec((None, 2, CTX, LANES), lambda b, g, t: (g, 0, N_LAT // CTX + b, 0))
    lat_spec = pl.BlockSpec((None, 2, S, LANES), lambda b, g, t: (g, 0, b, 0))
    in_specs = [q_spec, ctx_spec, ctx_spec]
    args = [q, k4, v4]
    n_keys = CTX
    if with_lat:
        in_specs += [lat_spec, lat_spec]
        args += [k4, v4]
        n_keys += {"global": S, "window": tq + 2 * WINDOW, "neighbourhood": NA_BAND * GRID_W}[mode]
    if mode == "window":
        in_specs = [pl.BlockSpec(memory_space=pltpu.SMEM)] + in_specs
        args = [sink] + args
    if mode == "neighbourhood":
        in_specs.append(pl.BlockSpec((None, 2, tq, NA_BAND * GRID_W),
                                     lambda b, g, t: (_na_variant(t), g, 0, 0)))
        args.append(table)
    m_rows = tq * max(len(qcs) for qcs, _ in stacks)
    return pl.pallas_call(
        functools.partial(_attn_kernel, stacks=stacks, tq=tq, mode=mode, with_lat=with_lat),
        out_shape=jax.ShapeDtypeStruct((N_LAT if with_lat else N_CTX, N_HEADS * HEAD_DIM), BF16),
        grid=(B, groups, nq),
        in_specs=in_specs,
        out_specs=pl.BlockSpec((tq, ow), lambda b, g, t: (b * nq + t, g)),
        scratch_shapes=[pltpu.VMEM((m_rows, n_keys), F32)],
        compiler_params=_params("arbitrary", "arbitrary", "arbitrary"),
        name="attn_" + mode + ("" if with_lat else "_ctx"),
    )(*args)


def _na_band_start(t):
    return jnp.clip(t * NA_QROWS - NA_ROWS // 2, 0, S // GRID_W - NA_BAND)


def _na_variant(t):
    last = S // GRID_W // NA_QROWS - 1
    return jnp.where(t < 2, t, jnp.where(t > last - 2, t - (last - 4), 2))


def _bias_cols_kernel(rpb_ref, sel_ref, o_ref):
    o_ref[...] = jnp.dot(rpb_ref[...], sel_ref[...], preferred_element_type=F32,
                         precision=lax.Precision.HIGHEST)


def _na_table(rpb):
    rows = S // GRID_W
    n_dr, n_dc = 2 * NA_ROWS - 1, 2 * NA_COLS - 1
    c = np.arange(GRID_W)[:, None]
    kc = np.arange(GRID_W)[None, :]
    cs = np.clip(c - NA_COLS // 2, 0, GRID_W - NA_COLS)
    col_ok = (kc >= cs) & (kc < cs + NA_COLS)
    sel = (np.arange(LANES)[:, None, None] == (kc - c + NA_COLS - 1)[None]) & col_ok[None]
    sel = jnp.asarray(sel.reshape(LANES, GRID_W * GRID_W), F32)
    rpb2 = jnp.pad(rpb.reshape(N_HEADS * n_dr, n_dc), ((0, 256 - N_HEADS * n_dr), (0, LANES - n_dc)))
    cols = pl.pallas_call(
        _bias_cols_kernel,
        out_shape=jax.ShapeDtypeStruct((256, GRID_W * GRID_W), F32),
        name="na_bias_cols",
    )(rpb2, sel)
    cols = cols[:N_HEADS * n_dr].reshape(N_HEADS, n_dr, GRID_W, GRID_W)
    cols = jnp.where(col_ok[None, None], cols, NEG)
    masked = jnp.full((N_HEADS, GRID_W, GRID_W), NEG, F32)
    variants = []
    for r0, bs in ((0, 0), (2, 0), (4, 0), (rows - 4, rows - NA_BAND), (rows - 2, rows - NA_BAND)):
        q_rows = []
        for r in range(r0, r0 + NA_QROWS):
            rs = min(max(r - NA_ROWS // 2, 0), rows - NA_ROWS)
            q_rows.append(jnp.concatenate(
                [cols[:, kr - r + NA_ROWS - 1] if rs <= kr < rs + NA_ROWS else masked
                 for kr in range(bs, bs + NA_BAND)], axis=2))
        variants.append(jnp.concatenate(q_rows, axis=1))
    return jnp.stack(variants)


def _router_route(h, wr_ref):
    logits = jnp.dot(h, wr_ref[...], preferred_element_type=F32, precision=lax.Precision.HIGHEST)
    lane = _lane(logits.shape).astype(F32)
    lg = jnp.where(lane < N_EXPERTS, logits, -jnp.inf)
    m1 = jnp.max(lg, axis=-1, keepdims=True)
    i1 = jnp.min(jnp.where(lg == m1, lane, float(LANES)), axis=-1, keepdims=True)
    lg2 = jnp.where(lane == i1, -jnp.inf, lg)
    m2 = jnp.max(lg2, axis=-1, keepdims=True)
    i2 = jnp.min(jnp.where(lg2 == m2, lane, float(LANES)), axis=-1, keepdims=True)
    e = jnp.exp(m2 - m1)
    return jnp.where(lane == 0, i1, jnp.where(lane == 1, i2, jnp.where(
        lane == 2, 1.0 / (1.0 + e), jnp.where(lane == 3, e / (1.0 + e), 0.0))))


def _oproj_kernel(*refs, moe, with_ctx):
    refs = list(refs)
    o_ref = refs.pop(0)
    oc_ref = refs.pop(0) if with_ctx else None
    if moe:
        wo_ref, x_ref, mod_ref, post_ref, pre_ref, wr_ref, xo_ref, h_ref, route_ref = refs
    else:
        wo_ref, x_ref, mod_ref, post_ref, pre_ref, xo_ref, h_ref = refs
    o = o_ref[...]
    if with_ctx:
        o = jnp.where(pl.program_id(0) < N_LAT // TM, o, oc_ref[...])
    y = jnp.dot(o, wo_ref[...], preferred_element_type=F32)
    x1 = x_ref[...] + mod_ref[:, G_M] * _rms(y, post_ref[...])
    xo_ref[...] = x1
    h = _rms(x1, pre_ref[...]) * (1.0 + mod_ref[:, SC_F]) + mod_ref[:, SH_F]
    h_ref[...] = h.astype(h_ref.dtype)
    if moe:
        route_ref[...] = _router_route(h, wr_ref)


def _oproj(o, o_ctx, wo, x, mod, post, pre, w_router):
    moe = w_router is not None
    with_ctx = o_ctx is not None
    n_rows = N_TOK if with_ctx else N_LAT
    n_lat = N_LAT // TM
    row = lambda i: (i, 0)
    const = lambda i: (0, 0)
    in_specs = [pl.BlockSpec((TM, D), lambda i: (jnp.minimum(i, n_lat - 1), 0))]
    args = [o]
    if with_ctx:
        in_specs.append(pl.BlockSpec((TM, D), lambda i: (jnp.maximum(i - n_lat, 0), 0)))
        args.append(o_ctx)
    in_specs += [pl.BlockSpec((D, D), const), pl.BlockSpec((TM, D), row),
                 pl.BlockSpec((None, 1, 6 * D), lambda i: (_mod_row(i), 0, 0)),
                 pl.BlockSpec((1, D), const), pl.BlockSpec((1, D), const)]
    args += [wo, x, mod, post, pre]
    out_shape = [jax.ShapeDtypeStruct((n_rows, D), F32),
                 jax.ShapeDtypeStruct((n_rows, D), F32 if moe else BF16)]
    out_specs = [pl.BlockSpec((TM, D), row), pl.BlockSpec((TM, D), row)]
    if moe:
        in_specs.append(pl.BlockSpec((D, LANES), const))
        out_shape.append(jax.ShapeDtypeStruct((n_rows, LANES), F32))
        out_specs.append(pl.BlockSpec((TM, LANES), row))
        args.append(w_router)
    return pl.pallas_call(
        functools.partial(_oproj_kernel, moe=moe, with_ctx=with_ctx),
        out_shape=tuple(out_shape),
        grid=(n_rows // TM,),
        in_specs=in_specs,
        out_specs=tuple(out_specs),
        compiler_params=_params("arbitrary"),
        name="oproj",
    )(*args)


def _swiglu(h, gw_ref, uw_ref, dw_ref):
    gate = jnp.dot(h, gw_ref[...], preferred_element_type=F32)
    up = jnp.dot(h, uw_ref[...], preferred_element_type=F32)
    return jnp.dot((jax.nn.silu(gate) * up).astype(BF16), dw_ref[...], preferred_element_type=F32)


def _ffn_kernel(h_ref, gw_ref, uw_ref, dw_ref, x_ref, mod_ref, post_ref, o_ref, acc_ref):
    j = pl.program_id(1)

    @pl.when(j == 0)
    def _():
        acc_ref[...] = jnp.zeros_like(acc_ref)

    acc_ref[...] += _swiglu(h_ref[...], gw_ref, uw_ref, dw_ref)

    @pl.when(j == pl.num_programs(1) - 1)
    def _():
        o_ref[...] = x_ref[...] + mod_ref[:, G_F] * _rms(acc_ref[...], post_ref[...])


def _ffn(h, w_gu, w_down, x, mod, post):
    row = lambda i, j: (i, 0)
    n_steps = D_FF // TF
    return pl.pallas_call(
        _ffn_kernel,
        out_shape=jax.ShapeDtypeStruct((N_TOK, D), F32),
        grid=(N_TOK // TM, n_steps),
        in_specs=[pl.BlockSpec((TM, D), row),
                  pl.BlockSpec((D, TF), lambda i, j: (0, j)),
                  pl.BlockSpec((D, TF), lambda i, j: (0, n_steps + j)),
                  pl.BlockSpec((TF, D), lambda i, j: (j, 0)),
                  pl.BlockSpec((TM, D), row),
                  pl.BlockSpec((None, 1, 6 * D), lambda i, j: (_mod_row(i), 0, 0)),
                  pl.BlockSpec((1, D), lambda i, j: (0, 0))],
        out_specs=pl.BlockSpec((TM, D), row),
        scratch_shapes=[pltpu.VMEM((TM, D), F32)],
        compiler_params=_params("arbitrary", "arbitrary"),
        name="ffn_dense",
    )(h, w_gu, w_gu, w_down, x, mod, post)


def _routing(route, n_rows):
    i1 = route[:, 0].astype(jnp.int32)
    i2 = route[:, 1].astype(jnp.int32)
    e = jnp.arange(N_EXPERTS, dtype=jnp.int32)
    hit1 = i1[:, None] == e
    hit2 = i2[:, None] == e
    sel = hit1.astype(jnp.int32) + hit2.astype(jnp.int32)
    cum = jnp.cumsum(sel, axis=0)
    tiles = (cum[-1] + TME - 1) // TME
    tile_end = jnp.cumsum(tiles)
    slot = ((tile_end - tiles) * TME)[None] + cum - sel
    slots = jnp.stack([jnp.sum(jnp.where(hit1, slot, 0), axis=1),
                       jnp.sum(jnp.where(hit2, slot, 0), axis=1)], axis=1).reshape(-1)
    j = jnp.arange(_n_slot_tiles(n_rows), dtype=jnp.int32)
    tile_expert = jnp.minimum(jnp.sum(j[:, None] >= tile_end[None], axis=1), N_EXPERTS - 1)
    return slots.astype(jnp.int32), tile_expert.astype(jnp.int32), tile_end[-1:].astype(jnp.int32)


def _n_slot_tiles(n_rows):
    return 2 * n_rows // TME + N_EXPERTS


def _row_copies(slots_ref, n, copy):
    def start(r, carry):
        for k in range(2):
            copy(r, k, slots_ref[2 * r + k]).start()
        return carry

    def wait(r, carry):
        for k in range(2):
            copy(r, k, slots_ref[2 * r + k]).wait()
        return carry

    lax.fori_loop(0, n, start, 0, unroll=8)
    lax.fori_loop(0, n, wait, 0, unroll=8)


def _dispatch_kernel(slots_ref, h_ref, init_ref, xs_ref, sem):
    del init_ref
    _row_copies(slots_ref, TMR, lambda r, k, s: pltpu.make_async_copy(
        h_ref.at[pl.ds(r, 1)], xs_ref.at[pl.ds(s, 1)], sem))


def _dispatch(slots, h, n_rows):
    n_slots = _n_slot_tiles(n_rows) * TME
    return pl.pallas_call(
        _dispatch_kernel,
        out_shape=jax.ShapeDtypeStruct((n_slots, D), F32),
        grid=(n_rows // TMR,),
        in_specs=[pl.BlockSpec((2 * TMR,), lambda i: (i,), memory_space=pltpu.SMEM),
                  pl.BlockSpec((TMR, D), lambda i: (i, 0)),
                  pl.BlockSpec(memory_space=pl.ANY)],
        out_specs=pl.BlockSpec(memory_space=pl.ANY),
        scratch_shapes=[pltpu.SemaphoreType.DMA(())],
        input_output_aliases={2: 0},
        compiler_params=_params("arbitrary"),
        name="moe_dispatch",
    )(slots, h, jnp.zeros((n_slots, D), F32))


def _expert_kernel(te_ref, nv_ref, x_ref, gw_ref, uw_ref, dw_ref, o_ref):
    del te_ref
    valid = pl.program_id(0) < nv_ref[0]

    @pl.when(valid)
    def _():
        o_ref[...] = _swiglu(x_ref[...].astype(BF16), gw_ref, uw_ref, dw_ref)

    @pl.when(jnp.logical_not(valid))
    def _():
        o_ref[...] = jnp.zeros_like(o_ref)


def _experts(tile_expert, n_valid, xs, w_gu, w_down):
    n_tiles = xs.shape[0] // TME
    return pl.pallas_call(
        _expert_kernel,
        out_shape=jax.ShapeDtypeStruct(xs.shape, F32),
        grid_spec=pltpu.PrefetchScalarGridSpec(
            num_scalar_prefetch=2,
            grid=(n_tiles,),
            in_specs=[pl.BlockSpec((TME, D), lambda j, te, nv: (j, 0)),
                      pl.BlockSpec((None, D, TF), lambda j, te, nv: (te[j], 0, 0)),
                      pl.BlockSpec((None, D, TF), lambda j, te, nv: (te[j], 0, 1)),
                      pl.BlockSpec((None, TF, D), lambda j, te, nv: (te[j], 0, 0))],
            out_specs=pl.BlockSpec((TME, D), lambda j, te, nv: (j, 0))),
        compiler_params=_params("arbitrary"),
        name="moe_experts",
    )(tile_expert, n_valid, xs, w_gu, w_gu, w_down)


def _combine_kernel(slots_ref, route_ref, x_ref, mod_ref, post_ref, ys_ref, o_ref, buf_ref, sem):
    _row_copies(slots_ref, TMR, lambda r, k, s: pltpu.make_async_copy(
        ys_ref.at[pl.ds(s, 1)], buf_ref.at[k, pl.ds(r, 1)], sem))
    y = route_ref[:, 2:3] * buf_ref[0] + route_ref[:, 3:4] * buf_ref[1]
    o_ref[...] = x_ref[...] + mod_ref[:, G_F] * _rms(y, post_ref[...])


def _combine(slots, route, x, mod, post, ys, n_rows):
    return pl.pallas_call(
        _combine_kernel,
        out_shape=jax.ShapeDtypeStruct((n_rows, D), F32),
        grid=(n_rows // TMR,),
        in_specs=[pl.BlockSpec((2 * TMR,), lambda i: (i,), memory_space=pltpu.SMEM),
                  pl.BlockSpec((TMR, LANES), lambda i: (i, 0)),
                  pl.BlockSpec((TMR, D), lambda i: (i, 0)),
                  pl.BlockSpec((None, 1, 6 * D), lambda i: (_mod_row(i * TMR // TM), 0, 0)),
                  pl.BlockSpec((1, D), lambda i: (0, 0)),
                  pl.BlockSpec(memory_space=pl.ANY)],
        out_specs=pl.BlockSpec((TMR, D), lambda i: (i, 0)),
        scratch_shapes=[pltpu.VMEM((2, TMR, D), F32), pltpu.SemaphoreType.DMA(())],
        compiler_params=_params("arbitrary"),
        name="moe_combine",
    )(slots, route, x, mod, post, ys)


def _moe(h, route, w_gu, w_down, x, mod, post, n_rows):
    slots, tile_expert, n_valid = _routing(route, n_rows)
    xs = _dispatch(slots, h, n_rows)
    ys = _experts(tile_expert, n_valid, xs, w_gu, w_down)
    return _combine(slots, route, x, mod, post, ys, n_rows)


def _rope_tables(rot_dim, lead, tail):
    t = jnp.arange(S, dtype=jnp.int32)
    row = (t // GRID_W).astype(F32)
    col = (t % GRID_W).astype(F32)
    quarter = rot_dim // 4
    inv = ROPE_THETA ** (-jnp.arange(quarter, dtype=F32) / quarter)
    ang = jnp.concatenate([row[:, None] * inv, col[:, None] * inv], axis=-1)
    cos, sin = jnp.cos(ang), jnp.sin(ang)
    reps = (LANES - lead - tail) // rot_dim
    cos_l = jnp.concatenate([jnp.ones((S, lead), F32)] + [cos, cos] * reps + [jnp.ones((S, tail), F32)], axis=1)
    sin_l = jnp.concatenate([jnp.zeros((S, lead), F32)] + [-sin, sin] * reps + [jnp.zeros((S, tail), F32)], axis=1)
    cos_all = jnp.concatenate([jnp.tile(cos_l, (B, 1)), jnp.ones((N_CTX, LANES), F32)], axis=0)
    sin_all = jnp.concatenate([jnp.tile(sin_l, (B, 1)), jnp.zeros((N_CTX, LANES), F32)], axis=0)
    return cos_all, sin_all


def _mla_weights(w_dkv, w_q_b, w_kv_b):
    lat = MLA_Q_LORA + MLA_KV_LORA
    pad = LANES - MLA_NOPE - MLA_ROPE
    wd = jnp.concatenate([w_dkv[:, :lat], jnp.zeros((D, MLA_NOPE), F32), w_dkv[:, lat:],
                          jnp.zeros((D, pad), F32)], axis=1)
    wq = jnp.pad(w_q_b.reshape(MLA_Q_LORA, N_HEADS, MLA_NOPE + MLA_ROPE), ((0, 0), (0, 0), (0, pad)))
    kvr = w_kv_b.reshape(MLA_KV_LORA, N_HEADS, 2 * HALF)
    k_part, v_part = kvr[..., :HALF], kvr[..., HALF:]
    zero = jnp.zeros_like(v_part)
    wk = jnp.concatenate([k_part, zero], axis=-1)
    even = (jnp.arange(N_HEADS) % 2 == 0)[None, :, None]
    wv = jnp.where(even, jnp.concatenate([v_part, zero], -1), jnp.concatenate([zero, v_part], -1))
    wkv = jnp.concatenate([wk.reshape(MLA_KV_LORA, -1), wv.reshape(MLA_KV_LORA, -1)], axis=1)
    return wd.astype(BF16), wq.reshape(MLA_Q_LORA, -1).astype(BF16), wkv.astype(BF16)


GQA_STACKS = ((((0, 0), (1, 1)), 0), (((0, 0), (1, 1)), 1))
MLA_STACKS = ((((0, 0),), 0), (((1, 0),), 1))
NA_STACKS = ((((0, 0),), 0), (((0, 0),), 1))


def kernel(x, c, ctx, c_ctx,
           l0_ada_w, l0_ada_b, l0_mix_pre, l0_mix_post, l0_ffn_pre, l0_ffn_post,
           l0_w_qkv, l0_q_norm, l0_k_norm, l0_w_o, l0_ffn_w_gu, l0_ffn_w_down,
           l1_ada_w, l1_ada_b, l1_mix_pre, l1_mix_post, l1_ffn_pre, l1_ffn_post,
           l1_w_qkv, l1_sink, l1_w_o, l1_router, l1_moe_w_gu, l1_moe_w_down,
           l2_ada_w, l2_ada_b, l2_mix_pre, l2_mix_post, l2_ffn_pre, l2_ffn_post,
           l2_w_dkv, l2_q_a_norm, l2_w_q_b, l2_kv_a_norm, l2_w_kv_b, l2_w_o, l2_ffn_w_gu, l2_ffn_w_down,
           l3_ada_w, l3_ada_b, l3_mix_pre, l3_mix_post, l3_ffn_pre, l3_ffn_post,
           l3_w_qkv, l3_rpb, l3_w_o, l3_router, l3_moe_w_gu, l3_moe_w_down):
    vec = lambda a: a.reshape(1, -1)
    tile2 = lambda a: jnp.tile(a, 2).reshape(1, LANES)
    bf = lambda a: a.astype(BF16)
    pad_router = lambda w: jnp.pad(w, ((0, 0), (0, LANES - N_EXPERTS)))

    xs = jnp.concatenate([x.reshape(N_LAT, D), ctx.reshape(N_CTX, D)], axis=0)
    c_all = jnp.concatenate([c, c_ctx[None], jnp.zeros((8 - B - 1, D), F32)], axis=0)
    mods = [_ada(c_all, w, b).reshape(8, 1, 6 * D)
            for w, b in ((l0_ada_w, l0_ada_b), (l1_ada_w, l1_ada_b), (l2_ada_w, l2_ada_b), (l3_ada_w, l3_ada_b))]
    cos64, sin64 = _rope_tables(HEAD_DIM, 0, 0)
    cos32, sin32 = _rope_tables(MLA_ROPE, MLA_NOPE, LANES - MLA_NOPE - MLA_ROPE)
    ones = jnp.ones((1, LANES), F32)
    idx = np.arange(LANES)
    bd = jnp.asarray(idx[:, None] // HALF == idx[None, :] // HALF, BF16)

    q, k4, v4 = _qkv(xs, mods[0], vec(l0_mix_pre), bf(l0_w_qkv), cos64, sin64,
                     tile2(l0_q_norm), tile2(l0_k_norm), bd, shared_kv=True, head_norm=True, rope=True)
    o = _attention(q, k4, v4, stacks=GQA_STACKS, mode="global", tq=TQ)
    oc = _attention(q, k4, v4, stacks=GQA_STACKS, mode="global", tq=CTX, ctx_queries=True)
    xs, h = _oproj(o, oc, bf(l0_w_o), xs, mods[0], vec(l0_mix_post), vec(l0_ffn_pre), None)
    xs = _ffn(h, bf(l0_ffn_w_gu), bf(l0_ffn_w_down), xs, mods[0], vec(l0_ffn_post))

    q, k4, v4 = _qkv(xs, mods[1], vec(l1_mix_pre), bf(l1_w_qkv), cos64, sin64,
                     ones, ones, bd, shared_kv=True, head_norm=False, rope=True)
    o = _attention(q, k4, v4, stacks=GQA_STACKS, mode="window", tq=TQ, sink=l1_sink)
    oc = _attention(q, k4, v4, stacks=GQA_STACKS, mode="window", tq=CTX, sink=l1_sink, ctx_queries=True)
    xs, h, route = _oproj(o, oc, bf(l1_w_o), xs, mods[1], vec(l1_mix_post), vec(l1_ffn_pre),
                          pad_router(l1_router))
    xs = _moe(h, route, bf(l1_moe_w_gu), bf(l1_moe_w_down), xs, mods[1], vec(l1_ffn_post), N_TOK)

    wd, wq, wkv = _mla_weights(l2_w_dkv, l2_w_q_b, l2_w_kv_b)
    q, k4, v4 = _mla_proj(xs, mods[2], vec(l2_mix_pre), wd, vec(l2_q_a_norm), vec(l2_kv_a_norm),
                          wq, wkv, cos32, sin32)
    o = _attention(q, k4, v4, stacks=MLA_STACKS, mode="global", tq=2 * TQ)
    oc = _attention(q, k4, v4, stacks=MLA_STACKS, mode="global", tq=CTX, ctx_queries=True)
    xs, h = _oproj(o, oc, bf(l2_w_o), xs, mods[2], vec(l2_mix_post), vec(l2_ffn_pre), None)
    xs = _ffn(h, bf(l2_ffn_w_gu), bf(l2_ffn_w_down), xs, mods[2], vec(l2_ffn_post))

    q, k4, v4 = _qkv(xs, mods[3], vec(l3_mix_pre), bf(l3_w_qkv), cos64, sin64,
                     ones, ones, bd, shared_kv=False, head_norm=False, rope=False)
    o = _attention(q, k4, v4, stacks=NA_STACKS, mode="neighbourhood", tq=NA_QROWS * GRID_W,
                   table=_na_table(l3_rpb))
    xl, h, route = _oproj(o, None, bf(l3_w_o), xs, mods[3], vec(l3_mix_post), vec(l3_ffn_pre),
                          pad_router(l3_router))
    xl = _moe(h, route, bf(l3_moe_w_gu), bf(l3_moe_w_down), xl, mods[3], vec(l3_ffn_post), N_LAT)
    return xl.reshape(B, S, D)
```

```python
import functools

import numpy as np
import jax
import jax.numpy as jnp
from jax import lax
from jax.experimental import pallas as pl
from jax.experimental.pallas import tpu as pltpu

D = 1024
B = 4
S = 4096
CTX = 256
GRID_W = 64
N_LAT = B * S
N_CTX = B * CTX
N_TOK = N_LAT + N_CTX
HEAD_DIM = 64
N_HEADS = 16
N_KV_HEADS = 4
WINDOW = 128
MLA_Q_LORA = 384
MLA_KV_LORA = 256
MLA_NOPE = 64
MLA_ROPE = 32
NA_ROWS = 8
NA_COLS = 16
D_FF = 2816
N_EXPERTS = 8
D_FF_EXPERT = 1408
ROPE_THETA = 10000.0
EPS = 1e-6
NEG = -1e30
LOG2E = 1.4426950408889634

LANES = 128
HALF = LANES // 2
TM = 512
TQ = 256
TK = 512
TF = 1408
TME = 256
TMR = 256
NA_QROWS = 4
NA_BAND = NA_QROWS + NA_ROWS - 1
VMEM_LIMIT = 56 * 1024 * 1024

F32 = jnp.float32
BF16 = jnp.bfloat16

SH_M, SC_M, G_M, SH_F, SC_F, G_F = (slice(i * D, (i + 1) * D) for i in range(6))


def _params(*sem):
    return pltpu.CompilerParams(dimension_semantics=sem, vmem_limit_bytes=VMEM_LIMIT)


def _mod_row(i):
    r0 = i * TM
    return jnp.where(r0 >= N_LAT, B, r0 // S)


def _rms(x, gain):
    return x * lax.rsqrt(jnp.mean(x * x, axis=-1, keepdims=True) + EPS) * gain


def _lane(shape):
    return lax.broadcasted_iota(jnp.int32, shape, len(shape) - 1)


def _ada_kernel(c_ref, w_ref, b_ref, o_ref):
    a = jax.nn.silu(c_ref[...])
    o_ref[...] = jnp.dot(a, w_ref[...], preferred_element_type=F32,
                         precision=lax.Precision.HIGHEST) + b_ref[...]


def _ada(c_all, w, b):
    tn = 1536
    return pl.pallas_call(
        _ada_kernel,
        out_shape=jax.ShapeDtypeStruct((8, 6 * D), F32),
        grid=(6 * D // tn,),
        in_specs=[pl.BlockSpec((8, D), lambda j: (0, 0)),
                  pl.BlockSpec((D, tn), lambda j: (0, j)),
                  pl.BlockSpec((1, tn), lambda j: (0, j))],
        out_specs=pl.BlockSpec((8, tn), lambda j: (0, j)),
        compiler_params=_params("arbitrary"),
        name="ada",
    )(c_all, w, b.reshape(1, 6 * D))


def _rope_chunk(t, cos, sin, half):
    lane = _lane(t.shape)
    rot = jnp.where((lane % (2 * half)) < half,
                    pltpu.roll(t, LANES - half, 1), pltpu.roll(t, half, 1))
    return t * cos + rot * sin


def _head_rms(t, gain, bd):
    t2 = t * t
    hi = t2.astype(BF16)
    lo = (t2 - hi.astype(F32)).astype(BF16)
    ss = (jnp.dot(hi, bd, preferred_element_type=F32)
          + jnp.dot(lo, bd, preferred_element_type=F32))
    return t * lax.rsqrt(ss * (1.0 / HEAD_DIM) + EPS) * gain


def _with_ones(v, lane, parity):
    if parity == 0:
        return jnp.where(lane < HALF, v, jnp.where(lane == LANES - 1, 1.0, 0.0))
    return jnp.where(lane >= HALF, v, jnp.where(lane == 0, 1.0, 0.0))


def _half_only(k, lane, parity):
    return jnp.where(lane < HALF, k, 0.0) if parity == 0 else jnp.where(lane >= HALF, k, 0.0)


def _put_v(v_ref, j, parity, v, v_t):
    v_ref[j, parity] = (v.T if v_t else v).astype(BF16)


def _qkv_kernel(x_ref, mod_ref, g_ref, w_ref, cos_ref, sin_ref, qg_ref, kg_ref, bd_ref,
                q_ref, k_ref, v_ref, *, shared_kv, head_norm, rope, v_t, q_scale):
    h = _rms(x_ref[...], g_ref[...]) * (1.0 + mod_ref[:, SC_M]) + mod_ref[:, SH_M]
    acc = jnp.dot(h.astype(BF16), w_ref[...], preferred_element_type=F32)
    lane = _lane((TM, LANES))
    n_q = N_HEADS * HEAD_DIM // LANES
    n_kv = (N_KV_HEADS if shared_kv else N_HEADS) * HEAD_DIM // LANES

    def finish(t, gain_ref):
        if head_norm:
            t = _head_rms(t, gain_ref[...], bd_ref[...])
        if rope:
            t = _rope_chunk(t, cos_ref[...], sin_ref[...], HEAD_DIM // 2)
        return t

    for c in range(n_q):
        t = finish(acc[:, c * LANES:(c + 1) * LANES], qg_ref)
        q_ref[:, c * LANES:(c + 1) * LANES] = (t * q_scale).astype(BF16)
    for c in range(n_kv):
        k = finish(acc[:, (n_q + c) * LANES:(n_q + c + 1) * LANES], kg_ref)
        v = acc[:, (n_q + n_kv + c) * LANES:(n_q + n_kv + c + 1) * LANES]
        if shared_kv:
            k_sw = pltpu.roll(k, HALF, 1)
            v_sw = pltpu.roll(v, HALF, 1)
            for half in range(2):
                j = 2 * c + half
                for parity in range(2):
                    ksrc = k if parity == half else k_sw
                    vsrc = v if parity == half else v_sw
                    k_ref[j, parity] = _half_only(ksrc, lane, parity).astype(BF16)
                    _put_v(v_ref, j, parity, _with_ones(vsrc, lane, parity), v_t)
        else:
            for parity in range(2):
                k_ref[c, parity] = _half_only(k, lane, parity).astype(BF16)
                _put_v(v_ref, c, parity, _with_ones(v, lane, parity), v_t)


def _kv_out(groups, transposed):
    if transposed:
        return (pl.BlockSpec((groups, 2, LANES, TM), lambda i: (0, 0, 0, i)),
                jax.ShapeDtypeStruct((groups, 2, LANES, N_TOK), BF16))
    return (pl.BlockSpec((groups, 2, TM, LANES), lambda i: (0, 0, i, 0)),
            jax.ShapeDtypeStruct((groups, 2, N_TOK, LANES), BF16))


def _qkv(x, mod, gain, w, cos, sin, qg, kg, bd, *, shared_kv, head_norm, rope, v_t, q_scale):
    n_out = w.shape[1]
    groups = N_KV_HEADS if shared_kv else N_HEADS // 2
    row = lambda i: (i, 0)
    const = lambda i: (0, 0)
    kv_spec, kv_shape = _kv_out(groups, False)
    v_spec, v_shape = _kv_out(groups, v_t)
    return pl.pallas_call(
        functools.partial(_qkv_kernel, shared_kv=shared_kv, head_norm=head_norm, rope=rope,
                          v_t=v_t, q_scale=q_scale),
        out_shape=(jax.ShapeDtypeStruct((N_TOK, N_HEADS * HEAD_DIM), BF16), kv_shape, v_shape),
        grid=(N_TOK // TM,),
        in_specs=[pl.BlockSpec((TM, D), row),
                  pl.BlockSpec((None, 1, 6 * D), lambda i: (_mod_row(i), 0, 0)),
                  pl.BlockSpec((1, D), const),
                  pl.BlockSpec((D, n_out), const),
                  pl.BlockSpec((TM, LANES), row),
                  pl.BlockSpec((TM, LANES), row),
                  pl.BlockSpec((1, LANES), const),
                  pl.BlockSpec((1, LANES), const),
                  pl.BlockSpec((LANES, LANES), const)],
        out_specs=(pl.BlockSpec((TM, N_HEADS * HEAD_DIM), row), kv_spec, v_spec),
        compiler_params=_params("arbitrary"),
        name="qkv",
    )(x, mod, gain, w, cos, sin, qg, kg, bd)


def _mla_kernel(x_ref, mod_ref, g_ref, wd_ref, qan_ref, kvan_ref, wq_ref, wkv_ref,
                cos_ref, sin_ref, q_ref, k_ref, v_ref):
    h = _rms(x_ref[...], g_ref[...]) * (1.0 + mod_ref[:, SC_M]) + mod_ref[:, SH_M]
    lat = jnp.dot(h.astype(BF16), wd_ref[...], preferred_element_type=F32)
    c_q = _rms(lat[:, :MLA_Q_LORA], qan_ref[...])
    c_kv = _rms(lat[:, MLA_Q_LORA:MLA_Q_LORA + MLA_KV_LORA], kvan_ref[...])
    cos = cos_ref[...]
    sin = sin_ref[...]
    k_rope = _rope_chunk(lat[:, MLA_Q_LORA + MLA_KV_LORA:], cos, sin, MLA_ROPE // 2)
    q = jnp.dot(c_q.astype(BF16), wq_ref[...], preferred_element_type=F32)
    kv = jnp.dot(c_kv.astype(BF16), wkv_ref[...], preferred_element_type=F32)
    scale = (MLA_NOPE + MLA_ROPE) ** -0.5 * LOG2E
    lane = _lane((TM, LANES))
    for hd in range(N_HEADS):
        sl = slice(hd * LANES, (hd + 1) * LANES)
        q_ref[:, sl] = (_rope_chunk(q[:, sl], cos, sin, MLA_ROPE // 2) * scale).astype(BF16)
        k_ref[hd // 2, hd % 2] = (kv[:, sl] + k_rope).astype(BF16)
        v = kv[:, N_HEADS * LANES + hd * LANES:N_HEADS * LANES + (hd + 1) * LANES]
        _put_v(v_ref, hd // 2, hd % 2, _with_ones(v, lane, hd % 2), True)


def _mla_proj(x, mod, gain, wd, qan, kvan, wq, wkv, cos, sin):
    row = lambda i: (i, 0)
    const = lambda i: (0, 0)
    kv_spec, kv_shape = _kv_out(N_HEADS // 2, False)
    v_spec, v_shape = _kv_out(N_HEADS // 2, True)
    return pl.pallas_call(
        _mla_kernel,
        out_shape=(jax.ShapeDtypeStruct((N_TOK, N_HEADS * LANES), BF16), kv_shape, v_shape),
        grid=(N_TOK // TM,),
        in_specs=[pl.BlockSpec((TM, D), row),
                  pl.BlockSpec((None, 1, 6 * D), lambda i: (_mod_row(i), 0, 0)),
                  pl.BlockSpec((1, D), const),
                  pl.BlockSpec(wd.shape, const),
                  pl.BlockSpec((1, MLA_Q_LORA), const),
                  pl.BlockSpec((1, MLA_KV_LORA), const),
                  pl.BlockSpec(wq.shape, const),
                  pl.BlockSpec(wkv.shape, const),
                  pl.BlockSpec((TM, LANES), row),
                  pl.BlockSpec((TM, LANES), row)],
        out_specs=(pl.BlockSpec((TM, N_HEADS * LANES), row), kv_spec, v_spec),
        compiler_params=_params("arbitrary"),
        name="mla_proj",
    )(x, mod, gain, wd, qan, kvan, wq, wkv, cos, sin)


def _attn_kernel(*refs, stacks, tq, mode, with_lat):
    refs = list(refs)
    sink_ref = refs.pop(0) if mode == "window" else None
    q_ref, kc_ref, vc_ref = refs[:3]
    refs = refs[3:]
    if with_lat:
        kl_ref, vl_ref = refs[:2]
        refs = refs[2:]
    tab_ref = refs.pop(0) if mode == "neighbourhood" else None
    o_ref, s_ref = refs
    g = pl.program_id(1)
    t = pl.program_id(2)

    chunks = [("ctx", 0, CTX)]
    mask = None
    if with_lat and mode == "window":
        span = tq + 2 * WINDOW
        q0 = t * tq
        start = pl.multiple_of(jnp.clip(q0 - WINDOW, 0, S - span), WINDOW)
        m_rows = tq * len(stacks[0][0])
        qi = q0 + lax.broadcasted_iota(jnp.int32, (m_rows, span), 0) % tq
        ki = start + lax.broadcasted_iota(jnp.int32, (m_rows, span), 1)
        mask = jnp.abs(qi - ki) <= WINDOW
        chunks += [("lat", start, span)]
    elif with_lat:
        start = pl.multiple_of(_na_band_start(t) * GRID_W, GRID_W)
        chunks += [("lat", start, NA_BAND * GRID_W)]

    outs = {}
    for qcs, kv in stacks:
        q = jnp.concatenate([q_ref[:, qc * LANES:(qc + 1) * LANES] for qc, _ in qcs], axis=0)
        rows = q.shape[0]
        sink_col = None
        if mode == "window":
            hd = [g * 2 * len(qcs) + 2 * qc + kv for qc, _ in qcs]
            row = lax.broadcasted_iota(jnp.int32, (rows, 1), 0)
            sink_col = jnp.where(row < tq, sink_ref[hd[0]], sink_ref[hd[-1]])

        col = 0
        m_fold, m_col = None, sink_col
        for kind, st, width in chunks:
            k = kc_ref[kv] if kind == "ctx" else kl_ref[kv, pl.ds(st, width), :]
            s = lax.dot_general(q, k, (((1,), (1,)), ((), ())), preferred_element_type=F32)
            if kind == "lat" and tab_ref is not None:
                s = s + tab_ref[kv]
            if kind == "lat" and mask is not None:
                s = jnp.where(mask, s, NEG)
            s_ref[0:rows, col:col + width] = s
            if width % LANES == 0:
                for j in range(width // LANES):
                    slab = s[:, j * LANES:(j + 1) * LANES]
                    m_fold = slab if m_fold is None else jnp.maximum(m_fold, slab)
            else:
                mc = jnp.max(s, axis=-1, keepdims=True)
                m_col = mc if m_col is None else jnp.maximum(m_col, mc)
            col += width
        m = jnp.max(m_fold, axis=-1, keepdims=True)
        if m_col is not None:
            m = jnp.maximum(m, m_col)

        col = 0
        acc = jnp.zeros((rows, LANES), F32)
        for kind, st, width in chunks:
            v = vc_ref[kv] if kind == "ctx" else vl_ref[kv, pl.ds(st, width), :]
            p = jnp.exp(s_ref[0:rows, col:col + width] - m).astype(BF16)
            acc = acc + jnp.dot(p, v, preferred_element_type=F32)
            col += width
        den = acc[:, LANES - 1:LANES] if kv == 0 else acc[:, 0:1]
        if sink_col is not None:
            den = den + jnp.exp(sink_col - m)
        lane = _lane(acc.shape)
        o = jnp.where(lane < HALF if kv == 0 else lane >= HALF, acc / den, 0.0)
        for i, (_, oc) in enumerate(qcs):
            part = o[i * tq:(i + 1) * tq]
            outs[oc] = part if oc not in outs else outs[oc] + part
    for oc, o in outs.items():
        o_ref[:, oc * LANES:(oc + 1) * LANES] = o.astype(o_ref.dtype)


def _attention(q, k4, v4, *, stacks, mode, tq, sink=None, table=None, ctx_queries=False):
    groups = k4.shape[0]
    qw = q.shape[1] // groups
    ow = N_HEADS * HEAD_DIM // groups
    with_lat = not ctx_queries
    nq = S // tq if with_lat else CTX // tq
    row0 = 0 if with_lat else N_LAT // tq

    q_spec = pl.BlockSpec((tq, qw), lambda b, g, t: (row0 + b * nq + t, g))
    ctx_spec = pl.BlockSpec((None, 2, CTX, LANES), lambda b, g, t: (g, 0, N_LAT // CTX + b, 0))
    lat_spec = pl.BlockSpec((None, 2, S, LANES), lambda b, g, t: (g, 0, b, 0))
    in_specs = [q_spec, ctx_spec, ctx_spec]
    args = [q, k4, v4]
    n_keys = CTX
    if with_lat:
        in_specs += [lat_spec, lat_spec]
        args += [k4, v4]
        n_keys += {"window": tq + 2 * WINDOW, "neighbourhood": NA_BAND * GRID_W}[mode]
    if mode == "window":
        in_specs = [pl.BlockSpec(memory_space=pltpu.SMEM)] + in_specs
        args = [sink] + args
    if mode == "neighbourhood":
        in_specs.append(pl.BlockSpec((None, 2, tq, NA_BAND * GRID_W),
                                     lambda b, g, t: (_na_variant(t), g, 0, 0)))
        args.append(table)
    m_rows = tq * max(len(qcs) for qcs, _ in stacks)
    return pl.pallas_call(
        functools.partial(_attn_kernel, stacks=stacks, tq=tq, mode=mode, with_lat=with_lat),
        out_shape=jax.ShapeDtypeStruct((N_LAT if with_lat else N_CTX, N_HEADS * HEAD_DIM), BF16),
        grid=(B, groups, nq),
        in_specs=in_specs,
        out_specs=pl.BlockSpec((tq, ow), lambda b, g, t: (b * nq + t, g)),
        scratch_shapes=[pltpu.VMEM((m_rows, n_keys), F32)],
        compiler_params=_params("arbitrary", "arbitrary", "arbitrary"),
        name="attn_" + mode + ("" if with_lat else "_ctx"),
    )(*args)


def _attn_t_kernel(*refs, stacks, tq, with_lat):
    refs = list(refs)
    q_ref, kc_ref, vtc_ref = refs[:3]
    refs = refs[3:]
    if with_lat:
        kl_ref, vtl_ref = refs[:2]
        refs = refs[2:]
    o_ref, s_ref = refs
    chunks = [("ctx", 0, CTX)]
    if with_lat:
        chunks += [("lat", c * TK, TK) for c in range(S // TK)]

    outs = {}
    for si, (qcs, kv) in enumerate(stacks):
        q = jnp.concatenate([q_ref[:, qc * LANES:(qc + 1) * LANES] for qc, _ in qcs], axis=0)
        cols = q.shape[0]

        row = 0
        m8 = None
        for kind, st, width in chunks:
            k = kc_ref[kv] if kind == "ctx" else kl_ref[kv, st:st + width, :]
            s = lax.dot_general(k, q, (((1,), (1,)), ((), ())), preferred_element_type=F32)
            s_ref[si, row:row + width, 0:cols] = s
            c8 = jnp.max(s.reshape(width // 8, 8, cols), axis=0)
            m8 = c8 if m8 is None else jnp.maximum(m8, c8)
            row += width
        m = jnp.max(m8, axis=0, keepdims=True)

        row = 0
        acc = jnp.zeros((LANES, cols), F32)
        for kind, st, width in chunks:
            vt = vtc_ref[kv] if kind == "ctx" else vtl_ref[kv, :, st:st + width]
            p = jnp.exp2(s_ref[si, row:row + width, 0:cols] - m).astype(BF16)
            acc = acc + jnp.dot(vt, p, preferred_element_type=F32)
            row += width
        den = acc[LANES - 1:LANES, :] if kv == 0 else acc[0:1, :]
        o = (acc / den).T
        lane = _lane(o.shape)
        o = jnp.where(lane < HALF if kv == 0 else lane >= HALF, o, 0.0)
        for i, (_, oc) in enumerate(qcs):
            part = o[i * tq:(i + 1) * tq]
            outs[oc] = part if oc not in outs else outs[oc] + part
    for oc, o in outs.items():
        o_ref[:, oc * LANES:(oc + 1) * LANES] = o.astype(o_ref.dtype)


def _attention_t(q, k4, vt4, *, stacks, tq, ctx_queries=False):
    groups = k4.shape[0]
    qw = q.shape[1] // groups
    ow = N_HEADS * HEAD_DIM // groups
    with_lat = not ctx_queries
    nq = S // tq if with_lat else CTX // tq
    row0 = 0 if with_lat else N_LAT // tq
    in_specs = [pl.BlockSpec((tq, qw), lambda b, g, t: (row0 + b * nq + t, g)),
                pl.BlockSpec((None, 2, CTX, LANES), lambda b, g, t: (g, 0, N_LAT // CTX + b, 0)),
                pl.BlockSpec((None, 2, LANES, CTX), lambda b, g, t: (g, 0, 0, N_LAT // CTX + b))]
    args = [q, k4, vt4]
    n_keys = CTX
    if with_lat:
        in_specs += [pl.BlockSpec((None, 2, S, LANES), lambda b, g, t: (g, 0, b, 0)),
                     pl.BlockSpec((None, 2, LANES, S), lambda b, g, t: (g, 0, 0, b))]
        args += [k4, vt4]
        n_keys += S
    m_rows = tq * max(len(qcs) for qcs, _ in stacks)
    return pl.pallas_call(
        functools.partial(_attn_t_kernel, stacks=stacks, tq=tq, with_lat=with_lat),
        out_shape=jax.ShapeDtypeStruct((N_LAT if with_lat else N_CTX, N_HEADS * HEAD_DIM), BF16),
        grid=(B, groups, nq),
        in_specs=in_specs,
        out_specs=pl.BlockSpec((tq, ow), lambda b, g, t: (b * nq + t, g)),
        scratch_shapes=[pltpu.VMEM((len(stacks), n_keys, m_rows), F32)],
        compiler_params=_params("arbitrary", "arbitrary", "arbitrary"),
        name="attn_global" + ("" if with_lat else "_ctx"),
    )(*args)


def _na_band_start(t):
    return jnp.clip(t * NA_QROWS - NA_ROWS // 2, 0, S // GRID_W - NA_BAND)


def _na_tiles():
    rows = S // GRID_W
    tiles = [(t * NA_QROWS, min(max(t * NA_QROWS - NA_ROWS // 2, 0), rows - NA_BAND))
             for t in range(rows // NA_QROWS)]
    lo = [rb for rb in tiles if rb[0] - NA_ROWS // 2 < 0]
    hi = [rb for rb in tiles if rb[0] - NA_ROWS // 2 > rows - NA_BAND]
    return lo + [tiles[len(lo)]] + hi, len(lo), len(hi)


def _na_variant(t):
    _, n_lo, n_hi = _na_tiles()
    first_hi = S // GRID_W // NA_QROWS - n_hi
    return jnp.where(t < n_lo, t, jnp.where(t >= first_hi, t - first_hi + n_lo + 1, n_lo))


def _bias_cols_kernel(rpb_ref, sel_ref, o_ref):
    o_ref[...] = jnp.dot(rpb_ref[...], sel_ref[...], preferred_element_type=F32,
                         precision=lax.Precision.HIGHEST)


def _na_table(rpb):
    rows = S // GRID_W
    n_dr, n_dc = 2 * NA_ROWS - 1, 2 * NA_COLS - 1
    c = np.arange(GRID_W)[:, None]
    kc = np.arange(GRID_W)[None, :]
    cs = np.clip(c - NA_COLS // 2, 0, GRID_W - NA_COLS)
    col_ok = (kc >= cs) & (kc < cs + NA_COLS)
    sel = (np.arange(LANES)[:, None, None] == (kc - c + NA_COLS - 1)[None]) & col_ok[None]
    sel = jnp.asarray(sel.reshape(LANES, GRID_W * GRID_W), F32)
    rpb2 = jnp.pad(rpb.reshape(N_HEADS * n_dr, n_dc), ((0, 256 - N_HEADS * n_dr), (0, LANES - n_dc)))
    cols = pl.pallas_call(
        _bias_cols_kernel,
        out_shape=jax.ShapeDtypeStruct((256, GRID_W * GRID_W), F32),
        name="na_bias_cols",
    )(rpb2, sel)
    cols = cols[:N_HEADS * n_dr].reshape(N_HEADS, n_dr, GRID_W, GRID_W)
    cols = jnp.where(col_ok[None, None], cols, NEG)
    masked = jnp.full((N_HEADS, GRID_W, GRID_W), NEG, F32)
    variants = []
    for r0, bs in _na_tiles()[0]:
        q_rows = []
        for r in range(r0, r0 + NA_QROWS):
            rs = min(max(r - NA_ROWS // 2, 0), rows - NA_ROWS)
            q_rows.append(jnp.concatenate(
                [cols[:, kr - r + NA_ROWS - 1] if rs <= kr < rs + NA_ROWS else masked
                 for kr in range(bs, bs + NA_BAND)], axis=2))
        variants.append(jnp.concatenate(q_rows, axis=1))
    return jnp.stack(variants)


def _router_route(h, wr_ref):
    logits = jnp.dot(h, wr_ref[...], preferred_element_type=F32, precision=lax.Precision.HIGHEST)
    lane = _lane(logits.shape).astype(F32)
    lg = jnp.where(lane < N_EXPERTS, logits, -jnp.inf)
    m1 = jnp.max(lg, axis=-1, keepdims=True)
    i1 = jnp.min(jnp.where(lg == m1, lane, float(LANES)), axis=-1, keepdims=True)
    lg2 = jnp.where(lane == i1, -jnp.inf, lg)
    m2 = jnp.max(lg2, axis=-1, keepdims=True)
    i2 = jnp.min(jnp.where(lg2 == m2, lane, float(LANES)), axis=-1, keepdims=True)
    e = jnp.exp(m2 - m1)
    return jnp.where(lane == 0, i1, jnp.where(lane == 1, i2, jnp.where(
        lane == 2, 1.0 / (1.0 + e), jnp.where(lane == 3, e / (1.0 + e), 0.0))))


def _oproj_kernel(*refs, moe, with_ctx):
    refs = list(refs)
    o_ref = refs.pop(0)
    oc_ref = refs.pop(0) if with_ctx else None
    if moe:
        wo_ref, x_ref, mod_ref, post_ref, pre_ref, wr_ref, xo_ref, h_ref, route_ref = refs
    else:
        wo_ref, x_ref, mod_ref, post_ref, pre_ref, xo_ref, h_ref = refs
    o = o_ref[...]
    if with_ctx:
        o = jnp.where(pl.program_id(0) < N_LAT // TM, o, oc_ref[...])
    y = jnp.dot(o, wo_ref[...], preferred_element_type=F32)
    x1 = x_ref[...] + mod_ref[:, G_M] * _rms(y, post_ref[...])
    xo_ref[...] = x1
    h = _rms(x1, pre_ref[...]) * (1.0 + mod_ref[:, SC_F]) + mod_ref[:, SH_F]
    h_ref[...] = h.astype(h_ref.dtype)
    if moe:
        route_ref[...] = _router_route(h, wr_ref)


def _oproj(o, o_ctx, wo, x, mod, post, pre, w_router):
    moe = w_router is not None
    with_ctx = o_ctx is not None
    n_rows = N_TOK if with_ctx else N_LAT
    n_lat = N_LAT // TM
    row = lambda i: (i, 0)
    const = lambda i: (0, 0)
    in_specs = [pl.BlockSpec((TM, D), lambda i: (jnp.minimum(i, n_lat - 1), 0))]
    args = [o]
    if with_ctx:
        in_specs.append(pl.BlockSpec((TM, D), lambda i: (jnp.maximum(i - n_lat, 0), 0)))
        args.append(o_ctx)
    in_specs += [pl.BlockSpec((D, D), const), pl.BlockSpec((TM, D), row),
                 pl.BlockSpec((None, 1, 6 * D), lambda i: (_mod_row(i), 0, 0)),
                 pl.BlockSpec((1, D), const), pl.BlockSpec((1, D), const)]
    args += [wo, x, mod, post, pre]
    out_shape = [jax.ShapeDtypeStruct((n_rows, D), F32),
                 jax.ShapeDtypeStruct((n_rows, D), F32 if moe else BF16)]
    out_specs = [pl.BlockSpec((TM, D), row), pl.BlockSpec((TM, D), row)]
    if moe:
        in_specs.append(pl.BlockSpec((D, LANES), const))
        out_shape.append(jax.ShapeDtypeStruct((n_rows, LANES), F32))
        out_specs.append(pl.BlockSpec((TM, LANES), row))
        args.append(w_router)
    return pl.pallas_call(
        functools.partial(_oproj_kernel, moe=moe, with_ctx=with_ctx),
        out_shape=tuple(out_shape),
        grid=(n_rows // TM,),
        in_specs=in_specs,
        out_specs=tuple(out_specs),
        compiler_params=_params("arbitrary"),
        name="oproj",
    )(*args)


def _swiglu(h, gw_ref, uw_ref, dw_ref):
    gate = jnp.dot(h, gw_ref[...], preferred_element_type=F32)
    up = jnp.dot(h, uw_ref[...], preferred_element_type=F32)
    return jnp.dot((jax.nn.silu(gate) * up).astype(BF16), dw_ref[...], preferred_element_type=F32)


def _ffn_kernel(h_ref, gw_ref, uw_ref, dw_ref, x_ref, mod_ref, post_ref, o_ref, acc_ref):
    j = pl.program_id(1)

    @pl.when(j == 0)
    def _():
        acc_ref[...] = jnp.zeros_like(acc_ref)

    acc_ref[...] += _swiglu(h_ref[...], gw_ref, uw_ref, dw_ref)

    @pl.when(j == pl.num_programs(1) - 1)
    def _():
        o_ref[...] = x_ref[...] + mod_ref[:, G_F] * _rms(acc_ref[...], post_ref[...])


def _ffn(h, w_gu, w_down, x, mod, post):
    row = lambda i, j: (i, 0)
    n_steps = D_FF // TF
    return pl.pallas_call(
        _ffn_kernel,
        out_shape=jax.ShapeDtypeStruct((N_TOK, D), F32),
        grid=(N_TOK // TM, n_steps),
        in_specs=[pl.BlockSpec((TM, D), row),
                  pl.BlockSpec((D, TF), lambda i, j: (0, j)),
                  pl.BlockSpec((D, TF), lambda i, j: (0, n_steps + j)),
                  pl.BlockSpec((TF, D), lambda i, j: (j, 0)),
                  pl.BlockSpec((TM, D), row),
                  pl.BlockSpec((None, 1, 6 * D), lambda i, j: (_mod_row(i), 0, 0)),
                  pl.BlockSpec((1, D), lambda i, j: (0, 0))],
        out_specs=pl.BlockSpec((TM, D), row),
        scratch_shapes=[pltpu.VMEM((TM, D), F32)],
        compiler_params=_params("arbitrary", "arbitrary"),
        name="ffn_dense",
    )(h, w_gu, w_gu, w_down, x, mod, post)


def _routing(route, n_rows):
    i1 = route[:, 0].astype(jnp.int32)
    i2 = route[:, 1].astype(jnp.int32)
    e = jnp.arange(N_EXPERTS, dtype=jnp.int32)
    hit1 = i1[:, None] == e
    hit2 = i2[:, None] == e
    sel = hit1.astype(jnp.int32) + hit2.astype(jnp.int32)
    cum = jnp.cumsum(sel, axis=0)
    tiles = (cum[-1] + TME - 1) // TME
    tile_end = jnp.cumsum(tiles)
    slot = ((tile_end - tiles) * TME)[None] + cum - sel
    slots = jnp.stack([jnp.sum(jnp.where(hit1, slot, 0), axis=1),
                       jnp.sum(jnp.where(hit2, slot, 0), axis=1)], axis=1).reshape(-1)
    j = jnp.arange(_n_slot_tiles(n_rows), dtype=jnp.int32)
    tile_expert = jnp.minimum(jnp.sum(j[:, None] >= tile_end[None], axis=1), N_EXPERTS - 1)
    return slots.astype(jnp.int32), tile_expert.astype(jnp.int32), tile_end[-1:].astype(jnp.int32)


def _n_slot_tiles(n_rows):
    return 2 * n_rows // TME + N_EXPERTS


def _row_copies(slots_ref, n, copy):
    def start(r, carry):
        for k in range(2):
            copy(r, k, slots_ref[2 * r + k]).start()
        return carry

    def wait(r, carry):
        for k in range(2):
            copy(r, k, slots_ref[2 * r + k]).wait()
        return carry

    lax.fori_loop(0, n, start, 0, unroll=8)
    lax.fori_loop(0, n, wait, 0, unroll=8)


def _dispatch_kernel(slots_ref, h_ref, init_ref, xs_ref, sem):
    del init_ref
    _row_copies(slots_ref, TMR, lambda r, k, s: pltpu.make_async_copy(
        h_ref.at[pl.ds(r, 1)], xs_ref.at[pl.ds(s, 1)], sem))


def _dispatch(slots, h, n_rows):
    n_slots = _n_slot_tiles(n_rows) * TME
    return pl.pallas_call(
        _dispatch_kernel,
        out_shape=jax.ShapeDtypeStruct((n_slots, D), F32),
        grid=(n_rows // TMR,),
        in_specs=[pl.BlockSpec((2 * TMR,), lambda i: (i,), memory_space=pltpu.SMEM),
                  pl.BlockSpec((TMR, D), lambda i: (i, 0)),
                  pl.BlockSpec(memory_space=pl.ANY)],
        out_specs=pl.BlockSpec(memory_space=pl.ANY),
        scratch_shapes=[pltpu.SemaphoreType.DMA(())],
        input_output_aliases={2: 0},
        compiler_params=_params("arbitrary"),
        name="moe_dispatch",
    )(slots, h, jnp.zeros((n_slots, D), F32))


def _expert_kernel(te_ref, nv_ref, x_ref, gw_ref, uw_ref, dw_ref, o_ref):
    del te_ref
    valid = pl.program_id(0) < nv_ref[0]

    @pl.when(valid)
    def _():
        o_ref[...] = _swiglu(x_ref[...].astype(BF16), gw_ref, uw_ref, dw_ref)

    @pl.when(jnp.logical_not(valid))
    def _():
        o_ref[...] = jnp.zeros_like(o_ref)


def _experts(tile_expert, n_valid, xs, w_gu, w_down):
    n_tiles = xs.shape[0] // TME
    return pl.pallas_call(
        _expert_kernel,
        out_shape=jax.ShapeDtypeStruct(xs.shape, F32),
        grid_spec=pltpu.PrefetchScalarGridSpec(
            num_scalar_prefetch=2,
            grid=(n_tiles,),
            in_specs=[pl.BlockSpec((TME, D), lambda j, te, nv: (j, 0)),
                      pl.BlockSpec((None, D, TF), lambda j, te, nv: (te[j], 0, 0)),
                      pl.BlockSpec((None, D, TF), lambda j, te, nv: (te[j], 0, 1)),
                      pl.BlockSpec((None, TF, D), lambda j, te, nv: (te[j], 0, 0))],
            out_specs=pl.BlockSpec((TME, D), lambda j, te, nv: (j, 0))),
        compiler_params=_params("arbitrary"),
        name="moe_experts",
    )(tile_expert, n_valid, xs, w_gu, w_gu, w_down)


def _combine_kernel(slots_ref, route_ref, x_ref, mod_ref, post_ref, ys_ref, o_ref, buf_ref, sem):
    _row_copies(slots_ref, TMR, lambda r, k, s: pltpu.make_async_copy(
        ys_ref.at[pl.ds(s, 1)], buf_ref.at[k, pl.ds(r, 1)], sem))
    y = route_ref[:, 2:3] * buf_ref[0] + route_ref[:, 3:4] * buf_ref[1]
    o_ref[...] = x_ref[...] + mod_ref[:, G_F] * _rms(y, post_ref[...])


def _combine(slots, route, x, mod, post, ys, n_rows):
    return pl.pallas_call(
        _combine_kernel,
        out_shape=jax.ShapeDtypeStruct((n_rows, D), F32),
        grid=(n_rows // TMR,),
        in_specs=[pl.BlockSpec((2 * TMR,), lambda i: (i,), memory_space=pltpu.SMEM),
                  pl.BlockSpec((TMR, LANES), lambda i: (i, 0)),
                  pl.BlockSpec((TMR, D), lambda i: (i, 0)),
                  pl.BlockSpec((None, 1, 6 * D), lambda i: (_mod_row(i * TMR // TM), 0, 0)),
                  pl.BlockSpec((1, D), lambda i: (0, 0)),
                  pl.BlockSpec(memory_space=pl.ANY)],
        out_specs=pl.BlockSpec((TMR, D), lambda i: (i, 0)),
        scratch_shapes=[pltpu.VMEM((2, TMR, D), F32), pltpu.SemaphoreType.DMA(())],
        compiler_params=_params("arbitrary"),
        name="moe_combine",
    )(slots, route, x, mod, post, ys)


def _moe(h, route, w_gu, w_down, x, mod, post, n_rows):
    slots, tile_expert, n_valid = _routing(route, n_rows)
    xs = _dispatch(slots, h, n_rows)
    ys = _experts(tile_expert, n_valid, xs, w_gu, w_down)
    return _combine(slots, route, x, mod, post, ys, n_rows)


def _rope_tables(rot_dim, lead, tail):
    t = jnp.arange(S, dtype=jnp.int32)
    row = (t // GRID_W).astype(F32)
    col = (t % GRID_W).astype(F32)
    quarter = rot_dim // 4
    inv = ROPE_THETA ** (-jnp.arange(quarter, dtype=F32) / quarter)
    ang = jnp.concatenate([row[:, None] * inv, col[:, None] * inv], axis=-1)
    cos, sin = jnp.cos(ang), jnp.sin(ang)
    reps = (LANES - lead - tail) // rot_dim
    cos_l = jnp.concatenate([jnp.ones((S, lead), F32)] + [cos, cos] * reps + [jnp.ones((S, tail), F32)], axis=1)
    sin_l = jnp.concatenate([jnp.zeros((S, lead), F32)] + [-sin, sin] * reps + [jnp.zeros((S, tail), F32)], axis=1)
    cos_all = jnp.concatenate([jnp.tile(cos_l, (B, 1)), jnp.ones((N_CTX, LANES), F32)], axis=0)
    sin_all = jnp.concatenate([jnp.tile(sin_l, (B, 1)), jnp.zeros((N_CTX, LANES), F32)], axis=0)
    return cos_all, sin_all


def _mla_weights(w_dkv, w_q_b, w_kv_b):
    lat = MLA_Q_LORA + MLA_KV_LORA
    pad = LANES - MLA_NOPE - MLA_ROPE
    wd = jnp.concatenate([w_dkv[:, :lat], jnp.zeros((D, MLA_NOPE), F32), w_dkv[:, lat:],
                          jnp.zeros((D, pad), F32)], axis=1)
    wq = jnp.pad(w_q_b.reshape(MLA_Q_LORA, N_HEADS, MLA_NOPE + MLA_ROPE), ((0, 0), (0, 0), (0, pad)))
    kvr = w_kv_b.reshape(MLA_KV_LORA, N_HEADS, 2 * HALF)
    k_part, v_part = kvr[..., :HALF], kvr[..., HALF:]
    zero = jnp.zeros_like(v_part)
    wk = jnp.concatenate([k_part, zero], axis=-1)
    even = (jnp.arange(N_HEADS) % 2 == 0)[None, :, None]
    wv = jnp.where(even, jnp.concatenate([v_part, zero], -1), jnp.concatenate([zero, v_part], -1))
    wkv = jnp.concatenate([wk.reshape(MLA_KV_LORA, -1), wv.reshape(MLA_KV_LORA, -1)], axis=1)
    return wd.astype(BF16), wq.reshape(MLA_Q_LORA, -1).astype(BF16), wkv.astype(BF16)


GQA_STACKS = ((((0, 0), (1, 1)), 0), (((0, 0), (1, 1)), 1))
MLA_STACKS = ((((0, 0),), 0), (((1, 0),), 1))
NA_STACKS = ((((0, 0),), 0), (((0, 0),), 1))


def kernel(x, c, ctx, c_ctx,
           l0_ada_w, l0_ada_b, l0_mix_pre, l0_mix_post, l0_ffn_pre, l0_ffn_post,
           l0_w_qkv, l0_q_norm, l0_k_norm, l0_w_o, l0_ffn_w_gu, l0_ffn_w_down,
           l1_ada_w, l1_ada_b, l1_mix_pre, l1_mix_post, l1_ffn_pre, l1_ffn_post,
           l1_w_qkv, l1_sink, l1_w_o, l1_router, l1_moe_w_gu, l1_moe_w_down,
           l2_ada_w, l2_ada_b, l2_mix_pre, l2_mix_post, l2_ffn_pre, l2_ffn_post,
           l2_w_dkv, l2_q_a_norm, l2_w_q_b, l2_kv_a_norm, l2_w_kv_b, l2_w_o, l2_ffn_w_gu, l2_ffn_w_down,
           l3_ada_w, l3_ada_b, l3_mix_pre, l3_mix_post, l3_ffn_pre, l3_ffn_post,
           l3_w_qkv, l3_rpb, l3_w_o, l3_router, l3_moe_w_gu, l3_moe_w_down):
    vec = lambda a: a.reshape(1, -1)
    tile2 = lambda a: jnp.tile(a, 2).reshape(1, LANES)
    bf = lambda a: a.astype(BF16)
    pad_router = lambda w: jnp.pad(w, ((0, 0), (0, LANES - N_EXPERTS)))

    xs = jnp.concatenate([x.reshape(N_LAT, D), ctx.reshape(N_CTX, D)], axis=0)
    c_all = jnp.concatenate([c, c_ctx[None], jnp.zeros((8 - B - 1, D), F32)], axis=0)
    mods = [_ada(c_all, w, b).reshape(8, 1, 6 * D)
            for w, b in ((l0_ada_w, l0_ada_b), (l1_ada_w, l1_ada_b), (l2_ada_w, l2_ada_b), (l3_ada_w, l3_ada_b))]
    cos64, sin64 = _rope_tables(HEAD_DIM, 0, 0)
    cos32, sin32 = _rope_tables(MLA_ROPE, MLA_NOPE, LANES - MLA_NOPE - MLA_ROPE)
    ones = jnp.ones((1, LANES), F32)
    idx = np.arange(LANES)
    bd = jnp.asarray(idx[:, None] // HALF == idx[None, :] // HALF, BF16)

    q, k4, v4 = _qkv(xs, mods[0], vec(l0_mix_pre), bf(l0_w_qkv), cos64, sin64,
                     tile2(l0_q_norm), tile2(l0_k_norm), bd, shared_kv=True, head_norm=True, rope=True,
                     v_t=True, q_scale=HEAD_DIM ** -0.5 * LOG2E)
    o = _attention_t(q, k4, v4, stacks=GQA_STACKS, tq=TQ)
    oc = _attention_t(q, k4, v4, stacks=GQA_STACKS, tq=CTX, ctx_queries=True)
    xs, h = _oproj(o, oc, bf(l0_w_o), xs, mods[0], vec(l0_mix_post), vec(l0_ffn_pre), None)
    xs = _ffn(h, bf(l0_ffn_w_gu), bf(l0_ffn_w_down), xs, mods[0], vec(l0_ffn_post))

    q, k4, v4 = _qkv(xs, mods[1], vec(l1_mix_pre), bf(l1_w_qkv), cos64, sin64,
                     ones, ones, bd, shared_kv=True, head_norm=False, rope=True,
                     v_t=False, q_scale=HEAD_DIM ** -0.5)
    o = _attention(q, k4, v4, stacks=GQA_STACKS, mode="window", tq=TQ, sink=l1_sink)
    oc = _attention(q, k4, v4, stacks=GQA_STACKS, mode="window", tq=CTX, sink=l1_sink, ctx_queries=True)
    xs, h, route = _oproj(o, oc, bf(l1_w_o), xs, mods[1], vec(l1_mix_post), vec(l1_ffn_pre),
                          pad_router(l1_router))
    xs = _moe(h, route, bf(l1_moe_w_gu), bf(l1_moe_w_down), xs, mods[1], vec(l1_ffn_post), N_TOK)

    wd, wq, wkv = _mla_weights(l2_w_dkv, l2_w_q_b, l2_w_kv_b)
    q, k4, v4 = _mla_proj(xs, mods[2], vec(l2_mix_pre), wd, vec(l2_q_a_norm), vec(l2_kv_a_norm),
                          wq, wkv, cos32, sin32)
    o = _attention_t(q, k4, v4, stacks=MLA_STACKS, tq=2 * TQ)
    oc = _attention_t(q, k4, v4, stacks=MLA_STACKS, tq=CTX, ctx_queries=True)
    xs, h = _oproj(o, oc, bf(l2_w_o), xs, mods[2], vec(l2_mix_post), vec(l2_ffn_pre), None)
    xs = _ffn(h, bf(l2_ffn_w_gu), bf(l2_ffn_w_down), xs, mods[2], vec(l2_ffn_post))

    q, k4, v4 = _qkv(xs, mods[3], vec(l3_mix_pre), bf(l3_w_qkv), cos64, sin64,
                     ones, ones, bd, shared_kv=False, head_norm=False, rope=False,
                     v_t=False, q_scale=HEAD_DIM ** -0.5)
    o = _attention(q, k4, v4, stacks=NA_STACKS, mode="neighbourhood", tq=NA_QROWS * GRID_W,
                   table=_na_table(l3_rpb))
    xl, h, route = _oproj(o, None, bf(l3_w_o), xs, mods[3], vec(l3_mix_post), vec(l3_ffn_pre),
                          pad_router(l3_router))
    xl = _moe(h, route, bf(l3_moe_w_gu), bf(l3_moe_w_down), xl, mods[3], vec(l3_ffn_post), N_LAT)
    return xl.reshape(B, S, D)
```

```python
import functools

import numpy as np
import jax
import jax.numpy as jnp
from jax import lax
from jax.experimental import pallas as pl
from jax.experimental.pallas import tpu as pltpu

D = 1024
B = 4
S = 4096
CTX = 256
GRID_W = 64
N_LAT = B * S
N_CTX = B * CTX
N_TOK = N_LAT + N_CTX
HEAD_DIM = 64
N_HEADS = 16
N_KV_HEADS = 4
WINDOW = 128
MLA_Q_LORA = 384
MLA_KV_LORA = 256
MLA_NOPE = 64
MLA_ROPE = 32
NA_ROWS = 8
NA_COLS = 16
D_FF = 2816
N_EXPERTS = 8
D_FF_EXPERT = 1408
ROPE_THETA = 10000.0
EPS = 1e-6
NEG = -1e30
LOG2E = 1.4426950408889634

LANES = 128
HALF = LANES // 2
TM = 512
TQ = 256
TK = 512
TF = 1408
TME = 256
TMR = 256
NA_QROWS = 4
NA_BAND = NA_QROWS + NA_ROWS - 1
VMEM_LIMIT = 56 * 1024 * 1024

F32 = jnp.float32
BF16 = jnp.bfloat16

SH_M, SC_M, G_M, SH_F, SC_F, G_F = (slice(i * D, (i + 1) * D) for i in range(6))


def _params(*sem, flags=None):
    return pltpu.CompilerParams(dimension_semantics=sem, vmem_limit_bytes=VMEM_LIMIT, flags=flags)


def _mod_row(i):
    r0 = i * TM
    return jnp.where(r0 >= N_LAT, B, r0 // S)


def _rms(x, gain):
    return x * lax.rsqrt(jnp.mean(x * x, axis=-1, keepdims=True) + EPS) * gain


def _lane(shape):
    return lax.broadcasted_iota(jnp.int32, shape, len(shape) - 1)


def _ada_kernel(c_ref, w_ref, b_ref, o_ref):
    a = jax.nn.silu(c_ref[...])
    o_ref[...] = jnp.dot(a, w_ref[...], preferred_element_type=F32,
                         precision=lax.Precision.HIGHEST) + b_ref[...]


def _ada(c_all, w, b):
    tn = 1536
    return pl.pallas_call(
        _ada_kernel,
        out_shape=jax.ShapeDtypeStruct((8, 6 * D), F32),
        grid=(6 * D // tn,),
        in_specs=[pl.BlockSpec((8, D), lambda j: (0, 0)),
                  pl.BlockSpec((D, tn), lambda j: (0, j)),
                  pl.BlockSpec((1, tn), lambda j: (0, j))],
        out_specs=pl.BlockSpec((8, tn), lambda j: (0, j)),
        compiler_params=_params("arbitrary"),
        name="ada",
    )(c_all, w, b.reshape(1, 6 * D))


def _rope_chunk(t, cos, sin, half):
    lane = _lane(t.shape)
    rot = jnp.where((lane % (2 * half)) < half,
                    pltpu.roll(t, LANES - half, 1), pltpu.roll(t, half, 1))
    return t * cos + rot * sin


def _head_rms(t, gain, bd):
    t2 = t * t
    hi = t2.astype(BF16)
    lo = (t2 - hi.astype(F32)).astype(BF16)
    ss = (jnp.dot(hi, bd, preferred_element_type=F32)
          + jnp.dot(lo, bd, preferred_element_type=F32))
    return t * lax.rsqrt(ss * (1.0 / HEAD_DIM) + EPS) * gain


def _with_ones(v, lane, parity):
    if parity == 0:
        return jnp.where(lane < HALF, v, jnp.where(lane == LANES - 1, 1.0, 0.0))
    return jnp.where(lane >= HALF, v, jnp.where(lane == 0, 1.0, 0.0))


def _half_only(k, lane, parity):
    return jnp.where(lane < HALF, k, 0.0) if parity == 0 else jnp.where(lane >= HALF, k, 0.0)


def _put_v(v_ref, j, parity, v, v_t):
    v_ref[j, parity] = (v.T if v_t else v).astype(BF16)


def _qkv_kernel(x_ref, mod_ref, g_ref, w_ref, cos_ref, sin_ref, qg_ref, kg_ref, bd_ref,
                q_ref, k_ref, v_ref, *, shared_kv, head_norm, rope, v_t, q_scale):
    h = _rms(x_ref[...], g_ref[...]) * (1.0 + mod_ref[:, SC_M]) + mod_ref[:, SH_M]
    acc = jnp.dot(h.astype(BF16), w_ref[...], preferred_element_type=F32)
    lane = _lane((TM, LANES))
    n_q = N_HEADS * HEAD_DIM // LANES
    n_kv = (N_KV_HEADS if shared_kv else N_HEADS) * HEAD_DIM // LANES

    def finish(t, gain_ref):
        if head_norm:
            t = _head_rms(t, gain_ref[...], bd_ref[...])
        if rope:
            t = _rope_chunk(t, cos_ref[...], sin_ref[...], HEAD_DIM // 2)
        return t

    for c in range(n_q):
        t = finish(acc[:, c * LANES:(c + 1) * LANES], qg_ref)
        q_ref[:, c * LANES:(c + 1) * LANES] = (t * q_scale).astype(BF16)
    for c in range(n_kv):
        k = finish(acc[:, (n_q + c) * LANES:(n_q + c + 1) * LANES], kg_ref)
        v = acc[:, (n_q + n_kv + c) * LANES:(n_q + n_kv + c + 1) * LANES]
        if shared_kv:
            k_sw = pltpu.roll(k, HALF, 1)
            v_sw = pltpu.roll(v, HALF, 1)
            for half in range(2):
                j = 2 * c + half
                for parity in range(2):
                    ksrc = k if parity == half else k_sw
                    vsrc = v if parity == half else v_sw
                    k_ref[j, parity] = _half_only(ksrc, lane, parity).astype(BF16)
                    _put_v(v_ref, j, parity, _with_ones(vsrc, lane, parity), v_t)
        else:
            for parity in range(2):
                k_ref[c, parity] = _half_only(k, lane, parity).astype(BF16)
                _put_v(v_ref, c, parity, _with_ones(v, lane, parity), v_t)


def _kv_out(groups, transposed):
    if transposed:
        return (pl.BlockSpec((groups, 2, LANES, TM), lambda i: (0, 0, 0, i)),
                jax.ShapeDtypeStruct((groups, 2, LANES, N_TOK), BF16))
    return (pl.BlockSpec((groups, 2, TM, LANES), lambda i: (0, 0, i, 0)),
            jax.ShapeDtypeStruct((groups, 2, N_TOK, LANES), BF16))


def _qkv(x, mod, gain, w, cos, sin, qg, kg, bd, *, shared_kv, head_norm, rope, v_t, q_scale):
    n_out = w.shape[1]
    groups = N_KV_HEADS if shared_kv else N_HEADS // 2
    row = lambda i: (i, 0)
    const = lambda i: (0, 0)
    kv_spec, kv_shape = _kv_out(groups, False)
    v_spec, v_shape = _kv_out(groups, v_t)
    return pl.pallas_call(
        functools.partial(_qkv_kernel, shared_kv=shared_kv, head_norm=head_norm, rope=rope,
                          v_t=v_t, q_scale=q_scale),
        out_shape=(jax.ShapeDtypeStruct((N_TOK, N_HEADS * HEAD_DIM), BF16), kv_shape, v_shape),
        grid=(N_TOK // TM,),
        in_specs=[pl.BlockSpec((TM, D), row),
                  pl.BlockSpec((None, 1, 6 * D), lambda i: (_mod_row(i), 0, 0)),
                  pl.BlockSpec((1, D), const),
                  pl.BlockSpec((D, n_out), const),
                  pl.BlockSpec((TM, LANES), row),
                  pl.BlockSpec((TM, LANES), row),
                  pl.BlockSpec((1, LANES), const),
                  pl.BlockSpec((1, LANES), const),
                  pl.BlockSpec((LANES, LANES), const)],
        out_specs=(pl.BlockSpec((TM, N_HEADS * HEAD_DIM), row), kv_spec, v_spec),
        compiler_params=_params("arbitrary"),
        name="qkv",
    )(x, mod, gain, w, cos, sin, qg, kg, bd)


def _mla_kernel(x_ref, mod_ref, g_ref, wd_ref, qan_ref, kvan_ref, wq_ref, wkv_ref,
                cos_ref, sin_ref, q_ref, k_ref, v_ref):
    h = _rms(x_ref[...], g_ref[...]) * (1.0 + mod_ref[:, SC_M]) + mod_ref[:, SH_M]
    lat = jnp.dot(h.astype(BF16), wd_ref[...], preferred_element_type=F32)
    c_q = _rms(lat[:, :MLA_Q_LORA], qan_ref[...])
    c_kv = _rms(lat[:, MLA_Q_LORA:MLA_Q_LORA + MLA_KV_LORA], kvan_ref[...])
    cos = cos_ref[...]
    sin = sin_ref[...]
    k_rope = _rope_chunk(lat[:, MLA_Q_LORA + MLA_KV_LORA:], cos, sin, MLA_ROPE // 2)
    q = jnp.dot(c_q.astype(BF16), wq_ref[...], preferred_element_type=F32)
    kv = jnp.dot(c_kv.astype(BF16), wkv_ref[...], preferred_element_type=F32)
    scale = (MLA_NOPE + MLA_ROPE) ** -0.5 * LOG2E
    lane = _lane((TM, LANES))
    for hd in range(N_HEADS):
        sl = slice(hd * LANES, (hd + 1) * LANES)
        q_ref[:, sl] = (_rope_chunk(q[:, sl], cos, sin, MLA_ROPE // 2) * scale).astype(BF16)
        k_ref[hd // 2, hd % 2] = (kv[:, sl] + k_rope).astype(BF16)
        v = kv[:, N_HEADS * LANES + hd * LANES:N_HEADS * LANES + (hd + 1) * LANES]
        _put_v(v_ref, hd // 2, hd % 2, _with_ones(v, lane, hd % 2), True)


def _mla_proj(x, mod, gain, wd, qan, kvan, wq, wkv, cos, sin):
    row = lambda i: (i, 0)
    const = lambda i: (0, 0)
    kv_spec, kv_shape = _kv_out(N_HEADS // 2, False)
    v_spec, v_shape = _kv_out(N_HEADS // 2, True)
    return pl.pallas_call(
        _mla_kernel,
        out_shape=(jax.ShapeDtypeStruct((N_TOK, N_HEADS * LANES), BF16), kv_shape, v_shape),
        grid=(N_TOK // TM,),
        in_specs=[pl.BlockSpec((TM, D), row),
                  pl.BlockSpec((None, 1, 6 * D), lambda i: (_mod_row(i), 0, 0)),
                  pl.BlockSpec((1, D), const),
                  pl.BlockSpec(wd.shape, const),
                  pl.BlockSpec((1, MLA_Q_LORA), const),
                  pl.BlockSpec((1, MLA_KV_LORA), const),
                  pl.BlockSpec(wq.shape, const),
                  pl.BlockSpec(wkv.shape, const),
                  pl.BlockSpec((TM, LANES), row),
                  pl.BlockSpec((TM, LANES), row)],
        out_specs=(pl.BlockSpec((TM, N_HEADS * LANES), row), kv_spec, v_spec),
        compiler_params=_params("arbitrary"),
        name="mla_proj",
    )(x, mod, gain, wd, qan, kvan, wq, wkv, cos, sin)


def _attn_kernel(*refs, stacks, tq, mode, with_lat):
    refs = list(refs)
    sink_ref = refs.pop(0) if mode == "window" else None
    q_ref, kc_ref, vc_ref = refs[:3]
    refs = refs[3:]
    if with_lat:
        kl_ref, vl_ref = refs[:2]
        refs = refs[2:]
    tab_ref = refs.pop(0) if mode == "neighbourhood" else None
    o_ref, s_ref = refs
    g = pl.program_id(1)
    t = pl.program_id(2)

    chunks = [("ctx", 0, CTX)]
    mask = None
    if with_lat and mode == "window":
        span = tq + 2 * WINDOW
        q0 = t * tq
        start = pl.multiple_of(jnp.clip(q0 - WINDOW, 0, S - span), WINDOW)
        m_rows = tq * len(stacks[0][0])
        qi = q0 + lax.broadcasted_iota(jnp.int32, (m_rows, span), 0) % tq
        ki = start + lax.broadcasted_iota(jnp.int32, (m_rows, span), 1)
        mask = jnp.abs(qi - ki) <= WINDOW
        chunks += [("lat", start, span)]
    elif with_lat:
        start = pl.multiple_of(_na_band_start(t) * GRID_W, GRID_W)
        chunks += [("lat", start, NA_BAND * GRID_W)]

    outs = {}
    for qcs, kv in stacks:
        q = jnp.concatenate([q_ref[:, qc * LANES:(qc + 1) * LANES] for qc, _ in qcs], axis=0)
        rows = q.shape[0]
        sink_col = None
        if mode == "window":
            hd = [g * 2 * len(qcs) + 2 * qc + kv for qc, _ in qcs]
            row = lax.broadcasted_iota(jnp.int32, (rows, 1), 0)
            sink_col = jnp.where(row < tq, sink_ref[hd[0]], sink_ref[hd[-1]])

        col = 0
        m_fold, m_col = None, sink_col
        for kind, st, width in chunks:
            k = kc_ref[kv] if kind == "ctx" else kl_ref[kv, pl.ds(st, width), :]
            s = lax.dot_general(q, k, (((1,), (1,)), ((), ())), preferred_element_type=F32)
            if kind == "lat" and tab_ref is not None:
                s = s + tab_ref[kv]
            if kind == "lat" and mask is not None:
                s = jnp.where(mask, s, NEG)
            s_ref[0:rows, col:col + width] = s
            if width % LANES == 0:
                for j in range(width // LANES):
                    slab = s[:, j * LANES:(j + 1) * LANES]
                    m_fold = slab if m_fold is None else jnp.maximum(m_fold, slab)
            else:
                mc = jnp.max(s, axis=-1, keepdims=True)
                m_col = mc if m_col is None else jnp.maximum(m_col, mc)
            col += width
        m = jnp.max(m_fold, axis=-1, keepdims=True)
        if m_col is not None:
            m = jnp.maximum(m, m_col)

        col = 0
        acc = jnp.zeros((rows, LANES), F32)
        for kind, st, width in chunks:
            v = vc_ref[kv] if kind == "ctx" else vl_ref[kv, pl.ds(st, width), :]
            p = jnp.exp(s_ref[0:rows, col:col + width] - m).astype(BF16)
            acc = acc + jnp.dot(p, v, preferred_element_type=F32)
            col += width
        den = acc[:, LANES - 1:LANES] if kv == 0 else acc[:, 0:1]
        if sink_col is not None:
            den = den + jnp.exp(sink_col - m)
        lane = _lane(acc.shape)
        o = jnp.where(lane < HALF if kv == 0 else lane >= HALF, acc / den, 0.0)
        for i, (_, oc) in enumerate(qcs):
            part = o[i * tq:(i + 1) * tq]
            outs[oc] = part if oc not in outs else outs[oc] + part
    for oc, o in outs.items():
        o_ref[:, oc * LANES:(oc + 1) * LANES] = o.astype(o_ref.dtype)


def _attention(q, k4, v4, *, stacks, mode, tq, sink=None, table=None, ctx_queries=False):
    groups = k4.shape[0]
    qw = q.shape[1] // groups
    ow = N_HEADS * HEAD_DIM // groups
    with_lat = not ctx_queries
    nq = S // tq if with_lat else CTX // tq
    row0 = 0 if with_lat else N_LAT // tq

    q_spec = pl.BlockSpec((tq, qw), lambda b, g, t: (row0 + b * nq + t, g))
    ctx_spec = pl.BlockSpec((None, 2, CTX, LANES), lambda b, g, t: (g, 0, N_LAT // CTX + b, 0))
    lat_spec = pl.BlockSpec((None, 2, S, LANES), lambda b, g, t: (g, 0, b, 0))
    in_specs = [q_spec, ctx_spec, ctx_spec]
    args = [q, k4, v4]
    n_keys = CTX
    if with_lat:
        in_specs += [lat_spec, lat_spec]
        args += [k4, v4]
        n_keys += {"window": tq + 2 * WINDOW, "neighbourhood": NA_BAND * GRID_W}[mode]
    if mode == "window":
        in_specs = [pl.BlockSpec(memory_space=pltpu.SMEM)] + in_specs
        args = [sink] + args
    if mode == "neighbourhood":
        in_specs.append(pl.BlockSpec((None, 2, tq, NA_BAND * GRID_W),
                                     lambda b, g, t: (_na_variant(t), g, 0, 0)))
        args.append(table)
    m_rows = tq * max(len(qcs) for qcs, _ in stacks)
    return pl.pallas_call(
        functools.partial(_attn_kernel, stacks=stacks, tq=tq, mode=mode, with_lat=with_lat),
        out_shape=jax.ShapeDtypeStruct((N_LAT if with_lat else N_CTX, N_HEADS * HEAD_DIM), BF16),
        grid=(B, groups, nq),
        in_specs=in_specs,
        out_specs=pl.BlockSpec((tq, ow), lambda b, g, t: (b * nq + t, g)),
        scratch_shapes=[pltpu.VMEM((m_rows, n_keys), F32)],
        compiler_params=_params("arbitrary", "arbitrary", "arbitrary"),
        name="attn_" + mode + ("" if with_lat else "_ctx"),
    )(*args)


def _attn_t_kernel(*refs, stacks, tq, subs, with_lat):
    refs = list(refs)
    q_ref, kc_ref, vtc_ref = refs[:3]
    refs = refs[3:]
    if with_lat:
        kl_ref, vtl_ref = refs[:2]
        refs = refs[2:]
    o_ref = refs[0]
    s_refs = refs[1:]
    chunks = [("ctx", 0, CTX)]
    if with_lat:
        chunks += [("lat", c * TK, TK) for c in range(S // TK)]
    rows = [sum(w for _, _, w in chunks[:c]) for c in range(len(chunks))]
    items = [(sub, qcs, kv) for sub in range(subs) for qcs, kv in stacks]
    qs = [jnp.concatenate([q_ref[sub * tq:(sub + 1) * tq, qc * LANES:(qc + 1) * LANES]
                           for qc, _ in qcs], axis=0) for sub, qcs, _ in items]
    cols = qs[0].shape[0]

    def score_chunk(it, c, m8):
        kind, st, width = chunks[c]
        kv = items[it][2]
        k = kc_ref[kv] if kind == "ctx" else kl_ref[kv, st:st + width, :]
        s = lax.dot_general(k, qs[it], (((1,), (1,)), ((), ())), preferred_element_type=F32)
        s_refs[it % len(s_refs)][rows[c]:rows[c] + width, 0:cols] = s
        c8 = jnp.max(s.reshape(width // 8, 8, cols), axis=0)
        return c8 if m8 is None else jnp.maximum(m8, c8)

    def prob_chunk(it, c, m):
        width = chunks[c][2]
        s = s_refs[it % len(s_refs)][rows[c]:rows[c] + width, 0:cols]
        return jnp.exp2(s - m).astype(BF16)

    def value_chunk(it, c, p, acc):
        kind, st, width = chunks[c]
        kv = items[it][2]
        vt = vtc_ref[kv] if kind == "ctx" else vtl_ref[kv, :, st:st + width]
        return acc + jnp.dot(vt, p, preferred_element_type=F32)

    n = len(items)
    ms, ps, accs = [None] * n, [[] for _ in range(n)], []
    for stage in range(n + 2):
        m8 = None
        acc = jnp.zeros((LANES, cols), F32)
        for c in range(len(chunks)):
            if 0 <= stage - 1 < n:
                ps[stage - 1].append(prob_chunk(stage - 1, c, ms[stage - 1]))
            if stage < n:
                m8 = score_chunk(stage, c, m8)
            if 0 <= stage - 2 < n:
                acc = value_chunk(stage - 2, c, ps[stage - 2][c], acc)
        if stage < n:
            ms[stage] = jnp.max(m8, axis=0, keepdims=True)
        if 0 <= stage - 2 < n:
            accs.append(acc)

    outs = {}
    for (sub, qcs, kv), acc in zip(items, accs):
        den = acc[LANES - 1:LANES, :] if kv == 0 else acc[0:1, :]
        o = (acc / den).T
        lane = _lane(o.shape)
        o = jnp.where(lane < HALF if kv == 0 else lane >= HALF, o, 0.0)
        for i, (_, oc) in enumerate(qcs):
            part = o[i * tq:(i + 1) * tq]
            outs[sub, oc] = part if (sub, oc) not in outs else outs[sub, oc] + part
    for (sub, oc), o in outs.items():
        o_ref[sub * tq:(sub + 1) * tq, oc * LANES:(oc + 1) * LANES] = o.astype(o_ref.dtype)


def _attention_t(q, k4, vt4, *, stacks, tq, subs=1, ctx_queries=False):
    groups = k4.shape[0]
    qw = q.shape[1] // groups
    ow = N_HEADS * HEAD_DIM // groups
    with_lat = not ctx_queries
    tqs = tq * subs
    nq = S // tqs if with_lat else CTX // tqs
    row0 = 0 if with_lat else N_LAT // tqs
    in_specs = [pl.BlockSpec((tqs, qw), lambda b, g, t: (row0 + b * nq + t, g)),
                pl.BlockSpec((None, 2, CTX, LANES), lambda b, g, t: (g, 0, N_LAT // CTX + b, 0)),
                pl.BlockSpec((None, 2, LANES, CTX), lambda b, g, t: (g, 0, 0, N_LAT // CTX + b))]
    args = [q, k4, vt4]
    n_keys = CTX
    if with_lat:
        in_specs += [pl.BlockSpec((None, 2, S, LANES), lambda b, g, t: (g, 0, b, 0)),
                     pl.BlockSpec((None, 2, LANES, S), lambda b, g, t: (g, 0, 0, b))]
        args += [k4, vt4]
        n_keys += S
    m_rows = tq * max(len(qcs) for qcs, _ in stacks)
    return pl.pallas_call(
        functools.partial(_attn_t_kernel, stacks=stacks, tq=tq, subs=subs, with_lat=with_lat),
        out_shape=jax.ShapeDtypeStruct((N_LAT if with_lat else N_CTX, N_HEADS * HEAD_DIM), BF16),
        grid=(B, groups, nq),
        in_specs=in_specs,
        out_specs=pl.BlockSpec((tqs, ow), lambda b, g, t: (b * nq + t, g)),
        scratch_shapes=[pltpu.VMEM((n_keys, m_rows), F32) for _ in range(2)],
        compiler_params=_params("arbitrary", "arbitrary", "arbitrary"),
        name="attn_global" + ("" if with_lat else "_ctx"),
    )(*args)


def _na_band_start(t):
    return jnp.clip(t * NA_QROWS - NA_ROWS // 2, 0, S // GRID_W - NA_BAND)


def _na_tiles():
    rows = S // GRID_W
    tiles = [(t * NA_QROWS, min(max(t * NA_QROWS - NA_ROWS // 2, 0), rows - NA_BAND))
             for t in range(rows // NA_QROWS)]
    lo = [rb for rb in tiles if rb[0] - NA_ROWS // 2 < 0]
    hi = [rb for rb in tiles if rb[0] - NA_ROWS // 2 > rows - NA_BAND]
    return lo + [tiles[len(lo)]] + hi, len(lo), len(hi)


def _na_variant(t):
    _, n_lo, n_hi = _na_tiles()
    first_hi = S // GRID_W // NA_QROWS - n_hi
    return jnp.where(t < n_lo, t, jnp.where(t >= first_hi, t - first_hi + n_lo + 1, n_lo))


def _bias_cols_kernel(rpb_ref, sel_ref, o_ref):
    o_ref[...] = jnp.dot(rpb_ref[...], sel_ref[...], preferred_element_type=F32,
                         precision=lax.Precision.HIGHEST)


def _na_table(rpb):
    rows = S // GRID_W
    n_dr, n_dc = 2 * NA_ROWS - 1, 2 * NA_COLS - 1
    c = np.arange(GRID_W)[:, None]
    kc = np.arange(GRID_W)[None, :]
    cs = np.clip(c - NA_COLS // 2, 0, GRID_W - NA_COLS)
    col_ok = (kc >= cs) & (kc < cs + NA_COLS)
    sel = (np.arange(LANES)[:, None, None] == (kc - c + NA_COLS - 1)[None]) & col_ok[None]
    sel = jnp.asarray(sel.reshape(LANES, GRID_W * GRID_W), F32)
    rpb2 = jnp.pad(rpb.reshape(N_HEADS * n_dr, n_dc), ((0, 256 - N_HEADS * n_dr), (0, LANES - n_dc)))
    cols = pl.pallas_call(
        _bias_cols_kernel,
        out_shape=jax.ShapeDtypeStruct((256, GRID_W * GRID_W), F32),
        name="na_bias_cols",
    )(rpb2, sel)
    cols = cols[:N_HEADS * n_dr].reshape(N_HEADS, n_dr, GRID_W, GRID_W)
    cols = jnp.where(col_ok[None, None], cols, NEG)
    masked = jnp.full((N_HEADS, GRID_W, GRID_W), NEG, F32)
    variants = []
    for r0, bs in _na_tiles()[0]:
        q_rows = []
        for r in range(r0, r0 + NA_QROWS):
            rs = min(max(r - NA_ROWS // 2, 0), rows - NA_ROWS)
            q_rows.append(jnp.concatenate(
                [cols[:, kr - r + NA_ROWS - 1] if rs <= kr < rs + NA_ROWS else masked
                 for kr in range(bs, bs + NA_BAND)], axis=2))
        variants.append(jnp.concatenate(q_rows, axis=1))
    return jnp.stack(variants)


def _router_route(h, wr_ref):
    logits = jnp.dot(h, wr_ref[...], preferred_element_type=F32, precision=lax.Precision.HIGHEST)
    lane = _lane(logits.shape).astype(F32)
    lg = jnp.where(lane < N_EXPERTS, logits, -jnp.inf)
    m1 = jnp.max(lg, axis=-1, keepdims=True)
    i1 = jnp.min(jnp.where(lg == m1, lane, float(LANES)), axis=-1, keepdims=True)
    lg2 = jnp.where(lane == i1, -jnp.inf, lg)
    m2 = jnp.max(lg2, axis=-1, keepdims=True)
    i2 = jnp.min(jnp.where(lg2 == m2, lane, float(LANES)), axis=-1, keepdims=True)
    e = jnp.exp(m2 - m1)
    return jnp.where(lane == 0, i1, jnp.where(lane == 1, i2, jnp.where(
        lane == 2, 1.0 / (1.0 + e), jnp.where(lane == 3, e / (1.0 + e), 0.0))))


def _oproj_kernel(*refs, moe, with_ctx):
    refs = list(refs)
    o_ref = refs.pop(0)
    oc_ref = refs.pop(0) if with_ctx else None
    if moe:
        wo_ref, x_ref, mod_ref, post_ref, pre_ref, wr_ref, xo_ref, h_ref, route_ref = refs
    else:
        wo_ref, x_ref, mod_ref, post_ref, pre_ref, xo_ref, h_ref = refs
    o = o_ref[...]
    if with_ctx:
        o = jnp.where(pl.program_id(0) < N_LAT // TM, o, oc_ref[...])
    y = jnp.dot(o, wo_ref[...], preferred_element_type=F32)
    x1 = x_ref[...] + mod_ref[:, G_M] * _rms(y, post_ref[...])
    xo_ref[...] = x1
    h = _rms(x1, pre_ref[...]) * (1.0 + mod_ref[:, SC_F]) + mod_ref[:, SH_F]
    h_ref[...] = h.astype(h_ref.dtype)
    if moe:
        route_ref[...] = _router_route(h, wr_ref)


def _oproj(o, o_ctx, wo, x, mod, post, pre, w_router):
    moe = w_router is not None
    with_ctx = o_ctx is not None
    n_rows = N_TOK if with_ctx else N_LAT
    n_lat = N_LAT // TM
    row = lambda i: (i, 0)
    const = lambda i: (0, 0)
    in_specs = [pl.BlockSpec((TM, D), lambda i: (jnp.minimum(i, n_lat - 1), 0))]
    args = [o]
    if with_ctx:
        in_specs.append(pl.BlockSpec((TM, D), lambda i: (jnp.maximum(i - n_lat, 0), 0)))
        args.append(o_ctx)
    in_specs += [pl.BlockSpec((D, D), const), pl.BlockSpec((TM, D), row),
                 pl.BlockSpec((None, 1, 6 * D), lambda i: (_mod_row(i), 0, 0)),
                 pl.BlockSpec((1, D), const), pl.BlockSpec((1, D), const)]
    args += [wo, x, mod, post, pre]
    out_shape = [jax.ShapeDtypeStruct((n_rows, D), F32),
                 jax.ShapeDtypeStruct((n_rows, D), F32 if moe else BF16)]
    out_specs = [pl.BlockSpec((TM, D), row), pl.BlockSpec((TM, D), row)]
    if moe:
        in_specs.append(pl.BlockSpec((D, LANES), const))
        out_shape.append(jax.ShapeDtypeStruct((n_rows, LANES), F32))
        out_specs.append(pl.BlockSpec((TM, LANES), row))
        args.append(w_router)
    return pl.pallas_call(
        functools.partial(_oproj_kernel, moe=moe, with_ctx=with_ctx),
        out_shape=tuple(out_shape),
        grid=(n_rows // TM,),
        in_specs=in_specs,
        out_specs=tuple(out_specs),
        compiler_params=_params("arbitrary"),
        name="oproj",
    )(*args)


def _swiglu(h, gw_ref, uw_ref, dw_ref):
    gate = jnp.dot(h, gw_ref[...], preferred_element_type=F32)
    up = jnp.dot(h, uw_ref[...], preferred_element_type=F32)
    return jnp.dot((jax.nn.silu(gate) * up).astype(BF16), dw_ref[...], preferred_element_type=F32)


def _ffn_kernel(h_ref, gw_ref, uw_ref, dw_ref, x_ref, mod_ref, post_ref, o_ref, acc_ref):
    j = pl.program_id(1)

    @pl.when(j == 0)
    def _():
        acc_ref[...] = jnp.zeros_like(acc_ref)

    acc_ref[...] += _swiglu(h_ref[...], gw_ref, uw_ref, dw_ref)

    @pl.when(j == pl.num_programs(1) - 1)
    def _():
        o_ref[...] = x_ref[...] + mod_ref[:, G_F] * _rms(acc_ref[...], post_ref[...])


def _ffn(h, w_gu, w_down, x, mod, post):
    row = lambda i, j: (i, 0)
    n_steps = D_FF // TF
    return pl.pallas_call(
        _ffn_kernel,
        out_shape=jax.ShapeDtypeStruct((N_TOK, D), F32),
        grid=(N_TOK // TM, n_steps),
        in_specs=[pl.BlockSpec((TM, D), row),
                  pl.BlockSpec((D, TF), lambda i, j: (0, j)),
                  pl.BlockSpec((D, TF), lambda i, j: (0, n_steps + j)),
                  pl.BlockSpec((TF, D), lambda i, j: (j, 0)),
                  pl.BlockSpec((TM, D), row),
                  pl.BlockSpec((None, 1, 6 * D), lambda i, j: (_mod_row(i), 0, 0)),
                  pl.BlockSpec((1, D), lambda i, j: (0, 0))],
        out_specs=pl.BlockSpec((TM, D), row),
        scratch_shapes=[pltpu.VMEM((TM, D), F32)],
        compiler_params=_params("arbitrary", "arbitrary"),
        name="ffn_dense",
    )(h, w_gu, w_gu, w_down, x, mod, post)


def _routing(route, n_rows):
    i1 = route[:, 0].astype(jnp.int32)
    i2 = route[:, 1].astype(jnp.int32)
    e = jnp.arange(N_EXPERTS, dtype=jnp.int32)
    hit1 = i1[:, None] == e
    hit2 = i2[:, None] == e
    sel = hit1.astype(jnp.int32) + hit2.astype(jnp.int32)
    cum = jnp.cumsum(sel, axis=0)
    tiles = (cum[-1] + TME - 1) // TME
    tile_end = jnp.cumsum(tiles)
    slot = ((tile_end - tiles) * TME)[None] + cum - sel
    slots = jnp.stack([jnp.sum(jnp.where(hit1, slot, 0), axis=1),
                       jnp.sum(jnp.where(hit2, slot, 0), axis=1)], axis=1).reshape(-1)
    j = jnp.arange(_n_slot_tiles(n_rows), dtype=jnp.int32)
    tile_expert = jnp.minimum(jnp.sum(j[:, None] >= tile_end[None], axis=1), N_EXPERTS - 1)
    return slots.astype(jnp.int32), tile_expert.astype(jnp.int32), tile_end[-1:].astype(jnp.int32)


def _n_slot_tiles(n_rows):
    return 2 * n_rows // TME + N_EXPERTS


def _row_copies(slots_ref, n, copy):
    def start(r, carry):
        for k in range(2):
            copy(r, k, slots_ref[2 * r + k]).start(priority=k)
        return carry

    def wait(r, carry):
        for k in range(2):
            copy(r, k, slots_ref[2 * r + k]).wait()
        return carry

    lax.fori_loop(0, n, start, 0, unroll=8)
    lax.fori_loop(0, n, wait, 0, unroll=8)


def _dispatch_kernel(slots_ref, h_ref, init_ref, xs_ref, sem):
    del init_ref
    _row_copies(slots_ref, TMR, lambda r, k, s: pltpu.make_async_copy(
        h_ref.at[pl.ds(r, 1)], xs_ref.at[pl.ds(s, 1)], sem))


def _dispatch(slots, h, n_rows):
    n_slots = _n_slot_tiles(n_rows) * TME
    return pl.pallas_call(
        _dispatch_kernel,
        out_shape=jax.ShapeDtypeStruct((n_slots, D), F32),
        grid=(n_rows // TMR,),
        in_specs=[pl.BlockSpec((2 * TMR,), lambda i: (i,), memory_space=pltpu.SMEM),
                  pl.BlockSpec((TMR, D), lambda i: (i, 0)),
                  pl.BlockSpec(memory_space=pl.ANY)],
        out_specs=pl.BlockSpec(memory_space=pl.ANY),
        scratch_shapes=[pltpu.SemaphoreType.DMA(())],
        input_output_aliases={2: 0},
        compiler_params=_params("arbitrary"),
        name="moe_dispatch",
    )(slots, h, jnp.zeros((n_slots, D), F32))


def _expert_kernel(te_ref, nv_ref, x_ref, gw_ref, uw_ref, dw_ref, o_ref):
    del te_ref
    valid = pl.program_id(0) < nv_ref[0]

    @pl.when(valid)
    def _():
        o_ref[...] = _swiglu(x_ref[...].astype(BF16), gw_ref, uw_ref, dw_ref)

    @pl.when(jnp.logical_not(valid))
    def _():
        o_ref[...] = jnp.zeros_like(o_ref)


def _experts(tile_expert, n_valid, xs, w_gu, w_down):
    n_tiles = xs.shape[0] // TME
    return pl.pallas_call(
        _expert_kernel,
        out_shape=jax.ShapeDtypeStruct(xs.shape, F32),
        grid_spec=pltpu.PrefetchScalarGridSpec(
            num_scalar_prefetch=2,
            grid=(n_tiles,),
            in_specs=[pl.BlockSpec((TME, D), lambda j, te, nv: (j, 0)),
                      pl.BlockSpec((None, D, TF), lambda j, te, nv: (te[j], 0, 0)),
                      pl.BlockSpec((None, D, TF), lambda j, te, nv: (te[j], 0, 1)),
                      pl.BlockSpec((None, TF, D), lambda j, te, nv: (te[j], 0, 0))],
            out_specs=pl.BlockSpec((TME, D), lambda j, te, nv: (j, 0))),
        compiler_params=_params("arbitrary"),
        name="moe_experts",
    )(tile_expert, n_valid, xs, w_gu, w_gu, w_down)


def _combine_kernel(slots_ref, route_ref, x_ref, mod_ref, post_ref, ys_ref, o_ref, buf_ref, sem):
    _row_copies(slots_ref, TMR, lambda r, k, s: pltpu.make_async_copy(
        ys_ref.at[pl.ds(s, 1)], buf_ref.at[k, pl.ds(r, 1)], sem))
    y = route_ref[:, 2:3] * buf_ref[0] + route_ref[:, 3:4] * buf_ref[1]
    o_ref[...] = x_ref[...] + mod_ref[:, G_F] * _rms(y, post_ref[...])


def _combine(slots, route, x, mod, post, ys, n_rows):
    return pl.pallas_call(
        _combine_kernel,
        out_shape=jax.ShapeDtypeStruct((n_rows, D), F32),
        grid=(n_rows // TMR,),
        in_specs=[pl.BlockSpec((2 * TMR,), lambda i: (i,), memory_space=pltpu.SMEM),
                  pl.BlockSpec((TMR, LANES), lambda i: (i, 0)),
                  pl.BlockSpec((TMR, D), lambda i: (i, 0)),
                  pl.BlockSpec((None, 1, 6 * D), lambda i: (_mod_row(i * TMR // TM), 0, 0)),
                  pl.BlockSpec((1, D), lambda i: (0, 0)),
                  pl.BlockSpec(memory_space=pl.ANY)],
        out_specs=pl.BlockSpec((TMR, D), lambda i: (i, 0)),
        scratch_shapes=[pltpu.VMEM((2, TMR, D), F32), pltpu.SemaphoreType.DMA(())],
        compiler_params=_params("arbitrary"),
        name="moe_combine",
    )(slots, route, x, mod, post, ys)


def _moe(h, route, w_gu, w_down, x, mod, post, n_rows):
    slots, tile_expert, n_valid = _routing(route, n_rows)
    xs = _dispatch(slots, h, n_rows)
    ys = _experts(tile_expert, n_valid, xs, w_gu, w_down)
    return _combine(slots, route, x, mod, post, ys, n_rows)


def _rope_tables(rot_dim, lead, tail):
    t = jnp.arange(S, dtype=jnp.int32)
    row = (t // GRID_W).astype(F32)
    col = (t % GRID_W).astype(F32)
    quarter = rot_dim // 4
    inv = ROPE_THETA ** (-jnp.arange(quarter, dtype=F32) / quarter)
    ang = jnp.concatenate([row[:, None] * inv, col[:, None] * inv], axis=-1)
    cos, sin = jnp.cos(ang), jnp.sin(ang)
    reps = (LANES - lead - tail) // rot_dim
    cos_l = jnp.concatenate([jnp.ones((S, lead), F32)] + [cos, cos] * reps + [jnp.ones((S, tail), F32)], axis=1)
    sin_l = jnp.concatenate([jnp.zeros((S, lead), F32)] + [-sin, sin] * reps + [jnp.zeros((S, tail), F32)], axis=1)
    cos_all = jnp.concatenate([jnp.tile(cos_l, (B, 1)), jnp.ones((N_CTX, LANES), F32)], axis=0)
    sin_all = jnp.concatenate([jnp.tile(sin_l, (B, 1)), jnp.zeros((N_CTX, LANES), F32)], axis=0)
    return cos_all, sin_all


def _mla_weights(w_dkv, w_q_b, w_kv_b):
    lat = MLA_Q_LORA + MLA_KV_LORA
    pad = LANES - MLA_NOPE - MLA_ROPE
    wd = jnp.concatenate([w_dkv[:, :lat], jnp.zeros((D, MLA_NOPE), F32), w_dkv[:, lat:],
                          jnp.zeros((D, pad), F32)], axis=1)
    wq = jnp.pad(w_q_b.reshape(MLA_Q_LORA, N_HEADS, MLA_NOPE + MLA_ROPE), ((0, 0), (0, 0), (0, pad)))
    kvr = w_kv_b.reshape(MLA_KV_LORA, N_HEADS, 2 * HALF)
    k_part, v_part = kvr[..., :HALF], kvr[..., HALF:]
    zero = jnp.zeros_like(v_part)
    wk = jnp.concatenate([k_part, zero], axis=-1)
    even = (jnp.arange(N_HEADS) % 2 == 0)[None, :, None]
    wv = jnp.where(even, jnp.concatenate([v_part, zero], -1), jnp.concatenate([zero, v_part], -1))
    wkv = jnp.concatenate([wk.reshape(MLA_KV_LORA, -1), wv.reshape(MLA_KV_LORA, -1)], axis=1)
    return wd.astype(BF16), wq.reshape(MLA_Q_LORA, -1).astype(BF16), wkv.astype(BF16)


GQA_STACKS = ((((0, 0), (1, 1)), 0), (((0, 0), (1, 1)), 1))
MLA_STACKS = ((((0, 0),), 0), (((1, 0),), 1))
NA_STACKS = ((((0, 0),), 0), (((0, 0),), 1))


def kernel(x, c, ctx, c_ctx,
           l0_ada_w, l0_ada_b, l0_mix_pre, l0_mix_post, l0_ffn_pre, l0_ffn_post,
           l0_w_qkv, l0_q_norm, l0_k_norm, l0_w_o, l0_ffn_w_gu, l0_ffn_w_down,
           l1_ada_w, l1_ada_b, l1_mix_pre, l1_mix_post, l1_ffn_pre, l1_ffn_post,
           l1_w_qkv, l1_sink, l1_w_o, l1_router, l1_moe_w_gu, l1_moe_w_down,
           l2_ada_w, l2_ada_b, l2_mix_pre, l2_mix_post, l2_ffn_pre, l2_ffn_post,
           l2_w_dkv, l2_q_a_norm, l2_w_q_b, l2_kv_a_norm, l2_w_kv_b, l2_w_o, l2_ffn_w_gu, l2_ffn_w_down,
           l3_ada_w, l3_ada_b, l3_mix_pre, l3_mix_post, l3_ffn_pre, l3_ffn_post,
           l3_w_qkv, l3_rpb, l3_w_o, l3_router, l3_moe_w_gu, l3_moe_w_down):
    vec = lambda a: a.reshape(1, -1)
    tile2 = lambda a: jnp.tile(a, 2).reshape(1, LANES)
    bf = lambda a: a.astype(BF16)
    pad_router = lambda w: jnp.pad(w, ((0, 0), (0, LANES - N_EXPERTS)))

    xs = jnp.concatenate([x.reshape(N_LAT, D), ctx.reshape(N_CTX, D)], axis=0)
    c_all = jnp.concatenate([c, c_ctx[None], jnp.zeros((8 - B - 1, D), F32)], axis=0)
    mods = [_ada(c_all, w, b).reshape(8, 1, 6 * D)
            for w, b in ((l0_ada_w, l0_ada_b), (l1_ada_w, l1_ada_b), (l2_ada_w, l2_ada_b), (l3_ada_w, l3_ada_b))]
    cos64, sin64 = _rope_tables(HEAD_DIM, 0, 0)
    cos32, sin32 = _rope_tables(MLA_ROPE, MLA_NOPE, LANES - MLA_NOPE - MLA_ROPE)
    ones = jnp.ones((1, LANES), F32)
    idx = np.arange(LANES)
    bd = jnp.asarray(idx[:, None] // HALF == idx[None, :] // HALF, BF16)

    q, k4, v4 = _qkv(xs, mods[0], vec(l0_mix_pre), bf(l0_w_qkv), cos64, sin64,
                     tile2(l0_q_norm), tile2(l0_k_norm), bd, shared_kv=True, head_norm=True, rope=True,
                     v_t=True, q_scale=HEAD_DIM ** -0.5 * LOG2E)
    o = _attention_t(q, k4, v4, stacks=GQA_STACKS, tq=TQ, subs=2)
    oc = _attention_t(q, k4, v4, stacks=GQA_STACKS, tq=CTX, ctx_queries=True)
    xs, h = _oproj(o, oc, bf(l0_w_o), xs, mods[0], vec(l0_mix_post), vec(l0_ffn_pre), None)
    xs = _ffn(h, bf(l0_ffn_w_gu), bf(l0_ffn_w_down), xs, mods[0], vec(l0_ffn_post))

    q, k4, v4 = _qkv(xs, mods[1], vec(l1_mix_pre), bf(l1_w_qkv), cos64, sin64,
                     ones, ones, bd, shared_kv=True, head_norm=False, rope=True,
                     v_t=False, q_scale=HEAD_DIM ** -0.5)
    o = _attention(q, k4, v4, stacks=GQA_STACKS, mode="window", tq=TQ, sink=l1_sink)
    oc = _attention(q, k4, v4, stacks=GQA_STACKS, mode="window", tq=CTX, sink=l1_sink, ctx_queries=True)
    xs, h, route = _oproj(o, oc, bf(l1_w_o), xs, mods[1], vec(l1_mix_post), vec(l1_ffn_pre),
                          pad_router(l1_router))
    xs = _moe(h, route, bf(l1_moe_w_gu), bf(l1_moe_w_down), xs, mods[1], vec(l1_ffn_post), N_TOK)

    wd, wq, wkv = _mla_weights(l2_w_dkv, l2_w_q_b, l2_w_kv_b)
    q, k4, v4 = _mla_proj(xs, mods[2], vec(l2_mix_pre), wd, vec(l2_q_a_norm), vec(l2_kv_a_norm),
                          wq, wkv, cos32, sin32)
    o = _attention_t(q, k4, v4, stacks=MLA_STACKS, tq=2 * TQ, subs=2)
    oc = _attention_t(q, k4, v4, stacks=MLA_STACKS, tq=CTX, ctx_queries=True)
    xs, h = _oproj(o, oc, bf(l2_w_o), xs, mods[2], vec(l2_mix_post), vec(l2_ffn_pre), None)
    xs = _ffn(h, bf(l2_ffn_w_gu), bf(l2_ffn_w_down), xs, mods[2], vec(l2_ffn_post))

    q, k4, v4 = _qkv(xs, mods[3], vec(l3_mix_pre), bf(l3_w_qkv), cos64, sin64,
                     ones, ones, bd, shared_kv=False, head_norm=False, rope=False,
                     v_t=False, q_scale=HEAD_DIM ** -0.5)
    o = _attention(q, k4, v4, stacks=NA_STACKS, mode="neighbourhood", tq=NA_QROWS * GRID_W,
                   table=_na_table(l3_rpb))
    xl, h, route = _oproj(o, None, bf(l3_w_o), xs, mods[3], vec(l3_mix_post), vec(l3_ffn_pre),
                          pad_router(l3_router))
    xl = _moe(h, route, bf(l3_moe_w_gu), bf(l3_moe_w_down), xl, mods[3], vec(l3_ffn_post), N_LAT)
    return xl.reshape(B, S, D)
```

```python
import functools

import numpy as np
import jax
import jax.numpy as jnp
from jax import lax
from jax.experimental import pallas as pl
from jax.experimental.pallas import tpu as pltpu

D = 1024
B = 4
S = 4096
CTX = 256
GRID_W = 64
N_LAT = B * S
N_CTX = B * CTX
N_TOK = N_LAT + N_CTX
HEAD_DIM = 64
N_HEADS = 16
N_KV_HEADS = 4
WINDOW = 128
MLA_Q_LORA = 384
MLA_KV_LORA = 256
MLA_NOPE = 64
MLA_ROPE = 32
NA_ROWS = 8
NA_COLS = 16
D_FF = 2816
N_EXPERTS = 8
D_FF_EXPERT = 1408
ROPE_THETA = 10000.0
EPS = 1e-6
NEG = -1e30
LOG2E = 1.4426950408889634

LANES = 128
HALF = LANES // 2
TM = 512
TQ = 256
TK = 512
TF = 1408
TME = 256
TMR = 256
NA_QROWS = 4
NA_BAND = NA_QROWS + NA_ROWS - 1
VMEM_LIMIT = 56 * 1024 * 1024

F32 = jnp.float32
BF16 = jnp.bfloat16

SH_M, SC_M, G_M, SH_F, SC_F, G_F = (slice(i * D, (i + 1) * D) for i in range(6))


def _params(*sem, flags=None):
    return pltpu.CompilerParams(dimension_semantics=sem, vmem_limit_bytes=VMEM_LIMIT, flags=flags)


def _mod_row(i):
    r0 = i * TM
    return jnp.where(r0 >= N_LAT, B, r0 // S)


def _rms(x, gain):
    return x * lax.rsqrt(jnp.mean(x * x, axis=-1, keepdims=True) + EPS) * gain


def _lane(shape):
    return lax.broadcasted_iota(jnp.int32, shape, len(shape) - 1)


def _ada_kernel(c_ref, w_ref, b_ref, o_ref):
    a = jax.nn.silu(c_ref[...])
    o_ref[...] = jnp.dot(a, w_ref[...], preferred_element_type=F32,
                         precision=lax.Precision.HIGHEST) + b_ref[...]


def _ada(c_all, w, b):
    tn = 1536
    return pl.pallas_call(
        _ada_kernel,
        out_shape=jax.ShapeDtypeStruct((8, 6 * D), F32),
        grid=(6 * D // tn,),
        in_specs=[pl.BlockSpec((8, D), lambda j: (0, 0)),
                  pl.BlockSpec((D, tn), lambda j: (0, j)),
                  pl.BlockSpec((1, tn), lambda j: (0, j))],
        out_specs=pl.BlockSpec((8, tn), lambda j: (0, j)),
        compiler_params=_params("arbitrary"),
        name="ada",
    )(c_all, w, b.reshape(1, 6 * D))


def _rope_chunk(t, cos, sin, half):
    lane = _lane(t.shape)
    rot = jnp.where((lane % (2 * half)) < half,
                    pltpu.roll(t, LANES - half, 1), pltpu.roll(t, half, 1))
    return t * cos + rot * sin


def _head_rms(t, gain, bd):
    t2 = t * t
    hi = t2.astype(BF16)
    lo = (t2 - hi.astype(F32)).astype(BF16)
    ss = (jnp.dot(hi, bd, preferred_element_type=F32)
          + jnp.dot(lo, bd, preferred_element_type=F32))
    return t * lax.rsqrt(ss * (1.0 / HEAD_DIM) + EPS) * gain


def _with_ones(v, lane, parity):
    if parity == 0:
        return jnp.where(lane < HALF, v, jnp.where(lane == LANES - 1, 1.0, 0.0))
    return jnp.where(lane >= HALF, v, jnp.where(lane == 0, 1.0, 0.0))


def _half_only(k, lane, parity):
    return jnp.where(lane < HALF, k, 0.0) if parity == 0 else jnp.where(lane >= HALF, k, 0.0)


def _put_v(v_ref, j, parity, v, v_t):
    v_ref[j, parity] = (v.T if v_t else v).astype(BF16)


def _qkv_kernel(x_ref, mod_ref, g_ref, w_ref, cos_ref, sin_ref, qg_ref, kg_ref, bd_ref,
                q_ref, k_ref, v_ref, *, shared_kv, head_norm, rope, v_t, q_scale):
    h = _rms(x_ref[...], g_ref[...]) * (1.0 + mod_ref[:, SC_M]) + mod_ref[:, SH_M]
    acc = jnp.dot(h.astype(BF16), w_ref[...], preferred_element_type=F32)
    lane = _lane((TM, LANES))
    n_q = N_HEADS * HEAD_DIM // LANES
    n_kv = (N_KV_HEADS if shared_kv else N_HEADS) * HEAD_DIM // LANES

    def finish(t, gain_ref):
        if head_norm:
            t = _head_rms(t, gain_ref[...], bd_ref[...])
        if rope:
            t = _rope_chunk(t, cos_ref[...], sin_ref[...], HEAD_DIM // 2)
        return t

    for c in range(n_q):
        t = finish(acc[:, c * LANES:(c + 1) * LANES], qg_ref)
        q_ref[:, c * LANES:(c + 1) * LANES] = (t * q_scale).astype(BF16)
    for c in range(n_kv):
        k = finish(acc[:, (n_q + c) * LANES:(n_q + c + 1) * LANES], kg_ref)
        v = acc[:, (n_q + n_kv + c) * LANES:(n_q + n_kv + c + 1) * LANES]
        if shared_kv:
            k_sw = pltpu.roll(k, HALF, 1)
            v_sw = pltpu.roll(v, HALF, 1)
            for half in range(2):
                j = 2 * c + half
                for parity in range(2):
                    ksrc = k if parity == half else k_sw
                    vsrc = v if parity == half else v_sw
                    k_ref[j, parity] = _half_only(ksrc, lane, parity).astype(BF16)
                    _put_v(v_ref, j, parity, _with_ones(vsrc, lane, parity), v_t)
        else:
            for parity in range(2):
                k_ref[c, parity] = _half_only(k, lane, parity).astype(BF16)
                _put_v(v_ref, c, parity, _with_ones(v, lane, parity), v_t)


def _kv_out(groups, transposed):
    if transposed:
        return (pl.BlockSpec((groups, 2, LANES, TM), lambda i: (0, 0, 0, i)),
                jax.ShapeDtypeStruct((groups, 2, LANES, N_TOK), BF16))
    return (pl.BlockSpec((groups, 2, TM, LANES), lambda i: (0, 0, i, 0)),
            jax.ShapeDtypeStruct((groups, 2, N_TOK, LANES), BF16))


def _qkv(x, mod, gain, w, cos, sin, qg, kg, bd, *, shared_kv, head_norm, rope, v_t, q_scale):
    n_out = w.shape[1]
    groups = N_KV_HEADS if shared_kv else N_HEADS // 2
    row = lambda i: (i, 0)
    const = lambda i: (0, 0)
    kv_spec, kv_shape = _kv_out(groups, False)
    v_spec, v_shape = _kv_out(groups, v_t)
    return pl.pallas_call(
        functools.partial(_qkv_kernel, shared_kv=shared_kv, head_norm=head_norm, rope=rope,
                          v_t=v_t, q_scale=q_scale),
        out_shape=(jax.ShapeDtypeStruct((N_TOK, N_HEADS * HEAD_DIM), BF16), kv_shape, v_shape),
        grid=(N_TOK // TM,),
        in_specs=[pl.BlockSpec((TM, D), row),
                  pl.BlockSpec((None, 1, 6 * D), lambda i: (_mod_row(i), 0, 0)),
                  pl.BlockSpec((1, D), const),
                  pl.BlockSpec((D, n_out), const),
                  pl.BlockSpec((TM, LANES), row),
                  pl.BlockSpec((TM, LANES), row),
                  pl.BlockSpec((1, LANES), const),
                  pl.BlockSpec((1, LANES), const),
                  pl.BlockSpec((LANES, LANES), const)],
        out_specs=(pl.BlockSpec((TM, N_HEADS * HEAD_DIM), row), kv_spec, v_spec),
        compiler_params=_params("arbitrary"),
        name="qkv",
    )(x, mod, gain, w, cos, sin, qg, kg, bd)


def _mla_kernel(x_ref, mod_ref, g_ref, wd_ref, qan_ref, kvan_ref, wq_ref, wkv_ref,
                cos_ref, sin_ref, q_ref, k_ref, v_ref):
    h = _rms(x_ref[...], g_ref[...]) * (1.0 + mod_ref[:, SC_M]) + mod_ref[:, SH_M]
    lat = jnp.dot(h.astype(BF16), wd_ref[...], preferred_element_type=F32)
    c_q = _rms(lat[:, :MLA_Q_LORA], qan_ref[...])
    c_kv = _rms(lat[:, MLA_Q_LORA:MLA_Q_LORA + MLA_KV_LORA], kvan_ref[...])
    cos = cos_ref[...]
    sin = sin_ref[...]
    k_rope = _rope_chunk(lat[:, MLA_Q_LORA + MLA_KV_LORA:], cos, sin, MLA_ROPE // 2)
    q = jnp.dot(c_q.astype(BF16), wq_ref[...], preferred_element_type=F32)
    kv = jnp.dot(c_kv.astype(BF16), wkv_ref[...], preferred_element_type=F32)
    scale = (MLA_NOPE + MLA_ROPE) ** -0.5 * LOG2E
    lane = _lane((TM, LANES))
    for hd in range(N_HEADS):
        sl = slice(hd * LANES, (hd + 1) * LANES)
        q_ref[:, sl] = (_rope_chunk(q[:, sl], cos, sin, MLA_ROPE // 2) * scale).astype(BF16)
        k_ref[hd // 2, hd % 2] = (kv[:, sl] + k_rope).astype(BF16)
        v = kv[:, N_HEADS * LANES + hd * LANES:N_HEADS * LANES + (hd + 1) * LANES]
        _put_v(v_ref, hd // 2, hd % 2, _with_ones(v, lane, hd % 2), True)


def _mla_proj(x, mod, gain, wd, qan, kvan, wq, wkv, cos, sin):
    row = lambda i: (i, 0)
    const = lambda i: (0, 0)
    kv_spec, kv_shape = _kv_out(N_HEADS // 2, False)
    v_spec, v_shape = _kv_out(N_HEADS // 2, True)
    return pl.pallas_call(
        _mla_kernel,
        out_shape=(jax.ShapeDtypeStruct((N_TOK, N_HEADS * LANES), BF16), kv_shape, v_shape),
        grid=(N_TOK // TM,),
        in_specs=[pl.BlockSpec((TM, D), row),
                  pl.BlockSpec((None, 1, 6 * D), lambda i: (_mod_row(i), 0, 0)),
                  pl.BlockSpec((1, D), const),
                  pl.BlockSpec(wd.shape, const),
                  pl.BlockSpec((1, MLA_Q_LORA), const),
                  pl.BlockSpec((1, MLA_KV_LORA), const),
                  pl.BlockSpec(wq.shape, const),
                  pl.BlockSpec(wkv.shape, const),
                  pl.BlockSpec((TM, LANES), row),
                  pl.BlockSpec((TM, LANES), row)],
        out_specs=(pl.BlockSpec((TM, N_HEADS * LANES), row), kv_spec, v_spec),
        compiler_params=_params("arbitrary"),
        name="mla_proj",
    )(x, mod, gain, wd, qan, kvan, wq, wkv, cos, sin)


def _attn_kernel(*refs, stacks, tq, subs, gp, mode, with_lat):
    refs = list(refs)
    sink_ref = refs.pop(0) if mode == "window" else None
    q_ref, kc_ref, vc_ref = refs[:3]
    refs = refs[3:]
    if with_lat:
        kl_ref, vl_ref = refs[:2]
        refs = refs[2:]
    tab_ref = refs.pop(0) if mode == "neighbourhood" else None
    o_ref = refs[0]
    s_refs = refs[1:]
    g = pl.program_id(1)
    t = pl.program_id(2)
    qw = q_ref.shape[1] // gp
    ow = o_ref.shape[1] // gp
    m_rows = tq * len(stacks[0][0])

    chunks, masks = [], []
    for sub in range(subs):
        ch, mask = [("ctx", 0, CTX)], None
        if with_lat and mode == "window":
            span = tq + 2 * WINDOW
            q0 = (t * subs + sub) * tq
            start = pl.multiple_of(jnp.clip(q0 - WINDOW, 0, S - span), WINDOW)
            qi = q0 + lax.broadcasted_iota(jnp.int32, (m_rows, span), 0) % tq
            ki = start + lax.broadcasted_iota(jnp.int32, (m_rows, span), 1)
            mask = jnp.abs(qi - ki) <= WINDOW
            ch.append(("lat", start, span))
        elif with_lat:
            start = pl.multiple_of(_na_band_start(t * subs + sub) * GRID_W, GRID_W)
            ch.append(("lat", start, NA_BAND * GRID_W))
        chunks.append(ch)
        masks.append(mask)
    cols = [[sum(w for _, _, w in ch[:c]) for c in range(len(ch))] for ch in chunks]

    items = [(gi, sub, qcs, kv) for gi in range(gp) for sub in range(subs) for qcs, kv in stacks]
    qs, sinks = [], []
    for gi, sub, qcs, kv in items:
        qs.append(jnp.concatenate(
            [q_ref[sub * tq:(sub + 1) * tq, gi * qw + qc * LANES:gi * qw + (qc + 1) * LANES]
             for qc, _ in qcs], axis=0))
        sink_col = None
        if mode == "window":
            hd = [(g * gp + gi) * 2 * len(qcs) + 2 * qc + kv for qc, _ in qcs]
            row = lax.broadcasted_iota(jnp.int32, (m_rows, 1), 0)
            sink_col = jnp.where(row < tq, sink_ref[hd[0]], sink_ref[hd[-1]])
        sinks.append(sink_col)

    def score_stage(it):
        gi, sub, _, kv = items[it]
        s_ref = s_refs[it % len(s_refs)]
        m_fold, m_col = None, sinks[it]
        for (kind, st, width), col in zip(chunks[sub], cols[sub]):
            k = kc_ref[gi, kv] if kind == "ctx" else kl_ref[gi, kv, pl.ds(st, width), :]
            s = lax.dot_general(qs[it], k, (((1,), (1,)), ((), ())), preferred_element_type=F32)
            if kind == "lat" and tab_ref is not None:
                s = s + tab_ref[2 * gi + kv]
            if kind == "lat" and masks[sub] is not None:
                s = jnp.where(masks[sub], s, NEG)
            s_ref[0:m_rows, col:col + width] = s
            if width % LANES == 0:
                for j in range(width // LANES):
                    slab = s[:, j * LANES:(j + 1) * LANES]
                    m_fold = slab if m_fold is None else jnp.maximum(m_fold, slab)
            else:
                mc = jnp.max(s, axis=-1, keepdims=True)
                m_col = mc if m_col is None else jnp.maximum(m_col, mc)
        m = jnp.max(m_fold, axis=-1, keepdims=True)
        return m if m_col is None else jnp.maximum(m, m_col)

    def prob_stage(it, m):
        sub = items[it][1]
        s_ref = s_refs[it % len(s_refs)]
        return [jnp.exp(s_ref[0:m_rows, col:col + width] - m).astype(BF16)
                for (_, _, width), col in zip(chunks[sub], cols[sub])]

    def value_stage(it, ps, m):
        gi, sub, _, kv = items[it]
        acc = jnp.zeros((m_rows, LANES), F32)
        for (kind, st, width), p in zip(chunks[sub], ps):
            v = vc_ref[gi, kv] if kind == "ctx" else vl_ref[gi, kv, pl.ds(st, width), :]
            acc = acc + jnp.dot(p, v, preferred_element_type=F32)
        den = acc[:, LANES - 1:LANES] if kv == 0 else acc[:, 0:1]
        if sinks[it] is not None:
            den = den + jnp.exp(sinks[it] - m)
        lane = _lane(acc.shape)
        return jnp.where(lane < HALF if kv == 0 else lane >= HALF, acc / den, 0.0)

    n = len(items)
    ms, ps, outs = [None] * n, [None] * n, {}
    for stage in range(n + 2):
        if 0 <= stage - 1 < n:
            ps[stage - 1] = prob_stage(stage - 1, ms[stage - 1])
        if stage < n:
            ms[stage] = score_stage(stage)
        if 0 <= stage - 2 < n:
            gi, sub, qcs, _ = items[stage - 2]
            o = value_stage(stage - 2, ps[stage - 2], ms[stage - 2])
            for i, (_, oc) in enumerate(qcs):
                key = (sub, gi * (ow // LANES) + oc)
                part = o[i * tq:(i + 1) * tq]
                outs[key] = part if key not in outs else outs[key] + part
    for (sub, oc), o in outs.items():
        o_ref[sub * tq:(sub + 1) * tq, oc * LANES:(oc + 1) * LANES] = o.astype(o_ref.dtype)


def _attention(q, k4, v4, *, stacks, mode, tq, subs=1, gp=1, sink=None, table=None,
               ctx_queries=False):
    groups = k4.shape[0] // gp
    qw = q.shape[1] // groups
    ow = N_HEADS * HEAD_DIM // groups
    with_lat = not ctx_queries
    tqs = tq * subs
    nq = S // tqs if with_lat else CTX // tqs
    row0 = 0 if with_lat else N_LAT // tqs

    q_spec = pl.BlockSpec((tqs, qw), lambda b, g, t: (row0 + b * nq + t, g))
    ctx_spec = pl.BlockSpec((gp, 2, CTX, LANES), lambda b, g, t: (g, 0, N_LAT // CTX + b, 0))
    lat_spec = pl.BlockSpec((gp, 2, S, LANES), lambda b, g, t: (g, 0, b, 0))
    in_specs = [q_spec, ctx_spec, ctx_spec]
    args = [q, k4, v4]
    n_keys = CTX
    if with_lat:
        in_specs += [lat_spec, lat_spec]
        args += [k4, v4]
        n_keys += {"window": tq + 2 * WINDOW, "neighbourhood": NA_BAND * GRID_W}[mode]
    if mode == "window":
        in_specs = [pl.BlockSpec(memory_space=pltpu.SMEM)] + in_specs
        args = [sink] + args
    if mode == "neighbourhood":
        in_specs.append(pl.BlockSpec((None, 2 * gp, tq, NA_BAND * GRID_W),
                                     lambda b, g, t: (_na_variant(t), g, 0, 0)))
        args.append(table)
    m_rows = tq * max(len(qcs) for qcs, _ in stacks)
    return pl.pallas_call(
        functools.partial(_attn_kernel, stacks=stacks, tq=tq, subs=subs, gp=gp, mode=mode,
                          with_lat=with_lat),
        out_shape=jax.ShapeDtypeStruct((N_LAT if with_lat else N_CTX, N_HEADS * HEAD_DIM), BF16),
        grid=(B, groups, nq),
        in_specs=in_specs,
        out_specs=pl.BlockSpec((tqs, ow), lambda b, g, t: (b * nq + t, g)),
        scratch_shapes=[pltpu.VMEM((m_rows, n_keys), F32) for _ in range(2)],
        compiler_params=_params("arbitrary", "arbitrary", "arbitrary"),
        name="attn_" + mode + ("" if with_lat else "_ctx"),
    )(*args)


def _attn_t_kernel(*refs, stacks, tq, subs, with_lat):
    refs = list(refs)
    q_ref, kc_ref, vtc_ref = refs[:3]
    refs = refs[3:]
    if with_lat:
        kl_ref, vtl_ref = refs[:2]
        refs = refs[2:]
    o_ref = refs[0]
    s_refs = refs[1:]
    chunks = [("ctx", 0, CTX)]
    if with_lat:
        chunks += [("lat", c * TK, TK) for c in range(S // TK)]
    rows = [sum(w for _, _, w in chunks[:c]) for c in range(len(chunks))]
    items = [(sub, qcs, kv) for sub in range(subs) for qcs, kv in stacks]
    qs = [jnp.concatenate([q_ref[sub * tq:(sub + 1) * tq, qc * LANES:(qc + 1) * LANES]
                           for qc, _ in qcs], axis=0) for sub, qcs, _ in items]
    cols = qs[0].shape[0]

    def score_chunk(it, c, m8):
        kind, st, width = chunks[c]
        kv = items[it][2]
        k = kc_ref[kv] if kind == "ctx" else kl_ref[kv, st:st + width, :]
        s = lax.dot_general(k, qs[it], (((1,), (1,)), ((), ())), preferred_element_type=F32)
        s_refs[it % len(s_refs)][rows[c]:rows[c] + width, 0:cols] = s
        c8 = jnp.max(s.reshape(width // 8, 8, cols), axis=0)
        return c8 if m8 is None else jnp.maximum(m8, c8)

    def prob_chunk(it, c, m):
        width = chunks[c][2]
        s = s_refs[it % len(s_refs)][rows[c]:rows[c] + width, 0:cols]
        return jnp.exp2(s - m).astype(BF16)

    def value_chunk(it, c, p, acc):
        kind, st, width = chunks[c]
        kv = items[it][2]
        vt = vtc_ref[kv] if kind == "ctx" else vtl_ref[kv, :, st:st + width]
        return acc + jnp.dot(vt, p, preferred_element_type=F32)

    n = len(items)
    ms, ps, accs = [None] * n, [[] for _ in range(n)], []
    for stage in range(n + 2):
        m8 = None
        acc = jnp.zeros((LANES, cols), F32)
        for c in range(len(chunks)):
            if 0 <= stage - 1 < n:
                ps[stage - 1].append(prob_chunk(stage - 1, c, ms[stage - 1]))
            if stage < n:
                m8 = score_chunk(stage, c, m8)
            if 0 <= stage - 2 < n:
                acc = value_chunk(stage - 2, c, ps[stage - 2][c], acc)
        if stage < n:
            ms[stage] = jnp.max(m8, axis=0, keepdims=True)
        if 0 <= stage - 2 < n:
            accs.append(acc)

    outs = {}
    for (sub, qcs, kv), acc in zip(items, accs):
        den = acc[LANES - 1:LANES, :] if kv == 0 else acc[0:1, :]
        o = (acc / den).T
        lane = _lane(o.shape)
        o = jnp.where(lane < HALF if kv == 0 else lane >= HALF, o, 0.0)
        for i, (_, oc) in enumerate(qcs):
            part = o[i * tq:(i + 1) * tq]
            outs[sub, oc] = part if (sub, oc) not in outs else outs[sub, oc] + part
    for (sub, oc), o in outs.items():
        o_ref[sub * tq:(sub + 1) * tq, oc * LANES:(oc + 1) * LANES] = o.astype(o_ref.dtype)


def _attention_t(q, k4, vt4, *, stacks, tq, subs=1, ctx_queries=False):
    groups = k4.shape[0]
    qw = q.shape[1] // groups
    ow = N_HEADS * HEAD_DIM // groups
    with_lat = not ctx_queries
    tqs = tq * subs
    nq = S // tqs if with_lat else CTX // tqs
    row0 = 0 if with_lat else N_LAT // tqs
    in_specs = [pl.BlockSpec((tqs, qw), lambda b, g, t: (row0 + b * nq + t, g)),
                pl.BlockSpec((None, 2, CTX, LANES), lambda b, g, t: (g, 0, N_LAT // CTX + b, 0)),
                pl.BlockSpec((None, 2, LANES, CTX), lambda b, g, t: (g, 0, 0, N_LAT // CTX + b))]
    args = [q, k4, vt4]
    n_keys = CTX
    if with_lat:
        in_specs += [pl.BlockSpec((None, 2, S, LANES), lambda b, g, t: (g, 0, b, 0)),
                     pl.BlockSpec((None, 2, LANES, S), lambda b, g, t: (g, 0, 0, b))]
        args += [k4, vt4]
        n_keys += S
    m_rows = tq * max(len(qcs) for qcs, _ in stacks)
    return pl.pallas_call(
        functools.partial(_attn_t_kernel, stacks=stacks, tq=tq, subs=subs, with_lat=with_lat),
        out_shape=jax.ShapeDtypeStruct((N_LAT if with_lat else N_CTX, N_HEADS * HEAD_DIM), BF16),
        grid=(B, groups, nq),
        in_specs=in_specs,
        out_specs=pl.BlockSpec((tqs, ow), lambda b, g, t: (b * nq + t, g)),
        scratch_shapes=[pltpu.VMEM((n_keys, m_rows), F32) for _ in range(2)],
        compiler_params=_params("arbitrary", "arbitrary", "arbitrary"),
        name="attn_global" + ("" if with_lat else "_ctx"),
    )(*args)


def _na_band_start(t):
    return jnp.clip(t * NA_QROWS - NA_ROWS // 2, 0, S // GRID_W - NA_BAND)


def _na_tiles():
    rows = S // GRID_W
    tiles = [(t * NA_QROWS, min(max(t * NA_QROWS - NA_ROWS // 2, 0), rows - NA_BAND))
             for t in range(rows // NA_QROWS)]
    lo = [rb for rb in tiles if rb[0] - NA_ROWS // 2 < 0]
    hi = [rb for rb in tiles if rb[0] - NA_ROWS // 2 > rows - NA_BAND]
    return lo + [tiles[len(lo)]] + hi, len(lo), len(hi)


def _na_variant(t):
    _, n_lo, n_hi = _na_tiles()
    first_hi = S // GRID_W // NA_QROWS - n_hi
    return jnp.where(t < n_lo, t, jnp.where(t >= first_hi, t - first_hi + n_lo + 1, n_lo))


def _bias_cols_kernel(rpb_ref, sel_ref, o_ref):
    o_ref[...] = jnp.dot(rpb_ref[...], sel_ref[...], preferred_element_type=F32,
                         precision=lax.Precision.HIGHEST)


def _na_table(rpb):
    rows = S // GRID_W
    n_dr, n_dc = 2 * NA_ROWS - 1, 2 * NA_COLS - 1
    c = np.arange(GRID_W)[:, None]
    kc = np.arange(GRID_W)[None, :]
    cs = np.clip(c - NA_COLS // 2, 0, GRID_W - NA_COLS)
    col_ok = (kc >= cs) & (kc < cs + NA_COLS)
    sel = (np.arange(LANES)[:, None, None] == (kc - c + NA_COLS - 1)[None]) & col_ok[None]
    sel = jnp.asarray(sel.reshape(LANES, GRID_W * GRID_W), F32)
    rpb2 = jnp.pad(rpb.reshape(N_HEADS * n_dr, n_dc), ((0, 256 - N_HEADS * n_dr), (0, LANES - n_dc)))
    cols = pl.pallas_call(
        _bias_cols_kernel,
        out_shape=jax.ShapeDtypeStruct((256, GRID_W * GRID_W), F32),
        name="na_bias_cols",
    )(rpb2, sel)
    cols = cols[:N_HEADS * n_dr].reshape(N_HEADS, n_dr, GRID_W, GRID_W)
    cols = jnp.where(col_ok[None, None], cols, NEG)
    masked = jnp.full((N_HEADS, GRID_W, GRID_W), NEG, F32)
    variants = []
    for r0, bs in _na_tiles()[0]:
        q_rows = []
        for r in range(r0, r0 + NA_QROWS):
            rs = min(max(r - NA_ROWS // 2, 0), rows - NA_ROWS)
            q_rows.append(jnp.concatenate(
                [cols[:, kr - r + NA_ROWS - 1] if rs <= kr < rs + NA_ROWS else masked
                 for kr in range(bs, bs + NA_BAND)], axis=2))
        variants.append(jnp.concatenate(q_rows, axis=1))
    return jnp.stack(variants)


def _router_route(h, wr_ref):
    logits = jnp.dot(h, wr_ref[...], preferred_element_type=F32, precision=lax.Precision.HIGHEST)
    lane = _lane(logits.shape).astype(F32)
    lg = jnp.where(lane < N_EXPERTS, logits, -jnp.inf)
    m1 = jnp.max(lg, axis=-1, keepdims=True)
    i1 = jnp.min(jnp.where(lg == m1, lane, float(LANES)), axis=-1, keepdims=True)
    lg2 = jnp.where(lane == i1, -jnp.inf, lg)
    m2 = jnp.max(lg2, axis=-1, keepdims=True)
    i2 = jnp.min(jnp.where(lg2 == m2, lane, float(LANES)), axis=-1, keepdims=True)
    e = jnp.exp(m2 - m1)
    return jnp.where(lane == 0, i1, jnp.where(lane == 1, i2, jnp.where(
        lane == 2, 1.0 / (1.0 + e), jnp.where(lane == 3, e / (1.0 + e), 0.0))))


def _oproj_kernel(*refs, moe, with_ctx):
    refs = list(refs)
    o_ref = refs.pop(0)
    oc_ref = refs.pop(0) if with_ctx else None
    if moe:
        wo_ref, x_ref, mod_ref, post_ref, pre_ref, wr_ref, xo_ref, h_ref, route_ref = refs
    else:
        wo_ref, x_ref, mod_ref, post_ref, pre_ref, xo_ref, h_ref = refs
    o = o_ref[...]
    if with_ctx:
        o = jnp.where(pl.program_id(0) < N_LAT // TM, o, oc_ref[...])
    y = jnp.dot(o, wo_ref[...], preferred_element_type=F32)
    x1 = x_ref[...] + mod_ref[:, G_M] * _rms(y, post_ref[...])
    xo_ref[...] = x1
    h = _rms(x1, pre_ref[...]) * (1.0 + mod_ref[:, SC_F]) + mod_ref[:, SH_F]
    h_ref[...] = h.astype(h_ref.dtype)
    if moe:
        route_ref[...] = _router_route(h, wr_ref)


def _oproj(o, o_ctx, wo, x, mod, post, pre, w_router):
    moe = w_router is not None
    with_ctx = o_ctx is not None
    n_rows = N_TOK if with_ctx else N_LAT
    n_lat = N_LAT // TM
    row = lambda i: (i, 0)
    const = lambda i: (0, 0)
    in_specs = [pl.BlockSpec((TM, D), lambda i: (jnp.minimum(i, n_lat - 1), 0))]
    args = [o]
    if with_ctx:
        in_specs.append(pl.BlockSpec((TM, D), lambda i: (jnp.maximum(i - n_lat, 0), 0)))
        args.append(o_ctx)
    in_specs += [pl.BlockSpec((D, D), const), pl.BlockSpec((TM, D), row),
                 pl.BlockSpec((None, 1, 6 * D), lambda i: (_mod_row(i), 0, 0)),
                 pl.BlockSpec((1, D), const), pl.BlockSpec((1, D), const)]
    args += [wo, x, mod, post, pre]
    out_shape = [jax.ShapeDtypeStruct((n_rows, D), F32),
                 jax.ShapeDtypeStruct((n_rows, D), F32 if moe else BF16)]
    out_specs = [pl.BlockSpec((TM, D), row), pl.BlockSpec((TM, D), row)]
    if moe:
        in_specs.append(pl.BlockSpec((D, LANES), const))
        out_shape.append(jax.ShapeDtypeStruct((n_rows, LANES), F32))
        out_specs.append(pl.BlockSpec((TM, LANES), row))
        args.append(w_router)
    return pl.pallas_call(
        functools.partial(_oproj_kernel, moe=moe, with_ctx=with_ctx),
        out_shape=tuple(out_shape),
        grid=(n_rows // TM,),
        in_specs=in_specs,
        out_specs=tuple(out_specs),
        compiler_params=_params("arbitrary"),
        name="oproj",
    )(*args)


def _swiglu(h, gw_ref, uw_ref, dw_ref):
    gate = jnp.dot(h, gw_ref[...], preferred_element_type=F32)
    up = jnp.dot(h, uw_ref[...], preferred_element_type=F32)
    return jnp.dot((jax.nn.silu(gate) * up).astype(BF16), dw_ref[...], preferred_element_type=F32)


def _ffn_kernel(h_ref, gw_ref, uw_ref, dw_ref, x_ref, mod_ref, post_ref, o_ref, acc_ref):
    j = pl.program_id(1)

    @pl.when(j == 0)
    def _():
        acc_ref[...] = jnp.zeros_like(acc_ref)

    acc_ref[...] += _swiglu(h_ref[...], gw_ref, uw_ref, dw_ref)

    @pl.when(j == pl.num_programs(1) - 1)
    def _():
        o_ref[...] = x_ref[...] + mod_ref[:, G_F] * _rms(acc_ref[...], post_ref[...])


def _ffn(h, w_gu, w_down, x, mod, post):
    row = lambda i, j: (i, 0)
    n_steps = D_FF // TF
    return pl.pallas_call(
        _ffn_kernel,
        out_shape=jax.ShapeDtypeStruct((N_TOK, D), F32),
        grid=(N_TOK // TM, n_steps),
        in_specs=[pl.BlockSpec((TM, D), row),
                  pl.BlockSpec((D, TF), lambda i, j: (0, j)),
                  pl.BlockSpec((D, TF), lambda i, j: (0, n_steps + j)),
                  pl.BlockSpec((TF, D), lambda i, j: (j, 0)),
                  pl.BlockSpec((TM, D), row),
                  pl.BlockSpec((None, 1, 6 * D), lambda i, j: (_mod_row(i), 0, 0)),
                  pl.BlockSpec((1, D), lambda i, j: (0, 0))],
        out_specs=pl.BlockSpec((TM, D), row),
        scratch_shapes=[pltpu.VMEM((TM, D), F32)],
        compiler_params=_params("arbitrary", "arbitrary"),
        name="ffn_dense",
    )(h, w_gu, w_gu, w_down, x, mod, post)


def _routing(route, n_rows):
    i1 = route[:, 0].astype(jnp.int32)
    i2 = route[:, 1].astype(jnp.int32)
    e = jnp.arange(N_EXPERTS, dtype=jnp.int32)
    hit1 = i1[:, None] == e
    hit2 = i2[:, None] == e
    sel = hit1.astype(jnp.int32) + hit2.astype(jnp.int32)
    cum = jnp.cumsum(sel, axis=0)
    tiles = (cum[-1] + TME - 1) // TME
    tile_end = jnp.cumsum(tiles)
    slot = ((tile_end - tiles) * TME)[None] + cum - sel
    slots = jnp.stack([jnp.sum(jnp.where(hit1, slot, 0), axis=1),
                       jnp.sum(jnp.where(hit2, slot, 0), axis=1)], axis=1).reshape(-1)
    j = jnp.arange(_n_slot_tiles(n_rows), dtype=jnp.int32)
    tile_expert = jnp.minimum(jnp.sum(j[:, None] >= tile_end[None], axis=1), N_EXPERTS - 1)
    return slots.astype(jnp.int32), tile_expert.astype(jnp.int32), tile_end[-1:].astype(jnp.int32)


def _n_slot_tiles(n_rows):
    return 2 * n_rows // TME + N_EXPERTS


def _row_copies(slots_ref, n, copy):
    def start(r, carry):
        for k in range(2):
            copy(r, k, slots_ref[2 * r + k]).start(priority=k)
        return carry

    def wait(r, carry):
        for k in range(2):
            copy(r, k, slots_ref[2 * r + k]).wait()
        return carry

    lax.fori_loop(0, n, start, 0, unroll=8)
    lax.fori_loop(0, n, wait, 0, unroll=8)


def _dispatch_kernel(slots_ref, h_ref, init_ref, xs_ref, sem):
    del init_ref
    _row_copies(slots_ref, TMR, lambda r, k, s: pltpu.make_async_copy(
        h_ref.at[pl.ds(r, 1)], xs_ref.at[pl.ds(s, 1)], sem))


def _dispatch(slots, h, n_rows):
    n_slots = _n_slot_tiles(n_rows) * TME
    return pl.pallas_call(
        _dispatch_kernel,
        out_shape=jax.ShapeDtypeStruct((n_slots, D), F32),
        grid=(n_rows // TMR,),
        in_specs=[pl.BlockSpec((2 * TMR,), lambda i: (i,), memory_space=pltpu.SMEM),
                  pl.BlockSpec((TMR, D), lambda i: (i, 0)),
                  pl.BlockSpec(memory_space=pl.ANY)],
        out_specs=pl.BlockSpec(memory_space=pl.ANY),
        scratch_shapes=[pltpu.SemaphoreType.DMA(())],
        input_output_aliases={2: 0},
        compiler_params=_params("arbitrary"),
        name="moe_dispatch",
    )(slots, h, jnp.zeros((n_slots, D), F32))


def _expert_kernel(te_ref, nv_ref, x_ref, gw_ref, uw_ref, dw_ref, o_ref):
    del te_ref
    valid = pl.program_id(0) < nv_ref[0]

    @pl.when(valid)
    def _():
        o_ref[...] = _swiglu(x_ref[...].astype(BF16), gw_ref, uw_ref, dw_ref)

    @pl.when(jnp.logical_not(valid))
    def _():
        o_ref[...] = jnp.zeros_like(o_ref)


def _experts(tile_expert, n_valid, xs, w_gu, w_down):
    n_tiles = xs.shape[0] // TME
    return pl.pallas_call(
        _expert_kernel,
        out_shape=jax.ShapeDtypeStruct(xs.shape, F32),
        grid_spec=pltpu.PrefetchScalarGridSpec(
            num_scalar_prefetch=2,
            grid=(n_tiles,),
            in_specs=[pl.BlockSpec((TME, D), lambda j, te, nv: (j, 0)),
                      pl.BlockSpec((None, D, TF), lambda j, te, nv: (te[j], 0, 0)),
                      pl.BlockSpec((None, D, TF), lambda j, te, nv: (te[j], 0, 1)),
                      pl.BlockSpec((None, TF, D), lambda j, te, nv: (te[j], 0, 0))],
            out_specs=pl.BlockSpec((TME, D), lambda j, te, nv: (j, 0))),
        compiler_params=_params("arbitrary"),
        name="moe_experts",
    )(tile_expert, n_valid, xs, w_gu, w_gu, w_down)


def _combine_kernel(slots_ref, route_ref, x_ref, mod_ref, post_ref, ys_ref, o_ref, buf_ref, sem):
    _row_copies(slots_ref, TMR, lambda r, k, s: pltpu.make_async_copy(
        ys_ref.at[pl.ds(s, 1)], buf_ref.at[k, pl.ds(r, 1)], sem))
    y = route_ref[:, 2:3] * buf_ref[0] + route_ref[:, 3:4] * buf_ref[1]
    o_ref[...] = x_ref[...] + mod_ref[:, G_F] * _rms(y, post_ref[...])


def _combine(slots, route, x, mod, post, ys, n_rows):
    return pl.pallas_call(
        _combine_kernel,
        out_shape=jax.ShapeDtypeStruct((n_rows, D), F32),
        grid=(n_rows // TMR,),
        in_specs=[pl.BlockSpec((2 * TMR,), lambda i: (i,), memory_space=pltpu.SMEM),
                  pl.BlockSpec((TMR, LANES), lambda i: (i, 0)),
                  pl.BlockSpec((TMR, D), lambda i: (i, 0)),
                  pl.BlockSpec((None, 1, 6 * D), lambda i: (_mod_row(i * TMR // TM), 0, 0)),
                  pl.BlockSpec((1, D), lambda i: (0, 0)),
                  pl.BlockSpec(memory_space=pl.ANY)],
        out_specs=pl.BlockSpec((TMR, D), lambda i: (i, 0)),
        scratch_shapes=[pltpu.VMEM((2, TMR, D), F32), pltpu.SemaphoreType.DMA(())],
        compiler_params=_params("arbitrary"),
        name="moe_combine",
    )(slots, route, x, mod, post, ys)


def _moe(h, route, w_gu, w_down, x, mod, post, n_rows):
    slots, tile_expert, n_valid = _routing(route, n_rows)
    xs = _dispatch(slots, h, n_rows)
    ys = _experts(tile_expert, n_valid, xs, w_gu, w_down)
    return _combine(slots, route, x, mod, post, ys, n_rows)


def _rope_tables(rot_dim, lead, tail):
    t = jnp.arange(S, dtype=jnp.int32)
    row = (t // GRID_W).astype(F32)
    col = (t % GRID_W).astype(F32)
    quarter = rot_dim // 4
    inv = ROPE_THETA ** (-jnp.arange(quarter, dtype=F32) / quarter)
    ang = jnp.concatenate([row[:, None] * inv, col[:, None] * inv], axis=-1)
    cos, sin = jnp.cos(ang), jnp.sin(ang)
    reps = (LANES - lead - tail) // rot_dim
    cos_l = jnp.concatenate([jnp.ones((S, lead), F32)] + [cos, cos] * reps + [jnp.ones((S, tail), F32)], axis=1)
    sin_l = jnp.concatenate([jnp.zeros((S, lead), F32)] + [-sin, sin] * reps + [jnp.zeros((S, tail), F32)], axis=1)
    cos_all = jnp.concatenate([jnp.tile(cos_l, (B, 1)), jnp.ones((N_CTX, LANES), F32)], axis=0)
    sin_all = jnp.concatenate([jnp.tile(sin_l, (B, 1)), jnp.zeros((N_CTX, LANES), F32)], axis=0)
    return cos_all, sin_all


def _mla_weights(w_dkv, w_q_b, w_kv_b):
    lat = MLA_Q_LORA + MLA_KV_LORA
    pad = LANES - MLA_NOPE - MLA_ROPE
    wd = jnp.concatenate([w_dkv[:, :lat], jnp.zeros((D, MLA_NOPE), F32), w_dkv[:, lat:],
                          jnp.zeros((D, pad), F32)], axis=1)
    wq = jnp.pad(w_q_b.reshape(MLA_Q_LORA, N_HEADS, MLA_NOPE + MLA_ROPE), ((0, 0), (0, 0), (0, pad)))
    kvr = w_kv_b.reshape(MLA_KV_LORA, N_HEADS, 2 * HALF)
    k_part, v_part = kvr[..., :HALF], kvr[..., HALF:]
    zero = jnp.zeros_like(v_part)
    wk = jnp.concatenate([k_part, zero], axis=-1)
    even = (jnp.arange(N_HEADS) % 2 == 0)[None, :, None]
    wv = jnp.where(even, jnp.concatenate([v_part, zero], -1), jnp.concatenate([zero, v_part], -1))
    wkv = jnp.concatenate([wk.reshape(MLA_KV_LORA, -1), wv.reshape(MLA_KV_LORA, -1)], axis=1)
    return wd.astype(BF16), wq.reshape(MLA_Q_LORA, -1).astype(BF16), wkv.astype(BF16)


GQA_STACKS = ((((0, 0), (1, 1)), 0), (((0, 0), (1, 1)), 1))
MLA_STACKS = ((((0, 0),), 0), (((1, 0),), 1))
NA_STACKS = ((((0, 0),), 0), (((0, 0),), 1))


def kernel(x, c, ctx, c_ctx,
           l0_ada_w, l0_ada_b, l0_mix_pre, l0_mix_post, l0_ffn_pre, l0_ffn_post,
           l0_w_qkv, l0_q_norm, l0_k_norm, l0_w_o, l0_ffn_w_gu, l0_ffn_w_down,
           l1_ada_w, l1_ada_b, l1_mix_pre, l1_mix_post, l1_ffn_pre, l1_ffn_post,
           l1_w_qkv, l1_sink, l1_w_o, l1_router, l1_moe_w_gu, l1_moe_w_down,
           l2_ada_w, l2_ada_b, l2_mix_pre, l2_mix_post, l2_ffn_pre, l2_ffn_post,
           l2_w_dkv, l2_q_a_norm, l2_w_q_b, l2_kv_a_norm, l2_w_kv_b, l2_w_o, l2_ffn_w_gu, l2_ffn_w_down,
           l3_ada_w, l3_ada_b, l3_mix_pre, l3_mix_post, l3_ffn_pre, l3_ffn_post,
           l3_w_qkv, l3_rpb, l3_w_o, l3_router, l3_moe_w_gu, l3_moe_w_down):
    vec = lambda a: a.reshape(1, -1)
    tile2 = lambda a: jnp.tile(a, 2).reshape(1, LANES)
    bf = lambda a: a.astype(BF16)
    pad_router = lambda w: jnp.pad(w, ((0, 0), (0, LANES - N_EXPERTS)))

    xs = jnp.concatenate([x.reshape(N_LAT, D), ctx.reshape(N_CTX, D)], axis=0)
    c_all = jnp.concatenate([c, c_ctx[None], jnp.zeros((8 - B - 1, D), F32)], axis=0)
    mods = [_ada(c_all, w, b).reshape(8, 1, 6 * D)
            for w, b in ((l0_ada_w, l0_ada_b), (l1_ada_w, l1_ada_b), (l2_ada_w, l2_ada_b), (l3_ada_w, l3_ada_b))]
    cos64, sin64 = _rope_tables(HEAD_DIM, 0, 0)
    cos32, sin32 = _rope_tables(MLA_ROPE, MLA_NOPE, LANES - MLA_NOPE - MLA_ROPE)
    ones = jnp.ones((1, LANES), F32)
    idx = np.arange(LANES)
    bd = jnp.asarray(idx[:, None] // HALF == idx[None, :] // HALF, BF16)

    q, k4, v4 = _qkv(xs, mods[0], vec(l0_mix_pre), bf(l0_w_qkv), cos64, sin64,
                     tile2(l0_q_norm), tile2(l0_k_norm), bd, shared_kv=True, head_norm=True, rope=True,
                     v_t=True, q_scale=HEAD_DIM ** -0.5 * LOG2E)
    o = _attention_t(q, k4, v4, stacks=GQA_STACKS, tq=TQ, subs=2)
    oc = _attention_t(q, k4, v4, stacks=GQA_STACKS, tq=CTX, ctx_queries=True)
    xs, h = _oproj(o, oc, bf(l0_w_o), xs, mods[0], vec(l0_mix_post), vec(l0_ffn_pre), None)
    xs = _ffn(h, bf(l0_ffn_w_gu), bf(l0_ffn_w_down), xs, mods[0], vec(l0_ffn_post))

    q, k4, v4 = _qkv(xs, mods[1], vec(l1_mix_pre), bf(l1_w_qkv), cos64, sin64,
                     ones, ones, bd, shared_kv=True, head_norm=False, rope=True,
                     v_t=False, q_scale=HEAD_DIM ** -0.5)
    o = _attention(q, k4, v4, stacks=GQA_STACKS, mode="window", tq=TQ, subs=2, sink=l1_sink)
    oc = _attention(q, k4, v4, stacks=GQA_STACKS, mode="window", tq=CTX, sink=l1_sink, ctx_queries=True)
    xs, h, route = _oproj(o, oc, bf(l1_w_o), xs, mods[1], vec(l1_mix_post), vec(l1_ffn_pre),
                          pad_router(l1_router))
    xs = _moe(h, route, bf(l1_moe_w_gu), bf(l1_moe_w_down), xs, mods[1], vec(l1_ffn_post), N_TOK)

    wd, wq, wkv = _mla_weights(l2_w_dkv, l2_w_q_b, l2_w_kv_b)
    q, k4, v4 = _mla_proj(xs, mods[2], vec(l2_mix_pre), wd, vec(l2_q_a_norm), vec(l2_kv_a_norm),
                          wq, wkv, cos32, sin32)
    o = _attention_t(q, k4, v4, stacks=MLA_STACKS, tq=2 * TQ, subs=2)
    oc = _attention_t(q, k4, v4, stacks=MLA_STACKS, tq=CTX, ctx_queries=True)
    xs, h = _oproj(o, oc, bf(l2_w_o), xs, mods[2], vec(l2_mix_post), vec(l2_ffn_pre), None)
    xs = _ffn(h, bf(l2_ffn_w_gu), bf(l2_ffn_w_down), xs, mods[2], vec(l2_ffn_post))

    q, k4, v4 = _qkv(xs, mods[3], vec(l3_mix_pre), bf(l3_w_qkv), cos64, sin64,
                     ones, ones, bd, shared_kv=False, head_norm=False, rope=False,
                     v_t=False, q_scale=HEAD_DIM ** -0.5)
    o = _attention(q, k4, v4, stacks=NA_STACKS, mode="neighbourhood", tq=NA_QROWS * GRID_W, gp=2,
                   table=_na_table(l3_rpb))
    xl, h, route = _oproj(o, None, bf(l3_w_o), xs, mods[3], vec(l3_mix_post), vec(l3_ffn_pre),
                          pad_router(l3_router))
    xl = _moe(h, route, bf(l3_moe_w_gu), bf(l3_moe_w_down), xl, mods[3], vec(l3_ffn_post), N_LAT)
    return xl.reshape(B, S, D)
```

```python
import functools

import numpy as np
import jax
import jax.numpy as jnp
from jax import lax
from jax.experimental import pallas as pl
from jax.experimental.pallas import tpu as pltpu

D = 1024
B = 4
S = 4096
CTX = 256
GRID_W = 64
N_LAT = B * S
N_CTX = B * CTX
N_TOK = N_LAT + N_CTX
HEAD_DIM = 64
N_HEADS = 16
N_KV_HEADS = 4
WINDOW = 128
MLA_Q_LORA = 384
MLA_KV_LORA = 256
MLA_NOPE = 64
MLA_ROPE = 32
NA_ROWS = 8
NA_COLS = 16
D_FF = 2816
N_EXPERTS = 8
D_FF_EXPERT = 1408
ROPE_THETA = 10000.0
EPS = 1e-6
NEG = -1e30
LOG2E = 1.4426950408889634

LANES = 128
HALF = LANES // 2
TM = 512
TQ = 256
TK = 512
TF = 1408
TME = 256
TMR = 256
NA_QROWS = 4
NA_BAND = NA_QROWS + NA_ROWS - 1
VMEM_LIMIT = 56 * 1024 * 1024

F32 = jnp.float32
BF16 = jnp.bfloat16

SH_M, SC_M, G_M, SH_F, SC_F, G_F = (slice(i * D, (i + 1) * D) for i in range(6))


def _params(*sem, flags=None):
    return pltpu.CompilerParams(dimension_semantics=sem, vmem_limit_bytes=VMEM_LIMIT, flags=flags)


def _mod_row(i):
    r0 = i * TM
    return jnp.where(r0 >= N_LAT, B, r0 // S)


def _rms(x, gain):
    return x * lax.rsqrt(jnp.mean(x * x, axis=-1, keepdims=True) + EPS) * gain


def _lane(shape):
    return lax.broadcasted_iota(jnp.int32, shape, len(shape) - 1)


def _ada_kernel(c_ref, w_ref, b_ref, o_ref):
    a = jax.nn.silu(c_ref[...])
    o_ref[...] = jnp.dot(a, w_ref[...], preferred_element_type=F32,
                         precision=lax.Precision.HIGHEST) + b_ref[...]


def _ada(c_all, w, b):
    tn = 1536
    return pl.pallas_call(
        _ada_kernel,
        out_shape=jax.ShapeDtypeStruct((8, 6 * D), F32),
        grid=(6 * D // tn,),
        in_specs=[pl.BlockSpec((8, D), lambda j: (0, 0)),
                  pl.BlockSpec((D, tn), lambda j: (0, j)),
                  pl.BlockSpec((1, tn), lambda j: (0, j))],
        out_specs=pl.BlockSpec((8, tn), lambda j: (0, j)),
        compiler_params=_params("arbitrary"),
        name="ada",
    )(c_all, w, b.reshape(1, 6 * D))


def _rope_chunk(t, cos, sin, half):
    lane = _lane(t.shape)
    rot = jnp.where((lane % (2 * half)) < half,
                    pltpu.roll(t, LANES - half, 1), pltpu.roll(t, half, 1))
    return t * cos + rot * sin


def _head_rms(t, gain, bd):
    t2 = t * t
    hi = t2.astype(BF16)
    lo = (t2 - hi.astype(F32)).astype(BF16)
    ss = (jnp.dot(hi, bd, preferred_element_type=F32)
          + jnp.dot(lo, bd, preferred_element_type=F32))
    return t * lax.rsqrt(ss * (1.0 / HEAD_DIM) + EPS) * gain


def _with_ones(v, lane, parity):
    if parity == 0:
        return jnp.where(lane < HALF, v, jnp.where(lane == LANES - 1, 1.0, 0.0))
    return jnp.where(lane >= HALF, v, jnp.where(lane == 0, 1.0, 0.0))


def _half_only(k, lane, parity):
    return jnp.where(lane < HALF, k, 0.0) if parity == 0 else jnp.where(lane >= HALF, k, 0.0)


def _put_v(v_ref, j, parity, v, v_t):
    v_ref[j, parity] = (v.T if v_t else v).astype(BF16)


def _qkv_kernel(x_ref, mod_ref, g_ref, w_ref, cos_ref, sin_ref, qg_ref, kg_ref, bd_ref,
                q_ref, k_ref, v_ref, *, shared_kv, head_norm, rope, v_t, q_scale):
    h = _rms(x_ref[...], g_ref[...]) * (1.0 + mod_ref[:, SC_M]) + mod_ref[:, SH_M]
    acc = jnp.dot(h.astype(BF16), w_ref[...], preferred_element_type=F32)
    lane = _lane((TM, LANES))
    n_q = N_HEADS * HEAD_DIM // LANES
    n_kv = (N_KV_HEADS if shared_kv else N_HEADS) * HEAD_DIM // LANES

    def finish(t, gain_ref):
        if head_norm:
            t = _head_rms(t, gain_ref[...], bd_ref[...])
        if rope:
            t = _rope_chunk(t, cos_ref[...], sin_ref[...], HEAD_DIM // 2)
        return t

    for c in range(n_q):
        t = finish(acc[:, c * LANES:(c + 1) * LANES], qg_ref)
        q_ref[:, c * LANES:(c + 1) * LANES] = (t * q_scale).astype(BF16)
    for c in range(n_kv):
        k = finish(acc[:, (n_q + c) * LANES:(n_q + c + 1) * LANES], kg_ref)
        v = acc[:, (n_q + n_kv + c) * LANES:(n_q + n_kv + c + 1) * LANES]
        if shared_kv:
            k_sw = pltpu.roll(k, HALF, 1)
            v_sw = pltpu.roll(v, HALF, 1)
            for half in range(2):
                j = 2 * c + half
                for parity in range(2):
                    ksrc = k if parity == half else k_sw
                    vsrc = v if parity == half else v_sw
                    k_ref[j, parity] = _half_only(ksrc, lane, parity).astype(BF16)
                    _put_v(v_ref, j, parity, _with_ones(vsrc, lane, parity), v_t)
        else:
            for parity in range(2):
                k_ref[c, parity] = _half_only(k, lane, parity).astype(BF16)
                _put_v(v_ref, c, parity, _with_ones(v, lane, parity), v_t)


def _kv_out(groups, transposed):
    if transposed:
        return (pl.BlockSpec((groups, 2, LANES, TM), lambda i: (0, 0, 0, i)),
                jax.ShapeDtypeStruct((groups, 2, LANES, N_TOK), BF16))
    return (pl.BlockSpec((groups, 2, TM, LANES), lambda i: (0, 0, i, 0)),
            jax.ShapeDtypeStruct((groups, 2, N_TOK, LANES), BF16))


def _qkv(x, mod, gain, w, cos, sin, qg, kg, bd, *, shared_kv, head_norm, rope, v_t, q_scale):
    n_out = w.shape[1]
    groups = N_KV_HEADS if shared_kv else N_HEADS // 2
    row = lambda i: (i, 0)
    const = lambda i: (0, 0)
    kv_spec, kv_shape = _kv_out(groups, False)
    v_spec, v_shape = _kv_out(groups, v_t)
    return pl.pallas_call(
        functools.partial(_qkv_kernel, shared_kv=shared_kv, head_norm=head_norm, rope=rope,
                          v_t=v_t, q_scale=q_scale),
        out_shape=(jax.ShapeDtypeStruct((N_TOK, N_HEADS * HEAD_DIM), BF16), kv_shape, v_shape),
        grid=(N_TOK // TM,),
        in_specs=[pl.BlockSpec((TM, D), row),
                  pl.BlockSpec((None, 1, 6 * D), lambda i: (_mod_row(i), 0, 0)),
                  pl.BlockSpec((1, D), const),
                  pl.BlockSpec((D, n_out), const),
                  pl.BlockSpec((TM, LANES), row),
                  pl.BlockSpec((TM, LANES), row),
                  pl.BlockSpec((1, LANES), const),
                  pl.BlockSpec((1, LANES), const),
                  pl.BlockSpec((LANES, LANES), const)],
        out_specs=(pl.BlockSpec((TM, N_HEADS * HEAD_DIM), row), kv_spec, v_spec),
        compiler_params=_params("arbitrary"),
        name="qkv",
    )(x, mod, gain, w, cos, sin, qg, kg, bd)


def _mla_kernel(x_ref, mod_ref, g_ref, wd_ref, qan_ref, kvan_ref, wq_ref, wkv_ref,
                cos_ref, sin_ref, q_ref, k_ref, v_ref):
    h = _rms(x_ref[...], g_ref[...]) * (1.0 + mod_ref[:, SC_M]) + mod_ref[:, SH_M]
    lat = jnp.dot(h.astype(BF16), wd_ref[...], preferred_element_type=F32)
    c_q = _rms(lat[:, :MLA_Q_LORA], qan_ref[...])
    c_kv = _rms(lat[:, MLA_Q_LORA:MLA_Q_LORA + MLA_KV_LORA], kvan_ref[...])
    cos = cos_ref[...]
    sin = sin_ref[...]
    k_rope = _rope_chunk(lat[:, MLA_Q_LORA + MLA_KV_LORA:], cos, sin, MLA_ROPE // 2)
    q = jnp.dot(c_q.astype(BF16), wq_ref[...], preferred_element_type=F32)
    kv = jnp.dot(c_kv.astype(BF16), wkv_ref[...], preferred_element_type=F32)
    scale = (MLA_NOPE + MLA_ROPE) ** -0.5 * LOG2E
    lane = _lane((TM, LANES))
    for hd in range(N_HEADS):
        sl = slice(hd * LANES, (hd + 1) * LANES)
        q_ref[:, sl] = (_rope_chunk(q[:, sl], cos, sin, MLA_ROPE // 2) * scale).astype(BF16)
        k_ref[hd // 2, hd % 2] = (kv[:, sl] + k_rope).astype(BF16)
        v = kv[:, N_HEADS * LANES + hd * LANES:N_HEADS * LANES + (hd + 1) * LANES]
        _put_v(v_ref, hd // 2, hd % 2, _with_ones(v, lane, hd % 2), True)


def _mla_proj(x, mod, gain, wd, qan, kvan, wq, wkv, cos, sin):
    row = lambda i: (i, 0)
    const = lambda i: (0, 0)
    kv_spec, kv_shape = _kv_out(N_HEADS // 2, False)
    v_spec, v_shape = _kv_out(N_HEADS // 2, True)
    return pl.pallas_call(
        _mla_kernel,
        out_shape=(jax.ShapeDtypeStruct((N_TOK, N_HEADS * LANES), BF16), kv_shape, v_shape),
        grid=(N_TOK // TM,),
        in_specs=[pl.BlockSpec((TM, D), row),
                  pl.BlockSpec((None, 1, 6 * D), lambda i: (_mod_row(i), 0, 0)),
                  pl.BlockSpec((1, D), const),
                  pl.BlockSpec(wd.shape, const),
                  pl.BlockSpec((1, MLA_Q_LORA), const),
                  pl.BlockSpec((1, MLA_KV_LORA), const),
                  pl.BlockSpec(wq.shape, const),
                  pl.BlockSpec(wkv.shape, const),
                  pl.BlockSpec((TM, LANES), row),
                  pl.BlockSpec((TM, LANES), row)],
        out_specs=(pl.BlockSpec((TM, N_HEADS * LANES), row), kv_spec, v_spec),
        compiler_params=_params("arbitrary"),
        name="mla_proj",
    )(x, mod, gain, wd, qan, kvan, wq, wkv, cos, sin)


def _attn_kernel(*refs, stacks, tq, subs, gp, mode, with_lat):
    refs = list(refs)
    sink_ref = refs.pop(0) if mode == "window" else None
    q_ref, kc_ref, vc_ref = refs[:3]
    refs = refs[3:]
    if with_lat:
        kl_ref, vl_ref = refs[:2]
        refs = refs[2:]
    tab_ref = refs.pop(0) if mode == "neighbourhood" else None
    o_ref = refs[0]
    s_refs = refs[1:]
    g = pl.program_id(1)
    t = pl.program_id(2)
    qw = q_ref.shape[1] // gp
    ow = o_ref.shape[1] // gp
    m_rows = tq * len(stacks[0][0])

    chunks, masks = [], []
    for sub in range(subs):
        ch, mask = [("ctx", 0, CTX)], None
        if with_lat and mode == "window":
            span = tq + 2 * WINDOW
            q0 = (t * subs + sub) * tq
            start = pl.multiple_of(jnp.clip(q0 - WINDOW, 0, S - span), WINDOW)
            qi = q0 + lax.broadcasted_iota(jnp.int32, (m_rows, span), 0) % tq
            ki = start + lax.broadcasted_iota(jnp.int32, (m_rows, span), 1)
            mask = jnp.abs(qi - ki) <= WINDOW
            ch.append(("lat", start, span))
        elif with_lat:
            start = pl.multiple_of(_na_band_start(t * subs + sub) * GRID_W, GRID_W)
            ch.append(("lat", start, NA_BAND * GRID_W))
        chunks.append(ch)
        masks.append(mask)
    cols = [[sum(w for _, _, w in ch[:c]) for c in range(len(ch))] for ch in chunks]

    items = [(gi, sub, qcs, kv) for gi in range(gp) for sub in range(subs) for qcs, kv in stacks]
    qs, sinks = [], []
    for gi, sub, qcs, kv in items:
        qs.append(jnp.concatenate(
            [q_ref[sub * tq:(sub + 1) * tq, gi * qw + qc * LANES:gi * qw + (qc + 1) * LANES]
             for qc, _ in qcs], axis=0))
        sink_col = None
        if mode == "window":
            hd = [(g * gp + gi) * 2 * len(qcs) + 2 * qc + kv for qc, _ in qcs]
            row = lax.broadcasted_iota(jnp.int32, (m_rows, 1), 0)
            sink_col = jnp.where(row < tq, sink_ref[hd[0]], sink_ref[hd[-1]])
        sinks.append(sink_col)

    def score_stage(it):
        gi, sub, _, kv = items[it]
        s_ref = s_refs[it % len(s_refs)]
        m_fold, m_col = None, sinks[it]
        for (kind, st, width), col in zip(chunks[sub], cols[sub]):
            k = kc_ref[gi, kv] if kind == "ctx" else kl_ref[gi, kv, pl.ds(st, width), :]
            s = lax.dot_general(qs[it], k, (((1,), (1,)), ((), ())), preferred_element_type=F32)
            if kind == "lat" and tab_ref is not None:
                s = s + tab_ref[2 * gi + kv]
            if kind == "lat" and masks[sub] is not None:
                s = jnp.where(masks[sub], s, NEG)
            s_ref[0:m_rows, col:col + width] = s
            if width % LANES == 0:
                for j in range(width // LANES):
                    slab = s[:, j * LANES:(j + 1) * LANES]
                    m_fold = slab if m_fold is None else jnp.maximum(m_fold, slab)
            else:
                mc = jnp.max(s, axis=-1, keepdims=True)
                m_col = mc if m_col is None else jnp.maximum(m_col, mc)
        m = jnp.max(m_fold, axis=-1, keepdims=True)
        return m if m_col is None else jnp.maximum(m, m_col)

    def prob_stage(it, m):
        sub = items[it][1]
        s_ref = s_refs[it % len(s_refs)]
        return [jnp.exp(s_ref[0:m_rows, col:col + width] - m).astype(BF16)
                for (_, _, width), col in zip(chunks[sub], cols[sub])]

    def value_stage(it, ps, m):
        gi, sub, _, kv = items[it]
        acc = jnp.zeros((m_rows, LANES), F32)
        for (kind, st, width), p in zip(chunks[sub], ps):
            v = vc_ref[gi, kv] if kind == "ctx" else vl_ref[gi, kv, pl.ds(st, width), :]
            acc = acc + jnp.dot(p, v, preferred_element_type=F32)
        den = acc[:, LANES - 1:LANES] if kv == 0 else acc[:, 0:1]
        if sinks[it] is not None:
            den = den + jnp.exp(sinks[it] - m)
        lane = _lane(acc.shape)
        return jnp.where(lane < HALF if kv == 0 else lane >= HALF, acc / den, 0.0)

    n = len(items)
    ms, ps, outs = [None] * n, [None] * n, {}
    for stage in range(n + 2):
        if 0 <= stage - 1 < n:
            ps[stage - 1] = prob_stage(stage - 1, ms[stage - 1])
        if stage < n:
            ms[stage] = score_stage(stage)
        if 0 <= stage - 2 < n:
            gi, sub, qcs, _ = items[stage - 2]
            o = value_stage(stage - 2, ps[stage - 2], ms[stage - 2])
            for i, (_, oc) in enumerate(qcs):
                key = (sub, gi * (ow // LANES) + oc)
                part = o[i * tq:(i + 1) * tq]
                outs[key] = part if key not in outs else outs[key] + part
    for (sub, oc), o in outs.items():
        o_ref[sub * tq:(sub + 1) * tq, oc * LANES:(oc + 1) * LANES] = o.astype(o_ref.dtype)


def _attention(q, k4, v4, *, stacks, mode, tq, subs=1, gp=1, sink=None, table=None,
               ctx_queries=False):
    groups = k4.shape[0] // gp
    qw = q.shape[1] // groups
    ow = N_HEADS * HEAD_DIM // groups
    with_lat = not ctx_queries
    tqs = tq * subs
    nq = S // tqs if with_lat else CTX // tqs
    row0 = 0 if with_lat else N_LAT // tqs

    q_spec = pl.BlockSpec((tqs, qw), lambda b, g, t: (row0 + b * nq + t, g))
    ctx_spec = pl.BlockSpec((gp, 2, CTX, LANES), lambda b, g, t: (g, 0, N_LAT // CTX + b, 0))
    lat_spec = pl.BlockSpec((gp, 2, S, LANES), lambda b, g, t: (g, 0, b, 0))
    in_specs = [q_spec, ctx_spec, ctx_spec]
    args = [q, k4, v4]
    n_keys = CTX
    if with_lat:
        in_specs += [lat_spec, lat_spec]
        args += [k4, v4]
        n_keys += {"window": tq + 2 * WINDOW, "neighbourhood": NA_BAND * GRID_W}[mode]
    if mode == "window":
        in_specs = [pl.BlockSpec(memory_space=pltpu.SMEM)] + in_specs
        args = [sink] + args
    if mode == "neighbourhood":
        in_specs.append(pl.BlockSpec((None, 2 * gp, tq, NA_BAND * GRID_W),
                                     lambda b, g, t: (_na_variant(t), g, 0, 0)))
        args.append(table)
    m_rows = tq * max(len(qcs) for qcs, _ in stacks)
    return pl.pallas_call(
        functools.partial(_attn_kernel, stacks=stacks, tq=tq, subs=subs, gp=gp, mode=mode,
                          with_lat=with_lat),
        out_shape=jax.ShapeDtypeStruct((N_LAT if with_lat else N_CTX, N_HEADS * HEAD_DIM), BF16),
        grid=(B, groups, nq),
        in_specs=in_specs,
        out_specs=pl.BlockSpec((tqs, ow), lambda b, g, t: (b * nq + t, g)),
        scratch_shapes=[pltpu.VMEM((m_rows, n_keys), F32) for _ in range(2)],
        compiler_params=_params("arbitrary", "arbitrary", "arbitrary"),
        name="attn_" + mode + ("" if with_lat else "_ctx"),
    )(*args)


def _attn_t_kernel(*refs, stacks, tq, subs, with_lat):
    refs = list(refs)
    q_ref, kc_ref, vtc_ref = refs[:3]
    refs = refs[3:]
    if with_lat:
        kl_ref, vtl_ref = refs[:2]
        refs = refs[2:]
    o_ref = refs[0]
    s_refs = refs[1:]
    chunks = [("ctx", 0, CTX)]
    if with_lat:
        chunks += [("lat", c * TK, TK) for c in range(S // TK)]
    rows = [sum(w for _, _, w in chunks[:c]) for c in range(len(chunks))]
    items = [(sub, qcs, kv) for sub in range(subs) for qcs, kv in stacks]
    qs = [jnp.concatenate([q_ref[sub * tq:(sub + 1) * tq, qc * LANES:(qc + 1) * LANES]
                           for qc, _ in qcs], axis=0) for sub, qcs, _ in items]
    cols = qs[0].shape[0]

    def score_chunk(it, c, m8):
        kind, st, width = chunks[c]
        kv = items[it][2]
        k = kc_ref[kv] if kind == "ctx" else kl_ref[kv, st:st + width, :]
        s = lax.dot_general(k, qs[it], (((1,), (1,)), ((), ())), preferred_element_type=F32)
        s_refs[it % len(s_refs)][rows[c]:rows[c] + width, 0:cols] = s
        c8 = jnp.max(s.reshape(width // 8, 8, cols), axis=0)
        return c8 if m8 is None else jnp.maximum(m8, c8)

    def prob_chunk(it, c, m):
        width = chunks[c][2]
        s = s_refs[it % len(s_refs)][rows[c]:rows[c] + width, 0:cols]
        return jnp.exp2(s - m).astype(BF16)

    def value_chunk(it, c, p, acc):
        kind, st, width = chunks[c]
        kv = items[it][2]
        vt = vtc_ref[kv] if kind == "ctx" else vtl_ref[kv, :, st:st + width]
        return acc + jnp.dot(vt, p, preferred_element_type=F32)

    n = len(items)
    ms, ps, accs = [None] * n, [[] for _ in range(n)], []
    for stage in range(n + 2):
        m8 = None
        acc = jnp.zeros((LANES, cols), F32)
        for c in range(len(chunks)):
            if 0 <= stage - 1 < n:
                ps[stage - 1].append(prob_chunk(stage - 1, c, ms[stage - 1]))
            if stage < n:
                m8 = score_chunk(stage, c, m8)
            if 0 <= stage - 2 < n:
                acc = value_chunk(stage - 2, c, ps[stage - 2][c], acc)
        if stage < n:
            ms[stage] = jnp.max(m8, axis=0, keepdims=True)
        if 0 <= stage - 2 < n:
            accs.append(acc)

    outs = {}
    for (sub, qcs, kv), acc in zip(items, accs):
        den = acc[LANES - 1:LANES, :] if kv == 0 else acc[0:1, :]
        o = (acc / den).T
        lane = _lane(o.shape)
        o = jnp.where(lane < HALF if kv == 0 else lane >= HALF, o, 0.0)
        for i, (_, oc) in enumerate(qcs):
            part = o[i * tq:(i + 1) * tq]
            outs[sub, oc] = part if (sub, oc) not in outs else outs[sub, oc] + part
    for (sub, oc), o in outs.items():
        o_ref[sub * tq:(sub + 1) * tq, oc * LANES:(oc + 1) * LANES] = o.astype(o_ref.dtype)


def _attention_t(q, k4, vt4, *, stacks, tq, subs=1, ctx_queries=False):
    groups = k4.shape[0]
    qw = q.shape[1] // groups
    ow = N_HEADS * HEAD_DIM // groups
    with_lat = not ctx_queries
    tqs = tq * subs
    nq = S // tqs if with_lat else CTX // tqs
    row0 = 0 if with_lat else N_LAT // tqs
    in_specs = [pl.BlockSpec((tqs, qw), lambda b, g, t: (row0 + b * nq + t, g)),
                pl.BlockSpec((None, 2, CTX, LANES), lambda b, g, t: (g, 0, N_LAT // CTX + b, 0)),
                pl.BlockSpec((None, 2, LANES, CTX), lambda b, g, t: (g, 0, 0, N_LAT // CTX + b))]
    args = [q, k4, vt4]
    n_keys = CTX
    if with_lat:
        in_specs += [pl.BlockSpec((None, 2, S, LANES), lambda b, g, t: (g, 0, b, 0)),
                     pl.BlockSpec((None, 2, LANES, S), lambda b, g, t: (g, 0, 0, b))]
        args += [k4, vt4]
        n_keys += S
    m_rows = tq * max(len(qcs) for qcs, _ in stacks)
    return pl.pallas_call(
        functools.partial(_attn_t_kernel, stacks=stacks, tq=tq, subs=subs, with_lat=with_lat),
        out_shape=jax.ShapeDtypeStruct((N_LAT if with_lat else N_CTX, N_HEADS * HEAD_DIM), BF16),
        grid=(B, groups, nq),
        in_specs=in_specs,
        out_specs=pl.BlockSpec((tqs, ow), lambda b, g, t: (b * nq + t, g)),
        scratch_shapes=[pltpu.VMEM((n_keys, m_rows), F32) for _ in range(2)],
        compiler_params=_params("arbitrary", "arbitrary", "arbitrary"),
        name="attn_global" + ("" if with_lat else "_ctx"),
    )(*args)


def _na_band_start(t):
    return jnp.clip(t * NA_QROWS - NA_ROWS // 2, 0, S // GRID_W - NA_BAND)


def _na_tiles():
    rows = S // GRID_W
    tiles = [(t * NA_QROWS, min(max(t * NA_QROWS - NA_ROWS // 2, 0), rows - NA_BAND))
             for t in range(rows // NA_QROWS)]
    lo = [rb for rb in tiles if rb[0] - NA_ROWS // 2 < 0]
    hi = [rb for rb in tiles if rb[0] - NA_ROWS // 2 > rows - NA_BAND]
    return lo + [tiles[len(lo)]] + hi, len(lo), len(hi)


def _na_variant(t):
    _, n_lo, n_hi = _na_tiles()
    first_hi = S // GRID_W // NA_QROWS - n_hi
    return jnp.where(t < n_lo, t, jnp.where(t >= first_hi, t - first_hi + n_lo + 1, n_lo))


def _bias_cols_kernel(rpb_ref, sel_ref, o_ref):
    o_ref[...] = jnp.dot(rpb_ref[...], sel_ref[...], preferred_element_type=F32,
                         precision=lax.Precision.HIGHEST)


def _na_table(rpb):
    rows = S // GRID_W
    n_dr, n_dc = 2 * NA_ROWS - 1, 2 * NA_COLS - 1
    c = np.arange(GRID_W)[:, None]
    kc = np.arange(GRID_W)[None, :]
    cs = np.clip(c - NA_COLS // 2, 0, GRID_W - NA_COLS)
    col_ok = (kc >= cs) & (kc < cs + NA_COLS)
    sel = (np.arange(LANES)[:, None, None] == (kc - c + NA_COLS - 1)[None]) & col_ok[None]
    sel = jnp.asarray(sel.reshape(LANES, GRID_W * GRID_W), F32)
    rpb2 = jnp.pad(rpb.reshape(N_HEADS * n_dr, n_dc), ((0, 256 - N_HEADS * n_dr), (0, LANES - n_dc)))
    cols = pl.pallas_call(
        _bias_cols_kernel,
        out_shape=jax.ShapeDtypeStruct((256, GRID_W * GRID_W), F32),
        name="na_bias_cols",
    )(rpb2, sel)
    cols = cols[:N_HEADS * n_dr].reshape(N_HEADS, n_dr, GRID_W, GRID_W)
    cols = jnp.where(col_ok[None, None], cols, NEG)
    masked = jnp.full((N_HEADS, GRID_W, GRID_W), NEG, F32)
    variants = []
    for r0, bs in _na_tiles()[0]:
        q_rows = []
        for r in range(r0, r0 + NA_QROWS):
            rs = min(max(r - NA_ROWS // 2, 0), rows - NA_ROWS)
            q_rows.append(jnp.concatenate(
                [cols[:, kr - r + NA_ROWS - 1] if rs <= kr < rs + NA_ROWS else masked
                 for kr in range(bs, bs + NA_BAND)], axis=2))
        variants.append(jnp.concatenate(q_rows, axis=1))
    return jnp.stack(variants)


def _router_route(h, wr_ref):
    w = wr_ref[...]
    h_hi, w_hi = h.astype(BF16), w.astype(BF16)
    h_lo = (h - h_hi.astype(F32)).astype(BF16)
    w_lo = (w - w_hi.astype(F32)).astype(BF16)
    logits = (jnp.dot(h_hi, w_hi, preferred_element_type=F32)
              + (jnp.dot(h_hi, w_lo, preferred_element_type=F32)
                 + jnp.dot(h_lo, w_hi, preferred_element_type=F32)))
    lane = _lane(logits.shape).astype(F32)
    lg = jnp.where(lane < N_EXPERTS, logits, -jnp.inf)
    m1 = jnp.max(lg, axis=-1, keepdims=True)
    i1 = jnp.min(jnp.where(lg == m1, lane, float(LANES)), axis=-1, keepdims=True)
    lg2 = jnp.where(lane == i1, -jnp.inf, lg)
    m2 = jnp.max(lg2, axis=-1, keepdims=True)
    i2 = jnp.min(jnp.where(lg2 == m2, lane, float(LANES)), axis=-1, keepdims=True)
    e = jnp.exp(m2 - m1)
    return jnp.where(lane == 0, i1, jnp.where(lane == 1, i2, jnp.where(
        lane == 2, 1.0 / (1.0 + e), jnp.where(lane == 3, e / (1.0 + e), 0.0))))


def _oproj_kernel(*refs, moe, with_ctx):
    refs = list(refs)
    o_ref = refs.pop(0)
    oc_ref = refs.pop(0) if with_ctx else None
    if moe:
        wo_ref, x_ref, mod_ref, post_ref, pre_ref, wr_ref, xo_ref, h_ref, route_ref = refs
    else:
        wo_ref, x_ref, mod_ref, post_ref, pre_ref, xo_ref, h_ref = refs
    o = o_ref[...]
    if with_ctx:
        o = jnp.where(pl.program_id(0) < N_LAT // TM, o, oc_ref[...])
    y = jnp.dot(o, wo_ref[...], preferred_element_type=F32)
    x1 = x_ref[...] + mod_ref[:, G_M] * _rms(y, post_ref[...])
    xo_ref[...] = x1
    h = _rms(x1, pre_ref[...]) * (1.0 + mod_ref[:, SC_F]) + mod_ref[:, SH_F]
    h_ref[...] = h.astype(h_ref.dtype)
    if moe:
        route_ref[...] = _router_route(h, wr_ref)


def _oproj(o, o_ctx, wo, x, mod, post, pre, w_router):
    moe = w_router is not None
    with_ctx = o_ctx is not None
    n_rows = N_TOK if with_ctx else N_LAT
    n_lat = N_LAT // TM
    row = lambda i: (i, 0)
    const = lambda i: (0, 0)
    in_specs = [pl.BlockSpec((TM, D), lambda i: (jnp.minimum(i, n_lat - 1), 0))]
    args = [o]
    if with_ctx:
        in_specs.append(pl.BlockSpec((TM, D), lambda i: (jnp.maximum(i - n_lat, 0), 0)))
        args.append(o_ctx)
    in_specs += [pl.BlockSpec((D, D), const), pl.BlockSpec((TM, D), row),
                 pl.BlockSpec((None, 1, 6 * D), lambda i: (_mod_row(i), 0, 0)),
                 pl.BlockSpec((1, D), const), pl.BlockSpec((1, D), const)]
    args += [wo, x, mod, post, pre]
    out_shape = [jax.ShapeDtypeStruct((n_rows, D), F32),
                 jax.ShapeDtypeStruct((n_rows, D), F32 if moe else BF16)]
    out_specs = [pl.BlockSpec((TM, D), row), pl.BlockSpec((TM, D), row)]
    if moe:
        in_specs.append(pl.BlockSpec((D, LANES), const))
        out_shape.append(jax.ShapeDtypeStruct((n_rows, LANES), F32))
        out_specs.append(pl.BlockSpec((TM, LANES), row))
        args.append(w_router)
    return pl.pallas_call(
        functools.partial(_oproj_kernel, moe=moe, with_ctx=with_ctx),
        out_shape=tuple(out_shape),
        grid=(n_rows // TM,),
        in_specs=in_specs,
        out_specs=tuple(out_specs),
        compiler_params=_params("arbitrary"),
        name="oproj",
    )(*args)


def _swiglu(h, gw_ref, uw_ref, dw_ref):
    gate = jnp.dot(h, gw_ref[...], preferred_element_type=F32)
    up = jnp.dot(h, uw_ref[...], preferred_element_type=F32)
    return jnp.dot((jax.nn.silu(gate) * up).astype(BF16), dw_ref[...], preferred_element_type=F32)


def _ffn_kernel(h_ref, gw_ref, uw_ref, dw_ref, x_ref, mod_ref, post_ref, o_ref, acc_ref):
    j = pl.program_id(1)

    @pl.when(j == 0)
    def _():
        acc_ref[...] = jnp.zeros_like(acc_ref)

    acc_ref[...] += _swiglu(h_ref[...], gw_ref, uw_ref, dw_ref)

    @pl.when(j == pl.num_programs(1) - 1)
    def _():
        o_ref[...] = x_ref[...] + mod_ref[:, G_F] * _rms(acc_ref[...], post_ref[...])


def _ffn(h, w_gu, w_down, x, mod, post):
    row = lambda i, j: (i, 0)
    n_steps = D_FF // TF
    return pl.pallas_call(
        _ffn_kernel,
        out_shape=jax.ShapeDtypeStruct((N_TOK, D), F32),
        grid=(N_TOK // TM, n_steps),
        in_specs=[pl.BlockSpec((TM, D), row),
                  pl.BlockSpec((D, TF), lambda i, j: (0, j)),
                  pl.BlockSpec((D, TF), lambda i, j: (0, n_steps + j)),
                  pl.BlockSpec((TF, D), lambda i, j: (j, 0)),
                  pl.BlockSpec((TM, D), row),
                  pl.BlockSpec((None, 1, 6 * D), lambda i, j: (_mod_row(i), 0, 0)),
                  pl.BlockSpec((1, D), lambda i, j: (0, 0))],
        out_specs=pl.BlockSpec((TM, D), row),
        scratch_shapes=[pltpu.VMEM((TM, D), F32)],
        compiler_params=_params("arbitrary", "arbitrary"),
        name="ffn_dense",
    )(h, w_gu, w_gu, w_down, x, mod, post)


def _routing(route, n_rows):
    i1 = route[:, 0].astype(jnp.int32)
    i2 = route[:, 1].astype(jnp.int32)
    e = jnp.arange(N_EXPERTS, dtype=jnp.int32)
    hit1 = i1[:, None] == e
    hit2 = i2[:, None] == e
    sel = hit1.astype(jnp.int32) + hit2.astype(jnp.int32)
    cum = jnp.cumsum(sel, axis=0)
    tiles = (cum[-1] + TME - 1) // TME
    tile_end = jnp.cumsum(tiles)
    slot = ((tile_end - tiles) * TME)[None] + cum - sel
    slots = jnp.stack([jnp.sum(jnp.where(hit1, slot, 0), axis=1),
                       jnp.sum(jnp.where(hit2, slot, 0), axis=1)], axis=1).reshape(-1)
    j = jnp.arange(_n_slot_tiles(n_rows), dtype=jnp.int32)
    tile_expert = jnp.minimum(jnp.sum(j[:, None] >= tile_end[None], axis=1), N_EXPERTS - 1)
    return slots.astype(jnp.int32), tile_expert.astype(jnp.int32), tile_end.astype(jnp.int32)


def _n_slot_tiles(n_rows):
    return 2 * n_rows // TME + N_EXPERTS


def _row_copies(slots_ref, n, copy):
    def start(r, carry):
        for k in range(2):
            copy(r, k, slots_ref[2 * r + k]).start(priority=k)
        return carry

    def wait(r, carry):
        for k in range(2):
            copy(r, k, slots_ref[2 * r + k]).wait()
        return carry

    lax.fori_loop(0, n, start, 0, unroll=8)
    lax.fori_loop(0, n, wait, 0, unroll=8)


def _dispatch_kernel(tend_ref, slots_ref, h_ref, xs_ref, zero_ref, zsem, sem, *, n_tiles):
    @pl.when(pl.program_id(0) == 0)
    def _():
        zero_ref[...] = jnp.zeros_like(zero_ref)
        tiles = []
        for e in range(N_EXPERTS):
            first = tend_ref[e - 1] if e else 0
            tiles.append((tend_ref[e] > first, tend_ref[e] - 1))
        for u in range(N_EXPERTS):
            tile = tend_ref[N_EXPERTS - 1] + u
            tiles.append((tile < n_tiles, tile))
        copies = [(ok, pltpu.make_async_copy(
            zero_ref, xs_ref.at[pl.ds(pl.multiple_of(jnp.maximum(tile, 0) * TME, TME), TME)], zsem))
            for ok, tile in tiles]
        for ok, cp in copies:
            pl.when(ok)(cp.start)
        for ok, cp in copies:
            pl.when(ok)(cp.wait)

    _row_copies(slots_ref, TMR, lambda r, k, s: pltpu.make_async_copy(
        h_ref.at[pl.ds(r, 1)], xs_ref.at[pl.ds(s, 1)], sem))


def _dispatch(tile_end, slots, h, n_rows):
    n_tiles = _n_slot_tiles(n_rows)
    return pl.pallas_call(
        functools.partial(_dispatch_kernel, n_tiles=n_tiles),
        out_shape=jax.ShapeDtypeStruct((n_tiles * TME, D), F32),
        grid_spec=pltpu.PrefetchScalarGridSpec(
            num_scalar_prefetch=1,
            grid=(n_rows // TMR,),
            in_specs=[pl.BlockSpec((2 * TMR,), lambda i, te: (i,), memory_space=pltpu.SMEM),
                      pl.BlockSpec((TMR, D), lambda i, te: (i, 0))],
            out_specs=pl.BlockSpec(memory_space=pl.ANY),
            scratch_shapes=[pltpu.VMEM((TME, D), F32), pltpu.SemaphoreType.DMA(()),
                            pltpu.SemaphoreType.DMA(())]),
        compiler_params=_params("arbitrary"),
        name="moe_dispatch",
    )(tile_end, slots, h)


def _expert_kernel(te_ref, tend_ref, x_ref, gw_ref, uw_ref, dw_ref, o_ref, gb_ref, ub_ref, db_ref):
    j = pl.program_id(0)
    valid = j < tend_ref[N_EXPERTS - 1]

    @pl.when(jnp.logical_or(j == 0, te_ref[j] != te_ref[jnp.maximum(j - 1, 0)]))
    def _():
        gb_ref[...] = gw_ref[...].astype(BF16)
        ub_ref[...] = uw_ref[...].astype(BF16)
        db_ref[...] = dw_ref[...].astype(BF16)

    @pl.when(valid)
    def _():
        o_ref[...] = _swiglu(x_ref[...].astype(BF16), gb_ref, ub_ref, db_ref)

    @pl.when(jnp.logical_not(valid))
    def _():
        o_ref[...] = jnp.zeros_like(o_ref)


def _experts(tile_expert, tile_end, xs, w_gu, w_down):
    n_tiles = xs.shape[0] // TME
    return pl.pallas_call(
        _expert_kernel,
        out_shape=jax.ShapeDtypeStruct(xs.shape, F32),
        grid_spec=pltpu.PrefetchScalarGridSpec(
            num_scalar_prefetch=2,
            grid=(n_tiles,),
            in_specs=[pl.BlockSpec((TME, D), lambda j, te, tend: (j, 0)),
                      pl.BlockSpec((None, D, TF), lambda j, te, tend: (te[j], 0, 0)),
                      pl.BlockSpec((None, D, TF), lambda j, te, tend: (te[j], 0, 1)),
                      pl.BlockSpec((None, TF, D), lambda j, te, tend: (te[j], 0, 0))],
            out_specs=pl.BlockSpec((TME, D), lambda j, te, tend: (j, 0)),
            scratch_shapes=[pltpu.VMEM((D, TF), BF16), pltpu.VMEM((D, TF), BF16),
                            pltpu.VMEM((TF, D), BF16)]),
        compiler_params=_params("arbitrary"),
        name="moe_experts",
    )(tile_expert, tile_end, xs, w_gu, w_gu, w_down)


def _combine_kernel(slots_ref, route_ref, x_ref, mod_ref, post_ref, ys_ref, o_ref, buf_ref, sem):
    _row_copies(slots_ref, TMR, lambda r, k, s: pltpu.make_async_copy(
        ys_ref.at[pl.ds(s, 1)], buf_ref.at[k, pl.ds(r, 1)], sem))
    y = route_ref[:, 2:3] * buf_ref[0] + route_ref[:, 3:4] * buf_ref[1]
    o_ref[...] = x_ref[...] + mod_ref[:, G_F] * _rms(y, post_ref[...])


def _combine(slots, route, x, mod, post, ys, n_rows):
    return pl.pallas_call(
        _combine_kernel,
        out_shape=jax.ShapeDtypeStruct((n_rows, D), F32),
        grid=(n_rows // TMR,),
        in_specs=[pl.BlockSpec((2 * TMR,), lambda i: (i,), memory_space=pltpu.SMEM),
                  pl.BlockSpec((TMR, LANES), lambda i: (i, 0)),
                  pl.BlockSpec((TMR, D), lambda i: (i, 0)),
                  pl.BlockSpec((None, 1, 6 * D), lambda i: (_mod_row(i * TMR // TM), 0, 0)),
                  pl.BlockSpec((1, D), lambda i: (0, 0)),
                  pl.BlockSpec(memory_space=pl.ANY)],
        out_specs=pl.BlockSpec((TMR, D), lambda i: (i, 0)),
        scratch_shapes=[pltpu.VMEM((2, TMR, D), F32), pltpu.SemaphoreType.DMA(())],
        compiler_params=_params("arbitrary"),
        name="moe_combine",
    )(slots, route, x, mod, post, ys)


def _moe(h, route, w_gu, w_down, x, mod, post, n_rows):
    slots, tile_expert, tile_end = _routing(route, n_rows)
    xs = _dispatch(tile_end, slots, h, n_rows)
    ys = _experts(tile_expert, tile_end, xs, w_gu, w_down)
    return _combine(slots, route, x, mod, post, ys, n_rows)


def _rope_tables(rot_dim, lead, tail):
    t = jnp.arange(S, dtype=jnp.int32)
    row = (t // GRID_W).astype(F32)
    col = (t % GRID_W).astype(F32)
    quarter = rot_dim // 4
    inv = ROPE_THETA ** (-jnp.arange(quarter, dtype=F32) / quarter)
    ang = jnp.concatenate([row[:, None] * inv, col[:, None] * inv], axis=-1)
    cos, sin = jnp.cos(ang), jnp.sin(ang)
    reps = (LANES - lead - tail) // rot_dim
    cos_l = jnp.concatenate([jnp.ones((S, lead), F32)] + [cos, cos] * reps + [jnp.ones((S, tail), F32)], axis=1)
    sin_l = jnp.concatenate([jnp.zeros((S, lead), F32)] + [-sin, sin] * reps + [jnp.zeros((S, tail), F32)], axis=1)
    cos_all = jnp.concatenate([jnp.tile(cos_l, (B, 1)), jnp.ones((N_CTX, LANES), F32)], axis=0)
    sin_all = jnp.concatenate([jnp.tile(sin_l, (B, 1)), jnp.zeros((N_CTX, LANES), F32)], axis=0)
    return cos_all, sin_all


def _mla_weights(w_dkv, w_q_b, w_kv_b):
    lat = MLA_Q_LORA + MLA_KV_LORA
    pad = LANES - MLA_NOPE - MLA_ROPE
    wd = jnp.concatenate([w_dkv[:, :lat], jnp.zeros((D, MLA_NOPE), F32), w_dkv[:, lat:],
                          jnp.zeros((D, pad), F32)], axis=1)
    wq = jnp.pad(w_q_b.reshape(MLA_Q_LORA, N_HEADS, MLA_NOPE + MLA_ROPE), ((0, 0), (0, 0), (0, pad)))
    kvr = w_kv_b.reshape(MLA_KV_LORA, N_HEADS, 2 * HALF)
    k_part, v_part = kvr[..., :HALF], kvr[..., HALF:]
    zero = jnp.zeros_like(v_part)
    wk = jnp.concatenate([k_part, zero], axis=-1)
    even = (jnp.arange(N_HEADS) % 2 == 0)[None, :, None]
    wv = jnp.where(even, jnp.concatenate([v_part, zero], -1), jnp.concatenate([zero, v_part], -1))
    wkv = jnp.concatenate([wk.reshape(MLA_KV_LORA, -1), wv.reshape(MLA_KV_LORA, -1)], axis=1)
    return wd.astype(BF16), wq.reshape(MLA_Q_LORA, -1).astype(BF16), wkv.astype(BF16)


GQA_STACKS = ((((0, 0), (1, 1)), 0), (((0, 0), (1, 1)), 1))
MLA_STACKS = ((((0, 0),), 0), (((1, 0),), 1))
NA_STACKS = ((((0, 0),), 0), (((0, 0),), 1))


def kernel(x, c, ctx, c_ctx,
           l0_ada_w, l0_ada_b, l0_mix_pre, l0_mix_post, l0_ffn_pre, l0_ffn_post,
           l0_w_qkv, l0_q_norm, l0_k_norm, l0_w_o, l0_ffn_w_gu, l0_ffn_w_down,
           l1_ada_w, l1_ada_b, l1_mix_pre, l1_mix_post, l1_ffn_pre, l1_ffn_post,
           l1_w_qkv, l1_sink, l1_w_o, l1_router, l1_moe_w_gu, l1_moe_w_down,
           l2_ada_w, l2_ada_b, l2_mix_pre, l2_mix_post, l2_ffn_pre, l2_ffn_post,
           l2_w_dkv, l2_q_a_norm, l2_w_q_b, l2_kv_a_norm, l2_w_kv_b, l2_w_o, l2_ffn_w_gu, l2_ffn_w_down,
           l3_ada_w, l3_ada_b, l3_mix_pre, l3_mix_post, l3_ffn_pre, l3_ffn_post,
           l3_w_qkv, l3_rpb, l3_w_o, l3_router, l3_moe_w_gu, l3_moe_w_down):
    vec = lambda a: a.reshape(1, -1)
    tile2 = lambda a: jnp.tile(a, 2).reshape(1, LANES)
    bf = lambda a: a.astype(BF16)
    pad_router = lambda w: jnp.pad(w, ((0, 0), (0, LANES - N_EXPERTS)))

    xs = jnp.concatenate([x.reshape(N_LAT, D), ctx.reshape(N_CTX, D)], axis=0)
    c_all = jnp.concatenate([c, c_ctx[None], jnp.zeros((8 - B - 1, D), F32)], axis=0)
    mods = [_ada(c_all, w, b).reshape(8, 1, 6 * D)
            for w, b in ((l0_ada_w, l0_ada_b), (l1_ada_w, l1_ada_b), (l2_ada_w, l2_ada_b), (l3_ada_w, l3_ada_b))]
    cos64, sin64 = _rope_tables(HEAD_DIM, 0, 0)
    cos32, sin32 = _rope_tables(MLA_ROPE, MLA_NOPE, LANES - MLA_NOPE - MLA_ROPE)
    ones = jnp.ones((1, LANES), F32)
    idx = np.arange(LANES)
    bd = jnp.asarray(idx[:, None] // HALF == idx[None, :] // HALF, BF16)

    q, k4, v4 = _qkv(xs, mods[0], vec(l0_mix_pre), bf(l0_w_qkv), cos64, sin64,
                     tile2(l0_q_norm), tile2(l0_k_norm), bd, shared_kv=True, head_norm=True, rope=True,
                     v_t=True, q_scale=HEAD_DIM ** -0.5 * LOG2E)
    o = _attention_t(q, k4, v4, stacks=GQA_STACKS, tq=TQ, subs=4)
    oc = _attention_t(q, k4, v4, stacks=GQA_STACKS, tq=CTX, ctx_queries=True)
    xs, h = _oproj(o, oc, bf(l0_w_o), xs, mods[0], vec(l0_mix_post), vec(l0_ffn_pre), None)
    xs = _ffn(h, bf(l0_ffn_w_gu), bf(l0_ffn_w_down), xs, mods[0], vec(l0_ffn_post))

    q, k4, v4 = _qkv(xs, mods[1], vec(l1_mix_pre), bf(l1_w_qkv), cos64, sin64,
                     ones, ones, bd, shared_kv=True, head_norm=False, rope=True,
                     v_t=False, q_scale=HEAD_DIM ** -0.5)
    o = _attention(q, k4, v4, stacks=GQA_STACKS, mode="window", tq=TQ, subs=2, sink=l1_sink)
    oc = _attention(q, k4, v4, stacks=GQA_STACKS, mode="window", tq=CTX, sink=l1_sink, ctx_queries=True)
    xs, h, route = _oproj(o, oc, bf(l1_w_o), xs, mods[1], vec(l1_mix_post), vec(l1_ffn_pre),
                          pad_router(l1_router))
    xs = _moe(h, route, l1_moe_w_gu, l1_moe_w_down, xs, mods[1], vec(l1_ffn_post), N_TOK)

    wd, wq, wkv = _mla_weights(l2_w_dkv, l2_w_q_b, l2_w_kv_b)
    q, k4, v4 = _mla_proj(xs, mods[2], vec(l2_mix_pre), wd, vec(l2_q_a_norm), vec(l2_kv_a_norm),
                          wq, wkv, cos32, sin32)
    o = _attention_t(q, k4, v4, stacks=MLA_STACKS, tq=2 * TQ, subs=4)
    oc = _attention_t(q, k4, v4, stacks=MLA_STACKS, tq=CTX, ctx_queries=True)
    xs, h = _oproj(o, oc, bf(l2_w_o), xs, mods[2], vec(l2_mix_post), vec(l2_ffn_pre), None)
    xs = _ffn(h, bf(l2_ffn_w_gu), bf(l2_ffn_w_down), xs, mods[2], vec(l2_ffn_post))

    q, k4, v4 = _qkv(xs, mods[3], vec(l3_mix_pre), bf(l3_w_qkv), cos64, sin64,
                     ones, ones, bd, shared_kv=False, head_norm=False, rope=False,
                     v_t=False, q_scale=HEAD_DIM ** -0.5)
    o = _attention(q, k4, v4, stacks=NA_STACKS, mode="neighbourhood", tq=NA_QROWS * GRID_W, gp=2,
                   table=_na_table(l3_rpb))
    xl, h, route = _oproj(o, None, bf(l3_w_o), xs, mods[3], vec(l3_mix_post), vec(l3_ffn_pre),
                          pad_router(l3_router))
    xl = _moe(h, route, l3_moe_w_gu, l3_moe_w_down, xl, mods[3], vec(l3_ffn_post), N_LAT)
    return xl.reshape(B, S, D)
```

```python
import functools

import numpy as np
import jax
import jax.numpy as jnp
from jax import lax
from jax.experimental import pallas as pl
from jax.experimental.pallas import tpu as pltpu

D = 1024
B = 4
S = 4096
CTX = 256
GRID_W = 64
N_LAT = B * S
N_CTX = B * CTX
N_TOK = N_LAT + N_CTX
HEAD_DIM = 64
N_HEADS = 16
N_KV_HEADS = 4
WINDOW = 128
MLA_Q_LORA = 384
MLA_KV_LORA = 256
MLA_NOPE = 64
MLA_ROPE = 32
NA_ROWS = 8
NA_COLS = 16
D_FF = 2816
N_EXPERTS = 8
D_FF_EXPERT = 1408
ROPE_THETA = 10000.0
EPS = 1e-6
NEG = -1e30
LOG2E = 1.4426950408889634

LANES = 128
HALF = LANES // 2
TM = 512
TQ = 256
TK = 512
TME = 256
TMR = 256
NA_QROWS = 4
NA_BAND = NA_QROWS + NA_ROWS - 1
VMEM_LIMIT = 56 * 1024 * 1024

F32 = jnp.float32
BF16 = jnp.bfloat16

SH_M, SC_M, G_M, SH_F, SC_F, G_F = (slice(i * D, (i + 1) * D) for i in range(6))


def _params(*sem, flags=None):
    return pltpu.CompilerParams(dimension_semantics=sem, vmem_limit_bytes=VMEM_LIMIT, flags=flags)


def _mod_row(i):
    r0 = i * TM
    return jnp.where(r0 >= N_LAT, B, r0 // S)


def _rms(x, gain):
    return x * lax.rsqrt(jnp.mean(x * x, axis=-1, keepdims=True) + EPS) * gain


def _lane(shape):
    return lax.broadcasted_iota(jnp.int32, shape, len(shape) - 1)


def _ada_kernel(c_ref, w_ref, b_ref, o_ref):
    a = jax.nn.silu(c_ref[...])
    o_ref[...] = jnp.dot(a, w_ref[...], preferred_element_type=F32,
                         precision=lax.Precision.HIGHEST) + b_ref[...]


def _ada(c_all, w, b):
    tn = 1536
    return pl.pallas_call(
        _ada_kernel,
        out_shape=jax.ShapeDtypeStruct((8, 6 * D), F32),
        grid=(6 * D // tn,),
        in_specs=[pl.BlockSpec((8, D), lambda j: (0, 0)),
                  pl.BlockSpec((D, tn), lambda j: (0, j)),
                  pl.BlockSpec((1, tn), lambda j: (0, j))],
        out_specs=pl.BlockSpec((8, tn), lambda j: (0, j)),
        compiler_params=_params("arbitrary"),
        name="ada",
    )(c_all, w, b.reshape(1, 6 * D))


def _rope_chunk(t, cos, sin, half):
    lane = _lane(t.shape)
    rot = jnp.where((lane % (2 * half)) < half,
                    pltpu.roll(t, LANES - half, 1), pltpu.roll(t, half, 1))
    return t * cos + rot * sin


def _head_rms(t, gain, bd):
    t2 = t * t
    hi = t2.astype(BF16)
    lo = (t2 - hi.astype(F32)).astype(BF16)
    ss = (jnp.dot(hi, bd, preferred_element_type=F32)
          + jnp.dot(lo, bd, preferred_element_type=F32))
    return t * lax.rsqrt(ss * (1.0 / HEAD_DIM) + EPS) * gain


def _with_ones(v, lane, parity):
    if parity == 0:
        return jnp.where(lane < HALF, v, jnp.where(lane == LANES - 1, 1.0, 0.0))
    return jnp.where(lane >= HALF, v, jnp.where(lane == 0, 1.0, 0.0))


def _half_only(k, lane, parity):
    return jnp.where(lane < HALF, k, 0.0) if parity == 0 else jnp.where(lane >= HALF, k, 0.0)


def _put_v(v_ref, j, parity, v, v_t):
    v_ref[j, parity] = (v.T if v_t else v).astype(BF16)


def _qkv_kernel(x_ref, mod_ref, g_ref, w_ref, cos_ref, sin_ref, qg_ref, kg_ref, bd_ref,
                q_ref, k_ref, v_ref, *, shared_kv, head_norm, rope, v_t, q_scale):
    h = _rms(x_ref[...], g_ref[...]) * (1.0 + mod_ref[:, SC_M]) + mod_ref[:, SH_M]
    acc = jnp.dot(h.astype(BF16), w_ref[...], preferred_element_type=F32)
    lane = _lane((TM, LANES))
    n_q = N_HEADS * HEAD_DIM // LANES
    n_kv = (N_KV_HEADS if shared_kv else N_HEADS) * HEAD_DIM // LANES

    def finish(t, gain_ref):
        if head_norm:
            t = _head_rms(t, gain_ref[...], bd_ref[...])
        if rope:
            t = _rope_chunk(t, cos_ref[...], sin_ref[...], HEAD_DIM // 2)
        return t

    for c in range(n_q):
        t = finish(acc[:, c * LANES:(c + 1) * LANES], qg_ref)
        q_ref[:, c * LANES:(c + 1) * LANES] = (t * q_scale).astype(BF16)
    for c in range(n_kv):
        k = finish(acc[:, (n_q + c) * LANES:(n_q + c + 1) * LANES], kg_ref)
        v = acc[:, (n_q + n_kv + c) * LANES:(n_q + n_kv + c + 1) * LANES]
        if shared_kv:
            k_sw = pltpu.roll(k, HALF, 1)
            v_sw = pltpu.roll(v, HALF, 1)
            for half in range(2):
                j = 2 * c + half
                for parity in range(2):
                    ksrc = k if parity == half else k_sw
                    vsrc = v if parity == half else v_sw
                    k_ref[j, parity] = _half_only(ksrc, lane, parity).astype(BF16)
                    _put_v(v_ref, j, parity, _with_ones(vsrc, lane, parity), v_t)
        else:
            for parity in range(2):
                k_ref[c, parity] = _half_only(k, lane, parity).astype(BF16)
                _put_v(v_ref, c, parity, _with_ones(v, lane, parity), v_t)


def _kv_out(groups, transposed):
    if transposed:
        return (pl.BlockSpec((groups, 2, LANES, TM), lambda i: (0, 0, 0, i)),
                jax.ShapeDtypeStruct((groups, 2, LANES, N_TOK), BF16))
    return (pl.BlockSpec((groups, 2, TM, LANES), lambda i: (0, 0, i, 0)),
            jax.ShapeDtypeStruct((groups, 2, N_TOK, LANES), BF16))


def _qkv(x, mod, gain, w, cos, sin, qg, kg, bd, *, shared_kv, head_norm, rope, v_t, q_scale):
    n_out = w.shape[1]
    groups = N_KV_HEADS if shared_kv else N_HEADS // 2
    row = lambda i: (i, 0)
    const = lambda i: (0, 0)
    kv_spec, kv_shape = _kv_out(groups, False)
    v_spec, v_shape = _kv_out(groups, v_t)
    return pl.pallas_call(
        functools.partial(_qkv_kernel, shared_kv=shared_kv, head_norm=head_norm, rope=rope,
                          v_t=v_t, q_scale=q_scale),
        out_shape=(jax.ShapeDtypeStruct((N_TOK, N_HEADS * HEAD_DIM), BF16), kv_shape, v_shape),
        grid=(N_TOK // TM,),
        in_specs=[pl.BlockSpec((TM, D), row),
                  pl.BlockSpec((None, 1, 6 * D), lambda i: (_mod_row(i), 0, 0)),
                  pl.BlockSpec((1, D), const),
                  pl.BlockSpec((D, n_out), const),
                  pl.BlockSpec((TM, LANES), row),
                  pl.BlockSpec((TM, LANES), row),
                  pl.BlockSpec((1, LANES), const),
                  pl.BlockSpec((1, LANES), const),
                  pl.BlockSpec((LANES, LANES), const)],
        out_specs=(pl.BlockSpec((TM, N_HEADS * HEAD_DIM), row), kv_spec, v_spec),
        compiler_params=_params("arbitrary"),
        name="qkv",
    )(x, mod, gain, w, cos, sin, qg, kg, bd)


def _mla_kernel(x_ref, mod_ref, g_ref, wd_ref, qan_ref, kvan_ref, wq_ref, wkv_ref,
                cos_ref, sin_ref, q_ref, k_ref, v_ref):
    h = _rms(x_ref[...], g_ref[...]) * (1.0 + mod_ref[:, SC_M]) + mod_ref[:, SH_M]
    lat = jnp.dot(h.astype(BF16), wd_ref[...], preferred_element_type=F32)
    c_q = _rms(lat[:, :MLA_Q_LORA], qan_ref[...])
    c_kv = _rms(lat[:, MLA_Q_LORA:MLA_Q_LORA + MLA_KV_LORA], kvan_ref[...])
    cos = cos_ref[...]
    sin = sin_ref[...]
    k_rope = _rope_chunk(lat[:, MLA_Q_LORA + MLA_KV_LORA:], cos, sin, MLA_ROPE // 2)
    q = jnp.dot(c_q.astype(BF16), wq_ref[...], preferred_element_type=F32)
    kv = jnp.dot(c_kv.astype(BF16), wkv_ref[...], preferred_element_type=F32)
    scale = (MLA_NOPE + MLA_ROPE) ** -0.5 * LOG2E
    lane = _lane((TM, LANES))
    for hd in range(N_HEADS):
        sl = slice(hd * LANES, (hd + 1) * LANES)
        q_ref[:, sl] = (_rope_chunk(q[:, sl], cos, sin, MLA_ROPE // 2) * scale).astype(BF16)
        k_ref[hd // 2, hd % 2] = (kv[:, sl] + k_rope).astype(BF16)
        v = kv[:, N_HEADS * LANES + hd * LANES:N_HEADS * LANES + (hd + 1) * LANES]
        _put_v(v_ref, hd // 2, hd % 2, _with_ones(v, lane, hd % 2), True)


def _mla_proj(x, mod, gain, wd, qan, kvan, wq, wkv, cos, sin):
    row = lambda i: (i, 0)
    const = lambda i: (0, 0)
    kv_spec, kv_shape = _kv_out(N_HEADS // 2, False)
    v_spec, v_shape = _kv_out(N_HEADS // 2, True)
    return pl.pallas_call(
        _mla_kernel,
        out_shape=(jax.ShapeDtypeStruct((N_TOK, N_HEADS * LANES), BF16), kv_shape, v_shape),
        grid=(N_TOK // TM,),
        in_specs=[pl.BlockSpec((TM, D), row),
                  pl.BlockSpec((None, 1, 6 * D), lambda i: (_mod_row(i), 0, 0)),
                  pl.BlockSpec((1, D), const),
                  pl.BlockSpec(wd.shape, const),
                  pl.BlockSpec((1, MLA_Q_LORA), const),
                  pl.BlockSpec((1, MLA_KV_LORA), const),
                  pl.BlockSpec(wq.shape, const),
                  pl.BlockSpec(wkv.shape, const),
                  pl.BlockSpec((TM, LANES), row),
                  pl.BlockSpec((TM, LANES), row)],
        out_specs=(pl.BlockSpec((TM, N_HEADS * LANES), row), kv_spec, v_spec),
        compiler_params=_params("arbitrary"),
        name="mla_proj",
    )(x, mod, gain, wd, qan, kvan, wq, wkv, cos, sin)


def _attn_kernel(*refs, stacks, tq, subs, gp, mode, with_lat):
    refs = list(refs)
    sink_ref = refs.pop(0) if mode == "window" else None
    q_ref, kc_ref, vc_ref = refs[:3]
    refs = refs[3:]
    if with_lat:
        kl_ref, vl_ref = refs[:2]
        refs = refs[2:]
    tab_ref = refs.pop(0) if mode == "neighbourhood" else None
    o_ref = refs[0]
    s_refs = refs[1:]
    g = pl.program_id(1)
    t = pl.program_id(2)
    qw = q_ref.shape[1] // gp
    ow = o_ref.shape[1] // gp
    m_rows = tq * len(stacks[0][0])

    chunks, masks = [], []
    for sub in range(subs):
        ch, mask = [("ctx", 0, CTX)], None
        if with_lat and mode == "window":
            span = tq + 2 * WINDOW
            q0 = (t * subs + sub) * tq
            start = pl.multiple_of(jnp.clip(q0 - WINDOW, 0, S - span), WINDOW)
            qi = q0 + lax.broadcasted_iota(jnp.int32, (m_rows, span), 0) % tq
            ki = start + lax.broadcasted_iota(jnp.int32, (m_rows, span), 1)
            mask = jnp.abs(qi - ki) <= WINDOW
            ch.append(("lat", start, span))
        elif with_lat:
            start = pl.multiple_of(_na_band_start(t * subs + sub) * GRID_W, GRID_W)
            ch.append(("lat", start, NA_BAND * GRID_W))
        chunks.append(ch)
        masks.append(mask)
    cols = [[sum(w for _, _, w in ch[:c]) for c in range(len(ch))] for ch in chunks]

    items = [(gi, sub, qcs, kv) for gi in range(gp) for sub in range(subs) for qcs, kv in stacks]
    qs, sinks = [], []
    for gi, sub, qcs, kv in items:
        qs.append(jnp.concatenate(
            [q_ref[sub * tq:(sub + 1) * tq, gi * qw + qc * LANES:gi * qw + (qc + 1) * LANES]
             for qc, _ in qcs], axis=0))
        sink_col = None
        if mode == "window":
            hd = [(g * gp + gi) * 2 * len(qcs) + 2 * qc + kv for qc, _ in qcs]
            row = lax.broadcasted_iota(jnp.int32, (m_rows, 1), 0)
            sink_col = jnp.where(row < tq, sink_ref[hd[0]], sink_ref[hd[-1]])
        sinks.append(sink_col)

    def score_stage(it):
        gi, sub, _, kv = items[it]
        s_ref = s_refs[it % len(s_refs)]
        m_fold, m_col = None, sinks[it]
        for (kind, st, width), col in zip(chunks[sub], cols[sub]):
            k = kc_ref[gi, kv] if kind == "ctx" else kl_ref[gi, kv, pl.ds(st, width), :]
            s = lax.dot_general(qs[it], k, (((1,), (1,)), ((), ())), preferred_element_type=F32)
            if kind == "lat" and tab_ref is not None:
                s = s + tab_ref[2 * gi + kv]
            if kind == "lat" and masks[sub] is not None:
                s = jnp.where(masks[sub], s, NEG)
            s_ref[0:m_rows, col:col + width] = s
            if width % LANES == 0:
                for j in range(width // LANES):
                    slab = s[:, j * LANES:(j + 1) * LANES]
                    m_fold = slab if m_fold is None else jnp.maximum(m_fold, slab)
            else:
                mc = jnp.max(s, axis=-1, keepdims=True)
                m_col = mc if m_col is None else jnp.maximum(m_col, mc)
        m = jnp.max(m_fold, axis=-1, keepdims=True)
        return m if m_col is None else jnp.maximum(m, m_col)

    def prob_stage(it, m):
        sub = items[it][1]
        s_ref = s_refs[it % len(s_refs)]
        return [jnp.exp(s_ref[0:m_rows, col:col + width] - m).astype(BF16)
                for (_, _, width), col in zip(chunks[sub], cols[sub])]

    def value_stage(it, ps, m):
        gi, sub, _, kv = items[it]
        acc = jnp.zeros((m_rows, LANES), F32)
        for (kind, st, width), p in zip(chunks[sub], ps):
            v = vc_ref[gi, kv] if kind == "ctx" else vl_ref[gi, kv, pl.ds(st, width), :]
            acc = acc + jnp.dot(p, v, preferred_element_type=F32)
        den = acc[:, LANES - 1:LANES] if kv == 0 else acc[:, 0:1]
        if sinks[it] is not None:
            den = den + jnp.exp(sinks[it] - m)
        lane = _lane(acc.shape)
        return jnp.where(lane < HALF if kv == 0 else lane >= HALF, acc / den, 0.0)

    n = len(items)
    ms, ps, outs = [None] * n, [None] * n, {}
    for stage in range(n + 2):
        if 0 <= stage - 1 < n:
            ps[stage - 1] = prob_stage(stage - 1, ms[stage - 1])
        if stage < n:
            ms[stage] = score_stage(stage)
        if 0 <= stage - 2 < n:
            gi, sub, qcs, _ = items[stage - 2]
            o = value_stage(stage - 2, ps[stage - 2], ms[stage - 2])
            for i, (_, oc) in enumerate(qcs):
                key = (sub, gi * (ow // LANES) + oc)
                part = o[i * tq:(i + 1) * tq]
                outs[key] = part if key not in outs else outs[key] + part
    for (sub, oc), o in outs.items():
        o_ref[sub * tq:(sub + 1) * tq, oc * LANES:(oc + 1) * LANES] = o.astype(o_ref.dtype)


def _attention(q, k4, v4, *, stacks, mode, tq, subs=1, gp=1, sink=None, table=None,
               ctx_queries=False):
    groups = k4.shape[0] // gp
    qw = q.shape[1] // groups
    ow = N_HEADS * HEAD_DIM // groups
    with_lat = not ctx_queries
    tqs = tq * subs
    nq = S // tqs if with_lat else CTX // tqs
    row0 = 0 if with_lat else N_LAT // tqs

    q_spec = pl.BlockSpec((tqs, qw), lambda b, g, t: (row0 + b * nq + t, g))
    ctx_spec = pl.BlockSpec((gp, 2, CTX, LANES), lambda b, g, t: (g, 0, N_LAT // CTX + b, 0))
    lat_spec = pl.BlockSpec((gp, 2, S, LANES), lambda b, g, t: (g, 0, b, 0))
    in_specs = [q_spec, ctx_spec, ctx_spec]
    args = [q, k4, v4]
    n_keys = CTX
    if with_lat:
        in_specs += [lat_spec, lat_spec]
        args += [k4, v4]
        n_keys += {"window": tq + 2 * WINDOW, "neighbourhood": NA_BAND * GRID_W}[mode]
    if mode == "window":
        in_specs = [pl.BlockSpec(memory_space=pltpu.SMEM)] + in_specs
        args = [sink] + args
    if mode == "neighbourhood":
        in_specs.append(pl.BlockSpec((None, 2 * gp, tq, NA_BAND * GRID_W),
                                     lambda b, g, t: (_na_variant(t), g, 0, 0)))
        args.append(table)
    m_rows = tq * max(len(qcs) for qcs, _ in stacks)
    return pl.pallas_call(
        functools.partial(_attn_kernel, stacks=stacks, tq=tq, subs=subs, gp=gp, mode=mode,
                          with_lat=with_lat),
        out_shape=jax.ShapeDtypeStruct((N_LAT if with_lat else N_CTX, N_HEADS * HEAD_DIM), BF16),
        grid=(B, groups, nq),
        in_specs=in_specs,
        out_specs=pl.BlockSpec((tqs, ow), lambda b, g, t: (b * nq + t, g)),
        scratch_shapes=[pltpu.VMEM((m_rows, n_keys), F32) for _ in range(2)],
        compiler_params=_params("arbitrary", "arbitrary", "arbitrary"),
        name="attn_" + mode + ("" if with_lat else "_ctx"),
    )(*args)


def _attn_t_kernel(*refs, stacks, tq, subs, with_lat):
    refs = list(refs)
    q_ref, kc_ref, vtc_ref = refs[:3]
    refs = refs[3:]
    if with_lat:
        kl_ref, vtl_ref = refs[:2]
        refs = refs[2:]
    o_ref = refs[0]
    s_refs = refs[1:]
    chunks = [("ctx", 0, CTX)]
    if with_lat:
        chunks += [("lat", c * TK, TK) for c in range(S // TK)]
    rows = [sum(w for _, _, w in chunks[:c]) for c in range(len(chunks))]
    items = [(sub, qcs, kv) for sub in range(subs) for qcs, kv in stacks]
    qs = [jnp.concatenate([q_ref[sub * tq:(sub + 1) * tq, qc * LANES:(qc + 1) * LANES]
                           for qc, _ in qcs], axis=0) for sub, qcs, _ in items]
    cols = qs[0].shape[0]

    def score_chunk(it, c, m8):
        kind, st, width = chunks[c]
        kv = items[it][2]
        k = kc_ref[kv] if kind == "ctx" else kl_ref[kv, st:st + width, :]
        s = lax.dot_general(k, qs[it], (((1,), (1,)), ((), ())), preferred_element_type=F32)
        s_refs[it % len(s_refs)][rows[c]:rows[c] + width, 0:cols] = s
        c8 = jnp.max(s.reshape(width // 8, 8, cols), axis=0)
        return c8 if m8 is None else jnp.maximum(m8, c8)

    def prob_chunk(it, c, m):
        width = chunks[c][2]
        s = s_refs[it % len(s_refs)][rows[c]:rows[c] + width, 0:cols]
        return jnp.exp2(s - m).astype(BF16)

    def value_chunk(it, c, p, acc):
        kind, st, width = chunks[c]
        kv = items[it][2]
        vt = vtc_ref[kv] if kind == "ctx" else vtl_ref[kv, :, st:st + width]
        return acc + jnp.dot(vt, p, preferred_element_type=F32)

    n = len(items)
    ms, ps, accs = [None] * n, [[] for _ in range(n)], []
    for stage in range(n + 2):
        m8 = None
        acc = jnp.zeros((LANES, cols), F32)
        for c in range(len(chunks)):
            if 0 <= stage - 1 < n:
                ps[stage - 1].append(prob_chunk(stage - 1, c, ms[stage - 1]))
            if stage < n:
                m8 = score_chunk(stage, c, m8)
            if 0 <= stage - 2 < n:
                acc = value_chunk(stage - 2, c, ps[stage - 2][c], acc)
        if stage < n:
            ms[stage] = jnp.max(m8, axis=0, keepdims=True)
        if 0 <= stage - 2 < n:
            accs.append(acc)

    outs = {}
    for (sub, qcs, kv), acc in zip(items, accs):
        den = acc[LANES - 1:LANES, :] if kv == 0 else acc[0:1, :]
        o = (acc / den).T
        lane = _lane(o.shape)
        o = jnp.where(lane < HALF if kv == 0 else lane >= HALF, o, 0.0)
        for i, (_, oc) in enumerate(qcs):
            part = o[i * tq:(i + 1) * tq]
            outs[sub, oc] = part if (sub, oc) not in outs else outs[sub, oc] + part
    for (sub, oc), o in outs.items():
        o_ref[sub * tq:(sub + 1) * tq, oc * LANES:(oc + 1) * LANES] = o.astype(o_ref.dtype)


def _attention_t(q, k4, vt4, *, stacks, tq, subs=1, ctx_queries=False):
    groups = k4.shape[0]
    qw = q.shape[1] // groups
    ow = N_HEADS * HEAD_DIM // groups
    with_lat = not ctx_queries
    tqs = tq * subs
    nq = S // tqs if with_lat else CTX // tqs
    row0 = 0 if with_lat else N_LAT // tqs
    in_specs = [pl.BlockSpec((tqs, qw), lambda b, g, t: (row0 + b * nq + t, g)),
                pl.BlockSpec((None, 2, CTX, LANES), lambda b, g, t: (g, 0, N_LAT // CTX + b, 0)),
                pl.BlockSpec((None, 2, LANES, CTX), lambda b, g, t: (g, 0, 0, N_LAT // CTX + b))]
    args = [q, k4, vt4]
    n_keys = CTX
    if with_lat:
        in_specs += [pl.BlockSpec((None, 2, S, LANES), lambda b, g, t: (g, 0, b, 0)),
                     pl.BlockSpec((None, 2, LANES, S), lambda b, g, t: (g, 0, 0, b))]
        args += [k4, vt4]
        n_keys += S
    m_rows = tq * max(len(qcs) for qcs, _ in stacks)
    return pl.pallas_call(
        functools.partial(_attn_t_kernel, stacks=stacks, tq=tq, subs=subs, with_lat=with_lat),
        out_shape=jax.ShapeDtypeStruct((N_LAT if with_lat else N_CTX, N_HEADS * HEAD_DIM), BF16),
        grid=(B, groups, nq),
        in_specs=in_specs,
        out_specs=pl.BlockSpec((tqs, ow), lambda b, g, t: (b * nq + t, g)),
        scratch_shapes=[pltpu.VMEM((n_keys, m_rows), F32) for _ in range(2)],
        compiler_params=_params("arbitrary", "arbitrary", "arbitrary"),
        name="attn_global" + ("" if with_lat else "_ctx"),
    )(*args)


def _na_band_start(t):
    return jnp.clip(t * NA_QROWS - NA_ROWS // 2, 0, S // GRID_W - NA_BAND)


def _na_tiles():
    rows = S // GRID_W
    tiles = [(t * NA_QROWS, min(max(t * NA_QROWS - NA_ROWS // 2, 0), rows - NA_BAND))
             for t in range(rows // NA_QROWS)]
    lo = [rb for rb in tiles if rb[0] - NA_ROWS // 2 < 0]
    hi = [rb for rb in tiles if rb[0] - NA_ROWS // 2 > rows - NA_BAND]
    return lo + [tiles[len(lo)]] + hi, len(lo), len(hi)


def _na_variant(t):
    _, n_lo, n_hi = _na_tiles()
    first_hi = S // GRID_W // NA_QROWS - n_hi
    return jnp.where(t < n_lo, t, jnp.where(t >= first_hi, t - first_hi + n_lo + 1, n_lo))


def _bias_cols_kernel(rpb_ref, sel_ref, o_ref):
    o_ref[...] = jnp.dot(rpb_ref[...], sel_ref[...], preferred_element_type=F32,
                         precision=lax.Precision.HIGHEST)


def _na_table(rpb):
    rows = S // GRID_W
    n_dr, n_dc = 2 * NA_ROWS - 1, 2 * NA_COLS - 1
    c = np.arange(GRID_W)[:, None]
    kc = np.arange(GRID_W)[None, :]
    cs = np.clip(c - NA_COLS // 2, 0, GRID_W - NA_COLS)
    col_ok = (kc >= cs) & (kc < cs + NA_COLS)
    sel = (np.arange(LANES)[:, None, None] == (kc - c + NA_COLS - 1)[None]) & col_ok[None]
    sel = jnp.asarray(sel.reshape(LANES, GRID_W * GRID_W), F32)
    rpb2 = jnp.pad(rpb.reshape(N_HEADS * n_dr, n_dc), ((0, 256 - N_HEADS * n_dr), (0, LANES - n_dc)))
    cols = pl.pallas_call(
        _bias_cols_kernel,
        out_shape=jax.ShapeDtypeStruct((256, GRID_W * GRID_W), F32),
        name="na_bias_cols",
    )(rpb2, sel)
    cols = cols[:N_HEADS * n_dr].reshape(N_HEADS, n_dr, GRID_W, GRID_W)
    cols = jnp.where(col_ok[None, None], cols, NEG)
    masked = jnp.full((N_HEADS, GRID_W, GRID_W), NEG, F32)
    variants = []
    for r0, bs in _na_tiles()[0]:
        q_rows = []
        for r in range(r0, r0 + NA_QROWS):
            rs = min(max(r - NA_ROWS // 2, 0), rows - NA_ROWS)
            q_rows.append(jnp.concatenate(
                [cols[:, kr - r + NA_ROWS - 1] if rs <= kr < rs + NA_ROWS else masked
                 for kr in range(bs, bs + NA_BAND)], axis=2))
        variants.append(jnp.concatenate(q_rows, axis=1))
    return jnp.stack(variants)


def _router_route(h, wr_ref):
    w = wr_ref[...]
    h_hi, w_hi = h.astype(BF16), w.astype(BF16)
    h_lo = (h - h_hi.astype(F32)).astype(BF16)
    w_lo = (w - w_hi.astype(F32)).astype(BF16)
    logits = (jnp.dot(h_hi, w_hi, preferred_element_type=F32)
              + (jnp.dot(h_hi, w_lo, preferred_element_type=F32)
                 + jnp.dot(h_lo, w_hi, preferred_element_type=F32)))
    lane = _lane(logits.shape).astype(F32)
    lg = jnp.where(lane < N_EXPERTS, logits, -jnp.inf)
    m1 = jnp.max(lg, axis=-1, keepdims=True)
    i1 = jnp.min(jnp.where(lg == m1, lane, float(LANES)), axis=-1, keepdims=True)
    lg2 = jnp.where(lane == i1, -jnp.inf, lg)
    m2 = jnp.max(lg2, axis=-1, keepdims=True)
    i2 = jnp.min(jnp.where(lg2 == m2, lane, float(LANES)), axis=-1, keepdims=True)
    e = jnp.exp(m2 - m1)
    return jnp.where(lane == 0, i1, jnp.where(lane == 1, i2, jnp.where(
        lane == 2, 1.0 / (1.0 + e), jnp.where(lane == 3, e / (1.0 + e), 0.0))))


def _oproj_kernel(*refs, moe, with_ctx):
    refs = list(refs)
    o_ref = refs.pop(0)
    oc_ref = refs.pop(0) if with_ctx else None
    if moe:
        wo_ref, x_ref, mod_ref, post_ref, pre_ref, wr_ref, xo_ref, h_ref, route_ref = refs
    else:
        wo_ref, x_ref, mod_ref, post_ref, pre_ref, xo_ref, h_ref = refs
    o = o_ref[...]
    if with_ctx:
        o = jnp.where(pl.program_id(0) < N_LAT // TM, o, oc_ref[...])
    y = jnp.dot(o, wo_ref[...], preferred_element_type=F32)
    x1 = x_ref[...] + mod_ref[:, G_M] * _rms(y, post_ref[...])
    xo_ref[...] = x1
    h = _rms(x1, pre_ref[...]) * (1.0 + mod_ref[:, SC_F]) + mod_ref[:, SH_F]
    h_ref[...] = h.astype(h_ref.dtype)
    if moe:
        route_ref[...] = _router_route(h, wr_ref)


def _oproj(o, o_ctx, wo, x, mod, post, pre, w_router):
    moe = w_router is not None
    with_ctx = o_ctx is not None
    n_rows = N_TOK if with_ctx else N_LAT
    n_lat = N_LAT // TM
    row = lambda i: (i, 0)
    const = lambda i: (0, 0)
    in_specs = [pl.BlockSpec((TM, D), lambda i: (jnp.minimum(i, n_lat - 1), 0))]
    args = [o]
    if with_ctx:
        in_specs.append(pl.BlockSpec((TM, D), lambda i: (jnp.maximum(i - n_lat, 0), 0)))
        args.append(o_ctx)
    in_specs += [pl.BlockSpec((D, D), const), pl.BlockSpec((TM, D), row),
                 pl.BlockSpec((None, 1, 6 * D), lambda i: (_mod_row(i), 0, 0)),
                 pl.BlockSpec((1, D), const), pl.BlockSpec((1, D), const)]
    args += [wo, x, mod, post, pre]
    out_shape = [jax.ShapeDtypeStruct((n_rows, D), F32),
                 jax.ShapeDtypeStruct((n_rows, D), F32 if moe else BF16)]
    out_specs = [pl.BlockSpec((TM, D), row), pl.BlockSpec((TM, D), row)]
    if moe:
        in_specs.append(pl.BlockSpec((D, LANES), const))
        out_shape.append(jax.ShapeDtypeStruct((n_rows, LANES), F32))
        out_specs.append(pl.BlockSpec((TM, LANES), row))
        args.append(w_router)
    return pl.pallas_call(
        functools.partial(_oproj_kernel, moe=moe, with_ctx=with_ctx),
        out_shape=tuple(out_shape),
        grid=(n_rows // TM,),
        in_specs=in_specs,
        out_specs=tuple(out_specs),
        compiler_params=_params("arbitrary"),
        name="oproj",
    )(*args)


def _swiglu(h, gu_ref, dw_ref):
    gu = jnp.dot(h, gu_ref[...], preferred_element_type=F32)
    half = gu.shape[1] // 2
    act = (jax.nn.silu(gu[:, :half]) * gu[:, half:]).astype(BF16)
    return jnp.dot(act, dw_ref[...], preferred_element_type=F32)


def _ffn_kernel(h_ref, gu_ref, dw_ref, x_ref, mod_ref, post_ref, o_ref):
    y = _swiglu(h_ref[...], gu_ref, dw_ref)
    o_ref[...] = x_ref[...] + mod_ref[:, G_F] * _rms(y, post_ref[...])


def _ffn(h, w_gu, w_down, x, mod, post):
    row = lambda i: (i, 0)
    const = lambda i: (0, 0)
    resident = pl.Buffered(1)
    return pl.pallas_call(
        _ffn_kernel,
        out_shape=jax.ShapeDtypeStruct((N_TOK, D), F32),
        grid=(N_TOK // TM,),
        in_specs=[pl.BlockSpec((TM, D), row),
                  pl.BlockSpec((D, 2 * D_FF), const, pipeline_mode=resident),
                  pl.BlockSpec((D_FF, D), const, pipeline_mode=resident),
                  pl.BlockSpec((TM, D), row),
                  pl.BlockSpec((None, 1, 6 * D), lambda i: (_mod_row(i), 0, 0)),
                  pl.BlockSpec((1, D), const)],
        out_specs=pl.BlockSpec((TM, D), row),
        compiler_params=_params("arbitrary"),
        name="ffn_dense",
    )(h, w_gu, w_down, x, mod, post)


def _routing(route, n_rows):
    i1 = route[:, 0].astype(jnp.int32)
    i2 = route[:, 1].astype(jnp.int32)
    e = jnp.arange(N_EXPERTS, dtype=jnp.int32)
    hit1 = i1[:, None] == e
    hit2 = i2[:, None] == e
    sel = hit1.astype(jnp.int32) + hit2.astype(jnp.int32)
    cum = jnp.cumsum(sel, axis=0)
    tiles = (cum[-1] + TME - 1) // TME
    tile_end = jnp.cumsum(tiles)
    slot = ((tile_end - tiles) * TME)[None] + cum - sel
    slots = jnp.stack([jnp.sum(jnp.where(hit1, slot, 0), axis=1),
                       jnp.sum(jnp.where(hit2, slot, 0), axis=1)], axis=1).reshape(-1)
    j = jnp.arange(_n_slot_tiles(n_rows), dtype=jnp.int32)
    tile_expert = jnp.minimum(jnp.sum(j[:, None] >= tile_end[None], axis=1), N_EXPERTS - 1)
    return slots.astype(jnp.int32), tile_expert.astype(jnp.int32), tile_end.astype(jnp.int32)


def _n_slot_tiles(n_rows):
    return 2 * n_rows // TME + N_EXPERTS


def _row_copies(slots_ref, n, copy):
    def start(r, carry):
        for k in range(2):
            copy(r, k, slots_ref[2 * r + k]).start(priority=k)
        return carry

    def wait(r, carry):
        for k in range(2):
            copy(r, k, slots_ref[2 * r + k]).wait()
        return carry

    lax.fori_loop(0, n, start, 0, unroll=8)
    lax.fori_loop(0, n, wait, 0, unroll=8)


def _dispatch_kernel(tend_ref, slots_ref, h_ref, xs_ref, zero_ref, zsem, sem, *, n_tiles):
    @pl.when(pl.program_id(0) == 0)
    def _():
        zero_ref[...] = jnp.zeros_like(zero_ref)
        tiles = []
        for e in range(N_EXPERTS):
            first = tend_ref[e - 1] if e else 0
            tiles.append((tend_ref[e] > first, tend_ref[e] - 1))
        for u in range(N_EXPERTS):
            tile = tend_ref[N_EXPERTS - 1] + u
            tiles.append((tile < n_tiles, tile))
        copies = [(ok, pltpu.make_async_copy(
            zero_ref, xs_ref.at[pl.ds(pl.multiple_of(jnp.maximum(tile, 0) * TME, TME), TME)], zsem))
            for ok, tile in tiles]
        for ok, cp in copies:
            pl.when(ok)(cp.start)
        for ok, cp in copies:
            pl.when(ok)(cp.wait)

    _row_copies(slots_ref, TMR, lambda r, k, s: pltpu.make_async_copy(
        h_ref.at[pl.ds(r, 1)], xs_ref.at[pl.ds(s, 1)], sem))


def _dispatch(tile_end, slots, h, n_rows):
    n_tiles = _n_slot_tiles(n_rows)
    return pl.pallas_call(
        functools.partial(_dispatch_kernel, n_tiles=n_tiles),
        out_shape=jax.ShapeDtypeStruct((n_tiles * TME, D), F32),
        grid_spec=pltpu.PrefetchScalarGridSpec(
            num_scalar_prefetch=1,
            grid=(n_rows // TMR,),
            in_specs=[pl.BlockSpec((2 * TMR,), lambda i, te: (i,), memory_space=pltpu.SMEM),
                      pl.BlockSpec((TMR, D), lambda i, te: (i, 0))],
            out_specs=pl.BlockSpec(memory_space=pl.ANY),
            scratch_shapes=[pltpu.VMEM((TME, D), F32), pltpu.SemaphoreType.DMA(()),
                            pltpu.SemaphoreType.DMA(())]),
        compiler_params=_params("arbitrary"),
        name="moe_dispatch",
    )(tile_end, slots, h)


def _expert_kernel(te_ref, tend_ref, x_ref, gu_ref, dw_ref, o_ref, gub_ref, db_ref):
    j = pl.program_id(0)
    valid = j < tend_ref[N_EXPERTS - 1]

    @pl.when(jnp.logical_or(j == 0, te_ref[j] != te_ref[jnp.maximum(j - 1, 0)]))
    def _():
        gub_ref[...] = gu_ref[...].astype(BF16)
        db_ref[...] = dw_ref[...].astype(BF16)

    @pl.when(valid)
    def _():
        o_ref[...] = _swiglu(x_ref[...].astype(BF16), gub_ref, db_ref)

    @pl.when(jnp.logical_not(valid))
    def _():
        o_ref[...] = jnp.zeros_like(o_ref)


def _experts(tile_expert, tile_end, xs, w_gu, w_down):
    n_tiles = xs.shape[0] // TME
    return pl.pallas_call(
        _expert_kernel,
        out_shape=jax.ShapeDtypeStruct(xs.shape, F32),
        grid_spec=pltpu.PrefetchScalarGridSpec(
            num_scalar_prefetch=2,
            grid=(n_tiles,),
            in_specs=[pl.BlockSpec((TME, D), lambda j, te, tend: (j, 0)),
                      pl.BlockSpec((None, D, 2 * D_FF_EXPERT), lambda j, te, tend: (te[j], 0, 0)),
                      pl.BlockSpec((None, D_FF_EXPERT, D), lambda j, te, tend: (te[j], 0, 0))],
            out_specs=pl.BlockSpec((TME, D), lambda j, te, tend: (j, 0)),
            scratch_shapes=[pltpu.VMEM((D, 2 * D_FF_EXPERT), BF16),
                            pltpu.VMEM((D_FF_EXPERT, D), BF16)]),
        compiler_params=_params("arbitrary"),
        name="moe_experts",
    )(tile_expert, tile_end, xs, w_gu, w_down)


def _combine_kernel(slots_ref, route_ref, x_ref, mod_ref, post_ref, ys_ref, o_ref, buf_ref, sem):
    _row_copies(slots_ref, TMR, lambda r, k, s: pltpu.make_async_copy(
        ys_ref.at[pl.ds(s, 1)], buf_ref.at[k, pl.ds(r, 1)], sem))
    y = route_ref[:, 2:3] * buf_ref[0] + route_ref[:, 3:4] * buf_ref[1]
    o_ref[...] = x_ref[...] + mod_ref[:, G_F] * _rms(y, post_ref[...])


def _combine(slots, route, x, mod, post, ys, n_rows):
    return pl.pallas_call(
        _combine_kernel,
        out_shape=jax.ShapeDtypeStruct((n_rows, D), F32),
        grid=(n_rows // TMR,),
        in_specs=[pl.BlockSpec((2 * TMR,), lambda i: (i,), memory_space=pltpu.SMEM),
                  pl.BlockSpec((TMR, LANES), lambda i: (i, 0)),
                  pl.BlockSpec((TMR, D), lambda i: (i, 0)),
                  pl.BlockSpec((None, 1, 6 * D), lambda i: (_mod_row(i * TMR // TM), 0, 0)),
                  pl.BlockSpec((1, D), lambda i: (0, 0)),
                  pl.BlockSpec(memory_space=pl.ANY)],
        out_specs=pl.BlockSpec((TMR, D), lambda i: (i, 0)),
        scratch_shapes=[pltpu.VMEM((2, TMR, D), F32), pltpu.SemaphoreType.DMA(())],
        compiler_params=_params("arbitrary"),
        name="moe_combine",
    )(slots, route, x, mod, post, ys)


def _moe(h, route, w_gu, w_down, x, mod, post, n_rows):
    slots, tile_expert, tile_end = _routing(route, n_rows)
    xs = _dispatch(tile_end, slots, h, n_rows)
    ys = _experts(tile_expert, tile_end, xs, w_gu, w_down)
    return _combine(slots, route, x, mod, post, ys, n_rows)


def _rope_tables(rot_dim, lead, tail):
    t = jnp.arange(S, dtype=jnp.int32)
    row = (t // GRID_W).astype(F32)
    col = (t % GRID_W).astype(F32)
    quarter = rot_dim // 4
    inv = ROPE_THETA ** (-jnp.arange(quarter, dtype=F32) / quarter)
    ang = jnp.concatenate([row[:, None] * inv, col[:, None] * inv], axis=-1)
    cos, sin = jnp.cos(ang), jnp.sin(ang)
    reps = (LANES - lead - tail) // rot_dim
    cos_l = jnp.concatenate([jnp.ones((S, lead), F32)] + [cos, cos] * reps + [jnp.ones((S, tail), F32)], axis=1)
    sin_l = jnp.concatenate([jnp.zeros((S, lead), F32)] + [-sin, sin] * reps + [jnp.zeros((S, tail), F32)], axis=1)
    cos_all = jnp.concatenate([jnp.tile(cos_l, (B, 1)), jnp.ones((N_CTX, LANES), F32)], axis=0)
    sin_all = jnp.concatenate([jnp.tile(sin_l, (B, 1)), jnp.zeros((N_CTX, LANES), F32)], axis=0)
    return cos_all, sin_all


def _mla_weights(w_dkv, w_q_b, w_kv_b):
    lat = MLA_Q_LORA + MLA_KV_LORA
    pad = LANES - MLA_NOPE - MLA_ROPE
    wd = jnp.concatenate([w_dkv[:, :lat], jnp.zeros((D, MLA_NOPE), F32), w_dkv[:, lat:],
                          jnp.zeros((D, pad), F32)], axis=1)
    wq = jnp.pad(w_q_b.reshape(MLA_Q_LORA, N_HEADS, MLA_NOPE + MLA_ROPE), ((0, 0), (0, 0), (0, pad)))
    kvr = w_kv_b.reshape(MLA_KV_LORA, N_HEADS, 2 * HALF)
    k_part, v_part = kvr[..., :HALF], kvr[..., HALF:]
    zero = jnp.zeros_like(v_part)
    wk = jnp.concatenate([k_part, zero], axis=-1)
    even = (jnp.arange(N_HEADS) % 2 == 0)[None, :, None]
    wv = jnp.where(even, jnp.concatenate([v_part, zero], -1), jnp.concatenate([zero, v_part], -1))
    wkv = jnp.concatenate([wk.reshape(MLA_KV_LORA, -1), wv.reshape(MLA_KV_LORA, -1)], axis=1)
    return wd.astype(BF16), wq.reshape(MLA_Q_LORA, -1).astype(BF16), wkv.astype(BF16)


GQA_STACKS = ((((0, 0), (1, 1)), 0), (((0, 0), (1, 1)), 1))
MLA_STACKS = ((((0, 0),), 0), (((1, 0),), 1))
NA_STACKS = ((((0, 0),), 0), (((0, 0),), 1))


def kernel(x, c, ctx, c_ctx,
           l0_ada_w, l0_ada_b, l0_mix_pre, l0_mix_post, l0_ffn_pre, l0_ffn_post,
           l0_w_qkv, l0_q_norm, l0_k_norm, l0_w_o, l0_ffn_w_gu, l0_ffn_w_down,
           l1_ada_w, l1_ada_b, l1_mix_pre, l1_mix_post, l1_ffn_pre, l1_ffn_post,
           l1_w_qkv, l1_sink, l1_w_o, l1_router, l1_moe_w_gu, l1_moe_w_down,
           l2_ada_w, l2_ada_b, l2_mix_pre, l2_mix_post, l2_ffn_pre, l2_ffn_post,
           l2_w_dkv, l2_q_a_norm, l2_w_q_b, l2_kv_a_norm, l2_w_kv_b, l2_w_o, l2_ffn_w_gu, l2_ffn_w_down,
           l3_ada_w, l3_ada_b, l3_mix_pre, l3_mix_post, l3_ffn_pre, l3_ffn_post,
           l3_w_qkv, l3_rpb, l3_w_o, l3_router, l3_moe_w_gu, l3_moe_w_down):
    vec = lambda a: a.reshape(1, -1)
    tile2 = lambda a: jnp.tile(a, 2).reshape(1, LANES)
    bf = lambda a: a.astype(BF16)
    pad_router = lambda w: jnp.pad(w, ((0, 0), (0, LANES - N_EXPERTS)))

    xs = jnp.concatenate([x.reshape(N_LAT, D), ctx.reshape(N_CTX, D)], axis=0)
    c_all = jnp.concatenate([c, c_ctx[None], jnp.zeros((8 - B - 1, D), F32)], axis=0)
    mods = [_ada(c_all, w, b).reshape(8, 1, 6 * D)
            for w, b in ((l0_ada_w, l0_ada_b), (l1_ada_w, l1_ada_b), (l2_ada_w, l2_ada_b), (l3_ada_w, l3_ada_b))]
    cos64, sin64 = _rope_tables(HEAD_DIM, 0, 0)
    cos32, sin32 = _rope_tables(MLA_ROPE, MLA_NOPE, LANES - MLA_NOPE - MLA_ROPE)
    ones = jnp.ones((1, LANES), F32)
    idx = np.arange(LANES)
    bd = jnp.asarray(idx[:, None] // HALF == idx[None, :] // HALF, BF16)

    q, k4, v4 = _qkv(xs, mods[0], vec(l0_mix_pre), bf(l0_w_qkv), cos64, sin64,
                     tile2(l0_q_norm), tile2(l0_k_norm), bd, shared_kv=True, head_norm=True, rope=True,
                     v_t=True, q_scale=HEAD_DIM ** -0.5 * LOG2E)
    o = _attention_t(q, k4, v4, stacks=GQA_STACKS, tq=TQ, subs=4)
    oc = _attention_t(q, k4, v4, stacks=GQA_STACKS, tq=CTX, ctx_queries=True)
    xs, h = _oproj(o, oc, bf(l0_w_o), xs, mods[0], vec(l0_mix_post), vec(l0_ffn_pre), None)
    xs = _ffn(h, bf(l0_ffn_w_gu), bf(l0_ffn_w_down), xs, mods[0], vec(l0_ffn_post))

    q, k4, v4 = _qkv(xs, mods[1], vec(l1_mix_pre), bf(l1_w_qkv), cos64, sin64,
                     ones, ones, bd, shared_kv=True, head_norm=False, rope=True,
                     v_t=False, q_scale=HEAD_DIM ** -0.5)
    o = _attention(q, k4, v4, stacks=GQA_STACKS, mode="window", tq=TQ, subs=2, sink=l1_sink)
    oc = _attention(q, k4, v4, stacks=GQA_STACKS, mode="window", tq=CTX, sink=l1_sink, ctx_queries=True)
    xs, h, route = _oproj(o, oc, bf(l1_w_o), xs, mods[1], vec(l1_mix_post), vec(l1_ffn_pre),
                          pad_router(l1_router))
    xs = _moe(h, route, l1_moe_w_gu, l1_moe_w_down, xs, mods[1], vec(l1_ffn_post), N_TOK)

    wd, wq, wkv = _mla_weights(l2_w_dkv, l2_w_q_b, l2_w_kv_b)
    q, k4, v4 = _mla_proj(xs, mods[2], vec(l2_mix_pre), wd, vec(l2_q_a_norm), vec(l2_kv_a_norm),
                          wq, wkv, cos32, sin32)
    o = _attention_t(q, k4, v4, stacks=MLA_STACKS, tq=2 * TQ, subs=4)
    oc = _attention_t(q, k4, v4, stacks=MLA_STACKS, tq=CTX, ctx_queries=True)
    xs, h = _oproj(o, oc, bf(l2_w_o), xs, mods[2], vec(l2_mix_post), vec(l2_ffn_pre), None)
    xs = _ffn(h, bf(l2_ffn_w_gu), bf(l2_ffn_w_down), xs, mods[2], vec(l2_ffn_post))

    q, k4, v4 = _qkv(xs, mods[3], vec(l3_mix_pre), bf(l3_w_qkv), cos64, sin64,
                     ones, ones, bd, shared_kv=False, head_norm=False, rope=False,
                     v_t=False, q_scale=HEAD_DIM ** -0.5)
    o = _attention(q, k4, v4, stacks=NA_STACKS, mode="neighbourhood", tq=NA_QROWS * GRID_W, gp=2,
                   table=_na_table(l3_rpb))
    xl, h, route = _oproj(o, None, bf(l3_w_o), xs, mods[3], vec(l3_mix_post), vec(l3_ffn_pre),
                          pad_router(l3_router))
    xl = _moe(h, route, l3_moe_w_gu, l3_moe_w_down, xl, mods[3], vec(l3_ffn_post), N_LAT)
    return xl.reshape(B, S, D)
```

```python
import functools

import numpy as np
import jax
import jax.numpy as jnp
from jax import lax
from jax.experimental import pallas as pl
from jax.experimental.pallas import tpu as pltpu

D = 1024
B = 4
S = 4096
CTX = 256
GRID_W = 64
N_LAT = B * S
N_CTX = B * CTX
N_TOK = N_LAT + N_CTX
HEAD_DIM = 64
N_HEADS = 16
N_KV_HEADS = 4
WINDOW = 128
MLA_Q_LORA = 384
MLA_KV_LORA = 256
MLA_NOPE = 64
MLA_ROPE = 32
NA_ROWS = 8
NA_COLS = 16
D_FF = 2816
N_EXPERTS = 8
D_FF_EXPERT = 1408
ROPE_THETA = 10000.0
EPS = 1e-6
NEG = -1e30
LOG2E = 1.4426950408889634

LANES = 128
HALF = LANES // 2
TM = 512
TQ = 256
TK = 512
TME = 256
TMR = 256
NA_QROWS = 4
NA_BAND = NA_QROWS + NA_ROWS - 1
VMEM_LIMIT = 56 * 1024 * 1024

F32 = jnp.float32
BF16 = jnp.bfloat16

SH_M, SC_M, G_M, SH_F, SC_F, G_F = (slice(i * D, (i + 1) * D) for i in range(6))


def _params(*sem, flags=None):
    return pltpu.CompilerParams(dimension_semantics=sem, vmem_limit_bytes=VMEM_LIMIT, flags=flags)


def _mod_row(i):
    r0 = i * TM
    return jnp.where(r0 >= N_LAT, B, r0 // S)


def _rms(x, gain):
    return x * lax.rsqrt(jnp.mean(x * x, axis=-1, keepdims=True) + EPS) * gain


def _lane(shape):
    return lax.broadcasted_iota(jnp.int32, shape, len(shape) - 1)


def _ada_kernel(c_ref, w_ref, b_ref, o_ref):
    a = jax.nn.silu(c_ref[...])
    o_ref[...] = jnp.dot(a, w_ref[...], preferred_element_type=F32,
                         precision=lax.Precision.HIGHEST) + b_ref[...]


def _ada(c_all, w, b):
    tn = 1536
    return pl.pallas_call(
        _ada_kernel,
        out_shape=jax.ShapeDtypeStruct((8, 6 * D), F32),
        grid=(6 * D // tn,),
        in_specs=[pl.BlockSpec((8, D), lambda j: (0, 0)),
                  pl.BlockSpec((D, tn), lambda j: (0, j)),
                  pl.BlockSpec((1, tn), lambda j: (0, j))],
        out_specs=pl.BlockSpec((8, tn), lambda j: (0, j)),
        compiler_params=_params("arbitrary"),
        name="ada",
    )(c_all, w, b.reshape(1, 6 * D))


def _rope_chunk(t, cos, sin, half):
    lane = _lane(t.shape)
    rot = jnp.where((lane % (2 * half)) < half,
                    pltpu.roll(t, LANES - half, 1), pltpu.roll(t, half, 1))
    return t * cos + rot * sin


def _head_rms(t, gain, bd):
    t2 = t * t
    hi = t2.astype(BF16)
    lo = (t2 - hi.astype(F32)).astype(BF16)
    ss = (jnp.dot(hi, bd, preferred_element_type=F32)
          + jnp.dot(lo, bd, preferred_element_type=F32))
    return t * lax.rsqrt(ss * (1.0 / HEAD_DIM) + EPS) * gain


def _with_ones(v, lane, parity):
    if parity == 0:
        return jnp.where(lane < HALF, v, jnp.where(lane == LANES - 1, 1.0, 0.0))
    return jnp.where(lane >= HALF, v, jnp.where(lane == 0, 1.0, 0.0))


def _half_only(k, lane, parity):
    return jnp.where(lane < HALF, k, 0.0) if parity == 0 else jnp.where(lane >= HALF, k, 0.0)


def _put_v(v_ref, j, parity, v, v_t):
    v_ref[j, parity] = (v.T if v_t else v).astype(BF16)


def _rows_of(lat, ctx):
    n_lat = N_LAT // TM
    if ctx is None:
        return [pl.BlockSpec((TM, D), lambda i: (i, 0))], [lat]
    return ([pl.BlockSpec((TM, D), lambda i: (jnp.minimum(i, n_lat - 1), 0)),
             pl.BlockSpec((TM, D), lambda i: (jnp.maximum(i - n_lat, 0), 0))], [lat, ctx])


def _pick_rows(lat_ref, ctx_ref):
    if ctx_ref is None:
        return lat_ref[...]
    return jnp.where(pl.program_id(0) < N_LAT // TM, lat_ref[...], ctx_ref[...])


def _qkv_kernel(*refs, shared_kv, head_norm, rope, v_t, q_scale, split_x):
    refs = list(refs)
    x = _pick_rows(refs.pop(0), refs.pop(0) if split_x else None)
    (mod_ref, g_ref, w_ref, cos_ref, sin_ref, qg_ref, kg_ref, bd_ref, q_ref, k_ref, v_ref) = refs
    h = _rms(x, g_ref[...]) * (1.0 + mod_ref[:, SC_M]) + mod_ref[:, SH_M]
    acc = jnp.dot(h.astype(BF16), w_ref[...], preferred_element_type=F32)
    lane = _lane((TM, LANES))
    n_q = N_HEADS * HEAD_DIM // LANES
    n_kv = (N_KV_HEADS if shared_kv else N_HEADS) * HEAD_DIM // LANES

    def finish(t, gain_ref):
        if head_norm:
            t = _head_rms(t, gain_ref[...], bd_ref[...])
        if rope:
            t = _rope_chunk(t, cos_ref[...], sin_ref[...], HEAD_DIM // 2)
        return t

    for c in range(n_q):
        t = finish(acc[:, c * LANES:(c + 1) * LANES], qg_ref)
        q_ref[:, c * LANES:(c + 1) * LANES] = (t * q_scale).astype(BF16)
    for c in range(n_kv):
        k = finish(acc[:, (n_q + c) * LANES:(n_q + c + 1) * LANES], kg_ref)
        v = acc[:, (n_q + n_kv + c) * LANES:(n_q + n_kv + c + 1) * LANES]
        if shared_kv:
            k_sw = pltpu.roll(k, HALF, 1)
            v_sw = pltpu.roll(v, HALF, 1)
            for half in range(2):
                j = 2 * c + half
                for parity in range(2):
                    ksrc = k if parity == half else k_sw
                    vsrc = v if parity == half else v_sw
                    k_ref[j, parity] = _half_only(ksrc, lane, parity).astype(BF16)
                    _put_v(v_ref, j, parity, _with_ones(vsrc, lane, parity), v_t)
        else:
            for parity in range(2):
                k_ref[c, parity] = _half_only(k, lane, parity).astype(BF16)
                _put_v(v_ref, c, parity, _with_ones(v, lane, parity), v_t)


def _kv_out(groups, transposed):
    if transposed:
        return (pl.BlockSpec((groups, 2, LANES, TM), lambda i: (0, 0, 0, i)),
                jax.ShapeDtypeStruct((groups, 2, LANES, N_TOK), BF16))
    return (pl.BlockSpec((groups, 2, TM, LANES), lambda i: (0, 0, i, 0)),
            jax.ShapeDtypeStruct((groups, 2, N_TOK, LANES), BF16))


def _qkv(x, mod, gain, w, cos, sin, qg, kg, bd, *, shared_kv, head_norm, rope, v_t, q_scale,
         x_ctx=None):
    n_out = w.shape[1]
    x_specs, x_args = _rows_of(x, x_ctx)
    groups = N_KV_HEADS if shared_kv else N_HEADS // 2
    row = lambda i: (i, 0)
    const = lambda i: (0, 0)
    kv_spec, kv_shape = _kv_out(groups, False)
    v_spec, v_shape = _kv_out(groups, v_t)
    return pl.pallas_call(
        functools.partial(_qkv_kernel, shared_kv=shared_kv, head_norm=head_norm, rope=rope,
                          v_t=v_t, q_scale=q_scale, split_x=x_ctx is not None),
        out_shape=(jax.ShapeDtypeStruct((N_TOK, N_HEADS * HEAD_DIM), BF16), kv_shape, v_shape),
        grid=(N_TOK // TM,),
        in_specs=x_specs + [
                  pl.BlockSpec((None, 1, 6 * D), lambda i: (_mod_row(i), 0, 0)),
                  pl.BlockSpec((1, D), const),
                  pl.BlockSpec((D, n_out), const),
                  pl.BlockSpec((TM, LANES), _rope_row),
                  pl.BlockSpec((TM, LANES), _rope_row),
                  pl.BlockSpec((1, LANES), const),
                  pl.BlockSpec((1, LANES), const),
                  pl.BlockSpec((LANES, LANES), const)],
        out_specs=(pl.BlockSpec((TM, N_HEADS * HEAD_DIM), row), kv_spec, v_spec),
        compiler_params=_params("arbitrary"),
        name="qkv",
    )(*x_args, mod, gain, w, cos, sin, qg, kg, bd)


def _mla_kernel(x_ref, mod_ref, g_ref, wd_ref, qan_ref, kvan_ref, wq_ref, wkv_ref,
                cos_ref, sin_ref, q_ref, k_ref, v_ref):
    h = _rms(x_ref[...], g_ref[...]) * (1.0 + mod_ref[:, SC_M]) + mod_ref[:, SH_M]
    lat = jnp.dot(h.astype(BF16), wd_ref[...], preferred_element_type=F32)
    c_q = _rms(lat[:, :MLA_Q_LORA], qan_ref[...])
    c_kv = _rms(lat[:, MLA_Q_LORA:MLA_Q_LORA + MLA_KV_LORA], kvan_ref[...])
    cos = cos_ref[...]
    sin = sin_ref[...]
    k_rope = _rope_chunk(lat[:, MLA_Q_LORA + MLA_KV_LORA:], cos, sin, MLA_ROPE // 2)
    q = jnp.dot(c_q.astype(BF16), wq_ref[...], preferred_element_type=F32)
    kv = jnp.dot(c_kv.astype(BF16), wkv_ref[...], preferred_element_type=F32)
    scale = (MLA_NOPE + MLA_ROPE) ** -0.5 * LOG2E
    lane = _lane((TM, LANES))
    for hd in range(N_HEADS):
        sl = slice(hd * LANES, (hd + 1) * LANES)
        q_ref[:, sl] = (_rope_chunk(q[:, sl], cos, sin, MLA_ROPE // 2) * scale).astype(BF16)
        k_ref[hd // 2, hd % 2] = (kv[:, sl] + k_rope).astype(BF16)
        v = kv[:, N_HEADS * LANES + hd * LANES:N_HEADS * LANES + (hd + 1) * LANES]
        _put_v(v_ref, hd // 2, hd % 2, _with_ones(v, lane, hd % 2), True)


def _mla_proj(x, mod, gain, wd, qan, kvan, wq, wkv, cos, sin):
    row = lambda i: (i, 0)
    const = lambda i: (0, 0)
    kv_spec, kv_shape = _kv_out(N_HEADS // 2, False)
    v_spec, v_shape = _kv_out(N_HEADS // 2, True)
    return pl.pallas_call(
        _mla_kernel,
        out_shape=(jax.ShapeDtypeStruct((N_TOK, N_HEADS * LANES), BF16), kv_shape, v_shape),
        grid=(N_TOK // TM,),
        in_specs=[pl.BlockSpec((TM, D), row),
                  pl.BlockSpec((None, 1, 6 * D), lambda i: (_mod_row(i), 0, 0)),
                  pl.BlockSpec((1, D), const),
                  pl.BlockSpec(wd.shape, const),
                  pl.BlockSpec((1, MLA_Q_LORA), const),
                  pl.BlockSpec((1, MLA_KV_LORA), const),
                  pl.BlockSpec(wq.shape, const),
                  pl.BlockSpec(wkv.shape, const),
                  pl.BlockSpec((TM, LANES), _rope_row),
                  pl.BlockSpec((TM, LANES), _rope_row)],
        out_specs=(pl.BlockSpec((TM, N_HEADS * LANES), row), kv_spec, v_spec),
        compiler_params=_params("arbitrary"),
        name="mla_proj",
    )(x, mod, gain, wd, qan, kvan, wq, wkv, cos, sin)


def _attn_kernel(*refs, stacks, tq, subs, gp, mode, with_lat):
    refs = list(refs)
    sink_ref = refs.pop(0) if mode == "window" else None
    q_ref, kc_ref, vc_ref = refs[:3]
    refs = refs[3:]
    if with_lat:
        kl_ref, vl_ref = refs[:2]
        refs = refs[2:]
    tab_ref = refs.pop(0) if mode == "neighbourhood" else None
    o_ref = refs[0]
    s_refs = refs[1:]
    g = pl.program_id(1)
    t = pl.program_id(2)
    qw = q_ref.shape[1] // gp
    ow = o_ref.shape[1] // gp
    m_rows = tq * len(stacks[0][0])

    chunks, masks = [], []
    for sub in range(subs):
        ch, mask = [("ctx", 0, CTX)], None
        if with_lat and mode == "window":
            span = tq + 2 * WINDOW
            q0 = (t * subs + sub) * tq
            start = pl.multiple_of(jnp.clip(q0 - WINDOW, 0, S - span), WINDOW)
            qi = q0 + lax.broadcasted_iota(jnp.int32, (m_rows, span), 0) % tq
            ki = start + lax.broadcasted_iota(jnp.int32, (m_rows, span), 1)
            mask = jnp.abs(qi - ki) <= WINDOW
            ch.append(("lat", start, span))
        elif with_lat:
            start = pl.multiple_of(_na_band_start(t * subs + sub) * GRID_W, GRID_W)
            ch.append(("lat", start, NA_BAND * GRID_W))
        chunks.append(ch)
        masks.append(mask)
    cols = [[sum(w for _, _, w in ch[:c]) for c in range(len(ch))] for ch in chunks]

    items = [(gi, sub, qcs, kv) for gi in range(gp) for sub in range(subs) for qcs, kv in stacks]
    qs, sinks = [], []
    for gi, sub, qcs, kv in items:
        qs.append(jnp.concatenate(
            [q_ref[sub * tq:(sub + 1) * tq, gi * qw + qc * LANES:gi * qw + (qc + 1) * LANES]
             for qc, _ in qcs], axis=0))
        sink_col = None
        if mode == "window":
            hd = [(g * gp + gi) * 2 * len(qcs) + 2 * qc + kv for qc, _ in qcs]
            row = lax.broadcasted_iota(jnp.int32, (m_rows, 1), 0)
            sink_col = jnp.where(row < tq, sink_ref[hd[0]], sink_ref[hd[-1]])
        sinks.append(sink_col)

    def score_stage(it):
        gi, sub, _, kv = items[it]
        s_ref = s_refs[it % len(s_refs)]
        m_fold, m_col = None, sinks[it]
        for (kind, st, width), col in zip(chunks[sub], cols[sub]):
            k = kc_ref[gi, kv] if kind == "ctx" else kl_ref[gi, kv, pl.ds(st, width), :]
            s = lax.dot_general(qs[it], k, (((1,), (1,)), ((), ())), preferred_element_type=F32)
            if kind == "lat" and tab_ref is not None:
                s = s + tab_ref[2 * gi + kv]
            if kind == "lat" and masks[sub] is not None:
                s = jnp.where(masks[sub], s, NEG)
            s_ref[0:m_rows, col:col + width] = s
            if width % LANES == 0:
                for j in range(width // LANES):
                    slab = s[:, j * LANES:(j + 1) * LANES]
                    m_fold = slab if m_fold is None else jnp.maximum(m_fold, slab)
            else:
                mc = jnp.max(s, axis=-1, keepdims=True)
                m_col = mc if m_col is None else jnp.maximum(m_col, mc)
        m = jnp.max(m_fold, axis=-1, keepdims=True)
        return m if m_col is None else jnp.maximum(m, m_col)

    def prob_stage(it, m):
        sub = items[it][1]
        s_ref = s_refs[it % len(s_refs)]
        return [jnp.exp(s_ref[0:m_rows, col:col + width] - m).astype(BF16)
                for (_, _, width), col in zip(chunks[sub], cols[sub])]

    def value_stage(it, ps, m):
        gi, sub, _, kv = items[it]
        acc = jnp.zeros((m_rows, LANES), F32)
        for (kind, st, width), p in zip(chunks[sub], ps):
            v = vc_ref[gi, kv] if kind == "ctx" else vl_ref[gi, kv, pl.ds(st, width), :]
            acc = acc + jnp.dot(p, v, preferred_element_type=F32)
        den = acc[:, LANES - 1:LANES] if kv == 0 else acc[:, 0:1]
        if sinks[it] is not None:
            den = den + jnp.exp(sinks[it] - m)
        lane = _lane(acc.shape)
        return jnp.where(lane < HALF if kv == 0 else lane >= HALF, acc / den, 0.0)

    n = len(items)
    ms, ps, outs = [None] * n, [None] * n, {}
    for stage in range(n + 2):
        if 0 <= stage - 1 < n:
            ps[stage - 1] = prob_stage(stage - 1, ms[stage - 1])
        if stage < n:
            ms[stage] = score_stage(stage)
        if 0 <= stage - 2 < n:
            gi, sub, qcs, _ = items[stage - 2]
            o = value_stage(stage - 2, ps[stage - 2], ms[stage - 2])
            for i, (_, oc) in enumerate(qcs):
                key = (sub, gi * (ow // LANES) + oc)
                part = o[i * tq:(i + 1) * tq]
                outs[key] = part if key not in outs else outs[key] + part
    for (sub, oc), o in outs.items():
        o_ref[sub * tq:(sub + 1) * tq, oc * LANES:(oc + 1) * LANES] = o.astype(o_ref.dtype)


def _attention(q, k4, v4, *, stacks, mode, tq, subs=1, gp=1, sink=None, table=None,
               ctx_queries=False):
    groups = k4.shape[0] // gp
    qw = q.shape[1] // groups
    ow = N_HEADS * HEAD_DIM // groups
    with_lat = not ctx_queries
    tqs = tq * subs
    nq = S // tqs if with_lat else CTX // tqs
    row0 = 0 if with_lat else N_LAT // tqs

    q_spec = pl.BlockSpec((tqs, qw), lambda b, g, t: (row0 + b * nq + t, g))
    ctx_spec = pl.BlockSpec((gp, 2, CTX, LANES), lambda b, g, t: (g, 0, N_LAT // CTX + b, 0))
    lat_spec = pl.BlockSpec((gp, 2, S, LANES), lambda b, g, t: (g, 0, b, 0))
    in_specs = [q_spec, ctx_spec, ctx_spec]
    args = [q, k4, v4]
    n_keys = CTX
    if with_lat:
        in_specs += [lat_spec, lat_spec]
        args += [k4, v4]
        n_keys += {"window": tq + 2 * WINDOW, "neighbourhood": NA_BAND * GRID_W}[mode]
    if mode == "window":
        in_specs = [pl.BlockSpec(memory_space=pltpu.SMEM)] + in_specs
        args = [sink] + args
    if mode == "neighbourhood":
        in_specs.append(pl.BlockSpec((None, 2 * gp, tq, NA_BAND * GRID_W),
                                     lambda b, g, t: (_na_variant(t), g, 0, 0)))
        args.append(table)
    m_rows = tq * max(len(qcs) for qcs, _ in stacks)
    return pl.pallas_call(
        functools.partial(_attn_kernel, stacks=stacks, tq=tq, subs=subs, gp=gp, mode=mode,
                          with_lat=with_lat),
        out_shape=jax.ShapeDtypeStruct((N_LAT if with_lat else N_CTX, N_HEADS * HEAD_DIM), BF16),
        grid=(B, groups, nq),
        in_specs=in_specs,
        out_specs=pl.BlockSpec((tqs, ow), lambda b, g, t: (b * nq + t, g)),
        scratch_shapes=[pltpu.VMEM((m_rows, n_keys), F32) for _ in range(2)],
        compiler_params=_params("arbitrary", "arbitrary", "arbitrary"),
        name="attn_" + mode + ("" if with_lat else "_ctx"),
    )(*args)


def _attn_t_kernel(*refs, stacks, tq, subs, with_lat):
    refs = list(refs)
    q_ref, kc_ref, vtc_ref = refs[:3]
    refs = refs[3:]
    if with_lat:
        kl_ref, vtl_ref = refs[:2]
        refs = refs[2:]
    o_ref = refs[0]
    s_refs = refs[1:]
    chunks = [("ctx", 0, CTX)]
    if with_lat:
        chunks += [("lat", c * TK, TK) for c in range(S // TK)]
    rows = [sum(w for _, _, w in chunks[:c]) for c in range(len(chunks))]
    items = [(sub, qcs, kv) for sub in range(subs) for qcs, kv in stacks]
    qs = [jnp.concatenate([q_ref[sub * tq:(sub + 1) * tq, qc * LANES:(qc + 1) * LANES]
                           for qc, _ in qcs], axis=0) for sub, qcs, _ in items]
    cols = qs[0].shape[0]

    def score_chunk(it, c, m8):
        kind, st, width = chunks[c]
        kv = items[it][2]
        k = kc_ref[kv] if kind == "ctx" else kl_ref[kv, st:st + width, :]
        s = lax.dot_general(k, qs[it], (((1,), (1,)), ((), ())), preferred_element_type=F32)
        s_refs[it % len(s_refs)][rows[c]:rows[c] + width, 0:cols] = s
        c8 = jnp.max(s.reshape(width // 8, 8, cols), axis=0)
        return c8 if m8 is None else jnp.maximum(m8, c8)

    def prob_chunk(it, c, m):
        width = chunks[c][2]
        s = s_refs[it % len(s_refs)][rows[c]:rows[c] + width, 0:cols]
        return jnp.exp2(s - m).astype(BF16)

    def value_chunk(it, c, p, acc):
        kind, st, width = chunks[c]
        kv = items[it][2]
        vt = vtc_ref[kv] if kind == "ctx" else vtl_ref[kv, :, st:st + width]
        return acc + jnp.dot(vt, p, preferred_element_type=F32)

    n = len(items)
    ms, ps, accs = [None] * n, [[] for _ in range(n)], []
    for stage in range(n + 2):
        m8 = None
        acc = jnp.zeros((LANES, cols), F32)
        for c in range(len(chunks)):
            if 0 <= stage - 1 < n:
                ps[stage - 1].append(prob_chunk(stage - 1, c, ms[stage - 1]))
            if stage < n:
                m8 = score_chunk(stage, c, m8)
            if 0 <= stage - 2 < n:
                acc = value_chunk(stage - 2, c, ps[stage - 2][c], acc)
        if stage < n:
            ms[stage] = jnp.max(m8, axis=0, keepdims=True)
        if 0 <= stage - 2 < n:
            accs.append(acc)

    outs = {}
    for (sub, qcs, kv), acc in zip(items, accs):
        den = acc[LANES - 1:LANES, :] if kv == 0 else acc[0:1, :]
        o = (acc / den).T
        lane = _lane(o.shape)
        o = jnp.where(lane < HALF if kv == 0 else lane >= HALF, o, 0.0)
        for i, (_, oc) in enumerate(qcs):
            part = o[i * tq:(i + 1) * tq]
            outs[sub, oc] = part if (sub, oc) not in outs else outs[sub, oc] + part
    for (sub, oc), o in outs.items():
        o_ref[sub * tq:(sub + 1) * tq, oc * LANES:(oc + 1) * LANES] = o.astype(o_ref.dtype)


def _attention_t(q, k4, vt4, *, stacks, tq, subs=1, ctx_queries=False):
    groups = k4.shape[0]
    qw = q.shape[1] // groups
    ow = N_HEADS * HEAD_DIM // groups
    with_lat = not ctx_queries
    tqs = tq * subs
    nq = S // tqs if with_lat else CTX // tqs
    row0 = 0 if with_lat else N_LAT // tqs
    in_specs = [pl.BlockSpec((tqs, qw), lambda b, g, t: (row0 + b * nq + t, g)),
                pl.BlockSpec((None, 2, CTX, LANES), lambda b, g, t: (g, 0, N_LAT // CTX + b, 0)),
                pl.BlockSpec((None, 2, LANES, CTX), lambda b, g, t: (g, 0, 0, N_LAT // CTX + b))]
    args = [q, k4, vt4]
    n_keys = CTX
    if with_lat:
        in_specs += [pl.BlockSpec((None, 2, S, LANES), lambda b, g, t: (g, 0, b, 0)),
                     pl.BlockSpec((None, 2, LANES, S), lambda b, g, t: (g, 0, 0, b))]
        args += [k4, vt4]
        n_keys += S
    m_rows = tq * max(len(qcs) for qcs, _ in stacks)
    return pl.pallas_call(
        functools.partial(_attn_t_kernel, stacks=stacks, tq=tq, subs=subs, with_lat=with_lat),
        out_shape=jax.ShapeDtypeStruct((N_LAT if with_lat else N_CTX, N_HEADS * HEAD_DIM), BF16),
        grid=(B, groups, nq),
        in_specs=in_specs,
        out_specs=pl.BlockSpec((tqs, ow), lambda b, g, t: (b * nq + t, g)),
        scratch_shapes=[pltpu.VMEM((n_keys, m_rows), F32) for _ in range(2)],
        compiler_params=_params("arbitrary", "arbitrary", "arbitrary"),
        name="attn_global" + ("" if with_lat else "_ctx"),
    )(*args)


def _na_band_start(t):
    return jnp.clip(t * NA_QROWS - NA_ROWS // 2, 0, S // GRID_W - NA_BAND)


def _na_tiles():
    rows = S // GRID_W
    tiles = [(t * NA_QROWS, min(max(t * NA_QROWS - NA_ROWS // 2, 0), rows - NA_BAND))
             for t in range(rows // NA_QROWS)]
    lo = [rb for rb in tiles if rb[0] - NA_ROWS // 2 < 0]
    hi = [rb for rb in tiles if rb[0] - NA_ROWS // 2 > rows - NA_BAND]
    return lo + [tiles[len(lo)]] + hi, len(lo), len(hi)


def _na_variant(t):
    _, n_lo, n_hi = _na_tiles()
    first_hi = S // GRID_W // NA_QROWS - n_hi
    return jnp.where(t < n_lo, t, jnp.where(t >= first_hi, t - first_hi + n_lo + 1, n_lo))


def _bias_cols_kernel(rpb_ref, sel_ref, o_ref):
    o_ref[...] = jnp.dot(rpb_ref[...], sel_ref[...], preferred_element_type=F32,
                         precision=lax.Precision.HIGHEST)


def _na_table(rpb):
    rows = S // GRID_W
    n_dr, n_dc = 2 * NA_ROWS - 1, 2 * NA_COLS - 1
    c = np.arange(GRID_W)[:, None]
    kc = np.arange(GRID_W)[None, :]
    cs = np.clip(c - NA_COLS // 2, 0, GRID_W - NA_COLS)
    col_ok = (kc >= cs) & (kc < cs + NA_COLS)
    sel = (np.arange(LANES)[:, None, None] == (kc - c + NA_COLS - 1)[None]) & col_ok[None]
    sel = jnp.asarray(sel.reshape(LANES, GRID_W * GRID_W), F32)
    rpb2 = jnp.pad(rpb.reshape(N_HEADS * n_dr, n_dc), ((0, 256 - N_HEADS * n_dr), (0, LANES - n_dc)))
    cols = pl.pallas_call(
        _bias_cols_kernel,
        out_shape=jax.ShapeDtypeStruct((256, GRID_W * GRID_W), F32),
        name="na_bias_cols",
    )(rpb2, sel)
    cols = cols[:N_HEADS * n_dr].reshape(N_HEADS, n_dr, GRID_W, GRID_W)
    cols = jnp.where(col_ok[None, None], cols, NEG)
    masked = jnp.full((N_HEADS, GRID_W, GRID_W), NEG, F32)
    variants = []
    for r0, bs in _na_tiles()[0]:
        q_rows = []
        for r in range(r0, r0 + NA_QROWS):
            rs = min(max(r - NA_ROWS // 2, 0), rows - NA_ROWS)
            q_rows.append(jnp.concatenate(
                [cols[:, kr - r + NA_ROWS - 1] if rs <= kr < rs + NA_ROWS else masked
                 for kr in range(bs, bs + NA_BAND)], axis=2))
        variants.append(jnp.concatenate(q_rows, axis=1))
    return jnp.stack(variants)


def _router_route(h, wr_ref):
    w = wr_ref[...]
    h_hi, w_hi = h.astype(BF16), w.astype(BF16)
    h_lo = (h - h_hi.astype(F32)).astype(BF16)
    w_lo = (w - w_hi.astype(F32)).astype(BF16)
    logits = (jnp.dot(h_hi, w_hi, preferred_element_type=F32)
              + (jnp.dot(h_hi, w_lo, preferred_element_type=F32)
                 + jnp.dot(h_lo, w_hi, preferred_element_type=F32)))
    lane = _lane(logits.shape).astype(F32)
    lg = jnp.where(lane < N_EXPERTS, logits, -jnp.inf)
    m1 = jnp.max(lg, axis=-1, keepdims=True)
    i1 = jnp.min(jnp.where(lg == m1, lane, float(LANES)), axis=-1, keepdims=True)
    lg2 = jnp.where(lane == i1, -jnp.inf, lg)
    m2 = jnp.max(lg2, axis=-1, keepdims=True)
    i2 = jnp.min(jnp.where(lg2 == m2, lane, float(LANES)), axis=-1, keepdims=True)
    e = jnp.exp(m2 - m1)
    return jnp.where(lane == 0, i1, jnp.where(lane == 1, i2, jnp.where(
        lane == 2, 1.0 / (1.0 + e), jnp.where(lane == 3, e / (1.0 + e), 0.0))))


def _oproj_kernel(*refs, moe, with_ctx, split_x):
    refs = list(refs)
    o = _pick_rows(refs.pop(0), refs.pop(0) if with_ctx else None)
    x = _pick_rows(refs.pop(0), refs.pop(0) if split_x else None)
    if moe:
        wo_ref, mod_ref, post_ref, pre_ref, wr_ref, xo_ref, h_ref, route_ref = refs
    else:
        wo_ref, mod_ref, post_ref, pre_ref, xo_ref, h_ref = refs
    y = jnp.dot(o, wo_ref[...], preferred_element_type=F32)
    x1 = x + mod_ref[:, G_M] * _rms(y, post_ref[...])
    xo_ref[...] = x1
    h = _rms(x1, pre_ref[...]) * (1.0 + mod_ref[:, SC_F]) + mod_ref[:, SH_F]
    h_ref[...] = h.astype(h_ref.dtype)
    if moe:
        route_ref[...] = _router_route(h, wr_ref)


def _oproj(o, o_ctx, wo, x, mod, post, pre, w_router, x_ctx=None):
    moe = w_router is not None
    with_ctx = o_ctx is not None
    n_rows = N_TOK if with_ctx else N_LAT
    row = lambda i: (i, 0)
    const = lambda i: (0, 0)
    o_specs, o_args = _rows_of(o, o_ctx)
    x_specs, x_args = _rows_of(x, x_ctx)
    in_specs = o_specs + x_specs + [
        pl.BlockSpec((D, D), const),
        pl.BlockSpec((None, 1, 6 * D), lambda i: (_mod_row(i), 0, 0)),
        pl.BlockSpec((1, D), const), pl.BlockSpec((1, D), const)]
    args = o_args + x_args + [wo, mod, post, pre]
    out_shape = [jax.ShapeDtypeStruct((n_rows, D), F32),
                 jax.ShapeDtypeStruct((n_rows, D), F32 if moe else BF16)]
    out_specs = [pl.BlockSpec((TM, D), row), pl.BlockSpec((TM, D), row)]
    if moe:
        in_specs.append(pl.BlockSpec((D, LANES), const))
        out_shape.append(jax.ShapeDtypeStruct((n_rows, LANES), F32))
        out_specs.append(pl.BlockSpec((TM, LANES), row))
        args.append(w_router)
    return pl.pallas_call(
        functools.partial(_oproj_kernel, moe=moe, with_ctx=with_ctx, split_x=x_ctx is not None),
        out_shape=tuple(out_shape),
        grid=(n_rows // TM,),
        in_specs=in_specs,
        out_specs=tuple(out_specs),
        compiler_params=_params("arbitrary"),
        name="oproj",
    )(*args)


def _swiglu(h, gu_ref, dw_ref):
    gu = jnp.dot(h, gu_ref[...], preferred_element_type=F32)
    half = gu.shape[1] // 2
    act = (jax.nn.silu(gu[:, :half]) * gu[:, half:]).astype(BF16)
    return jnp.dot(act, dw_ref[...], preferred_element_type=F32)


def _ffn_kernel(h_ref, gu_ref, dw_ref, x_ref, mod_ref, post_ref, o_ref):
    y = _swiglu(h_ref[...], gu_ref, dw_ref)
    o_ref[...] = x_ref[...] + mod_ref[:, G_F] * _rms(y, post_ref[...])


def _ffn(h, w_gu, w_down, x, mod, post):
    row = lambda i: (i, 0)
    const = lambda i: (0, 0)
    resident = pl.Buffered(1)
    return pl.pallas_call(
        _ffn_kernel,
        out_shape=jax.ShapeDtypeStruct((N_TOK, D), F32),
        grid=(N_TOK // TM,),
        in_specs=[pl.BlockSpec((TM, D), row),
                  pl.BlockSpec((D, 2 * D_FF), const, pipeline_mode=resident),
                  pl.BlockSpec((D_FF, D), const, pipeline_mode=resident),
                  pl.BlockSpec((TM, D), row),
                  pl.BlockSpec((None, 1, 6 * D), lambda i: (_mod_row(i), 0, 0)),
                  pl.BlockSpec((1, D), const)],
        out_specs=pl.BlockSpec((TM, D), row),
        compiler_params=_params("arbitrary"),
        name="ffn_dense",
    )(h, w_gu, w_down, x, mod, post)


def _routing(route, n_rows):
    i1 = route[:, 0].astype(jnp.int32)
    i2 = route[:, 1].astype(jnp.int32)
    e = jnp.arange(N_EXPERTS, dtype=jnp.int32)
    hit1 = i1[:, None] == e
    hit2 = i2[:, None] == e
    sel = hit1.astype(jnp.int32) + hit2.astype(jnp.int32)
    cum = jnp.cumsum(sel, axis=0)
    tiles = (cum[-1] + TME - 1) // TME
    tile_end = jnp.cumsum(tiles)
    slot = ((tile_end - tiles) * TME)[None] + cum - sel
    slots = jnp.stack([jnp.sum(jnp.where(hit1, slot, 0), axis=1),
                       jnp.sum(jnp.where(hit2, slot, 0), axis=1)], axis=1).reshape(-1)
    j = jnp.arange(_n_slot_tiles(n_rows), dtype=jnp.int32)
    tile_expert = jnp.minimum(jnp.sum(j[:, None] >= tile_end[None], axis=1), N_EXPERTS - 1)
    return slots.astype(jnp.int32), tile_expert.astype(jnp.int32), tile_end.astype(jnp.int32)


def _n_slot_tiles(n_rows):
    return 2 * n_rows // TME + N_EXPERTS


def _row_copies(slots_ref, n, copy):
    def start(r, carry):
        for k in range(2):
            copy(r, k, slots_ref[2 * r + k]).start(priority=k)
        return carry

    def wait(r, carry):
        for k in range(2):
            copy(r, k, slots_ref[2 * r + k]).wait()
        return carry

    lax.fori_loop(0, n, start, 0, unroll=8)
    lax.fori_loop(0, n, wait, 0, unroll=8)


def _dispatch_kernel(tend_ref, slots_ref, h_ref, xs_ref, zero_ref, zsem, sem, *, n_tiles):
    @pl.when(pl.program_id(0) == 0)
    def _():
        zero_ref[...] = jnp.zeros_like(zero_ref)
        tiles = []
        for e in range(N_EXPERTS):
            first = tend_ref[e - 1] if e else 0
            tiles.append((tend_ref[e] > first, tend_ref[e] - 1))
        for u in range(N_EXPERTS):
            tile = tend_ref[N_EXPERTS - 1] + u
            tiles.append((tile < n_tiles, tile))
        copies = [(ok, pltpu.make_async_copy(
            zero_ref, xs_ref.at[pl.ds(pl.multiple_of(jnp.maximum(tile, 0) * TME, TME), TME)], zsem))
            for ok, tile in tiles]
        for ok, cp in copies:
            pl.when(ok)(cp.start)
        for ok, cp in copies:
            pl.when(ok)(cp.wait)

    _row_copies(slots_ref, TMR, lambda r, k, s: pltpu.make_async_copy(
        h_ref.at[pl.ds(r, 1)], xs_ref.at[pl.ds(s, 1)], sem))


def _dispatch(tile_end, slots, h, n_rows):
    n_tiles = _n_slot_tiles(n_rows)
    return pl.pallas_call(
        functools.partial(_dispatch_kernel, n_tiles=n_tiles),
        out_shape=jax.ShapeDtypeStruct((n_tiles * TME, D), F32),
        grid_spec=pltpu.PrefetchScalarGridSpec(
            num_scalar_prefetch=1,
            grid=(n_rows // TMR,),
            in_specs=[pl.BlockSpec((2 * TMR,), lambda i, te: (i,), memory_space=pltpu.SMEM),
                      pl.BlockSpec((TMR, D), lambda i, te: (i, 0))],
            out_specs=pl.BlockSpec(memory_space=pl.ANY),
            scratch_shapes=[pltpu.VMEM((TME, D), F32), pltpu.SemaphoreType.DMA(()),
                            pltpu.SemaphoreType.DMA(())]),
        compiler_params=_params("arbitrary"),
        name="moe_dispatch",
    )(tile_end, slots, h)


def _expert_kernel(te_ref, tend_ref, x_ref, gu_ref, dw_ref, o_ref, gub_ref, db_ref):
    j = pl.program_id(0)
    valid = j < tend_ref[N_EXPERTS - 1]

    @pl.when(jnp.logical_or(j == 0, te_ref[j] != te_ref[jnp.maximum(j - 1, 0)]))
    def _():
        gub_ref[...] = gu_ref[...].astype(BF16)
        db_ref[...] = dw_ref[...].astype(BF16)

    @pl.when(valid)
    def _():
        o_ref[...] = _swiglu(x_ref[...].astype(BF16), gub_ref, db_ref)

    @pl.when(jnp.logical_not(valid))
    def _():
        o_ref[...] = jnp.zeros_like(o_ref)


def _experts(tile_expert, tile_end, xs, w_gu, w_down):
    n_tiles = xs.shape[0] // TME
    return pl.pallas_call(
        _expert_kernel,
        out_shape=jax.ShapeDtypeStruct(xs.shape, F32),
        grid_spec=pltpu.PrefetchScalarGridSpec(
            num_scalar_prefetch=2,
            grid=(n_tiles,),
            in_specs=[pl.BlockSpec((TME, D), lambda j, te, tend: (j, 0)),
                      pl.BlockSpec((None, D, 2 * D_FF_EXPERT), lambda j, te, tend: (te[j], 0, 0)),
                      pl.BlockSpec((None, D_FF_EXPERT, D), lambda j, te, tend: (te[j], 0, 0))],
            out_specs=pl.BlockSpec((TME, D), lambda j, te, tend: (j, 0)),
            scratch_shapes=[pltpu.VMEM((D, 2 * D_FF_EXPERT), BF16),
                            pltpu.VMEM((D_FF_EXPERT, D), BF16)]),
        compiler_params=_params("arbitrary"),
        name="moe_experts",
    )(tile_expert, tile_end, xs, w_gu, w_down)


def _combine_kernel(slots_ref, route_ref, x_ref, mod_ref, post_ref, ys_ref, o_ref, buf_ref, sem):
    _row_copies(slots_ref, TMR, lambda r, k, s: pltpu.make_async_copy(
        ys_ref.at[pl.ds(s, 1)], buf_ref.at[k, pl.ds(r, 1)], sem))
    y = route_ref[:, 2:3] * buf_ref[0] + route_ref[:, 3:4] * buf_ref[1]
    o_ref[...] = x_ref[...] + mod_ref[:, G_F] * _rms(y, post_ref[...])


def _combine(slots, route, x, mod, post, ys, n_rows):
    return pl.pallas_call(
        _combine_kernel,
        out_shape=jax.ShapeDtypeStruct((n_rows, D), F32),
        grid=(n_rows // TMR,),
        in_specs=[pl.BlockSpec((2 * TMR,), lambda i: (i,), memory_space=pltpu.SMEM),
                  pl.BlockSpec((TMR, LANES), lambda i: (i, 0)),
                  pl.BlockSpec((TMR, D), lambda i: (i, 0)),
                  pl.BlockSpec((None, 1, 6 * D), lambda i: (_mod_row(i * TMR // TM), 0, 0)),
                  pl.BlockSpec((1, D), lambda i: (0, 0)),
                  pl.BlockSpec(memory_space=pl.ANY)],
        out_specs=pl.BlockSpec((TMR, D), lambda i: (i, 0)),
        scratch_shapes=[pltpu.VMEM((2, TMR, D), F32), pltpu.SemaphoreType.DMA(())],
        compiler_params=_params("arbitrary"),
        name="moe_combine",
    )(slots, route, x, mod, post, ys)


def _moe(h, route, w_gu, w_down, x, mod, post, n_rows):
    slots, tile_expert, tile_end = _routing(route, n_rows)
    xs = _dispatch(tile_end, slots, h, n_rows)
    ys = _experts(tile_expert, tile_end, xs, w_gu, w_down)
    return _combine(slots, route, x, mod, post, ys, n_rows)


def _rope_tables(rot_dim, lead, tail):
    t = jnp.arange(S, dtype=jnp.int32)
    row = (t // GRID_W).astype(F32)
    col = (t % GRID_W).astype(F32)
    quarter = rot_dim // 4
    inv = ROPE_THETA ** (-jnp.arange(quarter, dtype=F32) / quarter)
    ang = jnp.concatenate([row[:, None] * inv, col[:, None] * inv], axis=-1)
    cos, sin = jnp.cos(ang), jnp.sin(ang)
    reps = (LANES - lead - tail) // rot_dim
    cos_l = jnp.concatenate([jnp.ones((S, lead), F32)] + [cos, cos] * reps + [jnp.ones((S, tail), F32)], axis=1)
    sin_l = jnp.concatenate([jnp.zeros((S, lead), F32)] + [-sin, sin] * reps + [jnp.zeros((S, tail), F32)], axis=1)
    cos_all = jnp.concatenate([cos_l, jnp.ones((TM, LANES), F32)], axis=0)
    sin_all = jnp.concatenate([sin_l, jnp.zeros((TM, LANES), F32)], axis=0)
    return cos_all, sin_all


def _rope_row(i):
    return (jnp.where(i < N_LAT // TM, i % (S // TM), S // TM), 0)


def _mla_weights(w_dkv, w_q_b, w_kv_b):
    lat = MLA_Q_LORA + MLA_KV_LORA
    pad = LANES - MLA_NOPE - MLA_ROPE
    wd = jnp.concatenate([w_dkv[:, :lat], jnp.zeros((D, MLA_NOPE), F32), w_dkv[:, lat:],
                          jnp.zeros((D, pad), F32)], axis=1)
    wq = jnp.pad(w_q_b.reshape(MLA_Q_LORA, N_HEADS, MLA_NOPE + MLA_ROPE), ((0, 0), (0, 0), (0, pad)))
    kvr = w_kv_b.reshape(MLA_KV_LORA, N_HEADS, 2 * HALF)
    k_part, v_part = kvr[..., :HALF], kvr[..., HALF:]
    zero = jnp.zeros_like(v_part)
    wk = jnp.concatenate([k_part, zero], axis=-1)
    even = (jnp.arange(N_HEADS) % 2 == 0)[None, :, None]
    wv = jnp.where(even, jnp.concatenate([v_part, zero], -1), jnp.concatenate([zero, v_part], -1))
    wkv = jnp.concatenate([wk.reshape(MLA_KV_LORA, -1), wv.reshape(MLA_KV_LORA, -1)], axis=1)
    return wd.astype(BF16), wq.reshape(MLA_Q_LORA, -1).astype(BF16), wkv.astype(BF16)


GQA_STACKS = ((((0, 0), (1, 1)), 0), (((0, 0), (1, 1)), 1))
MLA_STACKS = ((((0, 0),), 0), (((1, 0),), 1))
NA_STACKS = ((((0, 0),), 0), (((0, 0),), 1))


def kernel(x, c, ctx, c_ctx,
           l0_ada_w, l0_ada_b, l0_mix_pre, l0_mix_post, l0_ffn_pre, l0_ffn_post,
           l0_w_qkv, l0_q_norm, l0_k_norm, l0_w_o, l0_ffn_w_gu, l0_ffn_w_down,
           l1_ada_w, l1_ada_b, l1_mix_pre, l1_mix_post, l1_ffn_pre, l1_ffn_post,
           l1_w_qkv, l1_sink, l1_w_o, l1_router, l1_moe_w_gu, l1_moe_w_down,
           l2_ada_w, l2_ada_b, l2_mix_pre, l2_mix_post, l2_ffn_pre, l2_ffn_post,
           l2_w_dkv, l2_q_a_norm, l2_w_q_b, l2_kv_a_norm, l2_w_kv_b, l2_w_o, l2_ffn_w_gu, l2_ffn_w_down,
           l3_ada_w, l3_ada_b, l3_mix_pre, l3_mix_post, l3_ffn_pre, l3_ffn_post,
           l3_w_qkv, l3_rpb, l3_w_o, l3_router, l3_moe_w_gu, l3_moe_w_down):
    vec = lambda a: a.reshape(1, -1)
    tile2 = lambda a: jnp.tile(a, 2).reshape(1, LANES)
    bf = lambda a: a.astype(BF16)
    pad_router = lambda w: jnp.pad(w, ((0, 0), (0, LANES - N_EXPERTS)))

    x_lat, x_ctx = x.reshape(N_LAT, D), ctx.reshape(N_CTX, D)
    c_all = jnp.concatenate([c, c_ctx[None], jnp.zeros((8 - B - 1, D), F32)], axis=0)
    mods = [_ada(c_all, w, b).reshape(8, 1, 6 * D)
            for w, b in ((l0_ada_w, l0_ada_b), (l1_ada_w, l1_ada_b), (l2_ada_w, l2_ada_b), (l3_ada_w, l3_ada_b))]
    cos64, sin64 = _rope_tables(HEAD_DIM, 0, 0)
    cos32, sin32 = _rope_tables(MLA_ROPE, MLA_NOPE, LANES - MLA_NOPE - MLA_ROPE)
    ones = jnp.ones((1, LANES), F32)
    idx = np.arange(LANES)
    bd = jnp.asarray(idx[:, None] // HALF == idx[None, :] // HALF, BF16)

    q, k4, v4 = _qkv(x_lat, mods[0], vec(l0_mix_pre), bf(l0_w_qkv), cos64, sin64,
                     tile2(l0_q_norm), tile2(l0_k_norm), bd, shared_kv=True, head_norm=True, rope=True,
                     v_t=True, q_scale=HEAD_DIM ** -0.5 * LOG2E, x_ctx=x_ctx)
    o = _attention_t(q, k4, v4, stacks=GQA_STACKS, tq=TQ, subs=4)
    oc = _attention_t(q, k4, v4, stacks=GQA_STACKS, tq=CTX, ctx_queries=True)
    xs, h = _oproj(o, oc, bf(l0_w_o), x_lat, mods[0], vec(l0_mix_post), vec(l0_ffn_pre), None,
                   x_ctx=x_ctx)
    xs = _ffn(h, bf(l0_ffn_w_gu), bf(l0_ffn_w_down), xs, mods[0], vec(l0_ffn_post))

    q, k4, v4 = _qkv(xs, mods[1], vec(l1_mix_pre), bf(l1_w_qkv), cos64, sin64,
                     ones, ones, bd, shared_kv=True, head_norm=False, rope=True,
                     v_t=False, q_scale=HEAD_DIM ** -0.5)
    o = _attention(q, k4, v4, stacks=GQA_STACKS, mode="window", tq=TQ, subs=4, sink=l1_sink)
    oc = _attention(q, k4, v4, stacks=GQA_STACKS, mode="window", tq=CTX, sink=l1_sink, ctx_queries=True)
    xs, h, route = _oproj(o, oc, bf(l1_w_o), xs, mods[1], vec(l1_mix_post), vec(l1_ffn_pre),
                          pad_router(l1_router))
    xs = _moe(h, route, l1_moe_w_gu, l1_moe_w_down, xs, mods[1], vec(l1_ffn_post), N_TOK)

    wd, wq, wkv = _mla_weights(l2_w_dkv, l2_w_q_b, l2_w_kv_b)
    q, k4, v4 = _mla_proj(xs, mods[2], vec(l2_mix_pre), wd, vec(l2_q_a_norm), vec(l2_kv_a_norm),
                          wq, wkv, cos32, sin32)
    o = _attention_t(q, k4, v4, stacks=MLA_STACKS, tq=2 * TQ, subs=4)
    oc = _attention_t(q, k4, v4, stacks=MLA_STACKS, tq=CTX, ctx_queries=True)
    xs, h = _oproj(o, oc, bf(l2_w_o), xs, mods[2], vec(l2_mix_post), vec(l2_ffn_pre), None)
    xs = _ffn(h, bf(l2_ffn_w_gu), bf(l2_ffn_w_down), xs, mods[2], vec(l2_ffn_post))

    q, k4, v4 = _qkv(xs, mods[3], vec(l3_mix_pre), bf(l3_w_qkv), cos64, sin64,
                     ones, ones, bd, shared_kv=False, head_norm=False, rope=False,
                     v_t=False, q_scale=HEAD_DIM ** -0.5)
    o = _attention(q, k4, v4, stacks=NA_STACKS, mode="neighbourhood", tq=NA_QROWS * GRID_W, gp=4,
                   table=_na_table(l3_rpb))
    xl, h, route = _oproj(o, None, bf(l3_w_o), xs, mods[3], vec(l3_mix_post), vec(l3_ffn_pre),
                          pad_router(l3_router))
    xl = _moe(h, route, l3_moe_w_gu, l3_moe_w_down, xl, mods[3], vec(l3_ffn_post), N_LAT)
    return xl.reshape(B, S, D)
```

```python
import functools

import numpy as np
import jax
import jax.numpy as jnp
from jax import lax
from jax.experimental import pallas as pl
from jax.experimental.pallas import tpu as pltpu

D = 1024
B = 4
S = 4096
CTX = 256
GRID_W = 64
N_LAT = B * S
N_CTX = B * CTX
N_TOK = N_LAT + N_CTX
HEAD_DIM = 64
N_HEADS = 16
N_KV_HEADS = 4
WINDOW = 128
MLA_Q_LORA = 384
MLA_KV_LORA = 256
MLA_NOPE = 64
MLA_ROPE = 32
NA_ROWS = 8
NA_COLS = 16
D_FF = 2816
N_EXPERTS = 8
D_FF_EXPERT = 1408
ROPE_THETA = 10000.0
EPS = 1e-6
NEG = -1e30
LOG2E = 1.4426950408889634

LANES = 128
HALF = LANES // 2
TM = 512
TQ = 256
TK = 512
TME = 256
TMR = 256
NA_QROWS = 4
NA_BAND = NA_QROWS + NA_ROWS - 1
VMEM_LIMIT = 56 * 1024 * 1024

F32 = jnp.float32
BF16 = jnp.bfloat16

SH_M, SC_M, G_M, SH_F, SC_F, G_F = (slice(i * D, (i + 1) * D) for i in range(6))


def _params(*sem, flags=None):
    return pltpu.CompilerParams(dimension_semantics=sem, vmem_limit_bytes=VMEM_LIMIT, flags=flags)


def _mod_row(i):
    r0 = i * TM
    return jnp.where(r0 >= N_LAT, B, r0 // S)


def _rms(x, gain):
    return x * lax.rsqrt(jnp.mean(x * x, axis=-1, keepdims=True) + EPS) * gain


def _lane(shape):
    return lax.broadcasted_iota(jnp.int32, shape, len(shape) - 1)


def _ada_kernel(c_ref, w_ref, b_ref, o_ref):
    a = jax.nn.silu(c_ref[...])
    o_ref[...] = jnp.dot(a, w_ref[...], preferred_element_type=F32,
                         precision=lax.Precision.HIGHEST) + b_ref[...]


def _ada(c_all, w, b):
    tn = 1536
    return pl.pallas_call(
        _ada_kernel,
        out_shape=jax.ShapeDtypeStruct((8, 6 * D), F32),
        grid=(6 * D // tn,),
        in_specs=[pl.BlockSpec((8, D), lambda j: (0, 0)),
                  pl.BlockSpec((D, tn), lambda j: (0, j)),
                  pl.BlockSpec((1, tn), lambda j: (0, j))],
        out_specs=pl.BlockSpec((8, tn), lambda j: (0, j)),
        compiler_params=_params("arbitrary"),
        name="ada",
    )(c_all, w, b.reshape(1, 6 * D))


def _rope_chunk(t, cos, sin, half):
    lane = _lane(t.shape)
    rot = jnp.where((lane % (2 * half)) < half,
                    pltpu.roll(t, LANES - half, 1), pltpu.roll(t, half, 1))
    return t * cos + rot * sin


def _head_rms(t, gain, bd):
    t2 = t * t
    hi = t2.astype(BF16)
    lo = (t2 - hi.astype(F32)).astype(BF16)
    ss = (jnp.dot(hi, bd, preferred_element_type=F32)
          + jnp.dot(lo, bd, preferred_element_type=F32))
    return t * lax.rsqrt(ss * (1.0 / HEAD_DIM) + EPS) * gain


def _with_ones(v, lane, parity):
    if parity == 0:
        return jnp.where(lane < HALF, v, jnp.where(lane == LANES - 1, 1.0, 0.0))
    return jnp.where(lane >= HALF, v, jnp.where(lane == 0, 1.0, 0.0))


def _half_only(k, lane, parity):
    return jnp.where(lane < HALF, k, 0.0) if parity == 0 else jnp.where(lane >= HALF, k, 0.0)


def _put_v(v_ref, j, parity, v, v_t):
    v_ref[j, parity] = (v.T if v_t else v).astype(BF16)


def _rows_of(lat, ctx):
    n_lat = N_LAT // TM
    if ctx is None:
        return [pl.BlockSpec((TM, D), lambda i: (i, 0))], [lat]
    return ([pl.BlockSpec((TM, D), lambda i: (jnp.minimum(i, n_lat - 1), 0)),
             pl.BlockSpec((TM, D), lambda i: (jnp.maximum(i - n_lat, 0), 0))], [lat, ctx])


def _pick_rows(lat_ref, ctx_ref):
    if ctx_ref is None:
        return lat_ref[...]
    return jnp.where(pl.program_id(0) < N_LAT // TM, lat_ref[...], ctx_ref[...])


def _qkv_kernel(*refs, shared_kv, head_norm, rope, v_t, q_scale, split_x):
    refs = list(refs)
    x = _pick_rows(refs.pop(0), refs.pop(0) if split_x else None)
    (mod_ref, g_ref, w_ref, cos_ref, sin_ref, qg_ref, kg_ref, bd_ref, q_ref, k_ref, v_ref) = refs
    h = _rms(x, g_ref[...]) * (1.0 + mod_ref[:, SC_M]) + mod_ref[:, SH_M]
    acc = jnp.dot(h.astype(BF16), w_ref[...], preferred_element_type=F32)
    lane = _lane((TM, LANES))
    n_q = N_HEADS * HEAD_DIM // LANES
    n_kv = (N_KV_HEADS if shared_kv else N_HEADS) * HEAD_DIM // LANES

    def finish(t, gain_ref):
        if head_norm:
            t = _head_rms(t, gain_ref[...], bd_ref[...])
        if rope:
            t = _rope_chunk(t, cos_ref[...], sin_ref[...], HEAD_DIM // 2)
        return t

    for c in range(n_q):
        t = finish(acc[:, c * LANES:(c + 1) * LANES], qg_ref)
        q_ref[:, c * LANES:(c + 1) * LANES] = (t * q_scale).astype(BF16)
    for c in range(n_kv):
        k = finish(acc[:, (n_q + c) * LANES:(n_q + c + 1) * LANES], kg_ref)
        v = acc[:, (n_q + n_kv + c) * LANES:(n_q + n_kv + c + 1) * LANES]
        if shared_kv:
            k_sw = pltpu.roll(k, HALF, 1)
            v_sw = pltpu.roll(v, HALF, 1)
            for half in range(2):
                j = 2 * c + half
                for parity in range(2):
                    ksrc = k if parity == half else k_sw
                    vsrc = v if parity == half else v_sw
                    k_ref[j, parity] = _half_only(ksrc, lane, parity).astype(BF16)
                    _put_v(v_ref, j, parity, _with_ones(vsrc, lane, parity), v_t)
        else:
            for parity in range(2):
                k_ref[c, parity] = _half_only(k, lane, parity).astype(BF16)
                _put_v(v_ref, c, parity, _with_ones(v, lane, parity), v_t)


def _kv_out(groups, transposed):
    if transposed:
        return (pl.BlockSpec((groups, 2, LANES, TM), lambda i: (0, 0, 0, i)),
                jax.ShapeDtypeStruct((groups, 2, LANES, N_TOK), BF16))
    return (pl.BlockSpec((groups, 2, TM, LANES), lambda i: (0, 0, i, 0)),
            jax.ShapeDtypeStruct((groups, 2, N_TOK, LANES), BF16))


def _qkv(x, mod, gain, w, cos, sin, qg, kg, bd, *, shared_kv, head_norm, rope, v_t, q_scale,
         x_ctx=None):
    n_out = w.shape[1]
    x_specs, x_args = _rows_of(x, x_ctx)
    groups = N_KV_HEADS if shared_kv else N_HEADS // 2
    row = lambda i: (i, 0)
    const = lambda i: (0, 0)
    kv_spec, kv_shape = _kv_out(groups, False)
    v_spec, v_shape = _kv_out(groups, v_t)
    return pl.pallas_call(
        functools.partial(_qkv_kernel, shared_kv=shared_kv, head_norm=head_norm, rope=rope,
                          v_t=v_t, q_scale=q_scale, split_x=x_ctx is not None),
        out_shape=(jax.ShapeDtypeStruct((N_TOK, N_HEADS * HEAD_DIM), BF16), kv_shape, v_shape),
        grid=(N_TOK // TM,),
        in_specs=x_specs + [
                  pl.BlockSpec((None, 1, 6 * D), lambda i: (_mod_row(i), 0, 0)),
                  pl.BlockSpec((1, D), const),
                  pl.BlockSpec((D, n_out), const),
                  pl.BlockSpec((TM, LANES), _rope_row),
                  pl.BlockSpec((TM, LANES), _rope_row),
                  pl.BlockSpec((1, LANES), const),
                  pl.BlockSpec((1, LANES), const),
                  pl.BlockSpec((LANES, LANES), const)],
        out_specs=(pl.BlockSpec((TM, N_HEADS * HEAD_DIM), row), kv_spec, v_spec),
        compiler_params=_params("arbitrary"),
        name="qkv",
    )(*x_args, mod, gain, w, cos, sin, qg, kg, bd)


def _mla_kernel(x_ref, mod_ref, g_ref, wd_ref, qan_ref, kvan_ref, wq_ref, wkv_ref,
                cos_ref, sin_ref, q_ref, k_ref, v_ref):
    h = _rms(x_ref[...], g_ref[...]) * (1.0 + mod_ref[:, SC_M]) + mod_ref[:, SH_M]
    lat = jnp.dot(h.astype(BF16), wd_ref[...], preferred_element_type=F32)
    c_q = _rms(lat[:, :MLA_Q_LORA], qan_ref[...])
    c_kv = _rms(lat[:, MLA_Q_LORA:MLA_Q_LORA + MLA_KV_LORA], kvan_ref[...])
    cos = cos_ref[...]
    sin = sin_ref[...]
    k_rope = _rope_chunk(lat[:, MLA_Q_LORA + MLA_KV_LORA:], cos, sin, MLA_ROPE // 2)
    q = jnp.dot(c_q.astype(BF16), wq_ref[...], preferred_element_type=F32)
    kv = jnp.dot(c_kv.astype(BF16), wkv_ref[...], preferred_element_type=F32)
    scale = (MLA_NOPE + MLA_ROPE) ** -0.5 * LOG2E
    lane = _lane((TM, LANES))
    for hd in range(N_HEADS):
        sl = slice(hd * LANES, (hd + 1) * LANES)
        q_ref[:, sl] = (_rope_chunk(q[:, sl], cos, sin, MLA_ROPE // 2) * scale).astype(BF16)
        k_ref[hd // 2, hd % 2] = (kv[:, sl] + k_rope).astype(BF16)
        v = kv[:, N_HEADS * LANES + hd * LANES:N_HEADS * LANES + (hd + 1) * LANES]
        _put_v(v_ref, hd // 2, hd % 2, _with_ones(v, lane, hd % 2), True)


def _mla_proj(x, mod, gain, wd, qan, kvan, wq, wkv, cos, sin):
    row = lambda i: (i, 0)
    const = lambda i: (0, 0)
    kv_spec, kv_shape = _kv_out(N_HEADS // 2, False)
    v_spec, v_shape = _kv_out(N_HEADS // 2, True)
    return pl.pallas_call(
        _mla_kernel,
        out_shape=(jax.ShapeDtypeStruct((N_TOK, N_HEADS * LANES), BF16), kv_shape, v_shape),
        grid=(N_TOK // TM,),
        in_specs=[pl.BlockSpec((TM, D), row),
                  pl.BlockSpec((None, 1, 6 * D), lambda i: (_mod_row(i), 0, 0)),
                  pl.BlockSpec((1, D), const),
                  pl.BlockSpec(wd.shape, const),
                  pl.BlockSpec((1, MLA_Q_LORA), const),
                  pl.BlockSpec((1, MLA_KV_LORA), const),
                  pl.BlockSpec(wq.shape, const),
                  pl.BlockSpec(wkv.shape, const),
                  pl.BlockSpec((TM, LANES), _rope_row),
                  pl.BlockSpec((TM, LANES), _rope_row)],
        out_specs=(pl.BlockSpec((TM, N_HEADS * LANES), row), kv_spec, v_spec),
        compiler_params=_params("arbitrary"),
        name="mla_proj",
    )(x, mod, gain, wd, qan, kvan, wq, wkv, cos, sin)


def _attn_kernel(*refs, stacks, tq, subs, gp, mode, with_lat):
    refs = list(refs)
    sink_ref = refs.pop(0) if mode == "window" else None
    q_ref, kc_ref, vc_ref = refs[:3]
    refs = refs[3:]
    if with_lat:
        kl_ref, vl_ref = refs[:2]
        refs = refs[2:]
    tab_ref = refs.pop(0) if mode == "neighbourhood" else None
    o_ref = refs[0]
    s_refs = refs[1:]
    g = pl.program_id(1)
    t = pl.program_id(2)
    qw = q_ref.shape[1] // gp
    ow = o_ref.shape[1] // gp
    m_rows = tq * len(stacks[0][0])

    chunks, masks = [], []
    for sub in range(subs):
        ch, mask = [("ctx", 0, CTX)], None
        if with_lat and mode == "window":
            span = tq + 2 * WINDOW
            q0 = (t * subs + sub) * tq
            start = pl.multiple_of(jnp.clip(q0 - WINDOW, 0, S - span), WINDOW)
            qi = q0 + lax.broadcasted_iota(jnp.int32, (m_rows, span), 0) % tq
            ki = start + lax.broadcasted_iota(jnp.int32, (m_rows, span), 1)
            mask = jnp.abs(qi - ki) <= WINDOW
            ch.append(("lat", start, span))
        elif with_lat:
            start = pl.multiple_of(_na_band_start(t * subs + sub) * GRID_W, GRID_W)
            ch.append(("lat", start, NA_BAND * GRID_W))
        chunks.append(ch)
        masks.append(mask)
    cols = [[sum(w for _, _, w in ch[:c]) for c in range(len(ch))] for ch in chunks]

    items = [(gi, sub, qcs, kv) for gi in range(gp) for sub in range(subs) for qcs, kv in stacks]
    qs, sinks = [], []
    for gi, sub, qcs, kv in items:
        qs.append(jnp.concatenate(
            [q_ref[sub * tq:(sub + 1) * tq, gi * qw + qc * LANES:gi * qw + (qc + 1) * LANES]
             for qc, _ in qcs], axis=0))
        sink_col = None
        if mode == "window":
            hd = [(g * gp + gi) * 2 * len(qcs) + 2 * qc + kv for qc, _ in qcs]
            row = lax.broadcasted_iota(jnp.int32, (m_rows, 1), 0)
            sink_col = jnp.where(row < tq, sink_ref[hd[0]], sink_ref[hd[-1]])
        sinks.append(sink_col)

    def score_stage(it):
        gi, sub, _, kv = items[it]
        s_ref = s_refs[it % len(s_refs)]
        m_fold, m_col = None, sinks[it]
        for (kind, st, width), col in zip(chunks[sub], cols[sub]):
            k = kc_ref[gi, kv] if kind == "ctx" else kl_ref[gi, kv, pl.ds(st, width), :]
            s = lax.dot_general(qs[it], k, (((1,), (1,)), ((), ())), preferred_element_type=F32)
            if kind == "lat" and tab_ref is not None:
                s = s + tab_ref[2 * gi + kv]
            if kind == "lat" and masks[sub] is not None:
                s = jnp.where(masks[sub], s, NEG)
            s_ref[0:m_rows, col:col + width] = s
            if width % LANES == 0:
                for j in range(width // LANES):
                    slab = s[:, j * LANES:(j + 1) * LANES]
                    m_fold = slab if m_fold is None else jnp.maximum(m_fold, slab)
            else:
                mc = jnp.max(s, axis=-1, keepdims=True)
                m_col = mc if m_col is None else jnp.maximum(m_col, mc)
        m = jnp.max(m_fold, axis=-1, keepdims=True)
        return m if m_col is None else jnp.maximum(m, m_col)

    def prob_stage(it, m):
        sub = items[it][1]
        s_ref = s_refs[it % len(s_refs)]
        return [jnp.exp(s_ref[0:m_rows, col:col + width] - m).astype(BF16)
                for (_, _, width), col in zip(chunks[sub], cols[sub])]

    def value_stage(it, ps, m):
        gi, sub, _, kv = items[it]
        acc = jnp.zeros((m_rows, LANES), F32)
        for (kind, st, width), p in zip(chunks[sub], ps):
            v = vc_ref[gi, kv] if kind == "ctx" else vl_ref[gi, kv, pl.ds(st, width), :]
            acc = acc + jnp.dot(p, v, preferred_element_type=F32)
        den = acc[:, LANES - 1:LANES] if kv == 0 else acc[:, 0:1]
        if sinks[it] is not None:
            den = den + jnp.exp(sinks[it] - m)
        lane = _lane(acc.shape)
        return jnp.where(lane < HALF if kv == 0 else lane >= HALF, acc / den, 0.0)

    n = len(items)
    ms, ps, outs = [None] * n, [None] * n, {}
    for stage in range(n + 2):
        if 0 <= stage - 1 < n:
            ps[stage - 1] = prob_stage(stage - 1, ms[stage - 1])
        if stage < n:
            ms[stage] = score_stage(stage)
        if 0 <= stage - 2 < n:
            gi, sub, qcs, _ = items[stage - 2]
            o = value_stage(stage - 2, ps[stage - 2], ms[stage - 2])
            for i, (_, oc) in enumerate(qcs):
                key = (sub, gi * (ow // LANES) + oc)
                part = o[i * tq:(i + 1) * tq]
                outs[key] = part if key not in outs else outs[key] + part
    for (sub, oc), o in outs.items():
        o_ref[sub * tq:(sub + 1) * tq, oc * LANES:(oc + 1) * LANES] = o.astype(o_ref.dtype)


def _attention(q, k4, v4, *, stacks, mode, tq, subs=1, gp=1, sink=None, table=None,
               ctx_queries=False):
    groups = k4.shape[0] // gp
    qw = q.shape[1] // groups
    ow = N_HEADS * HEAD_DIM // groups
    with_lat = not ctx_queries
    tqs = tq * subs
    nq = S // tqs if with_lat else CTX // tqs
    row0 = 0 if with_lat else N_LAT // tqs

    q_spec = pl.BlockSpec((tqs, qw), lambda b, g, t: (row0 + b * nq + t, g))
    ctx_spec = pl.BlockSpec((gp, 2, CTX, LANES), lambda b, g, t: (g, 0, N_LAT // CTX + b, 0))
    lat_spec = pl.BlockSpec((gp, 2, S, LANES), lambda b, g, t: (g, 0, b, 0))
    in_specs = [q_spec, ctx_spec, ctx_spec]
    args = [q, k4, v4]
    n_keys = CTX
    if with_lat:
        in_specs += [lat_spec, lat_spec]
        args += [k4, v4]
        n_keys += {"window": tq + 2 * WINDOW, "neighbourhood": NA_BAND * GRID_W}[mode]
    if mode == "window":
        in_specs = [pl.BlockSpec(memory_space=pltpu.SMEM)] + in_specs
        args = [sink] + args
    if mode == "neighbourhood":
        in_specs.append(pl.BlockSpec((None, 2 * gp, tq, NA_BAND * GRID_W),
                                     lambda b, g, t: (_na_variant(t), g, 0, 0)))
        args.append(table)
    m_rows = tq * max(len(qcs) for qcs, _ in stacks)
    return pl.pallas_call(
        functools.partial(_attn_kernel, stacks=stacks, tq=tq, subs=subs, gp=gp, mode=mode,
                          with_lat=with_lat),
        out_shape=jax.ShapeDtypeStruct((N_LAT if with_lat else N_CTX, N_HEADS * HEAD_DIM), BF16),
        grid=(B, groups, nq),
        in_specs=in_specs,
        out_specs=pl.BlockSpec((tqs, ow), lambda b, g, t: (b * nq + t, g)),
        scratch_shapes=[pltpu.VMEM((m_rows, n_keys), F32) for _ in range(2)],
        compiler_params=_params("arbitrary", "arbitrary", "arbitrary"),
        name="attn_" + mode + ("" if with_lat else "_ctx"),
    )(*args)


def _attn_t_kernel(*refs, stacks, tq, subs, with_lat):
    refs = list(refs)
    q_ref, kc_ref, vtc_ref = refs[:3]
    refs = refs[3:]
    if with_lat:
        kl_ref, vtl_ref = refs[:2]
        refs = refs[2:]
    o_ref = refs[0]
    s_refs = refs[1:]
    chunks = [("ctx", 0, CTX)]
    if with_lat:
        chunks += [("lat", c * TK, TK) for c in range(S // TK)]
    rows = [sum(w for _, _, w in chunks[:c]) for c in range(len(chunks))]
    items = [(sub, qcs, kv) for sub in range(subs) for qcs, kv in stacks]
    qs = [jnp.concatenate([q_ref[sub * tq:(sub + 1) * tq, qc * LANES:(qc + 1) * LANES]
                           for qc, _ in qcs], axis=0) for sub, qcs, _ in items]
    cols = qs[0].shape[0]

    def score_chunk(it, c, m8):
        kind, st, width = chunks[c]
        kv = items[it][2]
        k = kc_ref[kv] if kind == "ctx" else kl_ref[kv, st:st + width, :]
        s = lax.dot_general(k, qs[it], (((1,), (1,)), ((), ())), preferred_element_type=F32)
        s_refs[it % len(s_refs)][rows[c]:rows[c] + width, 0:cols] = s
        c8 = jnp.max(s.reshape(width // 8, 8, cols), axis=0)
        return c8 if m8 is None else jnp.maximum(m8, c8)

    def prob_chunk(it, c, m):
        width = chunks[c][2]
        s = s_refs[it % len(s_refs)][rows[c]:rows[c] + width, 0:cols]
        return jnp.exp2(s - m).astype(BF16)

    def value_chunk(it, c, p, acc):
        kind, st, width = chunks[c]
        kv = items[it][2]
        vt = vtc_ref[kv] if kind == "ctx" else vtl_ref[kv, :, st:st + width]
        return acc + jnp.dot(vt, p, preferred_element_type=F32)

    n = len(items)
    ms, ps, accs = [None] * n, [[] for _ in range(n)], []
    for stage in range(n + 2):
        m8 = None
        acc = jnp.zeros((LANES, cols), F32)
        for c in range(len(chunks)):
            if 0 <= stage - 1 < n:
                ps[stage - 1].append(prob_chunk(stage - 1, c, ms[stage - 1]))
            if stage < n:
                m8 = score_chunk(stage, c, m8)
            if 0 <= stage - 2 < n:
                acc = value_chunk(stage - 2, c, ps[stage - 2][c], acc)
        if stage < n:
            ms[stage] = jnp.max(m8, axis=0, keepdims=True)
        if 0 <= stage - 2 < n:
            accs.append(acc)

    outs = {}
    for (sub, qcs, kv), acc in zip(items, accs):
        den = acc[LANES - 1:LANES, :] if kv == 0 else acc[0:1, :]
        o = (acc / den).T
        lane = _lane(o.shape)
        o = jnp.where(lane < HALF if kv == 0 else lane >= HALF, o, 0.0)
        for i, (_, oc) in enumerate(qcs):
            part = o[i * tq:(i + 1) * tq]
            outs[sub, oc] = part if (sub, oc) not in outs else outs[sub, oc] + part
    for (sub, oc), o in outs.items():
        o_ref[sub * tq:(sub + 1) * tq, oc * LANES:(oc + 1) * LANES] = o.astype(o_ref.dtype)


def _attention_t(q, k4, vt4, *, stacks, tq, subs=1, ctx_queries=False):
    groups = k4.shape[0]
    qw = q.shape[1] // groups
    ow = N_HEADS * HEAD_DIM // groups
    with_lat = not ctx_queries
    tqs = tq * subs
    nq = S // tqs if with_lat else CTX // tqs
    row0 = 0 if with_lat else N_LAT // tqs
    in_specs = [pl.BlockSpec((tqs, qw), lambda b, g, t: (row0 + b * nq + t, g)),
                pl.BlockSpec((None, 2, CTX, LANES), lambda b, g, t: (g, 0, N_LAT // CTX + b, 0)),
                pl.BlockSpec((None, 2, LANES, CTX), lambda b, g, t: (g, 0, 0, N_LAT // CTX + b))]
    args = [q, k4, vt4]
    n_keys = CTX
    if with_lat:
        in_specs += [pl.BlockSpec((None, 2, S, LANES), lambda b, g, t: (g, 0, b, 0)),
                     pl.BlockSpec((None, 2, LANES, S), lambda b, g, t: (g, 0, 0, b))]
        args += [k4, vt4]
        n_keys += S
    m_rows = tq * max(len(qcs) for qcs, _ in stacks)
    return pl.pallas_call(
        functools.partial(_attn_t_kernel, stacks=stacks, tq=tq, subs=subs, with_lat=with_lat),
        out_shape=jax.ShapeDtypeStruct((N_LAT if with_lat else N_CTX, N_HEADS * HEAD_DIM), BF16),
        grid=(B, groups, nq),
        in_specs=in_specs,
        out_specs=pl.BlockSpec((tqs, ow), lambda b, g, t: (b * nq + t, g)),
        scratch_shapes=[pltpu.VMEM((n_keys, m_rows), F32) for _ in range(2)],
        compiler_params=_params("arbitrary", "arbitrary", "arbitrary"),
        name="attn_global" + ("" if with_lat else "_ctx"),
    )(*args)


def _na_band_start(t):
    return jnp.clip(t * NA_QROWS - NA_ROWS // 2, 0, S // GRID_W - NA_BAND)


def _na_tiles():
    rows = S // GRID_W
    tiles = [(t * NA_QROWS, min(max(t * NA_QROWS - NA_ROWS // 2, 0), rows - NA_BAND))
             for t in range(rows // NA_QROWS)]
    lo = [rb for rb in tiles if rb[0] - NA_ROWS // 2 < 0]
    hi = [rb for rb in tiles if rb[0] - NA_ROWS // 2 > rows - NA_BAND]
    return lo + [tiles[len(lo)]] + hi, len(lo), len(hi)


def _na_variant(t):
    _, n_lo, n_hi = _na_tiles()
    first_hi = S // GRID_W // NA_QROWS - n_hi
    return jnp.where(t < n_lo, t, jnp.where(t >= first_hi, t - first_hi + n_lo + 1, n_lo))


def _bias_cols_kernel(rpb_ref, sel_ref, o_ref):
    o_ref[...] = jnp.dot(rpb_ref[...], sel_ref[...], preferred_element_type=F32,
                         precision=lax.Precision.HIGHEST)


def _na_table(rpb):
    rows = S // GRID_W
    n_dr, n_dc = 2 * NA_ROWS - 1, 2 * NA_COLS - 1
    c = np.arange(GRID_W)[:, None]
    kc = np.arange(GRID_W)[None, :]
    cs = np.clip(c - NA_COLS // 2, 0, GRID_W - NA_COLS)
    col_ok = (kc >= cs) & (kc < cs + NA_COLS)
    sel = (np.arange(LANES)[:, None, None] == (kc - c + NA_COLS - 1)[None]) & col_ok[None]
    sel = jnp.asarray(sel.reshape(LANES, GRID_W * GRID_W), F32)
    rpb2 = jnp.pad(rpb.reshape(N_HEADS * n_dr, n_dc), ((0, 256 - N_HEADS * n_dr), (0, LANES - n_dc)))
    cols = pl.pallas_call(
        _bias_cols_kernel,
        out_shape=jax.ShapeDtypeStruct((256, GRID_W * GRID_W), F32),
        name="na_bias_cols",
    )(rpb2, sel)
    cols = cols[:N_HEADS * n_dr].reshape(N_HEADS, n_dr, GRID_W, GRID_W)
    cols = jnp.where(col_ok[None, None], cols, NEG)
    masked = jnp.full((N_HEADS, GRID_W, GRID_W), NEG, F32)
    variants = []
    for r0, bs in _na_tiles()[0]:
        q_rows = []
        for r in range(r0, r0 + NA_QROWS):
            rs = min(max(r - NA_ROWS // 2, 0), rows - NA_ROWS)
            q_rows.append(jnp.concatenate(
                [cols[:, kr - r + NA_ROWS - 1] if rs <= kr < rs + NA_ROWS else masked
                 for kr in range(bs, bs + NA_BAND)], axis=2))
        variants.append(jnp.concatenate(q_rows, axis=1))
    return jnp.stack(variants)


def _router_route(h, wr_ref):
    w = wr_ref[...]
    h_hi, w_hi = h.astype(BF16), w.astype(BF16)
    h_lo = (h - h_hi.astype(F32)).astype(BF16)
    w_lo = (w - w_hi.astype(F32)).astype(BF16)
    logits = (jnp.dot(h_hi, w_hi, preferred_element_type=F32)
              + (jnp.dot(h_hi, w_lo, preferred_element_type=F32)
                 + jnp.dot(h_lo, w_hi, preferred_element_type=F32)))
    lane = _lane(logits.shape).astype(F32)
    lg = jnp.where(lane < N_EXPERTS, logits, -jnp.inf)
    m1 = jnp.max(lg, axis=-1, keepdims=True)
    i1 = jnp.min(jnp.where(lg == m1, lane, float(LANES)), axis=-1, keepdims=True)
    lg2 = jnp.where(lane == i1, -jnp.inf, lg)
    m2 = jnp.max(lg2, axis=-1, keepdims=True)
    i2 = jnp.min(jnp.where(lg2 == m2, lane, float(LANES)), axis=-1, keepdims=True)
    e = jnp.exp(m2 - m1)
    return jnp.where(lane == 0, i1, jnp.where(lane == 1, i2, jnp.where(
        lane == 2, 1.0 / (1.0 + e), jnp.where(lane == 3, e / (1.0 + e), 0.0))))


def _oproj_kernel(*refs, moe, with_ctx, split_x):
    refs = list(refs)
    o = _pick_rows(refs.pop(0), refs.pop(0) if with_ctx else None)
    x = _pick_rows(refs.pop(0), refs.pop(0) if split_x else None)
    if moe:
        wo_ref, mod_ref, post_ref, pre_ref, wr_ref, xo_ref, h_ref, route_ref = refs
    else:
        wo_ref, mod_ref, post_ref, pre_ref, xo_ref, h_ref = refs
    y = jnp.dot(o, wo_ref[...], preferred_element_type=F32)
    x1 = x + mod_ref[:, G_M] * _rms(y, post_ref[...])
    xo_ref[...] = x1
    h = _rms(x1, pre_ref[...]) * (1.0 + mod_ref[:, SC_F]) + mod_ref[:, SH_F]
    h_ref[...] = h.astype(h_ref.dtype)
    if moe:
        route_ref[...] = _router_route(h, wr_ref)


def _oproj(o, o_ctx, wo, x, mod, post, pre, w_router, x_ctx=None):
    moe = w_router is not None
    with_ctx = o_ctx is not None
    n_rows = N_TOK if with_ctx else N_LAT
    row = lambda i: (i, 0)
    const = lambda i: (0, 0)
    o_specs, o_args = _rows_of(o, o_ctx)
    x_specs, x_args = _rows_of(x, x_ctx)
    in_specs = o_specs + x_specs + [
        pl.BlockSpec((D, D), const),
        pl.BlockSpec((None, 1, 6 * D), lambda i: (_mod_row(i), 0, 0)),
        pl.BlockSpec((1, D), const), pl.BlockSpec((1, D), const)]
    args = o_args + x_args + [wo, mod, post, pre]
    out_shape = [jax.ShapeDtypeStruct((n_rows, D), F32),
                 jax.ShapeDtypeStruct((n_rows, D), F32 if moe else BF16)]
    out_specs = [pl.BlockSpec((TM, D), row), pl.BlockSpec((TM, D), row)]
    if moe:
        in_specs.append(pl.BlockSpec((D, LANES), const))
        out_shape.append(jax.ShapeDtypeStruct((n_rows, LANES), F32))
        out_specs.append(pl.BlockSpec((TM, LANES), row))
        args.append(w_router)
    return pl.pallas_call(
        functools.partial(_oproj_kernel, moe=moe, with_ctx=with_ctx, split_x=x_ctx is not None),
        out_shape=tuple(out_shape),
        grid=(n_rows // TM,),
        in_specs=in_specs,
        out_specs=tuple(out_specs),
        compiler_params=_params("arbitrary"),
        name="oproj",
    )(*args)


def _swiglu(h, gu_ref, dw_ref):
    gu = jnp.dot(h, gu_ref[...], preferred_element_type=F32)
    half = gu.shape[1] // 2
    act = (jax.nn.silu(gu[:, :half]) * gu[:, half:]).astype(BF16)
    return jnp.dot(act, dw_ref[...], preferred_element_type=F32)


def _ffn_kernel(h_ref, gu_ref, dw_ref, x_ref, mod_ref, post_ref, o_ref):
    y = _swiglu(h_ref[...], gu_ref, dw_ref)
    o_ref[...] = x_ref[...] + mod_ref[:, G_F] * _rms(y, post_ref[...])


def _ffn(h, w_gu, w_down, x, mod, post):
    row = lambda i: (i, 0)
    const = lambda i: (0, 0)
    resident = pl.Buffered(1)
    return pl.pallas_call(
        _ffn_kernel,
        out_shape=jax.ShapeDtypeStruct((N_TOK, D), F32),
        grid=(N_TOK // TM,),
        in_specs=[pl.BlockSpec((TM, D), row),
                  pl.BlockSpec((D, 2 * D_FF), const, pipeline_mode=resident),
                  pl.BlockSpec((D_FF, D), const, pipeline_mode=resident),
                  pl.BlockSpec((TM, D), row),
                  pl.BlockSpec((None, 1, 6 * D), lambda i: (_mod_row(i), 0, 0)),
                  pl.BlockSpec((1, D), const)],
        out_specs=pl.BlockSpec((TM, D), row),
        compiler_params=_params("arbitrary"),
        name="ffn_dense",
    )(h, w_gu, w_down, x, mod, post)


def _routing(route, n_rows):
    i1 = route[:, 0].astype(jnp.int32)
    i2 = route[:, 1].astype(jnp.int32)
    e = jnp.arange(N_EXPERTS, dtype=jnp.int32)
    hit1 = i1[:, None] == e
    hit2 = i2[:, None] == e
    sel = hit1.astype(jnp.int32) + hit2.astype(jnp.int32)
    cum = jnp.cumsum(sel, axis=0)
    tiles = (cum[-1] + TME - 1) // TME
    tile_end = jnp.cumsum(tiles)
    slot = ((tile_end - tiles) * TME)[None] + cum - sel
    slots = jnp.stack([jnp.sum(jnp.where(hit1, slot, 0), axis=1),
                       jnp.sum(jnp.where(hit2, slot, 0), axis=1)], axis=1).reshape(-1)
    j = jnp.arange(_n_slot_tiles(n_rows), dtype=jnp.int32)
    tile_expert = jnp.minimum(jnp.sum(j[:, None] >= tile_end[None], axis=1), N_EXPERTS - 1)
    return slots.astype(jnp.int32), tile_expert.astype(jnp.int32), tile_end.astype(jnp.int32)


def _n_slot_tiles(n_rows):
    return 2 * n_rows // TME + N_EXPERTS


def _row_copies(slots_ref, n, copy):
    def start(r, carry):
        for k in range(2):
            copy(r, k, slots_ref[2 * r + k]).start(priority=k)
        return carry

    def wait(r, carry):
        for k in range(2):
            copy(r, k, slots_ref[2 * r + k]).wait()
        return carry

    lax.fori_loop(0, n, start, 0, unroll=8)
    lax.fori_loop(0, n, wait, 0, unroll=8)


def _dispatch_kernel(tend_ref, slots_ref, h_ref, xs_ref, zero_ref, zsem, sem, *, n_tiles):
    @pl.when(pl.program_id(0) == 0)
    def _():
        zero_ref[...] = jnp.zeros_like(zero_ref)
        tiles = []
        for e in range(N_EXPERTS):
            first = tend_ref[e - 1] if e else 0
            tiles.append((tend_ref[e] > first, tend_ref[e] - 1))
        for u in range(N_EXPERTS):
            tile = tend_ref[N_EXPERTS - 1] + u
            tiles.append((tile < n_tiles, tile))
        copies = [(ok, pltpu.make_async_copy(
            zero_ref, xs_ref.at[pl.ds(pl.multiple_of(jnp.maximum(tile, 0) * TME, TME), TME)], zsem))
            for ok, tile in tiles]
        for ok, cp in copies:
            pl.when(ok)(cp.start)
        for ok, cp in copies:
            pl.when(ok)(cp.wait)

    _row_copies(slots_ref, TMR, lambda r, k, s: pltpu.make_async_copy(
        h_ref.at[pl.ds(r, 1)], xs_ref.at[pl.ds(s, 1)], sem))


def _dispatch(tile_end, slots, h, n_rows):
    n_tiles = _n_slot_tiles(n_rows)
    return pl.pallas_call(
        functools.partial(_dispatch_kernel, n_tiles=n_tiles),
        out_shape=jax.ShapeDtypeStruct((n_tiles * TME, D), F32),
        grid_spec=pltpu.PrefetchScalarGridSpec(
            num_scalar_prefetch=1,
            grid=(n_rows // TMR,),
            in_specs=[pl.BlockSpec((2 * TMR,), lambda i, te: (i,), memory_space=pltpu.SMEM),
                      pl.BlockSpec((TMR, D), lambda i, te: (i, 0))],
            out_specs=pl.BlockSpec(memory_space=pl.ANY),
            scratch_shapes=[pltpu.VMEM((TME, D), F32), pltpu.SemaphoreType.DMA(()),
                            pltpu.SemaphoreType.DMA(())]),
        compiler_params=_params("arbitrary"),
        name="moe_dispatch",
    )(tile_end, slots, h)


def _expert_kernel(te_ref, tend_ref, x_ref, gu_ref, dw_ref, o_ref, gub_ref, db_ref):
    j = pl.program_id(0)
    valid = j < tend_ref[N_EXPERTS - 1]

    @pl.when(jnp.logical_or(j == 0, te_ref[j] != te_ref[jnp.maximum(j - 1, 0)]))
    def _():
        gub_ref[...] = gu_ref[...].astype(BF16)
        db_ref[...] = dw_ref[...].astype(BF16)

    @pl.when(valid)
    def _():
        o_ref[...] = _swiglu(x_ref[...].astype(BF16), gub_ref, db_ref)

    @pl.when(jnp.logical_not(valid))
    def _():
        o_ref[...] = jnp.zeros_like(o_ref)


def _experts(tile_expert, tile_end, xs, w_gu, w_down):
    n_tiles = xs.shape[0] // TME
    return pl.pallas_call(
        _expert_kernel,
        out_shape=jax.ShapeDtypeStruct(xs.shape, F32),
        grid_spec=pltpu.PrefetchScalarGridSpec(
            num_scalar_prefetch=2,
            grid=(n_tiles,),
            in_specs=[pl.BlockSpec((TME, D), lambda j, te, tend: (j, 0)),
                      pl.BlockSpec((None, D, 2 * D_FF_EXPERT), lambda j, te, tend: (te[j], 0, 0)),
                      pl.BlockSpec((None, D_FF_EXPERT, D), lambda j, te, tend: (te[j], 0, 0))],
            out_specs=pl.BlockSpec((TME, D), lambda j, te, tend: (j, 0)),
            scratch_shapes=[pltpu.VMEM((D, 2 * D_FF_EXPERT), BF16),
                            pltpu.VMEM((D_FF_EXPERT, D), BF16)]),
        compiler_params=_params("arbitrary"),
        name="moe_experts",
    )(tile_expert, tile_end, xs, w_gu, w_down)


def _combine_kernel(slots_ref, route_ref, x_ref, mod_ref, post_ref, ys_ref, o_ref, buf_ref, sem):
    _row_copies(slots_ref, TMR, lambda r, k, s: pltpu.make_async_copy(
        ys_ref.at[pl.ds(s, 1)], buf_ref.at[k, pl.ds(r, 1)], sem))
    y = route_ref[:, 2:3] * buf_ref[0] + route_ref[:, 3:4] * buf_ref[1]
    o_ref[...] = x_ref[...] + mod_ref[:, G_F] * _rms(y, post_ref[...])


def _combine(slots, route, x, mod, post, ys, n_rows):
    return pl.pallas_call(
        _combine_kernel,
        out_shape=jax.ShapeDtypeStruct((n_rows, D), F32),
        grid=(n_rows // TMR,),
        in_specs=[pl.BlockSpec((2 * TMR,), lambda i: (i,), memory_space=pltpu.SMEM),
                  pl.BlockSpec((TMR, LANES), lambda i: (i, 0)),
                  pl.BlockSpec((TMR, D), lambda i: (i, 0)),
                  pl.BlockSpec((None, 1, 6 * D), lambda i: (_mod_row(i * TMR // TM), 0, 0)),
                  pl.BlockSpec((1, D), lambda i: (0, 0)),
                  pl.BlockSpec(memory_space=pl.ANY)],
        out_specs=pl.BlockSpec((TMR, D), lambda i: (i, 0)),
        scratch_shapes=[pltpu.VMEM((2, TMR, D), F32), pltpu.SemaphoreType.DMA(())],
        compiler_params=_params("arbitrary"),
        name="moe_combine",
    )(slots, route, x, mod, post, ys)


def _slot_maps(slots, n_rows):
    n_slots = _n_slot_tiles(n_rows) * TME
    tok = jnp.repeat(jnp.arange(n_rows, dtype=jnp.int32), 2)
    choice = jnp.tile(jnp.arange(2, dtype=jnp.int32), n_rows)
    src = jnp.zeros((n_slots,), jnp.int32).at[slots].set(tok, unique_indices=True)
    dst = jnp.full((n_slots,), -1, jnp.int32).at[slots].set(choice * n_rows + tok,
                                                            unique_indices=True)
    pad = dst < 0
    dst = jnp.where(pad, 2 * n_rows - 1 + jnp.cumsum(pad.astype(jnp.int32)), dst)
    return src, dst


def _fused_expert_kernel(te_ref, src0_ref, srcn_ref, dstp_ref, dstc_ref, h_ref, gu_ref, dw_ref,
                         y_ref, xbuf, ybuf, gub_ref, db_ref, gsem, ssem):
    j = pl.program_id(0)
    last = pl.num_programs(0) - 1
    cur = j % 2
    nxt = 1 - cur

    def gather(idx_ref, buf, r):
        return pltpu.make_async_copy(h_ref.at[pl.ds(idx_ref[r], 1)], xbuf.at[buf, pl.ds(r, 1)],
                                     gsem.at[buf])

    def scatter(idx_ref, buf, r):
        return pltpu.make_async_copy(ybuf.at[buf, pl.ds(r, 1)], y_ref.at[pl.ds(idx_ref[r], 1)],
                                     ssem.at[buf])

    def wait_all(copy):
        lax.fori_loop(0, TME, lambda r, c: (copy(r).wait(), c)[1], 0, unroll=8)

    @pl.when(jnp.logical_or(j == 0, te_ref[j] != te_ref[jnp.maximum(j - 1, 0)]))
    def _():
        gub_ref[...] = gu_ref[...].astype(BF16)
        db_ref[...] = dw_ref[...].astype(BF16)

    @pl.when(j == 0)
    def _():
        for r in range(TME):
            gather(src0_ref, 0, r).start()

    wait_all(lambda r: gather(srcn_ref, cur, r))

    @pl.when(j >= 2)
    def _():
        wait_all(lambda r: scatter(dstp_ref, cur, r))

    def step(with_scatter):
        for r in range(TME):
            gather(srcn_ref, nxt, r).start()
            if with_scatter:
                scatter(dstp_ref, nxt, r).start()
        ybuf[cur] = _swiglu(xbuf[cur].astype(BF16), gub_ref, db_ref)

    pl.when(j == 0)(lambda: step(False))
    pl.when(j > 0)(lambda: step(True))

    @pl.when(j == last)
    def _():
        for r in range(TME):
            scatter(dstc_ref, cur, r).start()
        wait_all(lambda r: gather(srcn_ref, nxt, r))
        wait_all(lambda r: scatter(dstp_ref, nxt, r))
        wait_all(lambda r: scatter(dstc_ref, cur, r))


def _fused_experts(tile_expert, src, dst, h, w_gu, w_down, n_rows):
    n_tiles = _n_slot_tiles(n_rows)
    smem = lambda f: pl.BlockSpec((TME,), f, memory_space=pltpu.SMEM)
    return pl.pallas_call(
        _fused_expert_kernel,
        out_shape=jax.ShapeDtypeStruct((n_tiles * TME, D), F32),
        grid_spec=pltpu.PrefetchScalarGridSpec(
            num_scalar_prefetch=1,
            grid=(n_tiles,),
            in_specs=[smem(lambda j, te: (0,)),
                      smem(lambda j, te: (jnp.minimum(j + 1, n_tiles - 1),)),
                      smem(lambda j, te: (jnp.maximum(j - 1, 0),)),
                      smem(lambda j, te: (j,)),
                      pl.BlockSpec(memory_space=pl.ANY),
                      pl.BlockSpec((None, D, 2 * D_FF_EXPERT), lambda j, te: (te[j], 0, 0)),
                      pl.BlockSpec((None, D_FF_EXPERT, D), lambda j, te: (te[j], 0, 0))],
            out_specs=pl.BlockSpec(memory_space=pl.ANY),
            scratch_shapes=[pltpu.VMEM((2, TME, D), F32), pltpu.VMEM((2, TME, D), F32),
                            pltpu.VMEM((D, 2 * D_FF_EXPERT), BF16),
                            pltpu.VMEM((D_FF_EXPERT, D), BF16),
                            pltpu.SemaphoreType.DMA((2,)), pltpu.SemaphoreType.DMA((2,))]),
        compiler_params=_params("arbitrary"),
        name="moe_experts",
    )(tile_expert, src, src, dst, dst, h, w_gu, w_down)


def _mix_kernel(route_ref, y1_ref, y2_ref, x_ref, mod_ref, post_ref, o_ref):
    y = route_ref[:, 2:3] * y1_ref[...] + route_ref[:, 3:4] * y2_ref[...]
    o_ref[...] = x_ref[...] + mod_ref[:, G_F] * _rms(y, post_ref[...])


def _mix(route, ys, x, mod, post, n_rows):
    row = lambda i: (i, 0)
    return pl.pallas_call(
        _mix_kernel,
        out_shape=jax.ShapeDtypeStruct((n_rows, D), F32),
        grid=(n_rows // TM,),
        in_specs=[pl.BlockSpec((TM, LANES), row),
                  pl.BlockSpec((TM, D), row),
                  pl.BlockSpec((TM, D), lambda i: (n_rows // TM + i, 0)),
                  pl.BlockSpec((TM, D), row),
                  pl.BlockSpec((None, 1, 6 * D), lambda i: (_mod_row(i), 0, 0)),
                  pl.BlockSpec((1, D), lambda i: (0, 0))],
        out_specs=pl.BlockSpec((TM, D), row),
        compiler_params=_params("arbitrary"),
        name="moe_mix",
    )(route, ys, ys, x, mod, post)


def _moe(h, route, w_gu, w_down, x, mod, post, n_rows):
    slots, tile_expert, _ = _routing(route, n_rows)
    src, dst = _slot_maps(slots, n_rows)
    ys = _fused_experts(tile_expert, src, dst, h, w_gu, w_down, n_rows)
    return _mix(route, ys, x, mod, post, n_rows)


def _rope_tables(rot_dim, lead, tail):
    t = jnp.arange(S, dtype=jnp.int32)
    row = (t // GRID_W).astype(F32)
    col = (t % GRID_W).astype(F32)
    quarter = rot_dim // 4
    inv = ROPE_THETA ** (-jnp.arange(quarter, dtype=F32) / quarter)
    ang = jnp.concatenate([row[:, None] * inv, col[:, None] * inv], axis=-1)
    cos, sin = jnp.cos(ang), jnp.sin(ang)
    reps = (LANES - lead - tail) // rot_dim
    cos_l = jnp.concatenate([jnp.ones((S, lead), F32)] + [cos, cos] * reps + [jnp.ones((S, tail), F32)], axis=1)
    sin_l = jnp.concatenate([jnp.zeros((S, lead), F32)] + [-sin, sin] * reps + [jnp.zeros((S, tail), F32)], axis=1)
    cos_all = jnp.concatenate([cos_l, jnp.ones((TM, LANES), F32)], axis=0)
    sin_all = jnp.concatenate([sin_l, jnp.zeros((TM, LANES), F32)], axis=0)
    return cos_all, sin_all


def _rope_row(i):
    return (jnp.where(i < N_LAT // TM, i % (S // TM), S // TM), 0)


def _mla_weights(w_dkv, w_q_b, w_kv_b):
    lat = MLA_Q_LORA + MLA_KV_LORA
    pad = LANES - MLA_NOPE - MLA_ROPE
    wd = jnp.concatenate([w_dkv[:, :lat], jnp.zeros((D, MLA_NOPE), F32), w_dkv[:, lat:],
                          jnp.zeros((D, pad), F32)], axis=1)
    wq = jnp.pad(w_q_b.reshape(MLA_Q_LORA, N_HEADS, MLA_NOPE + MLA_ROPE), ((0, 0), (0, 0), (0, pad)))
    kvr = w_kv_b.reshape(MLA_KV_LORA, N_HEADS, 2 * HALF)
    k_part, v_part = kvr[..., :HALF], kvr[..., HALF:]
    zero = jnp.zeros_like(v_part)
    wk = jnp.concatenate([k_part, zero], axis=-1)
    even = (jnp.arange(N_HEADS) % 2 == 0)[None, :, None]
    wv = jnp.where(even, jnp.concatenate([v_part, zero], -1), jnp.concatenate([zero, v_part], -1))
    wkv = jnp.concatenate([wk.reshape(MLA_KV_LORA, -1), wv.reshape(MLA_KV_LORA, -1)], axis=1)
    return wd.astype(BF16), wq.reshape(MLA_Q_LORA, -1).astype(BF16), wkv.astype(BF16)


GQA_STACKS = ((((0, 0), (1, 1)), 0), (((0, 0), (1, 1)), 1))
MLA_STACKS = ((((0, 0),), 0), (((1, 0),), 1))
NA_STACKS = ((((0, 0),), 0), (((0, 0),), 1))


def kernel(x, c, ctx, c_ctx,
           l0_ada_w, l0_ada_b, l0_mix_pre, l0_mix_post, l0_ffn_pre, l0_ffn_post,
           l0_w_qkv, l0_q_norm, l0_k_norm, l0_w_o, l0_ffn_w_gu, l0_ffn_w_down,
           l1_ada_w, l1_ada_b, l1_mix_pre, l1_mix_post, l1_ffn_pre, l1_ffn_post,
           l1_w_qkv, l1_sink, l1_w_o, l1_router, l1_moe_w_gu, l1_moe_w_down,
           l2_ada_w, l2_ada_b, l2_mix_pre, l2_mix_post, l2_ffn_pre, l2_ffn_post,
           l2_w_dkv, l2_q_a_norm, l2_w_q_b, l2_kv_a_norm, l2_w_kv_b, l2_w_o, l2_ffn_w_gu, l2_ffn_w_down,
           l3_ada_w, l3_ada_b, l3_mix_pre, l3_mix_post, l3_ffn_pre, l3_ffn_post,
           l3_w_qkv, l3_rpb, l3_w_o, l3_router, l3_moe_w_gu, l3_moe_w_down):
    vec = lambda a: a.reshape(1, -1)
    tile2 = lambda a: jnp.tile(a, 2).reshape(1, LANES)
    bf = lambda a: a.astype(BF16)
    pad_router = lambda w: jnp.pad(w, ((0, 0), (0, LANES - N_EXPERTS)))

    x_lat, x_ctx = x.reshape(N_LAT, D), ctx.reshape(N_CTX, D)
    c_all = jnp.concatenate([c, c_ctx[None], jnp.zeros((8 - B - 1, D), F32)], axis=0)
    mods = [_ada(c_all, w, b).reshape(8, 1, 6 * D)
            for w, b in ((l0_ada_w, l0_ada_b), (l1_ada_w, l1_ada_b), (l2_ada_w, l2_ada_b), (l3_ada_w, l3_ada_b))]
    cos64, sin64 = _rope_tables(HEAD_DIM, 0, 0)
    cos32, sin32 = _rope_tables(MLA_ROPE, MLA_NOPE, LANES - MLA_NOPE - MLA_ROPE)
    ones = jnp.ones((1, LANES), F32)
    idx = np.arange(LANES)
    bd = jnp.asarray(idx[:, None] // HALF == idx[None, :] // HALF, BF16)

    q, k4, v4 = _qkv(x_lat, mods[0], vec(l0_mix_pre), bf(l0_w_qkv), cos64, sin64,
                     tile2(l0_q_norm), tile2(l0_k_norm), bd, shared_kv=True, head_norm=True, rope=True,
                     v_t=True, q_scale=HEAD_DIM ** -0.5 * LOG2E, x_ctx=x_ctx)
    o = _attention_t(q, k4, v4, stacks=GQA_STACKS, tq=TQ, subs=4)
    oc = _attention_t(q, k4, v4, stacks=GQA_STACKS, tq=CTX, ctx_queries=True)
    xs, h = _oproj(o, oc, bf(l0_w_o), x_lat, mods[0], vec(l0_mix_post), vec(l0_ffn_pre), None,
                   x_ctx=x_ctx)
    xs = _ffn(h, bf(l0_ffn_w_gu), bf(l0_ffn_w_down), xs, mods[0], vec(l0_ffn_post))

    q, k4, v4 = _qkv(xs, mods[1], vec(l1_mix_pre), bf(l1_w_qkv), cos64, sin64,
                     ones, ones, bd, shared_kv=True, head_norm=False, rope=True,
                     v_t=False, q_scale=HEAD_DIM ** -0.5)
    o = _attention(q, k4, v4, stacks=GQA_STACKS, mode="window", tq=TQ, subs=4, sink=l1_sink)
    oc = _attention(q, k4, v4, stacks=GQA_STACKS, mode="window", tq=CTX, sink=l1_sink, ctx_queries=True)
    xs, h, route = _oproj(o, oc, bf(l1_w_o), xs, mods[1], vec(l1_mix_post), vec(l1_ffn_pre),
                          pad_router(l1_router))
    xs = _moe(h, route, l1_moe_w_gu, l1_moe_w_down, xs, mods[1], vec(l1_ffn_post), N_TOK)

    wd, wq, wkv = _mla_weights(l2_w_dkv, l2_w_q_b, l2_w_kv_b)
    q, k4, v4 = _mla_proj(xs, mods[2], vec(l2_mix_pre), wd, vec(l2_q_a_norm), vec(l2_kv_a_norm),
                          wq, wkv, cos32, sin32)
    o = _attention_t(q, k4, v4, stacks=MLA_STACKS, tq=2 * TQ, subs=4)
    oc = _attention_t(q, k4, v4, stacks=MLA_STACKS, tq=CTX, ctx_queries=True)
    xs, h = _oproj(o, oc, bf(l2_w_o), xs, mods[2], vec(l2_mix_post), vec(l2_ffn_pre), None)
    xs = _ffn(h, bf(l2_ffn_w_gu), bf(l2_ffn_w_down), xs, mods[2], vec(l2_ffn_post))

    q, k4, v4 = _qkv(xs, mods[3], vec(l3_mix_pre), bf(l3_w_qkv), cos64, sin64,
                     ones, ones, bd, shared_kv=False, head_norm=False, rope=False,
                     v_t=False, q_scale=HEAD_DIM ** -0.5)
    o = _attention(q, k4, v4, stacks=NA_STACKS, mode="neighbourhood", tq=NA_QROWS * GRID_W, gp=4,
                   table=_na_table(l3_rpb))
    xl, h, route = _oproj(o, None, bf(l3_w_o), xs, mods[3], vec(l3_mix_post), vec(l3_ffn_pre),
                          pad_router(l3_router))
    xl = _moe(h, route, l3_moe_w_gu, l3_moe_w_down, xl, mods[3], vec(l3_ffn_post), N_LAT)
    return xl.reshape(B, S, D)
```

```python
import functools

import numpy as np
import jax
import jax.numpy as jnp
from jax import lax
from jax.experimental import pallas as pl
from jax.experimental.pallas import tpu as pltpu

D = 1024
B = 4
S = 4096
CTX = 256
GRID_W = 64
N_LAT = B * S
N_CTX = B * CTX
N_TOK = N_LAT + N_CTX
HEAD_DIM = 64
N_HEADS = 16
N_KV_HEADS = 4
WINDOW = 128
MLA_Q_LORA = 384
MLA_KV_LORA = 256
MLA_NOPE = 64
MLA_ROPE = 32
NA_ROWS = 8
NA_COLS = 16
D_FF = 2816
N_EXPERTS = 8
D_FF_EXPERT = 1408
ROPE_THETA = 10000.0
EPS = 1e-6
NEG = -1e30
LOG2E = 1.4426950408889634

LANES = 128
HALF = LANES // 2
TM = 512
TQ = 256
TK = 512
TME = 256
TMR = 1024
NA_QROWS = 4
NA_BAND = NA_QROWS + NA_ROWS - 1
VMEM_LIMIT = 56 * 1024 * 1024

F32 = jnp.float32
BF16 = jnp.bfloat16

SH_M, SC_M, G_M, SH_F, SC_F, G_F = (slice(i * D, (i + 1) * D) for i in range(6))


def _params(*sem, flags=None):
    return pltpu.CompilerParams(dimension_semantics=sem, vmem_limit_bytes=VMEM_LIMIT, flags=flags)


def _mod_row(i):
    r0 = i * TM
    return jnp.where(r0 >= N_LAT, B, r0 // S)


def _rms(x, gain):
    return x * lax.rsqrt(jnp.mean(x * x, axis=-1, keepdims=True) + EPS) * gain


def _lane(shape):
    return lax.broadcasted_iota(jnp.int32, shape, len(shape) - 1)


def _ada_kernel(c_ref, w_ref, b_ref, o_ref):
    a = jax.nn.silu(c_ref[...])
    o_ref[...] = jnp.dot(a, w_ref[...], preferred_element_type=F32,
                         precision=lax.Precision.HIGHEST) + b_ref[...]


def _ada(c_all, w, b):
    tn = 1536
    return pl.pallas_call(
        _ada_kernel,
        out_shape=jax.ShapeDtypeStruct((8, 6 * D), F32),
        grid=(6 * D // tn,),
        in_specs=[pl.BlockSpec((8, D), lambda j: (0, 0)),
                  pl.BlockSpec((D, tn), lambda j: (0, j)),
                  pl.BlockSpec((1, tn), lambda j: (0, j))],
        out_specs=pl.BlockSpec((8, tn), lambda j: (0, j)),
        compiler_params=_params("arbitrary"),
        name="ada",
    )(c_all, w, b.reshape(1, 6 * D))


def _rope_chunk(t, cos, sin, half):
    lane = _lane(t.shape)
    rot = jnp.where((lane % (2 * half)) < half,
                    pltpu.roll(t, LANES - half, 1), pltpu.roll(t, half, 1))
    return t * cos + rot * sin


def _head_rms(t, gain, bd):
    t2 = t * t
    hi = t2.astype(BF16)
    lo = (t2 - hi.astype(F32)).astype(BF16)
    ss = (jnp.dot(hi, bd, preferred_element_type=F32)
          + jnp.dot(lo, bd, preferred_element_type=F32))
    return t * lax.rsqrt(ss * (1.0 / HEAD_DIM) + EPS) * gain


def _with_ones(v, lane, parity):
    if parity == 0:
        return jnp.where(lane < HALF, v, jnp.where(lane == LANES - 1, 1.0, 0.0))
    return jnp.where(lane >= HALF, v, jnp.where(lane == 0, 1.0, 0.0))


def _half_only(k, lane, parity):
    return jnp.where(lane < HALF, k, 0.0) if parity == 0 else jnp.where(lane >= HALF, k, 0.0)


def _put_v(v_ref, j, parity, v, v_t):
    v_ref[j, parity] = (v.T if v_t else v).astype(BF16)


def _rows_of(lat, ctx):
    n_lat = N_LAT // TM
    if ctx is None:
        return [pl.BlockSpec((TM, D), lambda i: (i, 0))], [lat]
    return ([pl.BlockSpec((TM, D), lambda i: (jnp.minimum(i, n_lat - 1), 0)),
             pl.BlockSpec((TM, D), lambda i: (jnp.maximum(i - n_lat, 0), 0))], [lat, ctx])


def _pick_rows(lat_ref, ctx_ref):
    if ctx_ref is None:
        return lat_ref[...]
    return jnp.where(pl.program_id(0) < N_LAT // TM, lat_ref[...], ctx_ref[...])


def _qkv_kernel(*refs, shared_kv, head_norm, rope, v_t, q_scale, split_x):
    refs = list(refs)
    x = _pick_rows(refs.pop(0), refs.pop(0) if split_x else None)
    (mod_ref, g_ref, w_ref, cos_ref, sin_ref, qg_ref, kg_ref, bd_ref, q_ref, k_ref, v_ref) = refs
    h = _rms(x, g_ref[...]) * (1.0 + mod_ref[:, SC_M]) + mod_ref[:, SH_M]
    acc = jnp.dot(h.astype(BF16), w_ref[...], preferred_element_type=F32)
    lane = _lane((TM, LANES))
    n_q = N_HEADS * HEAD_DIM // LANES
    n_kv = (N_KV_HEADS if shared_kv else N_HEADS) * HEAD_DIM // LANES

    def finish(t, gain_ref):
        if head_norm:
            t = _head_rms(t, gain_ref[...], bd_ref[...])
        if rope:
            t = _rope_chunk(t, cos_ref[...], sin_ref[...], HEAD_DIM // 2)
        return t

    for c in range(n_q):
        t = finish(acc[:, c * LANES:(c + 1) * LANES], qg_ref)
        q_ref[:, c * LANES:(c + 1) * LANES] = (t * q_scale).astype(BF16)
    for c in range(n_kv):
        k = finish(acc[:, (n_q + c) * LANES:(n_q + c + 1) * LANES], kg_ref)
        v = acc[:, (n_q + n_kv + c) * LANES:(n_q + n_kv + c + 1) * LANES]
        if shared_kv:
            k_sw = pltpu.roll(k, HALF, 1)
            v_sw = pltpu.roll(v, HALF, 1)
            for half in range(2):
                j = 2 * c + half
                for parity in range(2):
                    ksrc = k if parity == half else k_sw
                    vsrc = v if parity == half else v_sw
                    k_ref[j, parity] = _half_only(ksrc, lane, parity).astype(BF16)
                    _put_v(v_ref, j, parity, _with_ones(vsrc, lane, parity), v_t)
        else:
            for parity in range(2):
                k_ref[c, parity] = _half_only(k, lane, parity).astype(BF16)
                _put_v(v_ref, c, parity, _with_ones(v, lane, parity), v_t)


def _kv_out(groups, transposed):
    if transposed:
        return (pl.BlockSpec((groups, 2, LANES, TM), lambda i: (0, 0, 0, i)),
                jax.ShapeDtypeStruct((groups, 2, LANES, N_TOK), BF16))
    return (pl.BlockSpec((groups, 2, TM, LANES), lambda i: (0, 0, i, 0)),
            jax.ShapeDtypeStruct((groups, 2, N_TOK, LANES), BF16))


def _qkv(x, mod, gain, w, cos, sin, qg, kg, bd, *, shared_kv, head_norm, rope, v_t, q_scale,
         x_ctx=None):
    n_out = w.shape[1]
    x_specs, x_args = _rows_of(x, x_ctx)
    groups = N_KV_HEADS if shared_kv else N_HEADS // 2
    row = lambda i: (i, 0)
    const = lambda i: (0, 0)
    kv_spec, kv_shape = _kv_out(groups, False)
    v_spec, v_shape = _kv_out(groups, v_t)
    return pl.pallas_call(
        functools.partial(_qkv_kernel, shared_kv=shared_kv, head_norm=head_norm, rope=rope,
                          v_t=v_t, q_scale=q_scale, split_x=x_ctx is not None),
        out_shape=(jax.ShapeDtypeStruct((N_TOK, N_HEADS * HEAD_DIM), BF16), kv_shape, v_shape),
        grid=(N_TOK // TM,),
        in_specs=x_specs + [
                  pl.BlockSpec((None, 1, 6 * D), lambda i: (_mod_row(i), 0, 0)),
                  pl.BlockSpec((1, D), const),
                  pl.BlockSpec((D, n_out), const),
                  pl.BlockSpec((TM, LANES), _rope_row),
                  pl.BlockSpec((TM, LANES), _rope_row),
                  pl.BlockSpec((1, LANES), const),
                  pl.BlockSpec((1, LANES), const),
                  pl.BlockSpec((LANES, LANES), const)],
        out_specs=(pl.BlockSpec((TM, N_HEADS * HEAD_DIM), row), kv_spec, v_spec),
        compiler_params=_params("arbitrary"),
        name="qkv",
    )(*x_args, mod, gain, w, cos, sin, qg, kg, bd)


def _mla_kernel(x_ref, mod_ref, g_ref, wd_ref, qan_ref, kvan_ref, wq_ref, wkv_ref,
                cos_ref, sin_ref, q_ref, k_ref, v_ref):
    h = _rms(x_ref[...], g_ref[...]) * (1.0 + mod_ref[:, SC_M]) + mod_ref[:, SH_M]
    lat = jnp.dot(h.astype(BF16), wd_ref[...], preferred_element_type=F32)
    c_q = _rms(lat[:, :MLA_Q_LORA], qan_ref[...])
    c_kv = _rms(lat[:, MLA_Q_LORA:MLA_Q_LORA + MLA_KV_LORA], kvan_ref[...])
    cos = cos_ref[...]
    sin = sin_ref[...]
    k_rope = _rope_chunk(lat[:, MLA_Q_LORA + MLA_KV_LORA:], cos, sin, MLA_ROPE // 2)
    q = jnp.dot(c_q.astype(BF16), wq_ref[...], preferred_element_type=F32)
    kv = jnp.dot(c_kv.astype(BF16), wkv_ref[...], preferred_element_type=F32)
    scale = (MLA_NOPE + MLA_ROPE) ** -0.5 * LOG2E
    lane = _lane((TM, LANES))
    for hd in range(N_HEADS):
        sl = slice(hd * LANES, (hd + 1) * LANES)
        q_ref[:, sl] = (_rope_chunk(q[:, sl], cos, sin, MLA_ROPE // 2) * scale).astype(BF16)
        k_ref[hd // 2, hd % 2] = (kv[:, sl] + k_rope).astype(BF16)
        v = kv[:, N_HEADS * LANES + hd * LANES:N_HEADS * LANES + (hd + 1) * LANES]
        _put_v(v_ref, hd // 2, hd % 2, _with_ones(v, lane, hd % 2), True)


def _mla_proj(x, mod, gain, wd, qan, kvan, wq, wkv, cos, sin):
    row = lambda i: (i, 0)
    const = lambda i: (0, 0)
    kv_spec, kv_shape = _kv_out(N_HEADS // 2, False)
    v_spec, v_shape = _kv_out(N_HEADS // 2, True)
    return pl.pallas_call(
        _mla_kernel,
        out_shape=(jax.ShapeDtypeStruct((N_TOK, N_HEADS * LANES), BF16), kv_shape, v_shape),
        grid=(N_TOK // TM,),
        in_specs=[pl.BlockSpec((TM, D), row),
                  pl.BlockSpec((None, 1, 6 * D), lambda i: (_mod_row(i), 0, 0)),
                  pl.BlockSpec((1, D), const),
                  pl.BlockSpec(wd.shape, const),
                  pl.BlockSpec((1, MLA_Q_LORA), const),
                  pl.BlockSpec((1, MLA_KV_LORA), const),
                  pl.BlockSpec(wq.shape, const),
                  pl.BlockSpec(wkv.shape, const),
                  pl.BlockSpec((TM, LANES), _rope_row),
                  pl.BlockSpec((TM, LANES), _rope_row)],
        out_specs=(pl.BlockSpec((TM, N_HEADS * LANES), row), kv_spec, v_spec),
        compiler_params=_params("arbitrary"),
        name="mla_proj",
    )(x, mod, gain, wd, qan, kvan, wq, wkv, cos, sin)


def _attn_kernel(*refs, stacks, tq, subs, gp, mode, with_lat):
    refs = list(refs)
    sink_ref = refs.pop(0) if mode == "window" else None
    q_ref, kc_ref, vc_ref = refs[:3]
    refs = refs[3:]
    if with_lat:
        kl_ref, vl_ref = refs[:2]
        refs = refs[2:]
    tab_ref = refs.pop(0) if mode == "neighbourhood" else None
    o_ref = refs[0]
    s_refs = refs[1:]
    g = pl.program_id(1)
    t = pl.program_id(2)
    qw = q_ref.shape[1] // gp
    ow = o_ref.shape[1] // gp
    m_rows = tq * len(stacks[0][0])

    chunks, masks = [], []
    for sub in range(subs):
        ch, mask = [("ctx", 0, CTX)], None
        if with_lat and mode == "window":
            span = tq + 2 * WINDOW
            q0 = (t * subs + sub) * tq
            start = pl.multiple_of(jnp.clip(q0 - WINDOW, 0, S - span), WINDOW)
            qi = q0 + lax.broadcasted_iota(jnp.int32, (m_rows, span), 0) % tq
            ki = start + lax.broadcasted_iota(jnp.int32, (m_rows, span), 1)
            mask = jnp.abs(qi - ki) <= WINDOW
            ch.append(("lat", start, span))
        elif with_lat:
            start = pl.multiple_of(_na_band_start(t * subs + sub) * GRID_W, GRID_W)
            ch.append(("lat", start, NA_BAND * GRID_W))
        chunks.append(ch)
        masks.append(mask)
    cols = [[sum(w for _, _, w in ch[:c]) for c in range(len(ch))] for ch in chunks]

    items = [(gi, sub, qcs, kv) for gi in range(gp) for sub in range(subs) for qcs, kv in stacks]
    qs, sinks = [], []
    for gi, sub, qcs, kv in items:
        qs.append(jnp.concatenate(
            [q_ref[sub * tq:(sub + 1) * tq, gi * qw + qc * LANES:gi * qw + (qc + 1) * LANES]
             for qc, _ in qcs], axis=0))
        sink_col = None
        if mode == "window":
            hd = [(g * gp + gi) * 2 * len(qcs) + 2 * qc + kv for qc, _ in qcs]
            row = lax.broadcasted_iota(jnp.int32, (m_rows, 1), 0)
            sink_col = jnp.where(row < tq, sink_ref[hd[0]], sink_ref[hd[-1]])
        sinks.append(sink_col)

    def score_stage(it):
        gi, sub, _, kv = items[it]
        s_ref = s_refs[it % len(s_refs)]
        m_fold, m_col = None, sinks[it]
        for (kind, st, width), col in zip(chunks[sub], cols[sub]):
            k = kc_ref[gi, kv] if kind == "ctx" else kl_ref[gi, kv, pl.ds(st, width), :]
            s = lax.dot_general(qs[it], k, (((1,), (1,)), ((), ())), preferred_element_type=F32)
            if kind == "lat" and tab_ref is not None:
                s = s + tab_ref[2 * gi + kv]
            if kind == "lat" and masks[sub] is not None:
                s = jnp.where(masks[sub], s, NEG)
            s_ref[0:m_rows, col:col + width] = s
            if width % LANES == 0:
                for j in range(width // LANES):
                    slab = s[:, j * LANES:(j + 1) * LANES]
                    m_fold = slab if m_fold is None else jnp.maximum(m_fold, slab)
            else:
                mc = jnp.max(s, axis=-1, keepdims=True)
                m_col = mc if m_col is None else jnp.maximum(m_col, mc)
        m = jnp.max(m_fold, axis=-1, keepdims=True)
        return m if m_col is None else jnp.maximum(m, m_col)

    def prob_stage(it, m):
        sub = items[it][1]
        s_ref = s_refs[it % len(s_refs)]
        return [jnp.exp(s_ref[0:m_rows, col:col + width] - m).astype(BF16)
                for (_, _, width), col in zip(chunks[sub], cols[sub])]

    def value_stage(it, ps, m):
        gi, sub, _, kv = items[it]
        acc = jnp.zeros((m_rows, LANES), F32)
        for (kind, st, width), p in zip(chunks[sub], ps):
            v = vc_ref[gi, kv] if kind == "ctx" else vl_ref[gi, kv, pl.ds(st, width), :]
            acc = acc + jnp.dot(p, v, preferred_element_type=F32)
        den = acc[:, LANES - 1:LANES] if kv == 0 else acc[:, 0:1]
        if sinks[it] is not None:
            den = den + jnp.exp(sinks[it] - m)
        lane = _lane(acc.shape)
        return jnp.where(lane < HALF if kv == 0 else lane >= HALF, acc / den, 0.0)

    n = len(items)
    ms, ps, outs = [None] * n, [None] * n, {}
    for stage in range(n + 2):
        if 0 <= stage - 1 < n:
            ps[stage - 1] = prob_stage(stage - 1, ms[stage - 1])
        if stage < n:
            ms[stage] = score_stage(stage)
        if 0 <= stage - 2 < n:
            gi, sub, qcs, _ = items[stage - 2]
            o = value_stage(stage - 2, ps[stage - 2], ms[stage - 2])
            for i, (_, oc) in enumerate(qcs):
                key = (sub, gi * (ow // LANES) + oc)
                part = o[i * tq:(i + 1) * tq]
                outs[key] = part if key not in outs else outs[key] + part
    for (sub, oc), o in outs.items():
        o_ref[sub * tq:(sub + 1) * tq, oc * LANES:(oc + 1) * LANES] = o.astype(o_ref.dtype)


def _attention(q, k4, v4, *, stacks, mode, tq, subs=1, gp=1, sink=None, table=None,
               ctx_queries=False):
    groups = k4.shape[0] // gp
    qw = q.shape[1] // groups
    ow = N_HEADS * HEAD_DIM // groups
    with_lat = not ctx_queries
    tqs = tq * subs
    nq = S // tqs if with_lat else CTX // tqs
    row0 = 0 if with_lat else N_LAT // tqs

    q_spec = pl.BlockSpec((tqs, qw), lambda b, g, t: (row0 + b * nq + t, g))
    ctx_spec = pl.BlockSpec((gp, 2, CTX, LANES), lambda b, g, t: (g, 0, N_LAT // CTX + b, 0))
    lat_spec = pl.BlockSpec((gp, 2, S, LANES), lambda b, g, t: (g, 0, b, 0))
    in_specs = [q_spec, ctx_spec, ctx_spec]
    args = [q, k4, v4]
    n_keys = CTX
    if with_lat:
        in_specs += [lat_spec, lat_spec]
        args += [k4, v4]
        n_keys += {"window": tq + 2 * WINDOW, "neighbourhood": NA_BAND * GRID_W}[mode]
    if mode == "window":
        in_specs = [pl.BlockSpec(memory_space=pltpu.SMEM)] + in_specs
        args = [sink] + args
    if mode == "neighbourhood":
        in_specs.append(pl.BlockSpec((None, 2 * gp, tq, NA_BAND * GRID_W),
                                     lambda b, g, t: (_na_variant(t), g, 0, 0)))
        args.append(table)
    m_rows = tq * max(len(qcs) for qcs, _ in stacks)
    return pl.pallas_call(
        functools.partial(_attn_kernel, stacks=stacks, tq=tq, subs=subs, gp=gp, mode=mode,
                          with_lat=with_lat),
        out_shape=jax.ShapeDtypeStruct((N_LAT if with_lat else N_CTX, N_HEADS * HEAD_DIM), BF16),
        grid=(B, groups, nq),
        in_specs=in_specs,
        out_specs=pl.BlockSpec((tqs, ow), lambda b, g, t: (b * nq + t, g)),
        scratch_shapes=[pltpu.VMEM((m_rows, n_keys), F32) for _ in range(2)],
        compiler_params=_params("arbitrary", "arbitrary", "arbitrary"),
        name="attn_" + mode + ("" if with_lat else "_ctx"),
    )(*args)


def _attn_t_kernel(*refs, stacks, tq, subs, with_lat):
    refs = list(refs)
    q_ref, kc_ref, vtc_ref = refs[:3]
    refs = refs[3:]
    if with_lat:
        kl_ref, vtl_ref = refs[:2]
        refs = refs[2:]
    o_ref = refs[0]
    s_refs = refs[1:]
    chunks = [("ctx", 0, CTX)]
    if with_lat:
        chunks += [("lat", c * TK, TK) for c in range(S // TK)]
    rows = [sum(w for _, _, w in chunks[:c]) for c in range(len(chunks))]
    items = [(sub, qcs, kv) for sub in range(subs) for qcs, kv in stacks]
    qs = [jnp.concatenate([q_ref[sub * tq:(sub + 1) * tq, qc * LANES:(qc + 1) * LANES]
                           for qc, _ in qcs], axis=0) for sub, qcs, _ in items]
    cols = qs[0].shape[0]

    def score_chunk(it, c, m8):
        kind, st, width = chunks[c]
        kv = items[it][2]
        k = kc_ref[kv] if kind == "ctx" else kl_ref[kv, st:st + width, :]
        s = lax.dot_general(k, qs[it], (((1,), (1,)), ((), ())), preferred_element_type=F32)
        s_refs[it % len(s_refs)][rows[c]:rows[c] + width, 0:cols] = s
        c8 = jnp.max(s.reshape(width // 8, 8, cols), axis=0)
        return c8 if m8 is None else jnp.maximum(m8, c8)

    def prob_chunk(it, c, m):
        width = chunks[c][2]
        s = s_refs[it % len(s_refs)][rows[c]:rows[c] + width, 0:cols]
        return jnp.exp2(s - m).astype(BF16)

    def value_chunk(it, c, p, acc):
        kind, st, width = chunks[c]
        kv = items[it][2]
        vt = vtc_ref[kv] if kind == "ctx" else vtl_ref[kv, :, st:st + width]
        return acc + jnp.dot(vt, p, preferred_element_type=F32)

    n = len(items)
    ms, ps, accs = [None] * n, [[] for _ in range(n)], []
    for stage in range(n + 2):
        m8 = None
        acc = jnp.zeros((LANES, cols), F32)
        for c in range(len(chunks)):
            if 0 <= stage - 1 < n:
                ps[stage - 1].append(prob_chunk(stage - 1, c, ms[stage - 1]))
            if stage < n:
                m8 = score_chunk(stage, c, m8)
            if 0 <= stage - 2 < n:
                acc = value_chunk(stage - 2, c, ps[stage - 2][c], acc)
        if stage < n:
            ms[stage] = jnp.max(m8, axis=0, keepdims=True)
        if 0 <= stage - 2 < n:
            accs.append(acc)

    outs = {}
    for (sub, qcs, kv), acc in zip(items, accs):
        den = acc[LANES - 1:LANES, :] if kv == 0 else acc[0:1, :]
        o = (acc / den).T
        lane = _lane(o.shape)
        o = jnp.where(lane < HALF if kv == 0 else lane >= HALF, o, 0.0)
        for i, (_, oc) in enumerate(qcs):
            part = o[i * tq:(i + 1) * tq]
            outs[sub, oc] = part if (sub, oc) not in outs else outs[sub, oc] + part
    for (sub, oc), o in outs.items():
        o_ref[sub * tq:(sub + 1) * tq, oc * LANES:(oc + 1) * LANES] = o.astype(o_ref.dtype)


def _attention_t(q, k4, vt4, *, stacks, tq, subs=1, ctx_queries=False):
    groups = k4.shape[0]
    qw = q.shape[1] // groups
    ow = N_HEADS * HEAD_DIM // groups
    with_lat = not ctx_queries
    tqs = tq * subs
    nq = S // tqs if with_lat else CTX // tqs
    row0 = 0 if with_lat else N_LAT // tqs
    in_specs = [pl.BlockSpec((tqs, qw), lambda b, g, t: (row0 + b * nq + t, g)),
                pl.BlockSpec((None, 2, CTX, LANES), lambda b, g, t: (g, 0, N_LAT // CTX + b, 0)),
                pl.BlockSpec((None, 2, LANES, CTX), lambda b, g, t: (g, 0, 0, N_LAT // CTX + b))]
    args = [q, k4, vt4]
    n_keys = CTX
    if with_lat:
        in_specs += [pl.BlockSpec((None, 2, S, LANES), lambda b, g, t: (g, 0, b, 0)),
                     pl.BlockSpec((None, 2, LANES, S), lambda b, g, t: (g, 0, 0, b))]
        args += [k4, vt4]
        n_keys += S
    m_rows = tq * max(len(qcs) for qcs, _ in stacks)
    return pl.pallas_call(
        functools.partial(_attn_t_kernel, stacks=stacks, tq=tq, subs=subs, with_lat=with_lat),
        out_shape=jax.ShapeDtypeStruct((N_LAT if with_lat else N_CTX, N_HEADS * HEAD_DIM), BF16),
        grid=(B, groups, nq),
        in_specs=in_specs,
        out_specs=pl.BlockSpec((tqs, ow), lambda b, g, t: (b * nq + t, g)),
        scratch_shapes=[pltpu.VMEM((n_keys, m_rows), F32) for _ in range(2)],
        compiler_params=_params("arbitrary", "arbitrary", "arbitrary"),
        name="attn_global" + ("" if with_lat else "_ctx"),
    )(*args)


def _na_band_start(t):
    return jnp.clip(t * NA_QROWS - NA_ROWS // 2, 0, S // GRID_W - NA_BAND)


def _na_tiles():
    rows = S // GRID_W
    tiles = [(t * NA_QROWS, min(max(t * NA_QROWS - NA_ROWS // 2, 0), rows - NA_BAND))
             for t in range(rows // NA_QROWS)]
    lo = [rb for rb in tiles if rb[0] - NA_ROWS // 2 < 0]
    hi = [rb for rb in tiles if rb[0] - NA_ROWS // 2 > rows - NA_BAND]
    return lo + [tiles[len(lo)]] + hi, len(lo), len(hi)


def _na_variant(t):
    _, n_lo, n_hi = _na_tiles()
    first_hi = S // GRID_W // NA_QROWS - n_hi
    return jnp.where(t < n_lo, t, jnp.where(t >= first_hi, t - first_hi + n_lo + 1, n_lo))


def _bias_cols_kernel(rpb_ref, sel_ref, o_ref):
    o_ref[...] = jnp.dot(rpb_ref[...], sel_ref[...], preferred_element_type=F32,
                         precision=lax.Precision.HIGHEST)


def _na_table(rpb):
    rows = S // GRID_W
    n_dr, n_dc = 2 * NA_ROWS - 1, 2 * NA_COLS - 1
    c = np.arange(GRID_W)[:, None]
    kc = np.arange(GRID_W)[None, :]
    cs = np.clip(c - NA_COLS // 2, 0, GRID_W - NA_COLS)
    col_ok = (kc >= cs) & (kc < cs + NA_COLS)
    sel = (np.arange(LANES)[:, None, None] == (kc - c + NA_COLS - 1)[None]) & col_ok[None]
    sel = jnp.asarray(sel.reshape(LANES, GRID_W * GRID_W), F32)
    rpb2 = jnp.pad(rpb.reshape(N_HEADS * n_dr, n_dc), ((0, 256 - N_HEADS * n_dr), (0, LANES - n_dc)))
    cols = pl.pallas_call(
        _bias_cols_kernel,
        out_shape=jax.ShapeDtypeStruct((256, GRID_W * GRID_W), F32),
        name="na_bias_cols",
    )(rpb2, sel)
    cols = cols[:N_HEADS * n_dr].reshape(N_HEADS, n_dr, GRID_W, GRID_W)
    cols = jnp.where(col_ok[None, None], cols, NEG)
    masked = jnp.full((N_HEADS, GRID_W, GRID_W), NEG, F32)
    variants = []
    for r0, bs in _na_tiles()[0]:
        q_rows = []
        for r in range(r0, r0 + NA_QROWS):
            rs = min(max(r - NA_ROWS // 2, 0), rows - NA_ROWS)
            q_rows.append(jnp.concatenate(
                [cols[:, kr - r + NA_ROWS - 1] if rs <= kr < rs + NA_ROWS else masked
                 for kr in range(bs, bs + NA_BAND)], axis=2))
        variants.append(jnp.concatenate(q_rows, axis=1))
    return jnp.stack(variants)


def _router_route(h, wr_ref):
    w = wr_ref[...]
    h_hi, w_hi = h.astype(BF16), w.astype(BF16)
    h_lo = (h - h_hi.astype(F32)).astype(BF16)
    w_lo = (w - w_hi.astype(F32)).astype(BF16)
    logits = (jnp.dot(h_hi, w_hi, preferred_element_type=F32)
              + (jnp.dot(h_hi, w_lo, preferred_element_type=F32)
                 + jnp.dot(h_lo, w_hi, preferred_element_type=F32)))
    lane = _lane(logits.shape).astype(F32)
    lg = jnp.where(lane < N_EXPERTS, logits, -jnp.inf)
    m1 = jnp.max(lg, axis=-1, keepdims=True)
    i1 = jnp.min(jnp.where(lg == m1, lane, float(LANES)), axis=-1, keepdims=True)
    lg2 = jnp.where(lane == i1, -jnp.inf, lg)
    m2 = jnp.max(lg2, axis=-1, keepdims=True)
    i2 = jnp.min(jnp.where(lg2 == m2, lane, float(LANES)), axis=-1, keepdims=True)
    e = jnp.exp(m2 - m1)
    return jnp.where(lane == 0, i1, jnp.where(lane == 1, i2, jnp.where(
        lane == 2, 1.0 / (1.0 + e), jnp.where(lane == 3, e / (1.0 + e), 0.0))))


def _oproj_kernel(*refs, moe, with_ctx, split_x):
    refs = list(refs)
    o = _pick_rows(refs.pop(0), refs.pop(0) if with_ctx else None)
    x = _pick_rows(refs.pop(0), refs.pop(0) if split_x else None)
    if moe:
        wo_ref, mod_ref, post_ref, pre_ref, wr_ref, xo_ref, h_ref, route_ref = refs
    else:
        wo_ref, mod_ref, post_ref, pre_ref, xo_ref, h_ref = refs
    y = jnp.dot(o, wo_ref[...], preferred_element_type=F32)
    x1 = x + mod_ref[:, G_M] * _rms(y, post_ref[...])
    xo_ref[...] = x1
    h = _rms(x1, pre_ref[...]) * (1.0 + mod_ref[:, SC_F]) + mod_ref[:, SH_F]
    h_ref[...] = h.astype(h_ref.dtype)
    if moe:
        route_ref[...] = _router_route(h, wr_ref)


def _oproj(o, o_ctx, wo, x, mod, post, pre, w_router, x_ctx=None):
    moe = w_router is not None
    with_ctx = o_ctx is not None
    n_rows = N_TOK if with_ctx else N_LAT
    row = lambda i: (i, 0)
    const = lambda i: (0, 0)
    o_specs, o_args = _rows_of(o, o_ctx)
    x_specs, x_args = _rows_of(x, x_ctx)
    in_specs = o_specs + x_specs + [
        pl.BlockSpec((D, D), const),
        pl.BlockSpec((None, 1, 6 * D), lambda i: (_mod_row(i), 0, 0)),
        pl.BlockSpec((1, D), const), pl.BlockSpec((1, D), const)]
    args = o_args + x_args + [wo, mod, post, pre]
    out_shape = [jax.ShapeDtypeStruct((n_rows, D), F32),
                 jax.ShapeDtypeStruct((n_rows, D), F32 if moe else BF16)]
    out_specs = [pl.BlockSpec((TM, D), row), pl.BlockSpec((TM, D), row)]
    if moe:
        in_specs.append(pl.BlockSpec((D, LANES), const))
        out_shape.append(jax.ShapeDtypeStruct((n_rows, LANES), F32))
        out_specs.append(pl.BlockSpec((TM, LANES), row))
        args.append(w_router)
    return pl.pallas_call(
        functools.partial(_oproj_kernel, moe=moe, with_ctx=with_ctx, split_x=x_ctx is not None),
        out_shape=tuple(out_shape),
        grid=(n_rows // TM,),
        in_specs=in_specs,
        out_specs=tuple(out_specs),
        compiler_params=_params("arbitrary"),
        name="oproj",
    )(*args)


def _swiglu(h, gu_ref, dw_ref):
    gu = jnp.dot(h, gu_ref[...], preferred_element_type=F32)
    half = gu.shape[1] // 2
    act = (jax.nn.silu(gu[:, :half]) * gu[:, half:]).astype(BF16)
    return jnp.dot(act, dw_ref[...], preferred_element_type=F32)


def _ffn_kernel(h_ref, gu_ref, dw_ref, x_ref, mod_ref, post_ref, o_ref):
    y = _swiglu(h_ref[...], gu_ref, dw_ref)
    o_ref[...] = x_ref[...] + mod_ref[:, G_F] * _rms(y, post_ref[...])


def _ffn(h, w_gu, w_down, x, mod, post):
    row = lambda i: (i, 0)
    const = lambda i: (0, 0)
    resident = pl.Buffered(1)
    return pl.pallas_call(
        _ffn_kernel,
        out_shape=jax.ShapeDtypeStruct((N_TOK, D), F32),
        grid=(N_TOK // TM,),
        in_specs=[pl.BlockSpec((TM, D), row),
                  pl.BlockSpec((D, 2 * D_FF), const, pipeline_mode=resident),
                  pl.BlockSpec((D_FF, D), const, pipeline_mode=resident),
                  pl.BlockSpec((TM, D), row),
                  pl.BlockSpec((None, 1, 6 * D), lambda i: (_mod_row(i), 0, 0)),
                  pl.BlockSpec((1, D), const)],
        out_specs=pl.BlockSpec((TM, D), row),
        compiler_params=_params("arbitrary"),
        name="ffn_dense",
    )(h, w_gu, w_down, x, mod, post)


def _routing(route, n_rows):
    i1 = route[:, 0].astype(jnp.int32)
    i2 = route[:, 1].astype(jnp.int32)
    e = jnp.arange(N_EXPERTS, dtype=jnp.int32)
    hit1 = i1[:, None] == e
    hit2 = i2[:, None] == e
    sel = hit1.astype(jnp.int32) + hit2.astype(jnp.int32)
    cum = jnp.cumsum(sel, axis=0)
    tiles = (cum[-1] + TME - 1) // TME
    tile_end = jnp.cumsum(tiles)
    slot = ((tile_end - tiles) * TME)[None] + cum - sel
    slots = jnp.stack([jnp.sum(jnp.where(hit1, slot, 0), axis=1),
                       jnp.sum(jnp.where(hit2, slot, 0), axis=1)], axis=1).reshape(-1)
    j = jnp.arange(_n_slot_tiles(n_rows), dtype=jnp.int32)
    tile_expert = jnp.minimum(jnp.sum(j[:, None] >= tile_end[None], axis=1), N_EXPERTS - 1)
    return slots.astype(jnp.int32), tile_expert.astype(jnp.int32), tile_end.astype(jnp.int32)


def _n_slot_tiles(n_rows):
    return 2 * n_rows // TME + N_EXPERTS


def _row_copies(slots_ref, n, copy):
    def start(r, carry):
        for k in range(2):
            copy(r, k, slots_ref[2 * r + k]).start(priority=k)
        return carry

    def wait(r, carry):
        for k in range(2):
            copy(r, k, slots_ref[2 * r + k]).wait()
        return carry

    lax.fori_loop(0, n, start, 0, unroll=8)
    lax.fori_loop(0, n, wait, 0, unroll=8)


def _dispatch_kernel(tend_ref, slots_ref, h_ref, xs_ref, zero_ref, zsem, sem, *, n_tiles):
    @pl.when(pl.program_id(0) == 0)
    def _():
        zero_ref[...] = jnp.zeros_like(zero_ref)
        tiles = []
        for e in range(N_EXPERTS):
            first = tend_ref[e - 1] if e else 0
            tiles.append((tend_ref[e] > first, tend_ref[e] - 1))
        for u in range(N_EXPERTS):
            tile = tend_ref[N_EXPERTS - 1] + u
            tiles.append((tile < n_tiles, tile))
        copies = [(ok, pltpu.make_async_copy(
            zero_ref, xs_ref.at[pl.ds(pl.multiple_of(jnp.maximum(tile, 0) * TME, TME), TME)], zsem))
            for ok, tile in tiles]
        for ok, cp in copies:
            pl.when(ok)(cp.start)
        for ok, cp in copies:
            pl.when(ok)(cp.wait)

    _row_copies(slots_ref, TMR, lambda r, k, s: pltpu.make_async_copy(
        h_ref.at[pl.ds(r, 1)], xs_ref.at[pl.ds(s, 1)], sem))


def _dispatch(tile_end, slots, h, n_rows):
    n_tiles = _n_slot_tiles(n_rows)
    return pl.pallas_call(
        functools.partial(_dispatch_kernel, n_tiles=n_tiles),
        out_shape=jax.ShapeDtypeStruct((n_tiles * TME, D), F32),
        grid_spec=pltpu.PrefetchScalarGridSpec(
            num_scalar_prefetch=1,
            grid=(n_rows // TMR,),
            in_specs=[pl.BlockSpec((2 * TMR,), lambda i, te: (i,), memory_space=pltpu.SMEM),
                      pl.BlockSpec((TMR, D), lambda i, te: (i, 0))],
            out_specs=pl.BlockSpec(memory_space=pl.ANY),
            scratch_shapes=[pltpu.VMEM((TME, D), F32), pltpu.SemaphoreType.DMA(()),
                            pltpu.SemaphoreType.DMA(())]),
        compiler_params=_params("arbitrary"),
        name="moe_dispatch",
    )(tile_end, slots, h)


def _expert_kernel(te_ref, tend_ref, x_ref, gu_ref, dw_ref, o_ref, gub_ref, db_ref):
    j = pl.program_id(0)
    valid = j < tend_ref[N_EXPERTS - 1]

    @pl.when(jnp.logical_or(j == 0, te_ref[j] != te_ref[jnp.maximum(j - 1, 0)]))
    def _():
        gub_ref[...] = gu_ref[...].astype(BF16)
        db_ref[...] = dw_ref[...].astype(BF16)

    @pl.when(valid)
    def _():
        o_ref[...] = _swiglu(x_ref[...].astype(BF16), gub_ref, db_ref)

    @pl.when(jnp.logical_not(valid))
    def _():
        o_ref[...] = jnp.zeros_like(o_ref)


def _experts(tile_expert, tile_end, xs, w_gu, w_down):
    n_tiles = xs.shape[0] // TME
    return pl.pallas_call(
        _expert_kernel,
        out_shape=jax.ShapeDtypeStruct(xs.shape, F32),
        grid_spec=pltpu.PrefetchScalarGridSpec(
            num_scalar_prefetch=2,
            grid=(n_tiles,),
            in_specs=[pl.BlockSpec((TME, D), lambda j, te, tend: (j, 0)),
                      pl.BlockSpec((None, D, 2 * D_FF_EXPERT), lambda j, te, tend: (te[j], 0, 0)),
                      pl.BlockSpec((None, D_FF_EXPERT, D), lambda j, te, tend: (te[j], 0, 0))],
            out_specs=pl.BlockSpec((TME, D), lambda j, te, tend: (j, 0)),
            scratch_shapes=[pltpu.VMEM((D, 2 * D_FF_EXPERT), BF16),
                            pltpu.VMEM((D_FF_EXPERT, D), BF16)]),
        compiler_params=_params("arbitrary"),
        name="moe_experts",
    )(tile_expert, tile_end, xs, w_gu, w_down)


def _combine_kernel(slots_ref, route_ref, x_ref, mod_ref, post_ref, ys_ref, o_ref, buf_ref, sem):
    _row_copies(slots_ref, TMR, lambda r, k, s: pltpu.make_async_copy(
        ys_ref.at[pl.ds(s, 1)], buf_ref.at[k, pl.ds(r, 1)], sem))
    y = route_ref[:, 2:3] * buf_ref[0] + route_ref[:, 3:4] * buf_ref[1]
    o_ref[...] = x_ref[...] + mod_ref[:, G_F] * _rms(y, post_ref[...])


def _combine(slots, route, x, mod, post, ys, n_rows):
    return pl.pallas_call(
        _combine_kernel,
        out_shape=jax.ShapeDtypeStruct((n_rows, D), F32),
        grid=(n_rows // TMR,),
        in_specs=[pl.BlockSpec((2 * TMR,), lambda i: (i,), memory_space=pltpu.SMEM),
                  pl.BlockSpec((TMR, LANES), lambda i: (i, 0)),
                  pl.BlockSpec((TMR, D), lambda i: (i, 0)),
                  pl.BlockSpec((None, 1, 6 * D), lambda i: (_mod_row(i * TMR // TM), 0, 0)),
                  pl.BlockSpec((1, D), lambda i: (0, 0)),
                  pl.BlockSpec(memory_space=pl.ANY)],
        out_specs=pl.BlockSpec((TMR, D), lambda i: (i, 0)),
        scratch_shapes=[pltpu.VMEM((2, TMR, D), F32), pltpu.SemaphoreType.DMA(())],
        compiler_params=_params("arbitrary"),
        name="moe_combine",
    )(slots, route, x, mod, post, ys)


def _moe(h, route, w_gu, w_down, x, mod, post, n_rows):
    slots, tile_expert, tile_end = _routing(route, n_rows)
    xs = _dispatch(tile_end, slots, h, n_rows)
    ys = _experts(tile_expert, tile_end, xs, w_gu, w_down)
    return _combine(slots, route, x, mod, post, ys, n_rows)


def _rope_tables(rot_dim, lead, tail):
    t = jnp.arange(S, dtype=jnp.int32)
    row = (t // GRID_W).astype(F32)
    col = (t % GRID_W).astype(F32)
    quarter = rot_dim // 4
    inv = ROPE_THETA ** (-jnp.arange(quarter, dtype=F32) / quarter)
    ang = jnp.concatenate([row[:, None] * inv, col[:, None] * inv], axis=-1)
    cos, sin = jnp.cos(ang), jnp.sin(ang)
    reps = (LANES - lead - tail) // rot_dim
    cos_l = jnp.concatenate([jnp.ones((S, lead), F32)] + [cos, cos] * reps + [jnp.ones((S, tail), F32)], axis=1)
    sin_l = jnp.concatenate([jnp.zeros((S, lead), F32)] + [-sin, sin] * reps + [jnp.zeros((S, tail), F32)], axis=1)
    cos_all = jnp.concatenate([cos_l, jnp.ones((TM, LANES), F32)], axis=0)
    sin_all = jnp.concatenate([sin_l, jnp.zeros((TM, LANES), F32)], axis=0)
    return cos_all, sin_all


def _rope_row(i):
    return (jnp.where(i < N_LAT // TM, i % (S // TM), S // TM), 0)


def _mla_weights(w_dkv, w_q_b, w_kv_b):
    lat = MLA_Q_LORA + MLA_KV_LORA
    pad = LANES - MLA_NOPE - MLA_ROPE
    wd = jnp.concatenate([w_dkv[:, :lat], jnp.zeros((D, MLA_NOPE), F32), w_dkv[:, lat:],
                          jnp.zeros((D, pad), F32)], axis=1)
    wq = jnp.pad(w_q_b.reshape(MLA_Q_LORA, N_HEADS, MLA_NOPE + MLA_ROPE), ((0, 0), (0, 0), (0, pad)))
    kvr = w_kv_b.reshape(MLA_KV_LORA, N_HEADS, 2 * HALF)
    k_part, v_part = kvr[..., :HALF], kvr[..., HALF:]
    zero = jnp.zeros_like(v_part)
    wk = jnp.concatenate([k_part, zero], axis=-1)
    even = (jnp.arange(N_HEADS) % 2 == 0)[None, :, None]
    wv = jnp.where(even, jnp.concatenate([v_part, zero], -1), jnp.concatenate([zero, v_part], -1))
    wkv = jnp.concatenate([wk.reshape(MLA_KV_LORA, -1), wv.reshape(MLA_KV_LORA, -1)], axis=1)
    return wd.astype(BF16), wq.reshape(MLA_Q_LORA, -1).astype(BF16), wkv.astype(BF16)


GQA_STACKS = ((((0, 0), (1, 1)), 0), (((0, 0), (1, 1)), 1))
MLA_STACKS = ((((0, 0),), 0), (((1, 0),), 1))
NA_STACKS = ((((0, 0),), 0), (((0, 0),), 1))


def kernel(x, c, ctx, c_ctx,
           l0_ada_w, l0_ada_b, l0_mix_pre, l0_mix_post, l0_ffn_pre, l0_ffn_post,
           l0_w_qkv, l0_q_norm, l0_k_norm, l0_w_o, l0_ffn_w_gu, l0_ffn_w_down,
           l1_ada_w, l1_ada_b, l1_mix_pre, l1_mix_post, l1_ffn_pre, l1_ffn_post,
           l1_w_qkv, l1_sink, l1_w_o, l1_router, l1_moe_w_gu, l1_moe_w_down,
           l2_ada_w, l2_ada_b, l2_mix_pre, l2_mix_post, l2_ffn_pre, l2_ffn_post,
           l2_w_dkv, l2_q_a_norm, l2_w_q_b, l2_kv_a_norm, l2_w_kv_b, l2_w_o, l2_ffn_w_gu, l2_ffn_w_down,
           l3_ada_w, l3_ada_b, l3_mix_pre, l3_mix_post, l3_ffn_pre, l3_ffn_post,
           l3_w_qkv, l3_rpb, l3_w_o, l3_router, l3_moe_w_gu, l3_moe_w_down):
    vec = lambda a: a.reshape(1, -1)
    tile2 = lambda a: jnp.tile(a, 2).reshape(1, LANES)
    bf = lambda a: a.astype(BF16)
    pad_router = lambda w: jnp.pad(w, ((0, 0), (0, LANES - N_EXPERTS)))

    x_lat, x_ctx = x.reshape(N_LAT, D), ctx.reshape(N_CTX, D)
    c_all = jnp.concatenate([c, c_ctx[None], jnp.zeros((8 - B - 1, D), F32)], axis=0)
    mods = [_ada(c_all, w, b).reshape(8, 1, 6 * D)
            for w, b in ((l0_ada_w, l0_ada_b), (l1_ada_w, l1_ada_b), (l2_ada_w, l2_ada_b), (l3_ada_w, l3_ada_b))]
    cos64, sin64 = _rope_tables(HEAD_DIM, 0, 0)
    cos32, sin32 = _rope_tables(MLA_ROPE, MLA_NOPE, LANES - MLA_NOPE - MLA_ROPE)
    ones = jnp.ones((1, LANES), F32)
    idx = np.arange(LANES)
    bd = jnp.asarray(idx[:, None] // HALF == idx[None, :] // HALF, BF16)

    q, k4, v4 = _qkv(x_lat, mods[0], vec(l0_mix_pre), bf(l0_w_qkv), cos64, sin64,
                     tile2(l0_q_norm), tile2(l0_k_norm), bd, shared_kv=True, head_norm=True, rope=True,
                     v_t=True, q_scale=HEAD_DIM ** -0.5 * LOG2E, x_ctx=x_ctx)
    o = _attention_t(q, k4, v4, stacks=GQA_STACKS, tq=TQ, subs=4)
    oc = _attention_t(q, k4, v4, stacks=GQA_STACKS, tq=CTX, ctx_queries=True)
    xs, h = _oproj(o, oc, bf(l0_w_o), x_lat, mods[0], vec(l0_mix_post), vec(l0_ffn_pre), None,
                   x_ctx=x_ctx)
    xs = _ffn(h, bf(l0_ffn_w_gu), bf(l0_ffn_w_down), xs, mods[0], vec(l0_ffn_post))

    q, k4, v4 = _qkv(xs, mods[1], vec(l1_mix_pre), bf(l1_w_qkv), cos64, sin64,
                     ones, ones, bd, shared_kv=True, head_norm=False, rope=True,
                     v_t=False, q_scale=HEAD_DIM ** -0.5)
    o = _attention(q, k4, v4, stacks=GQA_STACKS, mode="window", tq=TQ, subs=4, sink=l1_sink)
    oc = _attention(q, k4, v4, stacks=GQA_STACKS, mode="window", tq=CTX, sink=l1_sink, ctx_queries=True)
    xs, h, route = _oproj(o, oc, bf(l1_w_o), xs, mods[1], vec(l1_mix_post), vec(l1_ffn_pre),
                          pad_router(l1_router))
    xs = _moe(h, route, l1_moe_w_gu, l1_moe_w_down, xs, mods[1], vec(l1_ffn_post), N_TOK)

    wd, wq, wkv = _mla_weights(l2_w_dkv, l2_w_q_b, l2_w_kv_b)
    q, k4, v4 = _mla_proj(xs, mods[2], vec(l2_mix_pre), wd, vec(l2_q_a_norm), vec(l2_kv_a_norm),
                          wq, wkv, cos32, sin32)
    o = _attention_t(q, k4, v4, stacks=MLA_STACKS, tq=2 * TQ, subs=4)
    oc = _attention_t(q, k4, v4, stacks=MLA_STACKS, tq=CTX, ctx_queries=True)
    xs, h = _oproj(o, oc, bf(l2_w_o), xs, mods[2], vec(l2_mix_post), vec(l2_ffn_pre), None)
    xs = _ffn(h, bf(l2_ffn_w_gu), bf(l2_ffn_w_down), xs, mods[2], vec(l2_ffn_post))

    q, k4, v4 = _qkv(xs, mods[3], vec(l3_mix_pre), bf(l3_w_qkv), cos64, sin64,
                     ones, ones, bd, shared_kv=False, head_norm=False, rope=False,
                     v_t=False, q_scale=HEAD_DIM ** -0.5)
    o = _attention(q, k4, v4, stacks=NA_STACKS, mode="neighbourhood", tq=NA_QROWS * GRID_W, gp=4,
                   table=_na_table(l3_rpb))
    xl, h, route = _oproj(o, None, bf(l3_w_o), xs, mods[3], vec(l3_mix_post), vec(l3_ffn_pre),
                          pad_router(l3_router))
    xl = _moe(h, route, l3_moe_w_gu, l3_moe_w_down, xl, mods[3], vec(l3_ffn_post), N_LAT)
    return xl.reshape(B, S, D)
```

```python
import functools

import numpy as np
import jax
import jax.numpy as jnp
from jax import lax
from jax.experimental import pallas as pl
from jax.experimental.pallas import tpu as pltpu

D = 1024
B = 4
S = 4096
CTX = 256
GRID_W = 64
N_LAT = B * S
N_CTX = B * CTX
N_TOK = N_LAT + N_CTX
HEAD_DIM = 64
N_HEADS = 16
N_KV_HEADS = 4
WINDOW = 128
MLA_Q_LORA = 384
MLA_KV_LORA = 256
MLA_NOPE = 64
MLA_ROPE = 32
NA_ROWS = 8
NA_COLS = 16
D_FF = 2816
N_EXPERTS = 8
D_FF_EXPERT = 1408
ROPE_THETA = 10000.0
EPS = 1e-6
NEG = -1e30
LOG2E = 1.4426950408889634

LANES = 128
HALF = LANES // 2
TM = 512
TQ = 256
TK = 512
TME = 256
TMR = 1024
NA_QROWS = 4
NA_BAND = NA_QROWS + NA_ROWS
VMEM_LIMIT = 56 * 1024 * 1024

F32 = jnp.float32
BF16 = jnp.bfloat16

SH_M, SC_M, G_M, SH_F, SC_F, G_F = (slice(i * D, (i + 1) * D) for i in range(6))


def _params(*sem, flags=None):
    return pltpu.CompilerParams(dimension_semantics=sem, vmem_limit_bytes=VMEM_LIMIT, flags=flags)


def _mod_row(i):
    r0 = i * TM
    return jnp.where(r0 >= N_LAT, B, r0 // S)


def _rms(x, gain):
    return x * lax.rsqrt(jnp.mean(x * x, axis=-1, keepdims=True) + EPS) * gain


def _lane(shape):
    return lax.broadcasted_iota(jnp.int32, shape, len(shape) - 1)


def _ada_kernel(c_ref, w_ref, b_ref, o_ref):
    a = jax.nn.silu(c_ref[...])
    o_ref[...] = jnp.dot(a, w_ref[...], preferred_element_type=F32,
                         precision=lax.Precision.HIGHEST) + b_ref[...]


def _ada(c_all, w, b):
    tn = 1536
    return pl.pallas_call(
        _ada_kernel,
        out_shape=jax.ShapeDtypeStruct((8, 6 * D), F32),
        grid=(6 * D // tn,),
        in_specs=[pl.BlockSpec((8, D), lambda j: (0, 0)),
                  pl.BlockSpec((D, tn), lambda j: (0, j)),
                  pl.BlockSpec((1, tn), lambda j: (0, j))],
        out_specs=pl.BlockSpec((8, tn), lambda j: (0, j)),
        compiler_params=_params("arbitrary"),
        name="ada",
    )(c_all, w, b.reshape(1, 6 * D))


def _rope_chunk(t, cos, sin, half):
    lane = _lane(t.shape)
    rot = jnp.where((lane % (2 * half)) < half,
                    pltpu.roll(t, LANES - half, 1), pltpu.roll(t, half, 1))
    return t * cos + rot * sin


def _head_rms(t, gain, bd):
    t2 = t * t
    hi = t2.astype(BF16)
    lo = (t2 - hi.astype(F32)).astype(BF16)
    ss = (jnp.dot(hi, bd, preferred_element_type=F32)
          + jnp.dot(lo, bd, preferred_element_type=F32))
    return t * lax.rsqrt(ss * (1.0 / HEAD_DIM) + EPS) * gain


def _with_ones(v, lane, parity):
    if parity == 0:
        return jnp.where(lane < HALF, v, jnp.where(lane == LANES - 1, 1.0, 0.0))
    return jnp.where(lane >= HALF, v, jnp.where(lane == 0, 1.0, 0.0))


def _half_only(k, lane, parity):
    return jnp.where(lane < HALF, k, 0.0) if parity == 0 else jnp.where(lane >= HALF, k, 0.0)


def _put_v(v_ref, j, parity, v):
    v_ref[j, parity] = v.T.astype(BF16)


def _rows_of(lat, ctx):
    n_lat = N_LAT // TM
    if ctx is None:
        return [pl.BlockSpec((TM, D), lambda i: (i, 0))], [lat]
    return ([pl.BlockSpec((TM, D), lambda i: (jnp.minimum(i, n_lat - 1), 0)),
             pl.BlockSpec((TM, D), lambda i: (jnp.maximum(i - n_lat, 0), 0))], [lat, ctx])


def _pick_rows(lat_ref, ctx_ref):
    if ctx_ref is None:
        return lat_ref[...]
    return jnp.where(pl.program_id(0) < N_LAT // TM, lat_ref[...], ctx_ref[...])


def _qkv_kernel(*refs, shared_kv, head_norm, rope, q_scale, split_x):
    refs = list(refs)
    x = _pick_rows(refs.pop(0), refs.pop(0) if split_x else None)
    (mod_ref, g_ref, w_ref, cos_ref, sin_ref, qg_ref, kg_ref, bd_ref, q_ref, k_ref, v_ref) = refs
    h = _rms(x, g_ref[...]) * (1.0 + mod_ref[:, SC_M]) + mod_ref[:, SH_M]
    acc = jnp.dot(h.astype(BF16), w_ref[...], preferred_element_type=F32)
    lane = _lane((TM, LANES))
    n_q = N_HEADS * HEAD_DIM // LANES
    n_kv = (N_KV_HEADS if shared_kv else N_HEADS) * HEAD_DIM // LANES

    def finish(t, gain_ref):
        if head_norm:
            t = _head_rms(t, gain_ref[...], bd_ref[...])
        if rope:
            t = _rope_chunk(t, cos_ref[...], sin_ref[...], HEAD_DIM // 2)
        return t

    for c in range(n_q):
        t = finish(acc[:, c * LANES:(c + 1) * LANES], qg_ref)
        q_ref[:, c * LANES:(c + 1) * LANES] = (t * q_scale).astype(BF16)
    for c in range(n_kv):
        k = finish(acc[:, (n_q + c) * LANES:(n_q + c + 1) * LANES], kg_ref)
        v = acc[:, (n_q + n_kv + c) * LANES:(n_q + n_kv + c + 1) * LANES]
        if shared_kv:
            k_sw = pltpu.roll(k, HALF, 1)
            v_sw = pltpu.roll(v, HALF, 1)
            for half in range(2):
                j = 2 * c + half
                for parity in range(2):
                    ksrc = k if parity == half else k_sw
                    vsrc = v if parity == half else v_sw
                    k_ref[j, parity] = _half_only(ksrc, lane, parity).astype(BF16)
                    _put_v(v_ref, j, parity, _with_ones(vsrc, lane, parity))
        else:
            for parity in range(2):
                k_ref[c, parity] = _half_only(k, lane, parity).astype(BF16)
                _put_v(v_ref, c, parity, _with_ones(v, lane, parity))


def _kv_out(groups, transposed):
    if transposed:
        return (pl.BlockSpec((groups, 2, LANES, TM), lambda i: (0, 0, 0, i)),
                jax.ShapeDtypeStruct((groups, 2, LANES, N_TOK), BF16))
    return (pl.BlockSpec((groups, 2, TM, LANES), lambda i: (0, 0, i, 0)),
            jax.ShapeDtypeStruct((groups, 2, N_TOK, LANES), BF16))


def _qkv(x, mod, gain, w, cos, sin, qg, kg, bd, *, shared_kv, head_norm, rope, q_scale,
         x_ctx=None):
    n_out = w.shape[1]
    x_specs, x_args = _rows_of(x, x_ctx)
    groups = N_KV_HEADS if shared_kv else N_HEADS // 2
    row = lambda i: (i, 0)
    const = lambda i: (0, 0)
    kv_spec, kv_shape = _kv_out(groups, False)
    v_spec, v_shape = _kv_out(groups, True)
    return pl.pallas_call(
        functools.partial(_qkv_kernel, shared_kv=shared_kv, head_norm=head_norm, rope=rope,
                          q_scale=q_scale, split_x=x_ctx is not None),
        out_shape=(jax.ShapeDtypeStruct((N_TOK, N_HEADS * HEAD_DIM), BF16), kv_shape, v_shape),
        grid=(N_TOK // TM,),
        in_specs=x_specs + [
                  pl.BlockSpec((None, 1, 6 * D), lambda i: (_mod_row(i), 0, 0)),
                  pl.BlockSpec((1, D), const),
                  pl.BlockSpec((D, n_out), const),
                  pl.BlockSpec((TM, LANES), _rope_row),
                  pl.BlockSpec((TM, LANES), _rope_row),
                  pl.BlockSpec((1, LANES), const),
                  pl.BlockSpec((1, LANES), const),
                  pl.BlockSpec((LANES, LANES), const)],
        out_specs=(pl.BlockSpec((TM, N_HEADS * HEAD_DIM), row), kv_spec, v_spec),
        compiler_params=_params("arbitrary"),
        name="qkv",
    )(*x_args, mod, gain, w, cos, sin, qg, kg, bd)


def _mla_kernel(x_ref, mod_ref, g_ref, wd_ref, qan_ref, kvan_ref, wq_ref, wkv_ref,
                cos_ref, sin_ref, q_ref, k_ref, v_ref):
    h = _rms(x_ref[...], g_ref[...]) * (1.0 + mod_ref[:, SC_M]) + mod_ref[:, SH_M]
    lat = jnp.dot(h.astype(BF16), wd_ref[...], preferred_element_type=F32)
    c_q = _rms(lat[:, :MLA_Q_LORA], qan_ref[...])
    c_kv = _rms(lat[:, MLA_Q_LORA:MLA_Q_LORA + MLA_KV_LORA], kvan_ref[...])
    cos = cos_ref[...]
    sin = sin_ref[...]
    k_rope = _rope_chunk(lat[:, MLA_Q_LORA + MLA_KV_LORA:], cos, sin, MLA_ROPE // 2)
    q = jnp.dot(c_q.astype(BF16), wq_ref[...], preferred_element_type=F32)
    kv = jnp.dot(c_kv.astype(BF16), wkv_ref[...], preferred_element_type=F32)
    scale = (MLA_NOPE + MLA_ROPE) ** -0.5 * LOG2E
    lane = _lane((TM, LANES))
    for hd in range(N_HEADS):
        sl = slice(hd * LANES, (hd + 1) * LANES)
        q_ref[:, sl] = (_rope_chunk(q[:, sl], cos, sin, MLA_ROPE // 2) * scale).astype(BF16)
        k_ref[hd // 2, hd % 2] = (kv[:, sl] + k_rope).astype(BF16)
        v = kv[:, N_HEADS * LANES + hd * LANES:N_HEADS * LANES + (hd + 1) * LANES]
        _put_v(v_ref, hd // 2, hd % 2, _with_ones(v, lane, hd % 2))


def _mla_proj(x, mod, gain, wd, qan, kvan, wq, wkv, cos, sin):
    row = lambda i: (i, 0)
    const = lambda i: (0, 0)
    kv_spec, kv_shape = _kv_out(N_HEADS // 2, False)
    v_spec, v_shape = _kv_out(N_HEADS // 2, True)
    return pl.pallas_call(
        _mla_kernel,
        out_shape=(jax.ShapeDtypeStruct((N_TOK, N_HEADS * LANES), BF16), kv_shape, v_shape),
        grid=(N_TOK // TM,),
        in_specs=[pl.BlockSpec((TM, D), row),
                  pl.BlockSpec((None, 1, 6 * D), lambda i: (_mod_row(i), 0, 0)),
                  pl.BlockSpec((1, D), const),
                  pl.BlockSpec(wd.shape, const),
                  pl.BlockSpec((1, MLA_Q_LORA), const),
                  pl.BlockSpec((1, MLA_KV_LORA), const),
                  pl.BlockSpec(wq.shape, const),
                  pl.BlockSpec(wkv.shape, const),
                  pl.BlockSpec((TM, LANES), _rope_row),
                  pl.BlockSpec((TM, LANES), _rope_row)],
        out_specs=(pl.BlockSpec((TM, N_HEADS * LANES), row), kv_spec, v_spec),
        compiler_params=_params("arbitrary"),
        name="mla_proj",
    )(x, mod, gain, wd, qan, kvan, wq, wkv, cos, sin)


def _attn_t_kernel(*refs, stacks, tq, subs, gp, mode, with_lat):
    refs = list(refs)
    sink_ref = refs.pop(0) if mode == "window" else None
    q_ref, kc_ref, vtc_ref = refs[:3]
    refs = refs[3:]
    if with_lat:
        kl_ref, vtl_ref = refs[:2]
        refs = refs[2:]
    tab_ref = refs.pop(0) if mode == "neighbourhood" else None
    o_ref = refs[0]
    s_refs = refs[1:]
    g = pl.program_id(1)
    t = pl.program_id(2)
    qw = q_ref.shape[1] // gp
    ow = o_ref.shape[1] // gp
    cols = tq * len(stacks[0][0])
    power = jnp.exp2 if mode == "global" else jnp.exp

    chunks, masks = [], []
    for sub in range(subs):
        ch, mask = [("ctx", 0, CTX)], None
        if with_lat and mode == "global":
            ch += [("lat", c * TK, TK) for c in range(S // TK)]
        elif with_lat and mode == "window":
            span = tq + 2 * WINDOW
            q0 = (t * subs + sub) * tq
            start = pl.multiple_of(jnp.clip(q0 - WINDOW, 0, S - span), WINDOW)
            ki = start + lax.broadcasted_iota(jnp.int32, (span, cols), 0)
            qi = q0 + lax.broadcasted_iota(jnp.int32, (span, cols), 1) % tq
            mask = jnp.abs(qi - ki) <= WINDOW
            ch.append(("lat", start, span))
        elif with_lat:
            start = pl.multiple_of(_na_band_start(t * subs + sub) * GRID_W, LANES)
            ch.append(("lat", start, NA_BAND * GRID_W))
        chunks.append(ch)
        masks.append(mask)
    rows = [[sum(w for _, _, w in ch[:c]) for c in range(len(ch))] for ch in chunks]
    n_chunks = len(chunks[0])

    items = [(gi, sub, qcs, kv) for gi in range(gp) for sub in range(subs) for qcs, kv in stacks]
    qs, sinks = [], []
    for gi, sub, qcs, kv in items:
        qs.append(jnp.concatenate(
            [q_ref[sub * tq:(sub + 1) * tq, gi * qw + qc * LANES:gi * qw + (qc + 1) * LANES]
             for qc, _ in qcs], axis=0))
        sink_row = None
        if mode == "window":
            hd = [(g * gp + gi) * 2 * len(qcs) + 2 * qc + kv for qc, _ in qcs]
            col = lax.broadcasted_iota(jnp.int32, (1, cols), 1)
            sink_row = jnp.where(col < tq, sink_ref[hd[0]], sink_ref[hd[-1]])
        sinks.append(sink_row)

    def score_chunk(it, c, m8):
        gi, sub, _, kv = items[it]
        kind, st, width = chunks[sub][c]
        k = kc_ref[gi, kv] if kind == "ctx" else kl_ref[gi, kv, pl.ds(st, width), :]
        s = lax.dot_general(k, qs[it], (((1,), (1,)), ((), ())), preferred_element_type=F32)
        if kind == "lat" and tab_ref is not None:
            s = s + tab_ref[2 * gi + kv]
        if kind == "lat" and masks[sub] is not None:
            s = jnp.where(masks[sub], s, NEG)
        s_refs[it % len(s_refs)][rows[sub][c]:rows[sub][c] + width, 0:cols] = s
        c8 = jnp.max(s.reshape(width // 8, 8, cols), axis=0)
        return c8 if m8 is None else jnp.maximum(m8, c8)

    def prob_chunk(it, c, m):
        sub = items[it][1]
        width = chunks[sub][c][2]
        s = s_refs[it % len(s_refs)][rows[sub][c]:rows[sub][c] + width, 0:cols]
        return power(s - m).astype(BF16)

    def value_chunk(it, c, p, acc):
        gi, sub, _, kv = items[it]
        kind, st, width = chunks[sub][c]
        vt = vtc_ref[gi, kv] if kind == "ctx" else vtl_ref[gi, kv, :, pl.ds(st, width)]
        return acc + jnp.dot(vt, p, preferred_element_type=F32)

    n = len(items)
    ms, ps, accs = [None] * n, [[] for _ in range(n)], []
    for stage in range(n + 2):
        m8 = None
        acc = jnp.zeros((LANES, cols), F32)
        for c in range(n_chunks):
            if 0 <= stage - 1 < n:
                ps[stage - 1].append(prob_chunk(stage - 1, c, ms[stage - 1]))
            if stage < n:
                m8 = score_chunk(stage, c, m8)
            if 0 <= stage - 2 < n:
                acc = value_chunk(stage - 2, c, ps[stage - 2][c], acc)
        if stage < n:
            m = jnp.max(m8, axis=0, keepdims=True)
            ms[stage] = m if sinks[stage] is None else jnp.maximum(m, sinks[stage])
        if 0 <= stage - 2 < n:
            accs.append(acc)

    outs = {}
    for it, ((gi, sub, qcs, kv), acc) in enumerate(zip(items, accs)):
        den = acc[LANES - 1:LANES, :] if kv == 0 else acc[0:1, :]
        if sinks[it] is not None:
            den = den + power(sinks[it] - ms[it])
        o = (acc / den).T
        lane = _lane(o.shape)
        o = jnp.where(lane < HALF if kv == 0 else lane >= HALF, o, 0.0)
        for i, (_, oc) in enumerate(qcs):
            key = (sub, gi * (ow // LANES) + oc)
            part = o[i * tq:(i + 1) * tq]
            outs[key] = part if key not in outs else outs[key] + part
    for (sub, oc), o in outs.items():
        o_ref[sub * tq:(sub + 1) * tq, oc * LANES:(oc + 1) * LANES] = o.astype(o_ref.dtype)


def _attention_t(q, k4, vt4, *, stacks, mode, tq, subs=1, gp=1, sink=None, table=None,
                 ctx_queries=False):
    groups = k4.shape[0] // gp
    qw = q.shape[1] // groups
    ow = N_HEADS * HEAD_DIM // groups
    with_lat = not ctx_queries
    tqs = tq * subs
    nq = S // tqs if with_lat else CTX // tqs
    row0 = 0 if with_lat else N_LAT // tqs
    in_specs = [pl.BlockSpec((tqs, qw), lambda b, g, t: (row0 + b * nq + t, g)),
                pl.BlockSpec((gp, 2, CTX, LANES), lambda b, g, t: (g, 0, N_LAT // CTX + b, 0)),
                pl.BlockSpec((gp, 2, LANES, CTX), lambda b, g, t: (g, 0, 0, N_LAT // CTX + b))]
    args = [q, k4, vt4]
    n_keys = CTX
    if with_lat:
        in_specs += [pl.BlockSpec((gp, 2, S, LANES), lambda b, g, t: (g, 0, b, 0)),
                     pl.BlockSpec((gp, 2, LANES, S), lambda b, g, t: (g, 0, 0, b))]
        args += [k4, vt4]
        n_keys += {"global": S, "window": tq + 2 * WINDOW, "neighbourhood": NA_BAND * GRID_W}[mode]
    if mode == "window":
        in_specs = [pl.BlockSpec(memory_space=pltpu.SMEM)] + in_specs
        args = [sink] + args
    if mode == "neighbourhood":
        in_specs.append(pl.BlockSpec((None, 2 * gp, NA_BAND * GRID_W, tq),
                                     lambda b, g, t: (_na_variant(t), g, 0, 0)))
        args.append(table)
    m_rows = tq * max(len(qcs) for qcs, _ in stacks)
    return pl.pallas_call(
        functools.partial(_attn_t_kernel, stacks=stacks, tq=tq, subs=subs, gp=gp, mode=mode,
                          with_lat=with_lat),
        out_shape=jax.ShapeDtypeStruct((N_LAT if with_lat else N_CTX, N_HEADS * HEAD_DIM), BF16),
        grid=(B, groups, nq),
        in_specs=in_specs,
        out_specs=pl.BlockSpec((tqs, ow), lambda b, g, t: (b * nq + t, g)),
        scratch_shapes=[pltpu.VMEM((n_keys, m_rows), F32) for _ in range(2)],
        compiler_params=_params("arbitrary", "arbitrary", "arbitrary"),
        name="attn_" + mode + ("" if with_lat else "_ctx"),
    )(*args)


def _na_band_start(t):
    return jnp.clip(t * NA_QROWS - NA_ROWS // 2, 0, S // GRID_W - NA_BAND)


def _na_tiles():
    rows = S // GRID_W
    tiles = [(t * NA_QROWS, min(max(t * NA_QROWS - NA_ROWS // 2, 0), rows - NA_BAND))
             for t in range(rows // NA_QROWS)]
    lo = [rb for rb in tiles if rb[0] - NA_ROWS // 2 < 0]
    hi = [rb for rb in tiles if rb[0] - NA_ROWS // 2 > rows - NA_BAND]
    return lo + [tiles[len(lo)]] + hi, len(lo), len(hi)


def _na_variant(t):
    _, n_lo, n_hi = _na_tiles()
    first_hi = S // GRID_W // NA_QROWS - n_hi
    return jnp.where(t < n_lo, t, jnp.where(t >= first_hi, t - first_hi + n_lo + 1, n_lo))


def _bias_cols_kernel(rpb_ref, sel_ref, o_ref):
    o_ref[...] = jnp.dot(rpb_ref[...], sel_ref[...], preferred_element_type=F32,
                         precision=lax.Precision.HIGHEST)


def _na_table(rpb):
    rows = S // GRID_W
    n_dr, n_dc = 2 * NA_ROWS - 1, 2 * NA_COLS - 1
    c = np.arange(GRID_W)[:, None]
    kc = np.arange(GRID_W)[None, :]
    cs = np.clip(c - NA_COLS // 2, 0, GRID_W - NA_COLS)
    col_ok = (kc >= cs) & (kc < cs + NA_COLS)
    sel = (np.arange(LANES)[:, None, None] == (kc - c + NA_COLS - 1)[None]) & col_ok[None]
    sel = jnp.asarray(sel.reshape(LANES, GRID_W * GRID_W), F32)
    rpb2 = jnp.pad(rpb.reshape(N_HEADS * n_dr, n_dc), ((0, 256 - N_HEADS * n_dr), (0, LANES - n_dc)))
    cols = pl.pallas_call(
        _bias_cols_kernel,
        out_shape=jax.ShapeDtypeStruct((256, GRID_W * GRID_W), F32),
        name="na_bias_cols",
    )(rpb2, sel)
    cols = cols[:N_HEADS * n_dr].reshape(N_HEADS, n_dr, GRID_W, GRID_W)
    cols = jnp.swapaxes(jnp.where(col_ok[None, None], cols, NEG), 2, 3)
    masked = jnp.full((N_HEADS, GRID_W, GRID_W), NEG, F32)
    variants = []
    for r0, bs in _na_tiles()[0]:
        key_rows = []
        for kr in range(bs, bs + NA_BAND):
            blocks = []
            for r in range(r0, r0 + NA_QROWS):
                rs = min(max(r - NA_ROWS // 2, 0), rows - NA_ROWS)
                blocks.append(cols[:, kr - r + NA_ROWS - 1] if rs <= kr < rs + NA_ROWS else masked)
            key_rows.append(jnp.concatenate(blocks, axis=2))
        variants.append(jnp.concatenate(key_rows, axis=1))
    return jnp.stack(variants)


def _router_route(h, wr_ref):
    w = wr_ref[...]
    h_hi, w_hi = h.astype(BF16), w.astype(BF16)
    h_lo = (h - h_hi.astype(F32)).astype(BF16)
    w_lo = (w - w_hi.astype(F32)).astype(BF16)
    logits = (jnp.dot(h_hi, w_hi, preferred_element_type=F32)
              + (jnp.dot(h_hi, w_lo, preferred_element_type=F32)
                 + jnp.dot(h_lo, w_hi, preferred_element_type=F32)))
    lane = _lane(logits.shape).astype(F32)
    lg = jnp.where(lane < N_EXPERTS, logits, -jnp.inf)
    m1 = jnp.max(lg, axis=-1, keepdims=True)
    i1 = jnp.min(jnp.where(lg == m1, lane, float(LANES)), axis=-1, keepdims=True)
    lg2 = jnp.where(lane == i1, -jnp.inf, lg)
    m2 = jnp.max(lg2, axis=-1, keepdims=True)
    i2 = jnp.min(jnp.where(lg2 == m2, lane, float(LANES)), axis=-1, keepdims=True)
    e = jnp.exp(m2 - m1)
    return jnp.where(lane == 0, i1, jnp.where(lane == 1, i2, jnp.where(
        lane == 2, 1.0 / (1.0 + e), jnp.where(lane == 3, e / (1.0 + e), 0.0))))


def _oproj_kernel(*refs, moe, with_ctx, split_x):
    refs = list(refs)
    o = _pick_rows(refs.pop(0), refs.pop(0) if with_ctx else None)
    x = _pick_rows(refs.pop(0), refs.pop(0) if split_x else None)
    if moe:
        wo_ref, mod_ref, post_ref, pre_ref, wr_ref, xo_ref, h_ref, route_ref = refs
    else:
        wo_ref, mod_ref, post_ref, pre_ref, xo_ref, h_ref = refs
    y = jnp.dot(o, wo_ref[...], preferred_element_type=F32)
    x1 = x + mod_ref[:, G_M] * _rms(y, post_ref[...])
    xo_ref[...] = x1
    h = _rms(x1, pre_ref[...]) * (1.0 + mod_ref[:, SC_F]) + mod_ref[:, SH_F]
    h_ref[...] = h.astype(h_ref.dtype)
    if moe:
        route_ref[...] = _router_route(h, wr_ref)


def _oproj(o, o_ctx, wo, x, mod, post, pre, w_router, x_ctx=None):
    moe = w_router is not None
    with_ctx = o_ctx is not None
    n_rows = N_TOK if with_ctx else N_LAT
    row = lambda i: (i, 0)
    const = lambda i: (0, 0)
    o_specs, o_args = _rows_of(o, o_ctx)
    x_specs, x_args = _rows_of(x, x_ctx)
    in_specs = o_specs + x_specs + [
        pl.BlockSpec((D, D), const),
        pl.BlockSpec((None, 1, 6 * D), lambda i: (_mod_row(i), 0, 0)),
        pl.BlockSpec((1, D), const), pl.BlockSpec((1, D), const)]
    args = o_args + x_args + [wo, mod, post, pre]
    out_shape = [jax.ShapeDtypeStruct((n_rows, D), F32),
                 jax.ShapeDtypeStruct((n_rows, D), F32 if moe else BF16)]
    out_specs = [pl.BlockSpec((TM, D), row), pl.BlockSpec((TM, D), row)]
    if moe:
        in_specs.append(pl.BlockSpec((D, LANES), const))
        out_shape.append(jax.ShapeDtypeStruct((n_rows, LANES), F32))
        out_specs.append(pl.BlockSpec((TM, LANES), row))
        args.append(w_router)
    return pl.pallas_call(
        functools.partial(_oproj_kernel, moe=moe, with_ctx=with_ctx, split_x=x_ctx is not None),
        out_shape=tuple(out_shape),
        grid=(n_rows // TM,),
        in_specs=in_specs,
        out_specs=tuple(out_specs),
        compiler_params=_params("arbitrary"),
        name="oproj",
    )(*args)


def _swiglu(h, gu_ref, dw_ref):
    gu = jnp.dot(h, gu_ref[...], preferred_element_type=F32)
    half = gu.shape[1] // 2
    act = (jax.nn.silu(gu[:, :half]) * gu[:, half:]).astype(BF16)
    return jnp.dot(act, dw_ref[...], preferred_element_type=F32)


def _ffn_kernel(h_ref, gu_ref, dw_ref, x_ref, mod_ref, post_ref, o_ref):
    y = _swiglu(h_ref[...], gu_ref, dw_ref)
    o_ref[...] = x_ref[...] + mod_ref[:, G_F] * _rms(y, post_ref[...])


def _ffn(h, w_gu, w_down, x, mod, post):
    row = lambda i: (i, 0)
    const = lambda i: (0, 0)
    resident = pl.Buffered(1)
    return pl.pallas_call(
        _ffn_kernel,
        out_shape=jax.ShapeDtypeStruct((N_TOK, D), F32),
        grid=(N_TOK // TM,),
        in_specs=[pl.BlockSpec((TM, D), row),
                  pl.BlockSpec((D, 2 * D_FF), const, pipeline_mode=resident),
                  pl.BlockSpec((D_FF, D), const, pipeline_mode=resident),
                  pl.BlockSpec((TM, D), row),
                  pl.BlockSpec((None, 1, 6 * D), lambda i: (_mod_row(i), 0, 0)),
                  pl.BlockSpec((1, D), const)],
        out_specs=pl.BlockSpec((TM, D), row),
        compiler_params=_params("arbitrary"),
        name="ffn_dense",
    )(h, w_gu, w_down, x, mod, post)


def _routing(route, n_rows):
    i1 = route[:, 0].astype(jnp.int32)
    i2 = route[:, 1].astype(jnp.int32)
    e = jnp.arange(N_EXPERTS, dtype=jnp.int32)
    hit1 = i1[:, None] == e
    hit2 = i2[:, None] == e
    sel = hit1.astype(jnp.int32) + hit2.astype(jnp.int32)
    cum = jnp.cumsum(sel, axis=0)
    tiles = (cum[-1] + TME - 1) // TME
    tile_end = jnp.cumsum(tiles)
    slot = ((tile_end - tiles) * TME)[None] + cum - sel
    slots = jnp.stack([jnp.sum(jnp.where(hit1, slot, 0), axis=1),
                       jnp.sum(jnp.where(hit2, slot, 0), axis=1)], axis=1).reshape(-1)
    j = jnp.arange(_n_slot_tiles(n_rows), dtype=jnp.int32)
    tile_expert = jnp.minimum(jnp.sum(j[:, None] >= tile_end[None], axis=1), N_EXPERTS - 1)
    return slots.astype(jnp.int32), tile_expert.astype(jnp.int32), tile_end.astype(jnp.int32)


def _n_slot_tiles(n_rows):
    return 2 * n_rows // TME + N_EXPERTS


def _row_copies(slots_ref, n, copy):
    def start(r, carry):
        for k in range(2):
            copy(r, k, slots_ref[2 * r + k]).start(priority=k)
        return carry

    def wait(r, carry):
        for k in range(2):
            copy(r, k, slots_ref[2 * r + k]).wait()
        return carry

    lax.fori_loop(0, n, start, 0, unroll=8)
    lax.fori_loop(0, n, wait, 0, unroll=8)


def _dispatch_kernel(tend_ref, slots_ref, h_ref, xs_ref, zero_ref, zsem, sem, *, n_tiles):
    @pl.when(pl.program_id(0) == 0)
    def _():
        zero_ref[...] = jnp.zeros_like(zero_ref)
        tiles = []
        for e in range(N_EXPERTS):
            first = tend_ref[e - 1] if e else 0
            tiles.append((tend_ref[e] > first, tend_ref[e] - 1))
        for u in range(N_EXPERTS):
            tile = tend_ref[N_EXPERTS - 1] + u
            tiles.append((tile < n_tiles, tile))
        copies = [(ok, pltpu.make_async_copy(
            zero_ref, xs_ref.at[pl.ds(pl.multiple_of(jnp.maximum(tile, 0) * TME, TME), TME)], zsem))
            for ok, tile in tiles]
        for ok, cp in copies:
            pl.when(ok)(cp.start)
        for ok, cp in copies:
            pl.when(ok)(cp.wait)

    _row_copies(slots_ref, TMR, lambda r, k, s: pltpu.make_async_copy(
        h_ref.at[pl.ds(r, 1)], xs_ref.at[pl.ds(s, 1)], sem))


def _dispatch(tile_end, slots, h, n_rows):
    n_tiles = _n_slot_tiles(n_rows)
    return pl.pallas_call(
        functools.partial(_dispatch_kernel, n_tiles=n_tiles),
        out_shape=jax.ShapeDtypeStruct((n_tiles * TME, D), F32),
        grid_spec=pltpu.PrefetchScalarGridSpec(
            num_scalar_prefetch=1,
            grid=(n_rows // TMR,),
            in_specs=[pl.BlockSpec((2 * TMR,), lambda i, te: (i,), memory_space=pltpu.SMEM),
                      pl.BlockSpec((TMR, D), lambda i, te: (i, 0))],
            out_specs=pl.BlockSpec(memory_space=pl.ANY),
            scratch_shapes=[pltpu.VMEM((TME, D), F32), pltpu.SemaphoreType.DMA(()),
                            pltpu.SemaphoreType.DMA(())]),
        compiler_params=_params("arbitrary"),
        name="moe_dispatch",
    )(tile_end, slots, h)


def _expert_kernel(te_ref, tend_ref, x_ref, gu_ref, dw_ref, o_ref, gub_ref, db_ref):
    j = pl.program_id(0)
    valid = j < tend_ref[N_EXPERTS - 1]

    @pl.when(jnp.logical_or(j == 0, te_ref[j] != te_ref[jnp.maximum(j - 1, 0)]))
    def _():
        gub_ref[...] = gu_ref[...].astype(BF16)
        db_ref[...] = dw_ref[...].astype(BF16)

    @pl.when(valid)
    def _():
        o_ref[...] = _swiglu(x_ref[...].astype(BF16), gub_ref, db_ref)

    @pl.when(jnp.logical_not(valid))
    def _():
        o_ref[...] = jnp.zeros_like(o_ref)


def _experts(tile_expert, tile_end, xs, w_gu, w_down):
    n_tiles = xs.shape[0] // TME
    return pl.pallas_call(
        _expert_kernel,
        out_shape=jax.ShapeDtypeStruct(xs.shape, F32),
        grid_spec=pltpu.PrefetchScalarGridSpec(
            num_scalar_prefetch=2,
            grid=(n_tiles,),
            in_specs=[pl.BlockSpec((TME, D), lambda j, te, tend: (j, 0)),
                      pl.BlockSpec((None, D, 2 * D_FF_EXPERT), lambda j, te, tend: (te[j], 0, 0)),
                      pl.BlockSpec((None, D_FF_EXPERT, D), lambda j, te, tend: (te[j], 0, 0))],
            out_specs=pl.BlockSpec((TME, D), lambda j, te, tend: (j, 0)),
            scratch_shapes=[pltpu.VMEM((D, 2 * D_FF_EXPERT), BF16),
                            pltpu.VMEM((D_FF_EXPERT, D), BF16)]),
        compiler_params=_params("arbitrary"),
        name="moe_experts",
    )(tile_expert, tile_end, xs, w_gu, w_down)


def _combine_kernel(slots_ref, route_ref, x_ref, mod_ref, post_ref, ys_ref, o_ref, buf_ref, sem):
    _row_copies(slots_ref, TMR, lambda r, k, s: pltpu.make_async_copy(
        ys_ref.at[pl.ds(s, 1)], buf_ref.at[k, pl.ds(r, 1)], sem))
    y = route_ref[:, 2:3] * buf_ref[0] + route_ref[:, 3:4] * buf_ref[1]
    o_ref[...] = x_ref[...] + mod_ref[:, G_F] * _rms(y, post_ref[...])


def _combine(slots, route, x, mod, post, ys, n_rows):
    return pl.pallas_call(
        _combine_kernel,
        out_shape=jax.ShapeDtypeStruct((n_rows, D), F32),
        grid=(n_rows // TMR,),
        in_specs=[pl.BlockSpec((2 * TMR,), lambda i: (i,), memory_space=pltpu.SMEM),
                  pl.BlockSpec((TMR, LANES), lambda i: (i, 0)),
                  pl.BlockSpec((TMR, D), lambda i: (i, 0)),
                  pl.BlockSpec((None, 1, 6 * D), lambda i: (_mod_row(i * TMR // TM), 0, 0)),
                  pl.BlockSpec((1, D), lambda i: (0, 0)),
                  pl.BlockSpec(memory_space=pl.ANY)],
        out_specs=pl.BlockSpec((TMR, D), lambda i: (i, 0)),
        scratch_shapes=[pltpu.VMEM((2, TMR, D), F32), pltpu.SemaphoreType.DMA(())],
        compiler_params=_params("arbitrary"),
        name="moe_combine",
    )(slots, route, x, mod, post, ys)


def _moe(h, route, w_gu, w_down, x, mod, post, n_rows):
    slots, tile_expert, tile_end = _routing(route, n_rows)
    xs = _dispatch(tile_end, slots, h, n_rows)
    ys = _experts(tile_expert, tile_end, xs, w_gu, w_down)
    return _combine(slots, route, x, mod, post, ys, n_rows)


def _rope_tables(rot_dim, lead, tail):
    t = jnp.arange(S, dtype=jnp.int32)
    row = (t // GRID_W).astype(F32)
    col = (t % GRID_W).astype(F32)
    quarter = rot_dim // 4
    inv = ROPE_THETA ** (-jnp.arange(quarter, dtype=F32) / quarter)
    ang = jnp.concatenate([row[:, None] * inv, col[:, None] * inv], axis=-1)
    cos, sin = jnp.cos(ang), jnp.sin(ang)
    reps = (LANES - lead - tail) // rot_dim
    cos_l = jnp.concatenate([jnp.ones((S, lead), F32)] + [cos, cos] * reps + [jnp.ones((S, tail), F32)], axis=1)
    sin_l = jnp.concatenate([jnp.zeros((S, lead), F32)] + [-sin, sin] * reps + [jnp.zeros((S, tail), F32)], axis=1)
    cos_all = jnp.concatenate([cos_l, jnp.ones((TM, LANES), F32)], axis=0)
    sin_all = jnp.concatenate([sin_l, jnp.zeros((TM, LANES), F32)], axis=0)
    return cos_all, sin_all


def _rope_row(i):
    return (jnp.where(i < N_LAT // TM, i % (S // TM), S // TM), 0)


def _mla_weights(w_dkv, w_q_b, w_kv_b):
    lat = MLA_Q_LORA + MLA_KV_LORA
    pad = LANES - MLA_NOPE - MLA_ROPE
    wd = jnp.concatenate([w_dkv[:, :lat], jnp.zeros((D, MLA_NOPE), F32), w_dkv[:, lat:],
                          jnp.zeros((D, pad), F32)], axis=1)
    wq = jnp.pad(w_q_b.reshape(MLA_Q_LORA, N_HEADS, MLA_NOPE + MLA_ROPE), ((0, 0), (0, 0), (0, pad)))
    kvr = w_kv_b.reshape(MLA_KV_LORA, N_HEADS, 2 * HALF)
    k_part, v_part = kvr[..., :HALF], kvr[..., HALF:]
    zero = jnp.zeros_like(v_part)
    wk = jnp.concatenate([k_part, zero], axis=-1)
    even = (jnp.arange(N_HEADS) % 2 == 0)[None, :, None]
    wv = jnp.where(even, jnp.concatenate([v_part, zero], -1), jnp.concatenate([zero, v_part], -1))
    wkv = jnp.concatenate([wk.reshape(MLA_KV_LORA, -1), wv.reshape(MLA_KV_LORA, -1)], axis=1)
    return wd.astype(BF16), wq.reshape(MLA_Q_LORA, -1).astype(BF16), wkv.astype(BF16)


GQA_STACKS = ((((0, 0), (1, 1)), 0), (((0, 0), (1, 1)), 1))
MLA_STACKS = ((((0, 0),), 0), (((1, 0),), 1))
NA_STACKS = ((((0, 0),), 0), (((0, 0),), 1))


def kernel(x, c, ctx, c_ctx,
           l0_ada_w, l0_ada_b, l0_mix_pre, l0_mix_post, l0_ffn_pre, l0_ffn_post,
           l0_w_qkv, l0_q_norm, l0_k_norm, l0_w_o, l0_ffn_w_gu, l0_ffn_w_down,
           l1_ada_w, l1_ada_b, l1_mix_pre, l1_mix_post, l1_ffn_pre, l1_ffn_post,
           l1_w_qkv, l1_sink, l1_w_o, l1_router, l1_moe_w_gu, l1_moe_w_down,
           l2_ada_w, l2_ada_b, l2_mix_pre, l2_mix_post, l2_ffn_pre, l2_ffn_post,
           l2_w_dkv, l2_q_a_norm, l2_w_q_b, l2_kv_a_norm, l2_w_kv_b, l2_w_o, l2_ffn_w_gu, l2_ffn_w_down,
           l3_ada_w, l3_ada_b, l3_mix_pre, l3_mix_post, l3_ffn_pre, l3_ffn_post,
           l3_w_qkv, l3_rpb, l3_w_o, l3_router, l3_moe_w_gu, l3_moe_w_down):
    vec = lambda a: a.reshape(1, -1)
    tile2 = lambda a: jnp.tile(a, 2).reshape(1, LANES)
    bf = lambda a: a.astype(BF16)
    pad_router = lambda w: jnp.pad(w, ((0, 0), (0, LANES - N_EXPERTS)))

    x_lat, x_ctx = x.reshape(N_LAT, D), ctx.reshape(N_CTX, D)
    c_all = jnp.concatenate([c, c_ctx[None], jnp.zeros((8 - B - 1, D), F32)], axis=0)
    mods = [_ada(c_all, w, b).reshape(8, 1, 6 * D)
            for w, b in ((l0_ada_w, l0_ada_b), (l1_ada_w, l1_ada_b), (l2_ada_w, l2_ada_b), (l3_ada_w, l3_ada_b))]
    cos64, sin64 = _rope_tables(HEAD_DIM, 0, 0)
    cos32, sin32 = _rope_tables(MLA_ROPE, MLA_NOPE, LANES - MLA_NOPE - MLA_ROPE)
    ones = jnp.ones((1, LANES), F32)
    idx = np.arange(LANES)
    bd = jnp.asarray(idx[:, None] // HALF == idx[None, :] // HALF, BF16)

    q, k4, v4 = _qkv(x_lat, mods[0], vec(l0_mix_pre), bf(l0_w_qkv), cos64, sin64,
                     tile2(l0_q_norm), tile2(l0_k_norm), bd, shared_kv=True, head_norm=True, rope=True,
                     q_scale=HEAD_DIM ** -0.5 * LOG2E, x_ctx=x_ctx)
    o = _attention_t(q, k4, v4, stacks=GQA_STACKS, mode="global", tq=TQ, subs=4)
    oc = _attention_t(q, k4, v4, stacks=GQA_STACKS, mode="global", tq=CTX, ctx_queries=True)
    xs, h = _oproj(o, oc, bf(l0_w_o), x_lat, mods[0], vec(l0_mix_post), vec(l0_ffn_pre), None,
                   x_ctx=x_ctx)
    xs = _ffn(h, bf(l0_ffn_w_gu), bf(l0_ffn_w_down), xs, mods[0], vec(l0_ffn_post))

    q, k4, v4 = _qkv(xs, mods[1], vec(l1_mix_pre), bf(l1_w_qkv), cos64, sin64,
                     ones, ones, bd, shared_kv=True, head_norm=False, rope=True,
                     q_scale=HEAD_DIM ** -0.5)
    o = _attention_t(q, k4, v4, stacks=GQA_STACKS, mode="window", tq=TQ, subs=4, sink=l1_sink)
    oc = _attention_t(q, k4, v4, stacks=GQA_STACKS, mode="window", tq=CTX, sink=l1_sink,
                      ctx_queries=True)
    xs, h, route = _oproj(o, oc, bf(l1_w_o), xs, mods[1], vec(l1_mix_post), vec(l1_ffn_pre),
                          pad_router(l1_router))
    xs = _moe(h, route, l1_moe_w_gu, l1_moe_w_down, xs, mods[1], vec(l1_ffn_post), N_TOK)

    wd, wq, wkv = _mla_weights(l2_w_dkv, l2_w_q_b, l2_w_kv_b)
    q, k4, v4 = _mla_proj(xs, mods[2], vec(l2_mix_pre), wd, vec(l2_q_a_norm), vec(l2_kv_a_norm),
                          wq, wkv, cos32, sin32)
    o = _attention_t(q, k4, v4, stacks=MLA_STACKS, mode="global", tq=2 * TQ, subs=4)
    oc = _attention_t(q, k4, v4, stacks=MLA_STACKS, mode="global", tq=CTX, ctx_queries=True)
    xs, h = _oproj(o, oc, bf(l2_w_o), xs, mods[2], vec(l2_mix_post), vec(l2_ffn_pre), None)
    xs = _ffn(h, bf(l2_ffn_w_gu), bf(l2_ffn_w_down), xs, mods[2], vec(l2_ffn_post))

    q, k4, v4 = _qkv(xs, mods[3], vec(l3_mix_pre), bf(l3_w_qkv), cos64, sin64,
                     ones, ones, bd, shared_kv=False, head_norm=False, rope=False,
                     q_scale=HEAD_DIM ** -0.5)
    o = _attention_t(q, k4, v4, stacks=NA_STACKS, mode="neighbourhood", tq=NA_QROWS * GRID_W, gp=4,
                     table=_na_table(l3_rpb))
    xl, h, route = _oproj(o, None, bf(l3_w_o), xs, mods[3], vec(l3_mix_post), vec(l3_ffn_pre),
                          pad_router(l3_router))
    xl = _moe(h, route, l3_moe_w_gu, l3_moe_w_down, xl, mods[3], vec(l3_ffn_post), N_LAT)
    return xl.reshape(B, S, D)
```

```python
import functools

import numpy as np
import jax
import jax.numpy as jnp
from jax import lax
from jax.experimental import pallas as pl
from jax.experimental.pallas import tpu as pltpu

D = 1024
B = 4
S = 4096
CTX = 256
GRID_W = 64
N_LAT = B * S
N_CTX = B * CTX
N_TOK = N_LAT + N_CTX
HEAD_DIM = 64
N_HEADS = 16
N_KV_HEADS = 4
WINDOW = 128
MLA_Q_LORA = 384
MLA_KV_LORA = 256
MLA_NOPE = 64
MLA_ROPE = 32
NA_ROWS = 8
NA_COLS = 16
D_FF = 2816
N_EXPERTS = 8
D_FF_EXPERT = 1408
ROPE_THETA = 10000.0
EPS = 1e-6
NEG = -1e30
LOG2E = 1.4426950408889634

LANES = 128
SUBLANES = 8
HALF = LANES // 2
TM = 512
TQ = 256
TK = 512
TME = 256
TMR = 1024
ADA_TN = 1536
DMA_UNROLL = 8
NA_QROWS = 4
NA_BAND = NA_QROWS + NA_ROWS - 1
VMEM_LIMIT = 56 * 1024 * 1024

F32 = jnp.float32
BF16 = jnp.bfloat16

SH_M, SC_M, G_M, SH_F, SC_F, G_F = (slice(i * D, (i + 1) * D) for i in range(6))


def _params(*sem):
    return pltpu.CompilerParams(dimension_semantics=sem, vmem_limit_bytes=VMEM_LIMIT)


def _mod_row(i):
    r0 = i * TM
    return jnp.where(r0 >= N_LAT, B, r0 // S)


def _rms(x, gain):
    return x * lax.rsqrt(jnp.mean(x * x, axis=-1, keepdims=True) + EPS) * gain


def _lane(shape):
    return lax.broadcasted_iota(jnp.int32, shape, len(shape) - 1)


def _ada_kernel(c_ref, w_ref, b_ref, o_ref):
    a = jax.nn.silu(c_ref[...])
    o_ref[...] = jnp.dot(a, w_ref[...], preferred_element_type=F32,
                         precision=lax.Precision.HIGHEST) + b_ref[...]


def _ada(c_all, w, b):
    tn = ADA_TN
    return pl.pallas_call(
        _ada_kernel,
        out_shape=jax.ShapeDtypeStruct((SUBLANES, 6 * D), F32),
        grid=(6 * D // tn,),
        in_specs=[pl.BlockSpec((SUBLANES, D), lambda j: (0, 0)),
                  pl.BlockSpec((D, tn), lambda j: (0, j)),
                  pl.BlockSpec((1, tn), lambda j: (0, j))],
        out_specs=pl.BlockSpec((SUBLANES, tn), lambda j: (0, j)),
        compiler_params=_params("arbitrary"),
        name="ada",
    )(c_all, w, b.reshape(1, 6 * D))


def _rope_chunk(t, cos, sin, half):
    lane = _lane(t.shape)
    rot = jnp.where((lane % (2 * half)) < half,
                    pltpu.roll(t, LANES - half, 1), pltpu.roll(t, half, 1))
    return t * cos + rot * sin


def _head_rms(t, gain, bd):
    t2 = t * t
    hi = t2.astype(BF16)
    lo = (t2 - hi.astype(F32)).astype(BF16)
    ss = (jnp.dot(hi, bd, preferred_element_type=F32)
          + jnp.dot(lo, bd, preferred_element_type=F32))
    return t * lax.rsqrt(ss * (1.0 / HEAD_DIM) + EPS) * gain


def _with_ones(v, lane, parity):
    if parity == 0:
        return jnp.where(lane < HALF, v, jnp.where(lane == LANES - 1, 1.0, 0.0))
    return jnp.where(lane >= HALF, v, jnp.where(lane == 0, 1.0, 0.0))


def _half_only(k, lane, parity):
    return jnp.where(lane < HALF, k, 0.0) if parity == 0 else jnp.where(lane >= HALF, k, 0.0)


def _put_v(v_ref, j, parity, v, v_t):
    v_ref[j, parity] = (v.T if v_t else v).astype(BF16)


def _rows_of(lat, ctx):
    n_lat = N_LAT // TM
    if ctx is None:
        return [pl.BlockSpec((TM, D), lambda i: (i, 0))], [lat]
    return ([pl.BlockSpec((TM, D), lambda i: (jnp.minimum(i, n_lat - 1), 0)),
             pl.BlockSpec((TM, D), lambda i: (jnp.maximum(i - n_lat, 0), 0))], [lat, ctx])


def _pick_rows(lat_ref, ctx_ref):
    if ctx_ref is None:
        return lat_ref[...]
    return jnp.where(pl.program_id(0) < N_LAT // TM, lat_ref[...], ctx_ref[...])


def _qkv_kernel(*refs, shared_kv, head_norm, rope, v_t, q_scale, split_x):
    refs = list(refs)
    x = _pick_rows(refs.pop(0), refs.pop(0) if split_x else None)
    (mod_ref, g_ref, w_ref, cos_ref, sin_ref, qg_ref, kg_ref, bd_ref, q_ref, k_ref, v_ref) = refs
    h = _rms(x, g_ref[...]) * (1.0 + mod_ref[:, SC_M]) + mod_ref[:, SH_M]
    acc = jnp.dot(h.astype(BF16), w_ref[...], preferred_element_type=F32)
    lane = _lane((TM, LANES))
    n_q = N_HEADS * HEAD_DIM // LANES
    n_kv = (N_KV_HEADS if shared_kv else N_HEADS) * HEAD_DIM // LANES

    def finish(t, gain_ref):
        if head_norm:
            t = _head_rms(t, gain_ref[...], bd_ref[...])
        if rope:
            t = _rope_chunk(t, cos_ref[...], sin_ref[...], HEAD_DIM // 2)
        return t

    for c in range(n_q):
        t = finish(acc[:, c * LANES:(c + 1) * LANES], qg_ref)
        q_ref[:, c * LANES:(c + 1) * LANES] = (t * q_scale).astype(BF16)
    for c in range(n_kv):
        k = finish(acc[:, (n_q + c) * LANES:(n_q + c + 1) * LANES], kg_ref)
        v = acc[:, (n_q + n_kv + c) * LANES:(n_q + n_kv + c + 1) * LANES]
        if shared_kv:
            k_sw = pltpu.roll(k, HALF, 1)
            v_sw = pltpu.roll(v, HALF, 1)
            for half in range(2):
                j = 2 * c + half
                for parity in range(2):
                    ksrc = k if parity == half else k_sw
                    vsrc = v if parity == half else v_sw
                    k_ref[j, parity] = _half_only(ksrc, lane, parity).astype(BF16)
                    _put_v(v_ref, j, parity, _with_ones(vsrc, lane, parity), v_t)
        else:
            for parity in range(2):
                k_ref[c, parity] = _half_only(k, lane, parity).astype(BF16)
                _put_v(v_ref, c, parity, _with_ones(v, lane, parity), v_t)


def _kv_out(groups, transposed):
    if transposed:
        return (pl.BlockSpec((groups, 2, LANES, TM), lambda i: (0, 0, 0, i)),
                jax.ShapeDtypeStruct((groups, 2, LANES, N_TOK), BF16))
    return (pl.BlockSpec((groups, 2, TM, LANES), lambda i: (0, 0, i, 0)),
            jax.ShapeDtypeStruct((groups, 2, N_TOK, LANES), BF16))


def _qkv(x, mod, gain, w, cos, sin, qg, kg, bd, *, shared_kv, head_norm, rope, v_t, q_scale,
         x_ctx=None):
    n_out = w.shape[1]
    x_specs, x_args = _rows_of(x, x_ctx)
    groups = N_KV_HEADS if shared_kv else N_HEADS // 2
    row = lambda i: (i, 0)
    const = lambda i: (0, 0)
    kv_spec, kv_shape = _kv_out(groups, False)
    v_spec, v_shape = _kv_out(groups, v_t)
    return pl.pallas_call(
        functools.partial(_qkv_kernel, shared_kv=shared_kv, head_norm=head_norm, rope=rope,
                          v_t=v_t, q_scale=q_scale, split_x=x_ctx is not None),
        out_shape=(jax.ShapeDtypeStruct((N_TOK, N_HEADS * HEAD_DIM), BF16), kv_shape, v_shape),
        grid=(N_TOK // TM,),
        in_specs=x_specs + [
                  pl.BlockSpec((None, 1, 6 * D), lambda i: (_mod_row(i), 0, 0)),
                  pl.BlockSpec((1, D), const),
                  pl.BlockSpec((D, n_out), const),
                  pl.BlockSpec((TM, LANES), _rope_row),
                  pl.BlockSpec((TM, LANES), _rope_row),
                  pl.BlockSpec((1, LANES), const),
                  pl.BlockSpec((1, LANES), const),
                  pl.BlockSpec((LANES, LANES), const)],
        out_specs=(pl.BlockSpec((TM, N_HEADS * HEAD_DIM), row), kv_spec, v_spec),
        compiler_params=_params("arbitrary"),
        name="qkv",
    )(*x_args, mod, gain, w, cos, sin, qg, kg, bd)


def _mla_kernel(x_ref, mod_ref, g_ref, wd_ref, qan_ref, kvan_ref, wq_ref, wkv_ref,
                cos_ref, sin_ref, q_ref, k_ref, v_ref):
    h = _rms(x_ref[...], g_ref[...]) * (1.0 + mod_ref[:, SC_M]) + mod_ref[:, SH_M]
    lat = jnp.dot(h.astype(BF16), wd_ref[...], preferred_element_type=F32)
    c_q = _rms(lat[:, :MLA_Q_LORA], qan_ref[...])
    c_kv = _rms(lat[:, MLA_Q_LORA:MLA_Q_LORA + MLA_KV_LORA], kvan_ref[...])
    cos = cos_ref[...]
    sin = sin_ref[...]
    k_rope = _rope_chunk(lat[:, MLA_Q_LORA + MLA_KV_LORA:], cos, sin, MLA_ROPE // 2)
    q = jnp.dot(c_q.astype(BF16), wq_ref[...], preferred_element_type=F32)
    kv = jnp.dot(c_kv.astype(BF16), wkv_ref[...], preferred_element_type=F32)
    scale = (MLA_NOPE + MLA_ROPE) ** -0.5 * LOG2E
    lane = _lane((TM, LANES))
    for hd in range(N_HEADS):
        sl = slice(hd * LANES, (hd + 1) * LANES)
        q_ref[:, sl] = (_rope_chunk(q[:, sl], cos, sin, MLA_ROPE // 2) * scale).astype(BF16)
        k_ref[hd // 2, hd % 2] = (kv[:, sl] + k_rope).astype(BF16)
        v = kv[:, N_HEADS * LANES + hd * LANES:N_HEADS * LANES + (hd + 1) * LANES]
        _put_v(v_ref, hd // 2, hd % 2, _with_ones(v, lane, hd % 2), True)


def _mla_proj(x, mod, gain, wd, qan, kvan, wq, wkv, cos, sin):
    row = lambda i: (i, 0)
    const = lambda i: (0, 0)
    kv_spec, kv_shape = _kv_out(N_HEADS // 2, False)
    v_spec, v_shape = _kv_out(N_HEADS // 2, True)
    return pl.pallas_call(
        _mla_kernel,
        out_shape=(jax.ShapeDtypeStruct((N_TOK, N_HEADS * LANES), BF16), kv_shape, v_shape),
        grid=(N_TOK // TM,),
        in_specs=[pl.BlockSpec((TM, D), row),
                  pl.BlockSpec((None, 1, 6 * D), lambda i: (_mod_row(i), 0, 0)),
                  pl.BlockSpec((1, D), const),
                  pl.BlockSpec(wd.shape, const),
                  pl.BlockSpec((1, MLA_Q_LORA), const),
                  pl.BlockSpec((1, MLA_KV_LORA), const),
                  pl.BlockSpec(wq.shape, const),
                  pl.BlockSpec(wkv.shape, const),
                  pl.BlockSpec((TM, LANES), _rope_row),
                  pl.BlockSpec((TM, LANES), _rope_row)],
        out_specs=(pl.BlockSpec((TM, N_HEADS * LANES), row), kv_spec, v_spec),
        compiler_params=_params("arbitrary"),
        name="mla_proj",
    )(x, mod, gain, wd, qan, kvan, wq, wkv, cos, sin)


def _attn_kernel(*refs, stacks, tq, subs, gp, mode, with_lat):
    refs = list(refs)
    sink_ref = refs.pop(0) if mode == "window" else None
    q_ref, kc_ref, vc_ref = refs[:3]
    refs = refs[3:]
    if with_lat:
        kl_ref, vl_ref = refs[:2]
        refs = refs[2:]
    tab_ref = refs.pop(0) if mode == "neighbourhood" else None
    o_ref = refs[0]
    s_refs = refs[1:]
    g = pl.program_id(1)
    t = pl.program_id(2)
    qw = q_ref.shape[1] // gp
    ow = o_ref.shape[1] // gp
    m_rows = tq * len(stacks[0][0])

    chunks, masks = [], []
    for sub in range(subs):
        ch, mask = [("ctx", 0, CTX)], None
        if with_lat and mode == "window":
            span = tq + 2 * WINDOW
            q0 = (t * subs + sub) * tq
            start = pl.multiple_of(jnp.clip(q0 - WINDOW, 0, S - span), WINDOW)
            qi = q0 + lax.broadcasted_iota(jnp.int32, (m_rows, span), 0) % tq
            ki = start + lax.broadcasted_iota(jnp.int32, (m_rows, span), 1)
            mask = jnp.abs(qi - ki) <= WINDOW
            ch.append(("lat", start, span))
        elif with_lat:
            start = pl.multiple_of(_na_band_start(t * subs + sub) * GRID_W, GRID_W)
            ch.append(("lat", start, NA_BAND * GRID_W))
        chunks.append(ch)
        masks.append(mask)
    cols = [[sum(w for _, _, w in ch[:c]) for c in range(len(ch))] for ch in chunks]

    items = [(gi, sub, qcs, kv) for gi in range(gp) for sub in range(subs) for qcs, kv in stacks]
    qs, sinks = [], []
    for gi, sub, qcs, kv in items:
        qs.append(jnp.concatenate(
            [q_ref[sub * tq:(sub + 1) * tq, gi * qw + qc * LANES:gi * qw + (qc + 1) * LANES]
             for qc, _ in qcs], axis=0))
        sink_col = None
        if mode == "window":
            hd = [(g * gp + gi) * 2 * len(qcs) + 2 * qc + kv for qc, _ in qcs]
            row = lax.broadcasted_iota(jnp.int32, (m_rows, 1), 0)
            sink_col = jnp.where(row < tq, sink_ref[hd[0]], sink_ref[hd[-1]])
        sinks.append(sink_col)

    def score_stage(it):
        gi, sub, _, kv = items[it]
        s_ref = s_refs[it % len(s_refs)]
        m_fold, m_col = None, sinks[it]
        for (kind, st, width), col in zip(chunks[sub], cols[sub]):
            k = kc_ref[gi, kv] if kind == "ctx" else kl_ref[gi, kv, pl.ds(st, width), :]
            s = lax.dot_general(qs[it], k, (((1,), (1,)), ((), ())), preferred_element_type=F32)
            if kind == "lat" and tab_ref is not None:
                s = s + tab_ref[2 * gi + kv]
            if kind == "lat" and masks[sub] is not None:
                s = jnp.where(masks[sub], s, NEG)
            s_ref[0:m_rows, col:col + width] = s
            if width % LANES == 0:
                for j in range(width // LANES):
                    slab = s[:, j * LANES:(j + 1) * LANES]
                    m_fold = slab if m_fold is None else jnp.maximum(m_fold, slab)
            else:
                mc = jnp.max(s, axis=-1, keepdims=True)
                m_col = mc if m_col is None else jnp.maximum(m_col, mc)
        m = jnp.max(m_fold, axis=-1, keepdims=True)
        return m if m_col is None else jnp.maximum(m, m_col)

    def prob_stage(it, m):
        sub = items[it][1]
        s_ref = s_refs[it % len(s_refs)]
        return [jnp.exp(s_ref[0:m_rows, col:col + width] - m).astype(BF16)
                for (_, _, width), col in zip(chunks[sub], cols[sub])]

    def value_stage(it, ps, m):
        gi, sub, _, kv = items[it]
        acc = jnp.zeros((m_rows, LANES), F32)
        for (kind, st, width), p in zip(chunks[sub], ps):
            v = vc_ref[gi, kv] if kind == "ctx" else vl_ref[gi, kv, pl.ds(st, width), :]
            acc = acc + jnp.dot(p, v, preferred_element_type=F32)
        den = acc[:, LANES - 1:LANES] if kv == 0 else acc[:, 0:1]
        if sinks[it] is not None:
            den = den + jnp.exp(sinks[it] - m)
        lane = _lane(acc.shape)
        return jnp.where(lane < HALF if kv == 0 else lane >= HALF, acc / den, 0.0)

    n = len(items)
    ms, ps, outs = [None] * n, [None] * n, {}
    for stage in range(n + 2):
        if 0 <= stage - 1 < n:
            ps[stage - 1] = prob_stage(stage - 1, ms[stage - 1])
        if stage < n:
            ms[stage] = score_stage(stage)
        if 0 <= stage - 2 < n:
            gi, sub, qcs, _ = items[stage - 2]
            o = value_stage(stage - 2, ps[stage - 2], ms[stage - 2])
            for i, (_, oc) in enumerate(qcs):
                key = (sub, gi * (ow // LANES) + oc)
                part = o[i * tq:(i + 1) * tq]
                outs[key] = part if key not in outs else outs[key] + part
    for (sub, oc), o in outs.items():
        o_ref[sub * tq:(sub + 1) * tq, oc * LANES:(oc + 1) * LANES] = o.astype(o_ref.dtype)


def _attention(q, k4, v4, *, stacks, mode, tq, subs=1, gp=1, sink=None, table=None,
               ctx_queries=False):
    groups = k4.shape[0] // gp
    qw = q.shape[1] // groups
    ow = N_HEADS * HEAD_DIM // groups
    with_lat = not ctx_queries
    tqs = tq * subs
    nq = S // tqs if with_lat else CTX // tqs
    row0 = 0 if with_lat else N_LAT // tqs

    q_spec = pl.BlockSpec((tqs, qw), lambda b, g, t: (row0 + b * nq + t, g))
    ctx_spec = pl.BlockSpec((gp, 2, CTX, LANES), lambda b, g, t: (g, 0, N_LAT // CTX + b, 0))
    lat_spec = pl.BlockSpec((gp, 2, S, LANES), lambda b, g, t: (g, 0, b, 0))
    in_specs = [q_spec, ctx_spec, ctx_spec]
    args = [q, k4, v4]
    n_keys = CTX
    if with_lat:
        in_specs += [lat_spec, lat_spec]
        args += [k4, v4]
        n_keys += {"window": tq + 2 * WINDOW, "neighbourhood": NA_BAND * GRID_W}[mode]
    if mode == "window":
        in_specs = [pl.BlockSpec(memory_space=pltpu.SMEM)] + in_specs
        args = [sink] + args
    if mode == "neighbourhood":
        in_specs.append(pl.BlockSpec((None, 2 * gp, tq, NA_BAND * GRID_W),
                                     lambda b, g, t: (_na_variant(t), g, 0, 0)))
        args.append(table)
    m_rows = tq * max(len(qcs) for qcs, _ in stacks)
    return pl.pallas_call(
        functools.partial(_attn_kernel, stacks=stacks, tq=tq, subs=subs, gp=gp, mode=mode,
                          with_lat=with_lat),
        out_shape=jax.ShapeDtypeStruct((N_LAT if with_lat else N_CTX, N_HEADS * HEAD_DIM), BF16),
        grid=(B, groups, nq),
        in_specs=in_specs,
        out_specs=pl.BlockSpec((tqs, ow), lambda b, g, t: (b * nq + t, g)),
        scratch_shapes=[pltpu.VMEM((m_rows, n_keys), F32) for _ in range(2)],
        compiler_params=_params("arbitrary", "arbitrary", "arbitrary"),
        name="attn_" + mode + ("" if with_lat else "_ctx"),
    )(*args)


def _attn_t_kernel(*refs, stacks, tq, subs, with_lat):
    refs = list(refs)
    q_ref, kc_ref, vtc_ref = refs[:3]
    refs = refs[3:]
    if with_lat:
        kl_ref, vtl_ref = refs[:2]
        refs = refs[2:]
    o_ref = refs[0]
    s_refs = refs[1:]
    chunks = [("ctx", 0, CTX)]
    if with_lat:
        chunks += [("lat", c * TK, TK) for c in range(S // TK)]
    rows = [sum(w for _, _, w in chunks[:c]) for c in range(len(chunks))]
    items = [(sub, qcs, kv) for sub in range(subs) for qcs, kv in stacks]
    qs = [jnp.concatenate([q_ref[sub * tq:(sub + 1) * tq, qc * LANES:(qc + 1) * LANES]
                           for qc, _ in qcs], axis=0) for sub, qcs, _ in items]
    cols = qs[0].shape[0]

    def score_chunk(it, c, m8):
        kind, st, width = chunks[c]
        kv = items[it][2]
        k = kc_ref[kv] if kind == "ctx" else kl_ref[kv, st:st + width, :]
        s = lax.dot_general(k, qs[it], (((1,), (1,)), ((), ())), preferred_element_type=F32)
        s_refs[it % len(s_refs)][rows[c]:rows[c] + width, 0:cols] = s
        c8 = jnp.max(s.reshape(width // 8, 8, cols), axis=0)
        return c8 if m8 is None else jnp.maximum(m8, c8)

    def prob_chunk(it, c, m):
        width = chunks[c][2]
        s = s_refs[it % len(s_refs)][rows[c]:rows[c] + width, 0:cols]
        return jnp.exp2(s - m).astype(BF16)

    def value_chunk(it, c, p, acc):
        kind, st, width = chunks[c]
        kv = items[it][2]
        vt = vtc_ref[kv] if kind == "ctx" else vtl_ref[kv, :, st:st + width]
        return acc + jnp.dot(vt, p, preferred_element_type=F32)

    n = len(items)
    ms, ps, accs = [None] * n, [[] for _ in range(n)], []
    for stage in range(n + 2):
        m8 = None
        acc = jnp.zeros((LANES, cols), F32)
        for c in range(len(chunks)):
            if 0 <= stage - 1 < n:
                ps[stage - 1].append(prob_chunk(stage - 1, c, ms[stage - 1]))
            if stage < n:
                m8 = score_chunk(stage, c, m8)
            if 0 <= stage - 2 < n:
                acc = value_chunk(stage - 2, c, ps[stage - 2][c], acc)
        if stage < n:
            ms[stage] = jnp.max(m8, axis=0, keepdims=True)
        if 0 <= stage - 2 < n:
            accs.append(acc)

    outs = {}
    for (sub, qcs, kv), acc in zip(items, accs):
        den = acc[LANES - 1:LANES, :] if kv == 0 else acc[0:1, :]
        o = (acc / den).T
        lane = _lane(o.shape)
        o = jnp.where(lane < HALF if kv == 0 else lane >= HALF, o, 0.0)
        for i, (_, oc) in enumerate(qcs):
            part = o[i * tq:(i + 1) * tq]
            outs[sub, oc] = part if (sub, oc) not in outs else outs[sub, oc] + part
    for (sub, oc), o in outs.items():
        o_ref[sub * tq:(sub + 1) * tq, oc * LANES:(oc + 1) * LANES] = o.astype(o_ref.dtype)


def _attention_t(q, k4, vt4, *, stacks, tq, subs=1, ctx_queries=False):
    groups = k4.shape[0]
    qw = q.shape[1] // groups
    ow = N_HEADS * HEAD_DIM // groups
    with_lat = not ctx_queries
    tqs = tq * subs
    nq = S // tqs if with_lat else CTX // tqs
    row0 = 0 if with_lat else N_LAT // tqs
    in_specs = [pl.BlockSpec((tqs, qw), lambda b, g, t: (row0 + b * nq + t, g)),
                pl.BlockSpec((None, 2, CTX, LANES), lambda b, g, t: (g, 0, N_LAT // CTX + b, 0)),
                pl.BlockSpec((None, 2, LANES, CTX), lambda b, g, t: (g, 0, 0, N_LAT // CTX + b))]
    args = [q, k4, vt4]
    n_keys = CTX
    if with_lat:
        in_specs += [pl.BlockSpec((None, 2, S, LANES), lambda b, g, t: (g, 0, b, 0)),
                     pl.BlockSpec((None, 2, LANES, S), lambda b, g, t: (g, 0, 0, b))]
        args += [k4, vt4]
        n_keys += S
    m_rows = tq * max(len(qcs) for qcs, _ in stacks)
    return pl.pallas_call(
        functools.partial(_attn_t_kernel, stacks=stacks, tq=tq, subs=subs, with_lat=with_lat),
        out_shape=jax.ShapeDtypeStruct((N_LAT if with_lat else N_CTX, N_HEADS * HEAD_DIM), BF16),
        grid=(B, groups, nq),
        in_specs=in_specs,
        out_specs=pl.BlockSpec((tqs, ow), lambda b, g, t: (b * nq + t, g)),
        scratch_shapes=[pltpu.VMEM((n_keys, m_rows), F32) for _ in range(2)],
        compiler_params=_params("arbitrary", "arbitrary", "arbitrary"),
        name="attn_global" + ("" if with_lat else "_ctx"),
    )(*args)


def _na_band_start(t):
    return jnp.clip(t * NA_QROWS - NA_ROWS // 2, 0, S // GRID_W - NA_BAND)


def _na_tiles():
    rows = S // GRID_W
    tiles = [(t * NA_QROWS, min(max(t * NA_QROWS - NA_ROWS // 2, 0), rows - NA_BAND))
             for t in range(rows // NA_QROWS)]
    lo = [rb for rb in tiles if rb[0] - NA_ROWS // 2 < 0]
    hi = [rb for rb in tiles if rb[0] - NA_ROWS // 2 > rows - NA_BAND]
    return lo + [tiles[len(lo)]] + hi, len(lo), len(hi)


def _na_variant(t):
    _, n_lo, n_hi = _na_tiles()
    first_hi = S // GRID_W // NA_QROWS - n_hi
    return jnp.where(t < n_lo, t, jnp.where(t >= first_hi, t - first_hi + n_lo + 1, n_lo))


def _bias_cols_kernel(rpb_ref, sel_ref, o_ref):
    o_ref[...] = jnp.dot(rpb_ref[...], sel_ref[...], preferred_element_type=F32,
                         precision=lax.Precision.HIGHEST)


def _na_table(rpb):
    rows = S // GRID_W
    n_dr, n_dc = 2 * NA_ROWS - 1, 2 * NA_COLS - 1
    c = np.arange(GRID_W)[:, None]
    kc = np.arange(GRID_W)[None, :]
    cs = np.clip(c - NA_COLS // 2, 0, GRID_W - NA_COLS)
    col_ok = (kc >= cs) & (kc < cs + NA_COLS)
    sel = (np.arange(LANES)[:, None, None] == (kc - c + NA_COLS - 1)[None]) & col_ok[None]
    sel = jnp.asarray(sel.reshape(LANES, GRID_W * GRID_W), F32)
    n_rows = -(-N_HEADS * n_dr // SUBLANES) * SUBLANES
    rpb2 = jnp.pad(rpb.reshape(N_HEADS * n_dr, n_dc), ((0, n_rows - N_HEADS * n_dr), (0, LANES - n_dc)))
    cols = pl.pallas_call(
        _bias_cols_kernel,
        out_shape=jax.ShapeDtypeStruct((n_rows, GRID_W * GRID_W), F32),
        name="na_bias_cols",
    )(rpb2, sel)
    cols = cols[:N_HEADS * n_dr].reshape(N_HEADS, n_dr, GRID_W, GRID_W)
    cols = jnp.where(col_ok[None, None], cols, NEG)
    masked = jnp.full((N_HEADS, GRID_W, GRID_W), NEG, F32)
    variants = []
    for r0, bs in _na_tiles()[0]:
        q_rows = []
        for r in range(r0, r0 + NA_QROWS):
            rs = min(max(r - NA_ROWS // 2, 0), rows - NA_ROWS)
            q_rows.append(jnp.concatenate(
                [cols[:, kr - r + NA_ROWS - 1] if rs <= kr < rs + NA_ROWS else masked
                 for kr in range(bs, bs + NA_BAND)], axis=2))
        variants.append(jnp.concatenate(q_rows, axis=1))
    return jnp.stack(variants)


def _router_route(h, wr_ref):
    w = wr_ref[...]
    h_hi, w_hi = h.astype(BF16), w.astype(BF16)
    h_lo = (h - h_hi.astype(F32)).astype(BF16)
    w_lo = (w - w_hi.astype(F32)).astype(BF16)
    logits = (jnp.dot(h_hi, w_hi, preferred_element_type=F32)
              + (jnp.dot(h_hi, w_lo, preferred_element_type=F32)
                 + jnp.dot(h_lo, w_hi, preferred_element_type=F32)))
    lane = _lane(logits.shape).astype(F32)
    lg = jnp.where(lane < N_EXPERTS, logits, -jnp.inf)
    m1 = jnp.max(lg, axis=-1, keepdims=True)
    i1 = jnp.min(jnp.where(lg == m1, lane, float(LANES)), axis=-1, keepdims=True)
    lg2 = jnp.where(lane == i1, -jnp.inf, lg)
    m2 = jnp.max(lg2, axis=-1, keepdims=True)
    i2 = jnp.min(jnp.where(lg2 == m2, lane, float(LANES)), axis=-1, keepdims=True)
    e = jnp.exp(m2 - m1)
    return jnp.where(lane == 0, i1, jnp.where(lane == 1, i2, jnp.where(
        lane == 2, 1.0 / (1.0 + e), jnp.where(lane == 3, e / (1.0 + e), 0.0))))


def _oproj_kernel(*refs, moe, with_ctx, split_x):
    refs = list(refs)
    o = _pick_rows(refs.pop(0), refs.pop(0) if with_ctx else None)
    x = _pick_rows(refs.pop(0), refs.pop(0) if split_x else None)
    if moe:
        wo_ref, mod_ref, post_ref, pre_ref, wr_ref, xo_ref, h_ref, route_ref = refs
    else:
        wo_ref, mod_ref, post_ref, pre_ref, xo_ref, h_ref = refs
    y = jnp.dot(o, wo_ref[...], preferred_element_type=F32)
    x1 = x + mod_ref[:, G_M] * _rms(y, post_ref[...])
    xo_ref[...] = x1
    h = _rms(x1, pre_ref[...]) * (1.0 + mod_ref[:, SC_F]) + mod_ref[:, SH_F]
    h_ref[...] = h.astype(h_ref.dtype)
    if moe:
        route_ref[...] = _router_route(h, wr_ref)


def _oproj(o, o_ctx, wo, x, mod, post, pre, w_router, x_ctx=None):
    moe = w_router is not None
    with_ctx = o_ctx is not None
    n_rows = N_TOK if with_ctx else N_LAT
    row = lambda i: (i, 0)
    const = lambda i: (0, 0)
    o_specs, o_args = _rows_of(o, o_ctx)
    x_specs, x_args = _rows_of(x, x_ctx)
    in_specs = o_specs + x_specs + [
        pl.BlockSpec((D, D), const),
        pl.BlockSpec((None, 1, 6 * D), lambda i: (_mod_row(i), 0, 0)),
        pl.BlockSpec((1, D), const), pl.BlockSpec((1, D), const)]
    args = o_args + x_args + [wo, mod, post, pre]
    out_shape = [jax.ShapeDtypeStruct((n_rows, D), F32),
                 jax.ShapeDtypeStruct((n_rows, D), F32 if moe else BF16)]
    out_specs = [pl.BlockSpec((TM, D), row), pl.BlockSpec((TM, D), row)]
    if moe:
        in_specs.append(pl.BlockSpec((D, LANES), const))
        out_shape.append(jax.ShapeDtypeStruct((n_rows, LANES), F32))
        out_specs.append(pl.BlockSpec((TM, LANES), row))
        args.append(w_router)
    return pl.pallas_call(
        functools.partial(_oproj_kernel, moe=moe, with_ctx=with_ctx, split_x=x_ctx is not None),
        out_shape=tuple(out_shape),
        grid=(n_rows // TM,),
        in_specs=in_specs,
        out_specs=tuple(out_specs),
        compiler_params=_params("arbitrary"),
        name="oproj",
    )(*args)


def _swiglu(h, gu_ref, dw_ref):
    gu = jnp.dot(h, gu_ref[...], preferred_element_type=F32)
    half = gu.shape[1] // 2
    act = (jax.nn.silu(gu[:, :half]) * gu[:, half:]).astype(BF16)
    return jnp.dot(act, dw_ref[...], preferred_element_type=F32)


def _ffn_kernel(h_ref, gu_ref, dw_ref, x_ref, mod_ref, post_ref, o_ref):
    y = _swiglu(h_ref[...], gu_ref, dw_ref)
    o_ref[...] = x_ref[...] + mod_ref[:, G_F] * _rms(y, post_ref[...])


def _ffn(h, w_gu, w_down, x, mod, post):
    row = lambda i: (i, 0)
    const = lambda i: (0, 0)
    resident = pl.Buffered(1)
    return pl.pallas_call(
        _ffn_kernel,
        out_shape=jax.ShapeDtypeStruct((N_TOK, D), F32),
        grid=(N_TOK // TM,),
        in_specs=[pl.BlockSpec((TM, D), row),
                  pl.BlockSpec((D, 2 * D_FF), const, pipeline_mode=resident),
                  pl.BlockSpec((D_FF, D), const, pipeline_mode=resident),
                  pl.BlockSpec((TM, D), row),
                  pl.BlockSpec((None, 1, 6 * D), lambda i: (_mod_row(i), 0, 0)),
                  pl.BlockSpec((1, D), const)],
        out_specs=pl.BlockSpec((TM, D), row),
        compiler_params=_params("arbitrary"),
        name="ffn_dense",
    )(h, w_gu, w_down, x, mod, post)


def _routing(route, n_rows):
    i1 = route[:, 0].astype(jnp.int32)
    i2 = route[:, 1].astype(jnp.int32)
    e = jnp.arange(N_EXPERTS, dtype=jnp.int32)
    hit1 = i1[:, None] == e
    hit2 = i2[:, None] == e
    sel = hit1.astype(jnp.int32) + hit2.astype(jnp.int32)
    cum = jnp.cumsum(sel, axis=0)
    tiles = (cum[-1] + TME - 1) // TME
    tile_end = jnp.cumsum(tiles)
    slot = ((tile_end - tiles) * TME)[None] + cum - sel
    slots = jnp.stack([jnp.sum(jnp.where(hit1, slot, 0), axis=1),
                       jnp.sum(jnp.where(hit2, slot, 0), axis=1)], axis=1).reshape(-1)
    j = jnp.arange(_n_slot_tiles(n_rows), dtype=jnp.int32)
    tile_expert = jnp.minimum(jnp.sum(j[:, None] >= tile_end[None], axis=1), N_EXPERTS - 1)
    return slots.astype(jnp.int32), tile_expert.astype(jnp.int32), tile_end.astype(jnp.int32)


def _n_slot_tiles(n_rows):
    return 2 * n_rows // TME + N_EXPERTS


def _row_copies(slots_ref, n, copy):
    def start(r, carry):
        for k in range(2):
            copy(r, k, slots_ref[2 * r + k]).start(priority=k)
        return carry

    def wait(r, carry):
        for k in range(2):
            copy(r, k, slots_ref[2 * r + k]).wait()
        return carry

    lax.fori_loop(0, n, start, 0, unroll=DMA_UNROLL)
    lax.fori_loop(0, n, wait, 0, unroll=DMA_UNROLL)


def _dispatch_kernel(tend_ref, slots_ref, h_ref, xs_ref, zero_ref, zsem, sem, *, n_tiles):
    @pl.when(pl.program_id(0) == 0)
    def _():
        zero_ref[...] = jnp.zeros_like(zero_ref)
        tiles = []
        for e in range(N_EXPERTS):
            first = tend_ref[e - 1] if e else 0
            tiles.append((tend_ref[e] > first, tend_ref[e] - 1))
        for u in range(N_EXPERTS):
            tile = tend_ref[N_EXPERTS - 1] + u
            tiles.append((tile < n_tiles, tile))
        copies = [(ok, pltpu.make_async_copy(
            zero_ref, xs_ref.at[pl.ds(pl.multiple_of(jnp.maximum(tile, 0) * TME, TME), TME)], zsem))
            for ok, tile in tiles]
        for ok, cp in copies:
            pl.when(ok)(cp.start)
        for ok, cp in copies:
            pl.when(ok)(cp.wait)

    _row_copies(slots_ref, TMR, lambda r, k, s: pltpu.make_async_copy(
        h_ref.at[pl.ds(r, 1)], xs_ref.at[pl.ds(s, 1)], sem))


def _dispatch(tile_end, slots, h, n_rows):
    n_tiles = _n_slot_tiles(n_rows)
    return pl.pallas_call(
        functools.partial(_dispatch_kernel, n_tiles=n_tiles),
        out_shape=jax.ShapeDtypeStruct((n_tiles * TME, D), F32),
        grid_spec=pltpu.PrefetchScalarGridSpec(
            num_scalar_prefetch=1,
            grid=(n_rows // TMR,),
            in_specs=[pl.BlockSpec((2 * TMR,), lambda i, te: (i,), memory_space=pltpu.SMEM),
                      pl.BlockSpec((TMR, D), lambda i, te: (i, 0))],
            out_specs=pl.BlockSpec(memory_space=pl.ANY),
            scratch_shapes=[pltpu.VMEM((TME, D), F32), pltpu.SemaphoreType.DMA(()),
                            pltpu.SemaphoreType.DMA(())]),
        compiler_params=_params("arbitrary"),
        name="moe_dispatch",
    )(tile_end, slots, h)


def _expert_kernel(te_ref, tend_ref, x_ref, gu_ref, dw_ref, o_ref, gub_ref, db_ref):
    j = pl.program_id(0)
    valid = j < tend_ref[N_EXPERTS - 1]

    @pl.when(jnp.logical_or(j == 0, te_ref[j] != te_ref[jnp.maximum(j - 1, 0)]))
    def _():
        gub_ref[...] = gu_ref[...].astype(BF16)
        db_ref[...] = dw_ref[...].astype(BF16)

    @pl.when(valid)
    def _():
        o_ref[...] = _swiglu(x_ref[...].astype(BF16), gub_ref, db_ref)

    @pl.when(jnp.logical_not(valid))
    def _():
        o_ref[...] = jnp.zeros_like(o_ref)


def _experts(tile_expert, tile_end, xs, w_gu, w_down):
    n_tiles = xs.shape[0] // TME
    return pl.pallas_call(
        _expert_kernel,
        out_shape=jax.ShapeDtypeStruct(xs.shape, F32),
        grid_spec=pltpu.PrefetchScalarGridSpec(
            num_scalar_prefetch=2,
            grid=(n_tiles,),
            in_specs=[pl.BlockSpec((TME, D), lambda j, te, tend: (j, 0)),
                      pl.BlockSpec((None, D, 2 * D_FF_EXPERT), lambda j, te, tend: (te[j], 0, 0)),
                      pl.BlockSpec((None, D_FF_EXPERT, D), lambda j, te, tend: (te[j], 0, 0))],
            out_specs=pl.BlockSpec((TME, D), lambda j, te, tend: (j, 0)),
            scratch_shapes=[pltpu.VMEM((D, 2 * D_FF_EXPERT), BF16),
                            pltpu.VMEM((D_FF_EXPERT, D), BF16)]),
        compiler_params=_params("arbitrary"),
        name="moe_experts",
    )(tile_expert, tile_end, xs, w_gu, w_down)


def _combine_kernel(slots_ref, route_ref, x_ref, mod_ref, post_ref, ys_ref, o_ref, buf_ref, sem):
    _row_copies(slots_ref, TMR, lambda r, k, s: pltpu.make_async_copy(
        ys_ref.at[pl.ds(s, 1)], buf_ref.at[k, pl.ds(r, 1)], sem))
    y = route_ref[:, 2:3] * buf_ref[0] + route_ref[:, 3:4] * buf_ref[1]
    o_ref[...] = x_ref[...] + mod_ref[:, G_F] * _rms(y, post_ref[...])


def _combine(slots, route, x, mod, post, ys, n_rows):
    return pl.pallas_call(
        _combine_kernel,
        out_shape=jax.ShapeDtypeStruct((n_rows, D), F32),
        grid=(n_rows // TMR,),
        in_specs=[pl.BlockSpec((2 * TMR,), lambda i: (i,), memory_space=pltpu.SMEM),
                  pl.BlockSpec((TMR, LANES), lambda i: (i, 0)),
                  pl.BlockSpec((TMR, D), lambda i: (i, 0)),
                  pl.BlockSpec((None, 1, 6 * D), lambda i: (_mod_row(i * TMR // TM), 0, 0)),
                  pl.BlockSpec((1, D), lambda i: (0, 0)),
                  pl.BlockSpec(memory_space=pl.ANY)],
        out_specs=pl.BlockSpec((TMR, D), lambda i: (i, 0)),
        scratch_shapes=[pltpu.VMEM((2, TMR, D), F32), pltpu.SemaphoreType.DMA(())],
        compiler_params=_params("arbitrary"),
        name="moe_combine",
    )(slots, route, x, mod, post, ys)


def _moe(h, route, w_gu, w_down, x, mod, post, n_rows):
    slots, tile_expert, tile_end = _routing(route, n_rows)
    xs = _dispatch(tile_end, slots, h, n_rows)
    ys = _experts(tile_expert, tile_end, xs, w_gu, w_down)
    return _combine(slots, route, x, mod, post, ys, n_rows)


def _rope_tables(rot_dim, lead, tail):
    t = jnp.arange(S, dtype=jnp.int32)
    row = (t // GRID_W).astype(F32)
    col = (t % GRID_W).astype(F32)
    quarter = rot_dim // 4
    inv = ROPE_THETA ** (-jnp.arange(quarter, dtype=F32) / quarter)
    ang = jnp.concatenate([row[:, None] * inv, col[:, None] * inv], axis=-1)
    cos, sin = jnp.cos(ang), jnp.sin(ang)
    reps = (LANES - lead - tail) // rot_dim
    cos_l = jnp.concatenate([jnp.ones((S, lead), F32)] + [cos, cos] * reps + [jnp.ones((S, tail), F32)], axis=1)
    sin_l = jnp.concatenate([jnp.zeros((S, lead), F32)] + [-sin, sin] * reps + [jnp.zeros((S, tail), F32)], axis=1)
    cos_all = jnp.concatenate([cos_l, jnp.ones((TM, LANES), F32)], axis=0)
    sin_all = jnp.concatenate([sin_l, jnp.zeros((TM, LANES), F32)], axis=0)
    return cos_all, sin_all


def _rope_row(i):
    return (jnp.where(i < N_LAT // TM, i % (S // TM), S // TM), 0)


def _mla_weights(w_dkv, w_q_b, w_kv_b):
    lat = MLA_Q_LORA + MLA_KV_LORA
    pad = LANES - MLA_NOPE - MLA_ROPE
    wd = jnp.concatenate([w_dkv[:, :lat], jnp.zeros((D, MLA_NOPE), F32), w_dkv[:, lat:],
                          jnp.zeros((D, pad), F32)], axis=1)
    wq = jnp.pad(w_q_b.reshape(MLA_Q_LORA, N_HEADS, MLA_NOPE + MLA_ROPE), ((0, 0), (0, 0), (0, pad)))
    kvr = w_kv_b.reshape(MLA_KV_LORA, N_HEADS, 2 * HALF)
    k_part, v_part = kvr[..., :HALF], kvr[..., HALF:]
    zero = jnp.zeros_like(v_part)
    wk = jnp.concatenate([k_part, zero], axis=-1)
    even = (jnp.arange(N_HEADS) % 2 == 0)[None, :, None]
    wv = jnp.where(even, jnp.concatenate([v_part, zero], -1), jnp.concatenate([zero, v_part], -1))
    wkv = jnp.concatenate([wk.reshape(MLA_KV_LORA, -1), wv.reshape(MLA_KV_LORA, -1)], axis=1)
    return wd.astype(BF16), wq.reshape(MLA_Q_LORA, -1).astype(BF16), wkv.astype(BF16)


GQA_STACKS = ((((0, 0), (1, 1)), 0), (((0, 0), (1, 1)), 1))
MLA_STACKS = ((((0, 0),), 0), (((1, 0),), 1))
NA_STACKS = ((((0, 0),), 0), (((0, 0),), 1))


def kernel(x, c, ctx, c_ctx,
           l0_ada_w, l0_ada_b, l0_mix_pre, l0_mix_post, l0_ffn_pre, l0_ffn_post,
           l0_w_qkv, l0_q_norm, l0_k_norm, l0_w_o, l0_ffn_w_gu, l0_ffn_w_down,
           l1_ada_w, l1_ada_b, l1_mix_pre, l1_mix_post, l1_ffn_pre, l1_ffn_post,
           l1_w_qkv, l1_sink, l1_w_o, l1_router, l1_moe_w_gu, l1_moe_w_down,
           l2_ada_w, l2_ada_b, l2_mix_pre, l2_mix_post, l2_ffn_pre, l2_ffn_post,
           l2_w_dkv, l2_q_a_norm, l2_w_q_b, l2_kv_a_norm, l2_w_kv_b, l2_w_o, l2_ffn_w_gu, l2_ffn_w_down,
           l3_ada_w, l3_ada_b, l3_mix_pre, l3_mix_post, l3_ffn_pre, l3_ffn_post,
           l3_w_qkv, l3_rpb, l3_w_o, l3_router, l3_moe_w_gu, l3_moe_w_down):
    vec = lambda a: a.reshape(1, -1)
    tile2 = lambda a: jnp.tile(a, 2).reshape(1, LANES)
    bf = lambda a: a.astype(BF16)
    pad_router = lambda w: jnp.pad(w, ((0, 0), (0, LANES - N_EXPERTS)))

    x_lat, x_ctx = x.reshape(N_LAT, D), ctx.reshape(N_CTX, D)
    c_all = jnp.concatenate([c, c_ctx[None], jnp.zeros((SUBLANES - B - 1, D), F32)], axis=0)
    mods = [_ada(c_all, w, b).reshape(SUBLANES, 1, 6 * D)
            for w, b in ((l0_ada_w, l0_ada_b), (l1_ada_w, l1_ada_b), (l2_ada_w, l2_ada_b), (l3_ada_w, l3_ada_b))]
    cos64, sin64 = _rope_tables(HEAD_DIM, 0, 0)
    cos32, sin32 = _rope_tables(MLA_ROPE, MLA_NOPE, LANES - MLA_NOPE - MLA_ROPE)
    ones = jnp.ones((1, LANES), F32)
    idx = np.arange(LANES)
    bd = jnp.asarray(idx[:, None] // HALF == idx[None, :] // HALF, BF16)

    q, k4, v4 = _qkv(x_lat, mods[0], vec(l0_mix_pre), bf(l0_w_qkv), cos64, sin64,
                     tile2(l0_q_norm), tile2(l0_k_norm), bd, shared_kv=True, head_norm=True, rope=True,
                     v_t=True, q_scale=HEAD_DIM ** -0.5 * LOG2E, x_ctx=x_ctx)
    o = _attention_t(q, k4, v4, stacks=GQA_STACKS, tq=TQ, subs=4)
    oc = _attention_t(q, k4, v4, stacks=GQA_STACKS, tq=CTX, ctx_queries=True)
    xs, h = _oproj(o, oc, bf(l0_w_o), x_lat, mods[0], vec(l0_mix_post), vec(l0_ffn_pre), None,
                   x_ctx=x_ctx)
    xs = _ffn(h, bf(l0_ffn_w_gu), bf(l0_ffn_w_down), xs, mods[0], vec(l0_ffn_post))

    q, k4, v4 = _qkv(xs, mods[1], vec(l1_mix_pre), bf(l1_w_qkv), cos64, sin64,
                     ones, ones, bd, shared_kv=True, head_norm=False, rope=True,
                     v_t=False, q_scale=HEAD_DIM ** -0.5)
    o = _attention(q, k4, v4, stacks=GQA_STACKS, mode="window", tq=TQ, subs=4, sink=l1_sink)
    oc = _attention(q, k4, v4, stacks=GQA_STACKS, mode="window", tq=CTX, sink=l1_sink, ctx_queries=True)
    xs, h, route = _oproj(o, oc, bf(l1_w_o), xs, mods[1], vec(l1_mix_post), vec(l1_ffn_pre),
                          pad_router(l1_router))
    xs = _moe(h, route, l1_moe_w_gu, l1_moe_w_down, xs, mods[1], vec(l1_ffn_post), N_TOK)

    wd, wq, wkv = _mla_weights(l2_w_dkv, l2_w_q_b, l2_w_kv_b)
    q, k4, v4 = _mla_proj(xs, mods[2], vec(l2_mix_pre), wd, vec(l2_q_a_norm), vec(l2_kv_a_norm),
                          wq, wkv, cos32, sin32)
    o = _attention_t(q, k4, v4, stacks=MLA_STACKS, tq=2 * TQ, subs=4)
    oc = _attention_t(q, k4, v4, stacks=MLA_STACKS, tq=CTX, ctx_queries=True)
    xs, h = _oproj(o, oc, bf(l2_w_o), xs, mods[2], vec(l2_mix_post), vec(l2_ffn_pre), None)
    xs = _ffn(h, bf(l2_ffn_w_gu), bf(l2_ffn_w_down), xs, mods[2], vec(l2_ffn_post))

    q, k4, v4 = _qkv(xs, mods[3], vec(l3_mix_pre), bf(l3_w_qkv), cos64, sin64,
                     ones, ones, bd, shared_kv=False, head_norm=False, rope=False,
                     v_t=False, q_scale=HEAD_DIM ** -0.5)
    o = _attention(q, k4, v4, stacks=NA_STACKS, mode="neighbourhood", tq=NA_QROWS * GRID_W, gp=4,
                   table=_na_table(l3_rpb))
    xl, h, route = _oproj(o, None, bf(l3_w_o), xs, mods[3], vec(l3_mix_post), vec(l3_ffn_pre),
                          pad_router(l3_router))
    xl = _moe(h, route, l3_moe_w_gu, l3_moe_w_down, xl, mods[3], vec(l3_ffn_post), N_LAT)
    return xl.reshape(B, S, D)
```

```python
import functools

import numpy as np
import jax
import jax.numpy as jnp
from jax import lax
from jax.experimental import pallas as pl
from jax.experimental.pallas import tpu as pltpu

D = 1024
B = 4
S = 4096
CTX = 256
GRID_W = 64
N_LAT = B * S
N_CTX = B * CTX
N_TOK = N_LAT + N_CTX
HEAD_DIM = 64
N_HEADS = 16
N_KV_HEADS = 4
WINDOW = 128
MLA_Q_LORA = 384
MLA_KV_LORA = 256
MLA_NOPE = 64
MLA_ROPE = 32
NA_ROWS = 8
NA_COLS = 16
D_FF = 2816
N_EXPERTS = 8
D_FF_EXPERT = 1408
ROPE_THETA = 10000.0
EPS = 1e-6
NEG = -1e30
LOG2E = 1.4426950408889634

LANES = 128
SUBLANES = 8
HALF = LANES // 2
TM = 512
TQ = 256
TK = 512
TME = 256
TMR = 1024
ADA_TN = 1536
DMA_UNROLL = 8
NA_QROWS = 4
NA_BAND = NA_QROWS + NA_ROWS - 1
VMEM_LIMIT = 56 * 1024 * 1024

F32 = jnp.float32
BF16 = jnp.bfloat16

SH_M, SC_M, G_M, SH_F, SC_F, G_F = (slice(i * D, (i + 1) * D) for i in range(6))


def _params(*sem):
    return pltpu.CompilerParams(dimension_semantics=sem, vmem_limit_bytes=VMEM_LIMIT)


def _mod_row(i):
    r0 = i * TM
    return jnp.where(r0 >= N_LAT, B, r0 // S)


def _rms(x, gain):
    return x * lax.rsqrt(jnp.mean(x * x, axis=-1, keepdims=True) + EPS) * gain


def _lane(shape):
    return lax.broadcasted_iota(jnp.int32, shape, len(shape) - 1)


def _ada_kernel(c_ref, w_ref, b_ref, o_ref):
    a = jax.nn.silu(c_ref[...])
    o_ref[...] = jnp.dot(a, w_ref[...], preferred_element_type=F32,
                         precision=lax.Precision.HIGHEST) + b_ref[...]


def _ada(c_all, w, b):
    tn = ADA_TN
    return pl.pallas_call(
        _ada_kernel,
        out_shape=jax.ShapeDtypeStruct((SUBLANES, 6 * D), F32),
        grid=(6 * D // tn,),
        in_specs=[pl.BlockSpec((SUBLANES, D), lambda j: (0, 0)),
                  pl.BlockSpec((D, tn), lambda j: (0, j)),
                  pl.BlockSpec((1, tn), lambda j: (0, j))],
        out_specs=pl.BlockSpec((SUBLANES, tn), lambda j: (0, j)),
        compiler_params=_params("arbitrary"),
        name="ada",
    )(c_all, w, b.reshape(1, 6 * D))


def _rope_chunk(t, cos, sin, half):
    lane = _lane(t.shape)
    rot = jnp.where((lane % (2 * half)) < half,
                    pltpu.roll(t, LANES - half, 1), pltpu.roll(t, half, 1))
    return t * cos + rot * sin


def _head_rms(t, gain, bd):
    t2 = t * t
    hi = t2.astype(BF16)
    lo = (t2 - hi.astype(F32)).astype(BF16)
    ss = (jnp.dot(hi, bd, preferred_element_type=F32)
          + jnp.dot(lo, bd, preferred_element_type=F32))
    return t * lax.rsqrt(ss * (1.0 / HEAD_DIM) + EPS) * gain


def _with_ones(v, lane, parity):
    if parity == 0:
        return jnp.where(lane < HALF, v, jnp.where(lane == LANES - 1, 1.0, 0.0))
    return jnp.where(lane >= HALF, v, jnp.where(lane == 0, 1.0, 0.0))


def _half_only(k, lane, parity):
    return jnp.where(lane < HALF, k, 0.0) if parity == 0 else jnp.where(lane >= HALF, k, 0.0)


def _put_v(v_ref, j, parity, v, v_t):
    v_ref[j, parity] = (v.T if v_t else v).astype(BF16)


def _rows_of(lat, ctx):
    n_lat = N_LAT // TM
    if ctx is None:
        return [pl.BlockSpec((TM, D), lambda i: (i, 0))], [lat]
    return ([pl.BlockSpec((TM, D), lambda i: (jnp.minimum(i, n_lat - 1), 0)),
             pl.BlockSpec((TM, D), lambda i: (jnp.maximum(i - n_lat, 0), 0))], [lat, ctx])


def _pick_rows(lat_ref, ctx_ref):
    if ctx_ref is None:
        return lat_ref[...]
    return jnp.where(pl.program_id(0) < N_LAT // TM, lat_ref[...], ctx_ref[...])


def _qkv_kernel(*refs, shared_kv, head_norm, rope, v_t, q_scale, split_x):
    refs = list(refs)
    x = _pick_rows(refs.pop(0), refs.pop(0) if split_x else None)
    (mod_ref, g_ref, w_ref, cos_ref, sin_ref, qg_ref, kg_ref, bd_ref, q_ref, k_ref, v_ref) = refs
    h = _rms(x, g_ref[...]) * (1.0 + mod_ref[:, SC_M]) + mod_ref[:, SH_M]
    acc = jnp.dot(h.astype(BF16), w_ref[...], preferred_element_type=F32)
    lane = _lane((TM, LANES))
    n_q = N_HEADS * HEAD_DIM // LANES
    n_kv = (N_KV_HEADS if shared_kv else N_HEADS) * HEAD_DIM // LANES

    def finish(t, gain_ref):
        if head_norm:
            t = _head_rms(t, gain_ref[...], bd_ref[...])
        if rope:
            t = _rope_chunk(t, cos_ref[...], sin_ref[...], HEAD_DIM // 2)
        return t

    for c in range(n_q):
        t = finish(acc[:, c * LANES:(c + 1) * LANES], qg_ref)
        q_ref[:, c * LANES:(c + 1) * LANES] = (t * q_scale).astype(BF16)
    for c in range(n_kv):
        k = finish(acc[:, (n_q + c) * LANES:(n_q + c + 1) * LANES], kg_ref)
        v = acc[:, (n_q + n_kv + c) * LANES:(n_q + n_kv + c + 1) * LANES]
        if shared_kv:
            k_sw = pltpu.roll(k, HALF, 1)
            v_sw = pltpu.roll(v, HALF, 1)
            for half in range(2):
                j = 2 * c + half
                for parity in range(2):
                    ksrc = k if parity == half else k_sw
                    vsrc = v if parity == half else v_sw
                    k_ref[j, parity] = _half_only(ksrc, lane, parity).astype(BF16)
                    _put_v(v_ref, j, parity, _with_ones(vsrc, lane, parity), v_t)
        else:
            for parity in range(2):
                k_ref[c, parity] = _half_only(k, lane, parity).astype(BF16)
                _put_v(v_ref, c, parity, _with_ones(v, lane, parity), v_t)


def _kv_out(groups, transposed):
    if transposed:
        return (pl.BlockSpec((groups, 2, LANES, TM), lambda i: (0, 0, 0, i)),
                jax.ShapeDtypeStruct((groups, 2, LANES, N_TOK), BF16))
    return (pl.BlockSpec((groups, 2, TM, LANES), lambda i: (0, 0, i, 0)),
            jax.ShapeDtypeStruct((groups, 2, N_TOK, LANES), BF16))


def _qkv(x, mod, gain, w, cos, sin, qg, kg, bd, *, shared_kv, head_norm, rope, v_t, q_scale,
         x_ctx=None):
    n_out = w.shape[1]
    x_specs, x_args = _rows_of(x, x_ctx)
    groups = N_KV_HEADS if shared_kv else N_HEADS // 2
    row = lambda i: (i, 0)
    const = lambda i: (0, 0)
    kv_spec, kv_shape = _kv_out(groups, False)
    v_spec, v_shape = _kv_out(groups, v_t)
    return pl.pallas_call(
        functools.partial(_qkv_kernel, shared_kv=shared_kv, head_norm=head_norm, rope=rope,
                          v_t=v_t, q_scale=q_scale, split_x=x_ctx is not None),
        out_shape=(jax.ShapeDtypeStruct((N_TOK, N_HEADS * HEAD_DIM), BF16), kv_shape, v_shape),
        grid=(N_TOK // TM,),
        in_specs=x_specs + [
                  pl.BlockSpec((None, 1, 6 * D), lambda i: (_mod_row(i), 0, 0)),
                  pl.BlockSpec((1, D), const),
                  pl.BlockSpec((D, n_out), const),
                  pl.BlockSpec((TM, LANES), _rope_row),
                  pl.BlockSpec((TM, LANES), _rope_row),
                  pl.BlockSpec((1, LANES), const),
                  pl.BlockSpec((1, LANES), const),
                  pl.BlockSpec((LANES, LANES), const)],
        out_specs=(pl.BlockSpec((TM, N_HEADS * HEAD_DIM), row), kv_spec, v_spec),
        compiler_params=_params("arbitrary"),
        name="qkv",
    )(*x_args, mod, gain, w, cos, sin, qg, kg, bd)


def _mla_kernel(x_ref, mod_ref, g_ref, wd_ref, qan_ref, kvan_ref, wq_ref, wkv_ref,
                cos_ref, sin_ref, q_ref, k_ref, v_ref):
    h = _rms(x_ref[...], g_ref[...]) * (1.0 + mod_ref[:, SC_M]) + mod_ref[:, SH_M]
    lat = jnp.dot(h.astype(BF16), wd_ref[...], preferred_element_type=F32)
    c_q = _rms(lat[:, :MLA_Q_LORA], qan_ref[...])
    c_kv = _rms(lat[:, MLA_Q_LORA:MLA_Q_LORA + MLA_KV_LORA], kvan_ref[...])
    cos = cos_ref[...]
    sin = sin_ref[...]
    k_rope = _rope_chunk(lat[:, MLA_Q_LORA + MLA_KV_LORA:], cos, sin, MLA_ROPE // 2)
    q = jnp.dot(c_q.astype(BF16), wq_ref[...], preferred_element_type=F32)
    kv = jnp.dot(c_kv.astype(BF16), wkv_ref[...], preferred_element_type=F32)
    scale = (MLA_NOPE + MLA_ROPE) ** -0.5 * LOG2E
    lane = _lane((TM, LANES))
    for hd in range(N_HEADS):
        sl = slice(hd * LANES, (hd + 1) * LANES)
        q_ref[:, sl] = (_rope_chunk(q[:, sl], cos, sin, MLA_ROPE // 2) * scale).astype(BF16)
        k_ref[hd // 2, hd % 2] = (kv[:, sl] + k_rope).astype(BF16)
        v = kv[:, N_HEADS * LANES + hd * LANES:N_HEADS * LANES + (hd + 1) * LANES]
        _put_v(v_ref, hd // 2, hd % 2, _with_ones(v, lane, hd % 2), True)


def _mla_proj(x, mod, gain, wd, qan, kvan, wq, wkv, cos, sin):
    row = lambda i: (i, 0)
    const = lambda i: (0, 0)
    kv_spec, kv_shape = _kv_out(N_HEADS // 2, False)
    v_spec, v_shape = _kv_out(N_HEADS // 2, True)
    return pl.pallas_call(
        _mla_kernel,
        out_shape=(jax.ShapeDtypeStruct((N_TOK, N_HEADS * LANES), BF16), kv_shape, v_shape),
        grid=(N_TOK // TM,),
        in_specs=[pl.BlockSpec((TM, D), row),
                  pl.BlockSpec((None, 1, 6 * D), lambda i: (_mod_row(i), 0, 0)),
                  pl.BlockSpec((1, D), const),
                  pl.BlockSpec(wd.shape, const),
                  pl.BlockSpec((1, MLA_Q_LORA), const),
                  pl.BlockSpec((1, MLA_KV_LORA), const),
                  pl.BlockSpec(wq.shape, const),
                  pl.BlockSpec(wkv.shape, const),
                  pl.BlockSpec((TM, LANES), _rope_row),
                  pl.BlockSpec((TM, LANES), _rope_row)],
        out_specs=(pl.BlockSpec((TM, N_HEADS * LANES), row), kv_spec, v_spec),
        compiler_params=_params("arbitrary"),
        name="mla_proj",
    )(x, mod, gain, wd, qan, kvan, wq, wkv, cos, sin)


def _attn_kernel(*refs, stacks, tq, subs, gp, mode, with_lat):
    refs = list(refs)
    sink_ref = refs.pop(0) if mode == "window" else None
    q_ref, kc_ref, vc_ref = refs[:3]
    refs = refs[3:]
    if with_lat:
        kl_ref, vl_ref = refs[:2]
        refs = refs[2:]
    tab_ref = refs.pop(0) if mode == "neighbourhood" else None
    o_ref = refs[0]
    s_refs = refs[1:]
    g = pl.program_id(1)
    t = pl.program_id(2)
    qw = q_ref.shape[1] // gp
    ow = o_ref.shape[1] // gp
    m_rows = tq * len(stacks[0][0])

    chunks, masks = [], []
    for sub in range(subs):
        ch, mask = [("ctx", 0, CTX)], None
        if with_lat and mode == "window":
            span = tq + 2 * WINDOW
            q0 = (t * subs + sub) * tq
            start = pl.multiple_of(jnp.clip(q0 - WINDOW, 0, S - span), WINDOW)
            qi = q0 + lax.broadcasted_iota(jnp.int32, (m_rows, span), 0) % tq
            ki = start + lax.broadcasted_iota(jnp.int32, (m_rows, span), 1)
            mask = jnp.abs(qi - ki) <= WINDOW
            ch.append(("lat", start, span))
        elif with_lat:
            start = pl.multiple_of(_na_band_start(t * subs + sub) * GRID_W, GRID_W)
            ch.append(("lat", start, NA_BAND * GRID_W))
        chunks.append(ch)
        masks.append(mask)
    cols = [[sum(w for _, _, w in ch[:c]) for c in range(len(ch))] for ch in chunks]

    items = [(gi, sub, qcs, kv) for gi in range(gp) for sub in range(subs) for qcs, kv in stacks]
    qs, sinks = [], []
    for gi, sub, qcs, kv in items:
        qs.append(jnp.concatenate(
            [q_ref[sub * tq:(sub + 1) * tq, gi * qw + qc * LANES:gi * qw + (qc + 1) * LANES]
             for qc, _ in qcs], axis=0))
        sink_col = None
        if mode == "window":
            hd = [(g * gp + gi) * 2 * len(qcs) + 2 * qc + kv for qc, _ in qcs]
            row = lax.broadcasted_iota(jnp.int32, (m_rows, 1), 0)
            sink_col = jnp.where(row < tq, sink_ref[hd[0]], sink_ref[hd[-1]])
        sinks.append(sink_col)

    def score_stage(it):
        gi, sub, _, kv = items[it]
        s_ref = s_refs[it % len(s_refs)]
        m_fold, m_col = None, sinks[it]
        for (kind, st, width), col in zip(chunks[sub], cols[sub]):
            k = kc_ref[gi, kv] if kind == "ctx" else kl_ref[gi, kv, pl.ds(st, width), :]
            s = lax.dot_general(qs[it], k, (((1,), (1,)), ((), ())), preferred_element_type=F32)
            if kind == "lat" and tab_ref is not None:
                s = s + tab_ref[2 * gi + kv]
            if kind == "lat" and masks[sub] is not None:
                s = jnp.where(masks[sub], s, NEG)
            s_ref[0:m_rows, col:col + width] = s
            if width % LANES == 0:
                for j in range(width // LANES):
                    slab = s[:, j * LANES:(j + 1) * LANES]
                    m_fold = slab if m_fold is None else jnp.maximum(m_fold, slab)
            else:
                mc = jnp.max(s, axis=-1, keepdims=True)
                m_col = mc if m_col is None else jnp.maximum(m_col, mc)
        m = jnp.max(m_fold, axis=-1, keepdims=True)
        return m if m_col is None else jnp.maximum(m, m_col)

    def prob_stage(it, m):
        sub = items[it][1]
        s_ref = s_refs[it % len(s_refs)]
        return [jnp.exp(s_ref[0:m_rows, col:col + width] - m).astype(BF16)
                for (_, _, width), col in zip(chunks[sub], cols[sub])]

    def value_stage(it, ps, m):
        gi, sub, _, kv = items[it]
        acc = jnp.zeros((m_rows, LANES), F32)
        for (kind, st, width), p in zip(chunks[sub], ps):
            v = vc_ref[gi, kv] if kind == "ctx" else vl_ref[gi, kv, pl.ds(st, width), :]
            acc = acc + jnp.dot(p, v, preferred_element_type=F32)
        den = acc[:, LANES - 1:LANES] if kv == 0 else acc[:, 0:1]
        if sinks[it] is not None:
            den = den + jnp.exp(sinks[it] - m)
        lane = _lane(acc.shape)
        return jnp.where(lane < HALF if kv == 0 else lane >= HALF, acc / den, 0.0)

    n = len(items)
    ms, ps, outs = [None] * n, [None] * n, {}
    for stage in range(n + 2):
        if 0 <= stage - 1 < n:
            ps[stage - 1] = prob_stage(stage - 1, ms[stage - 1])
        if stage < n:
            ms[stage] = score_stage(stage)
        if 0 <= stage - 2 < n:
            gi, sub, qcs, _ = items[stage - 2]
            o = value_stage(stage - 2, ps[stage - 2], ms[stage - 2])
            for i, (_, oc) in enumerate(qcs):
                key = (sub, gi * (ow // LANES) + oc)
                part = o[i * tq:(i + 1) * tq]
                outs[key] = part if key not in outs else outs[key] + part
    for (sub, oc), o in outs.items():
        o_ref[sub * tq:(sub + 1) * tq, oc * LANES:(oc + 1) * LANES] = o.astype(o_ref.dtype)


def _attention(q, k4, v4, *, stacks, mode, tq, subs=1, gp=1, sink=None, table=None,
               ctx_queries=False):
    groups = k4.shape[0] // gp
    qw = q.shape[1] // groups
    ow = N_HEADS * HEAD_DIM // groups
    with_lat = not ctx_queries
    tqs = tq * subs
    nq = S // tqs if with_lat else CTX // tqs
    row0 = 0 if with_lat else N_LAT // tqs

    q_spec = pl.BlockSpec((tqs, qw), lambda b, g, t: (row0 + b * nq + t, g))
    ctx_spec = pl.BlockSpec((gp, 2, CTX, LANES), lambda b, g, t: (g, 0, N_LAT // CTX + b, 0))
    lat_spec = pl.BlockSpec((gp, 2, S, LANES), lambda b, g, t: (g, 0, b, 0))
    in_specs = [q_spec, ctx_spec, ctx_spec]
    args = [q, k4, v4]
    n_keys = CTX
    if with_lat:
        in_specs += [lat_spec, lat_spec]
        args += [k4, v4]
        n_keys += {"window": tq + 2 * WINDOW, "neighbourhood": NA_BAND * GRID_W}[mode]
    if mode == "window":
        in_specs = [pl.BlockSpec(memory_space=pltpu.SMEM)] + in_specs
        args = [sink] + args
    if mode == "neighbourhood":
        in_specs.append(pl.BlockSpec((None, 2 * gp, tq, NA_BAND * GRID_W),
                                     lambda b, g, t: (_na_variant(t), g, 0, 0)))
        args.append(table)
    m_rows = tq * max(len(qcs) for qcs, _ in stacks)
    return pl.pallas_call(
        functools.partial(_attn_kernel, stacks=stacks, tq=tq, subs=subs, gp=gp, mode=mode,
                          with_lat=with_lat),
        out_shape=jax.ShapeDtypeStruct((N_LAT if with_lat else N_CTX, N_HEADS * HEAD_DIM), BF16),
        grid=(B, groups, nq),
        in_specs=in_specs,
        out_specs=pl.BlockSpec((tqs, ow), lambda b, g, t: (b * nq + t, g)),
        scratch_shapes=[pltpu.VMEM((m_rows, n_keys), F32) for _ in range(2)],
        compiler_params=_params("arbitrary", "arbitrary", "arbitrary"),
        name="attn_" + mode + ("" if with_lat else "_ctx"),
    )(*args)


def _attn_t_kernel(*refs, stacks, tq, subs, with_lat):
    refs = list(refs)
    q_ref, kc_ref, vtc_ref = refs[:3]
    refs = refs[3:]
    if with_lat:
        kl_ref, vtl_ref = refs[:2]
        refs = refs[2:]
    o_ref = refs[0]
    s_refs = refs[1:]
    chunks = [("ctx", 0, CTX)]
    if with_lat:
        chunks += [("lat", c * TK, TK) for c in range(S // TK)]
    rows = [sum(w for _, _, w in chunks[:c]) for c in range(len(chunks))]
    items = [(sub, qcs, kv) for sub in range(subs) for qcs, kv in stacks]
    qs = [jnp.concatenate([q_ref[sub * tq:(sub + 1) * tq, qc * LANES:(qc + 1) * LANES]
                           for qc, _ in qcs], axis=0) for sub, qcs, _ in items]
    cols = qs[0].shape[0]

    def score_chunk(it, c, m8):
        kind, st, width = chunks[c]
        kv = items[it][2]
        k = kc_ref[kv] if kind == "ctx" else kl_ref[kv, st:st + width, :]
        s = lax.dot_general(k, qs[it], (((1,), (1,)), ((), ())), preferred_element_type=F32)
        s_refs[it % len(s_refs)][rows[c]:rows[c] + width, 0:cols] = s
        c8 = jnp.max(s.reshape(width // 8, 8, cols), axis=0)
        return c8 if m8 is None else jnp.maximum(m8, c8)

    def prob_chunk(it, c, m):
        width = chunks[c][2]
        s = s_refs[it % len(s_refs)][rows[c]:rows[c] + width, 0:cols]
        return jnp.exp2(s - m).astype(BF16)

    def value_chunk(it, c, p, acc):
        kind, st, width = chunks[c]
        kv = items[it][2]
        vt = vtc_ref[kv] if kind == "ctx" else vtl_ref[kv, :, st:st + width]
        return acc + jnp.dot(vt, p, preferred_element_type=F32)

    n = len(items)
    ms, ps, accs = [None] * n, [[] for _ in range(n)], []
    for stage in range(n + 2):
        m8 = None
        acc = jnp.zeros((LANES, cols), F32)
        for c in range(len(chunks)):
            if 0 <= stage - 1 < n:
                ps[stage - 1].append(prob_chunk(stage - 1, c, ms[stage - 1]))
            if stage < n:
                m8 = score_chunk(stage, c, m8)
            if 0 <= stage - 2 < n:
                acc = value_chunk(stage - 2, c, ps[stage - 2][c], acc)
        if stage < n:
            ms[stage] = jnp.max(m8, axis=0, keepdims=True)
        if 0 <= stage - 2 < n:
            accs.append(acc)

    outs = {}
    for (sub, qcs, kv), acc in zip(items, accs):
        den = acc[LANES - 1:LANES, :] if kv == 0 else acc[0:1, :]
        o = (acc / den).T
        lane = _lane(o.shape)
        o = jnp.where(lane < HALF if kv == 0 else lane >= HALF, o, 0.0)
        for i, (_, oc) in enumerate(qcs):
            part = o[i * tq:(i + 1) * tq]
            outs[sub, oc] = part if (sub, oc) not in outs else outs[sub, oc] + part
    for (sub, oc), o in outs.items():
        o_ref[sub * tq:(sub + 1) * tq, oc * LANES:(oc + 1) * LANES] = o.astype(o_ref.dtype)


def _attention_t(q, k4, vt4, *, stacks, tq, subs=1, ctx_queries=False):
    groups = k4.shape[0]
    qw = q.shape[1] // groups
    ow = N_HEADS * HEAD_DIM // groups
    with_lat = not ctx_queries
    tqs = tq * subs
    nq = S // tqs if with_lat else CTX // tqs
    row0 = 0 if with_lat else N_LAT // tqs
    in_specs = [pl.BlockSpec((tqs, qw), lambda b, g, t: (row0 + b * nq + t, g)),
                pl.BlockSpec((None, 2, CTX, LANES), lambda b, g, t: (g, 0, N_LAT // CTX + b, 0)),
                pl.BlockSpec((None, 2, LANES, CTX), lambda b, g, t: (g, 0, 0, N_LAT // CTX + b))]
    args = [q, k4, vt4]
    n_keys = CTX
    if with_lat:
        in_specs += [pl.BlockSpec((None, 2, S, LANES), lambda b, g, t: (g, 0, b, 0)),
                     pl.BlockSpec((None, 2, LANES, S), lambda b, g, t: (g, 0, 0, b))]
        args += [k4, vt4]
        n_keys += S
    m_rows = tq * max(len(qcs) for qcs, _ in stacks)
    return pl.pallas_call(
        functools.partial(_attn_t_kernel, stacks=stacks, tq=tq, subs=subs, with_lat=with_lat),
        out_shape=jax.ShapeDtypeStruct((N_LAT if with_lat else N_CTX, N_HEADS * HEAD_DIM), BF16),
        grid=(B, groups, nq),
        in_specs=in_specs,
        out_specs=pl.BlockSpec((tqs, ow), lambda b, g, t: (b * nq + t, g)),
        scratch_shapes=[pltpu.VMEM((n_keys, m_rows), F32) for _ in range(2)],
        compiler_params=_params("arbitrary", "arbitrary", "arbitrary"),
        name="attn_global" + ("" if with_lat else "_ctx"),
    )(*args)


def _na_band_start(t):
    return jnp.clip(t * NA_QROWS - NA_ROWS // 2, 0, S // GRID_W - NA_BAND)


def _na_tiles():
    rows = S // GRID_W
    tiles = [(t * NA_QROWS, min(max(t * NA_QROWS - NA_ROWS // 2, 0), rows - NA_BAND))
             for t in range(rows // NA_QROWS)]
    lo = [rb for rb in tiles if rb[0] - NA_ROWS // 2 < 0]
    hi = [rb for rb in tiles if rb[0] - NA_ROWS // 2 > rows - NA_BAND]
    return lo + [tiles[len(lo)]] + hi, len(lo), len(hi)


def _na_variant(t):
    _, n_lo, n_hi = _na_tiles()
    first_hi = S // GRID_W // NA_QROWS - n_hi
    return jnp.where(t < n_lo, t, jnp.where(t >= first_hi, t - first_hi + n_lo + 1, n_lo))


def _bias_cols_kernel(rpb_ref, sel_ref, o_ref):
    o_ref[...] = jnp.dot(rpb_ref[...], sel_ref[...], preferred_element_type=F32,
                         precision=lax.Precision.HIGHEST)


def _na_table(rpb):
    rows = S // GRID_W
    n_dr, n_dc = 2 * NA_ROWS - 1, 2 * NA_COLS - 1
    c = np.arange(GRID_W)[:, None]
    kc = np.arange(GRID_W)[None, :]
    cs = np.clip(c - NA_COLS // 2, 0, GRID_W - NA_COLS)
    col_ok = (kc >= cs) & (kc < cs + NA_COLS)
    sel = (np.arange(LANES)[:, None, None] == (kc - c + NA_COLS - 1)[None]) & col_ok[None]
    sel = jnp.asarray(sel.reshape(LANES, GRID_W * GRID_W), F32)
    n_rows = -(-N_HEADS * n_dr // SUBLANES) * SUBLANES
    rpb2 = jnp.pad(rpb.reshape(N_HEADS * n_dr, n_dc), ((0, n_rows - N_HEADS * n_dr), (0, LANES - n_dc)))
    cols = pl.pallas_call(
        _bias_cols_kernel,
        out_shape=jax.ShapeDtypeStruct((n_rows, GRID_W * GRID_W), F32),
        name="na_bias_cols",
    )(rpb2, sel)
    cols = cols[:N_HEADS * n_dr].reshape(N_HEADS, n_dr, GRID_W, GRID_W)
    cols = jnp.where(col_ok[None, None], cols, NEG)
    masked = jnp.full((N_HEADS, GRID_W, GRID_W), NEG, F32)
    variants = []
    for r0, bs in _na_tiles()[0]:
        q_rows = []
        for r in range(r0, r0 + NA_QROWS):
            rs = min(max(r - NA_ROWS // 2, 0), rows - NA_ROWS)
            q_rows.append(jnp.concatenate(
                [cols[:, kr - r + NA_ROWS - 1] if rs <= kr < rs + NA_ROWS else masked
                 for kr in range(bs, bs + NA_BAND)], axis=2))
        variants.append(jnp.concatenate(q_rows, axis=1))
    return jnp.stack(variants)


def _router_route(h, wr_ref):
    w = wr_ref[...]
    h_hi, w_hi = h.astype(BF16), w.astype(BF16)
    h_lo = (h - h_hi.astype(F32)).astype(BF16)
    w_lo = (w - w_hi.astype(F32)).astype(BF16)
    logits = (jnp.dot(h_hi, w_hi, preferred_element_type=F32)
              + (jnp.dot(h_hi, w_lo, preferred_element_type=F32)
                 + jnp.dot(h_lo, w_hi, preferred_element_type=F32)))
    lane = _lane(logits.shape).astype(F32)
    lg = jnp.where(lane < N_EXPERTS, logits, -jnp.inf)
    m1 = jnp.max(lg, axis=-1, keepdims=True)
    i1 = jnp.min(jnp.where(lg == m1, lane, float(LANES)), axis=-1, keepdims=True)
    lg2 = jnp.where(lane == i1, -jnp.inf, lg)
    m2 = jnp.max(lg2, axis=-1, keepdims=True)
    i2 = jnp.min(jnp.where(lg2 == m2, lane, float(LANES)), axis=-1, keepdims=True)
    e = jnp.exp(m2 - m1)
    return jnp.where(lane == 0, i1, jnp.where(lane == 1, i2, jnp.where(
        lane == 2, 1.0 / (1.0 + e), jnp.where(lane == 3, e / (1.0 + e), 0.0))))


def _mix_out(o, x, wo_ref, mod_ref, post_ref, pre_ref):
    y = jnp.dot(o, wo_ref[...], preferred_element_type=F32)
    x1 = x + mod_ref[:, G_M] * _rms(y, post_ref[...])
    return x1, _rms(x1, pre_ref[...]) * (1.0 + mod_ref[:, SC_F]) + mod_ref[:, SH_F]


def _swiglu(h, gu_ref, dw_ref):
    gu = jnp.dot(h, gu_ref[...], preferred_element_type=F32)
    half = gu.shape[1] // 2
    act = (jax.nn.silu(gu[:, :half]) * gu[:, half:]).astype(BF16)
    return jnp.dot(act, dw_ref[...], preferred_element_type=F32)


def _oproj_route_kernel(*refs, with_ctx):
    refs = list(refs)
    o = _pick_rows(refs.pop(0), refs.pop(0) if with_ctx else None)
    x_ref, wo_ref, mod_ref, post_ref, pre_ref, wr_ref, xo_ref, h_ref, route_ref = refs
    x1, h = _mix_out(o, x_ref[...], wo_ref, mod_ref, post_ref, pre_ref)
    xo_ref[...] = x1
    h_ref[...] = h
    route_ref[...] = _router_route(h, wr_ref)


def _oproj_route(o, o_ctx, wo, x, mod, post, pre, w_router):
    with_ctx = o_ctx is not None
    n_rows = N_TOK if with_ctx else N_LAT
    row = lambda i: (i, 0)
    const = lambda i: (0, 0)
    o_specs, o_args = _rows_of(o, o_ctx)
    return pl.pallas_call(
        functools.partial(_oproj_route_kernel, with_ctx=with_ctx),
        out_shape=(jax.ShapeDtypeStruct((n_rows, D), F32), jax.ShapeDtypeStruct((n_rows, D), F32),
                   jax.ShapeDtypeStruct((n_rows, LANES), F32)),
        grid=(n_rows // TM,),
        in_specs=o_specs + [
            pl.BlockSpec((TM, D), row),
            pl.BlockSpec((D, D), const),
            pl.BlockSpec((None, 1, 6 * D), lambda i: (_mod_row(i), 0, 0)),
            pl.BlockSpec((1, D), const), pl.BlockSpec((1, D), const),
            pl.BlockSpec((D, LANES), const)],
        out_specs=(pl.BlockSpec((TM, D), row), pl.BlockSpec((TM, D), row),
                   pl.BlockSpec((TM, LANES), row)),
        compiler_params=_params("arbitrary"),
        name="oproj_route",
    )(*o_args, x, wo, mod, post, pre, w_router)


def _oproj_ffn_kernel(*refs, split_x):
    refs = list(refs)
    o = _pick_rows(refs.pop(0), refs.pop(0))
    x = _pick_rows(refs.pop(0), refs.pop(0) if split_x else None)
    wo_ref, mod_ref, post_ref, pre_ref, gu_ref, dw_ref, fpost_ref, o_ref = refs
    x1, h = _mix_out(o, x, wo_ref, mod_ref, post_ref, pre_ref)
    y = _swiglu(h.astype(BF16), gu_ref, dw_ref)
    o_ref[...] = x1 + mod_ref[:, G_F] * _rms(y, fpost_ref[...])


def _oproj_ffn(o, o_ctx, wo, x, mod, post, pre, w_gu, w_down, fpost, x_ctx=None):
    row = lambda i: (i, 0)
    const = lambda i: (0, 0)
    resident = pl.Buffered(1)
    o_specs, o_args = _rows_of(o, o_ctx)
    x_specs, x_args = _rows_of(x, x_ctx)
    return pl.pallas_call(
        functools.partial(_oproj_ffn_kernel, split_x=x_ctx is not None),
        out_shape=jax.ShapeDtypeStruct((N_TOK, D), F32),
        grid=(N_TOK // TM,),
        in_specs=o_specs + x_specs + [
            pl.BlockSpec((D, D), const, pipeline_mode=resident),
            pl.BlockSpec((None, 1, 6 * D), lambda i: (_mod_row(i), 0, 0)),
            pl.BlockSpec((1, D), const), pl.BlockSpec((1, D), const),
            pl.BlockSpec((D, 2 * D_FF), const, pipeline_mode=resident),
            pl.BlockSpec((D_FF, D), const, pipeline_mode=resident),
            pl.BlockSpec((1, D), const)],
        out_specs=pl.BlockSpec((TM, D), row),
        compiler_params=_params("arbitrary"),
        name="oproj_ffn",
    )(*o_args, *x_args, wo, mod, post, pre, w_gu, w_down, fpost)


def _routing(route, n_rows):
    i1 = route[:, 0].astype(jnp.int32)
    i2 = route[:, 1].astype(jnp.int32)
    e = jnp.arange(N_EXPERTS, dtype=jnp.int32)
    hit1 = i1[:, None] == e
    hit2 = i2[:, None] == e
    sel = hit1.astype(jnp.int32) + hit2.astype(jnp.int32)
    cum = jnp.cumsum(sel, axis=0)
    tiles = (cum[-1] + TME - 1) // TME
    tile_end = jnp.cumsum(tiles)
    slot = ((tile_end - tiles) * TME)[None] + cum - sel
    slots = jnp.stack([jnp.sum(jnp.where(hit1, slot, 0), axis=1),
                       jnp.sum(jnp.where(hit2, slot, 0), axis=1)], axis=1).reshape(-1)
    j = jnp.arange(_n_slot_tiles(n_rows), dtype=jnp.int32)
    tile_expert = jnp.minimum(jnp.sum(j[:, None] >= tile_end[None], axis=1), N_EXPERTS - 1)
    return slots.astype(jnp.int32), tile_expert.astype(jnp.int32), tile_end.astype(jnp.int32)


def _n_slot_tiles(n_rows):
    return 2 * n_rows // TME + N_EXPERTS


def _row_copies(slots_ref, n, copy):
    def start(r, carry):
        for k in range(2):
            copy(r, k, slots_ref[2 * r + k]).start(priority=k)
        return carry

    def wait(r, carry):
        for k in range(2):
            copy(r, k, slots_ref[2 * r + k]).wait()
        return carry

    lax.fori_loop(0, n, start, 0, unroll=DMA_UNROLL)
    lax.fori_loop(0, n, wait, 0, unroll=DMA_UNROLL)


def _dispatch_kernel(tend_ref, slots_ref, h_ref, xs_ref, zero_ref, zsem, sem, *, n_tiles):
    @pl.when(pl.program_id(0) == 0)
    def _():
        zero_ref[...] = jnp.zeros_like(zero_ref)
        tiles = []
        for e in range(N_EXPERTS):
            first = tend_ref[e - 1] if e else 0
            tiles.append((tend_ref[e] > first, tend_ref[e] - 1))
        for u in range(N_EXPERTS):
            tile = tend_ref[N_EXPERTS - 1] + u
            tiles.append((tile < n_tiles, tile))
        copies = [(ok, pltpu.make_async_copy(
            zero_ref, xs_ref.at[pl.ds(pl.multiple_of(jnp.maximum(tile, 0) * TME, TME), TME)], zsem))
            for ok, tile in tiles]
        for ok, cp in copies:
            pl.when(ok)(cp.start)
        for ok, cp in copies:
            pl.when(ok)(cp.wait)

    _row_copies(slots_ref, TMR, lambda r, k, s: pltpu.make_async_copy(
        h_ref.at[pl.ds(r, 1)], xs_ref.at[pl.ds(s, 1)], sem))


def _dispatch(tile_end, slots, h, n_rows):
    n_tiles = _n_slot_tiles(n_rows)
    return pl.pallas_call(
        functools.partial(_dispatch_kernel, n_tiles=n_tiles),
        out_shape=jax.ShapeDtypeStruct((n_tiles * TME, D), F32),
        grid_spec=pltpu.PrefetchScalarGridSpec(
            num_scalar_prefetch=1,
            grid=(n_rows // TMR,),
            in_specs=[pl.BlockSpec((2 * TMR,), lambda i, te: (i,), memory_space=pltpu.SMEM),
                      pl.BlockSpec((TMR, D), lambda i, te: (i, 0))],
            out_specs=pl.BlockSpec(memory_space=pl.ANY),
            scratch_shapes=[pltpu.VMEM((TME, D), F32), pltpu.SemaphoreType.DMA(()),
                            pltpu.SemaphoreType.DMA(())]),
        compiler_params=_params("arbitrary"),
        name="moe_dispatch",
    )(tile_end, slots, h)


def _expert_kernel(te_ref, tend_ref, x_ref, gu_ref, dw_ref, o_ref, gub_ref, db_ref):
    j = pl.program_id(0)
    valid = j < tend_ref[N_EXPERTS - 1]

    @pl.when(jnp.logical_or(j == 0, te_ref[j] != te_ref[jnp.maximum(j - 1, 0)]))
    def _():
        gub_ref[...] = gu_ref[...].astype(BF16)
        db_ref[...] = dw_ref[...].astype(BF16)

    @pl.when(valid)
    def _():
        o_ref[...] = _swiglu(x_ref[...].astype(BF16), gub_ref, db_ref)

    @pl.when(jnp.logical_not(valid))
    def _():
        o_ref[...] = jnp.zeros_like(o_ref)


def _experts(tile_expert, tile_end, xs, w_gu, w_down):
    n_tiles = xs.shape[0] // TME
    return pl.pallas_call(
        _expert_kernel,
        out_shape=jax.ShapeDtypeStruct(xs.shape, F32),
        grid_spec=pltpu.PrefetchScalarGridSpec(
            num_scalar_prefetch=2,
            grid=(n_tiles,),
            in_specs=[pl.BlockSpec((TME, D), lambda j, te, tend: (j, 0)),
                      pl.BlockSpec((None, D, 2 * D_FF_EXPERT), lambda j, te, tend: (te[j], 0, 0)),
                      pl.BlockSpec((None, D_FF_EXPERT, D), lambda j, te, tend: (te[j], 0, 0))],
            out_specs=pl.BlockSpec((TME, D), lambda j, te, tend: (j, 0)),
            scratch_shapes=[pltpu.VMEM((D, 2 * D_FF_EXPERT), BF16),
                            pltpu.VMEM((D_FF_EXPERT, D), BF16)]),
        compiler_params=_params("arbitrary"),
        name="moe_experts",
    )(tile_expert, tile_end, xs, w_gu, w_down)


def _combine_kernel(slots_ref, route_ref, x_ref, mod_ref, post_ref, ys_ref, o_ref, buf_ref, sem):
    _row_copies(slots_ref, TMR, lambda r, k, s: pltpu.make_async_copy(
        ys_ref.at[pl.ds(s, 1)], buf_ref.at[k, pl.ds(r, 1)], sem))
    y = route_ref[:, 2:3] * buf_ref[0] + route_ref[:, 3:4] * buf_ref[1]
    o_ref[...] = x_ref[...] + mod_ref[:, G_F] * _rms(y, post_ref[...])


def _combine(slots, route, x, mod, post, ys, n_rows):
    return pl.pallas_call(
        _combine_kernel,
        out_shape=jax.ShapeDtypeStruct((n_rows, D), F32),
        grid=(n_rows // TMR,),
        in_specs=[pl.BlockSpec((2 * TMR,), lambda i: (i,), memory_space=pltpu.SMEM),
                  pl.BlockSpec((TMR, LANES), lambda i: (i, 0)),
                  pl.BlockSpec((TMR, D), lambda i: (i, 0)),
                  pl.BlockSpec((None, 1, 6 * D), lambda i: (_mod_row(i * TMR // TM), 0, 0)),
                  pl.BlockSpec((1, D), lambda i: (0, 0)),
                  pl.BlockSpec(memory_space=pl.ANY)],
        out_specs=pl.BlockSpec((TMR, D), lambda i: (i, 0)),
        scratch_shapes=[pltpu.VMEM((2, TMR, D), F32), pltpu.SemaphoreType.DMA(())],
        compiler_params=_params("arbitrary"),
        name="moe_combine",
    )(slots, route, x, mod, post, ys)


def _moe(h, route, w_gu, w_down, x, mod, post, n_rows):
    slots, tile_expert, tile_end = _routing(route, n_rows)
    xs = _dispatch(tile_end, slots, h, n_rows)
    ys = _experts(tile_expert, tile_end, xs, w_gu, w_down)
    return _combine(slots, route, x, mod, post, ys, n_rows)


def _rope_tables(rot_dim, lead, tail):
    t = jnp.arange(S, dtype=jnp.int32)
    row = (t // GRID_W).astype(F32)
    col = (t % GRID_W).astype(F32)
    quarter = rot_dim // 4
    inv = ROPE_THETA ** (-jnp.arange(quarter, dtype=F32) / quarter)
    ang = jnp.concatenate([row[:, None] * inv, col[:, None] * inv], axis=-1)
    cos, sin = jnp.cos(ang), jnp.sin(ang)
    reps = (LANES - lead - tail) // rot_dim
    cos_l = jnp.concatenate([jnp.ones((S, lead), F32)] + [cos, cos] * reps + [jnp.ones((S, tail), F32)], axis=1)
    sin_l = jnp.concatenate([jnp.zeros((S, lead), F32)] + [-sin, sin] * reps + [jnp.zeros((S, tail), F32)], axis=1)
    cos_all = jnp.concatenate([cos_l, jnp.ones((TM, LANES), F32)], axis=0)
    sin_all = jnp.concatenate([sin_l, jnp.zeros((TM, LANES), F32)], axis=0)
    return cos_all, sin_all


def _rope_row(i):
    return (jnp.where(i < N_LAT // TM, i % (S // TM), S // TM), 0)


def _mla_weights(w_dkv, w_q_b, w_kv_b):
    lat = MLA_Q_LORA + MLA_KV_LORA
    pad = LANES - MLA_NOPE - MLA_ROPE
    wd = jnp.concatenate([w_dkv[:, :lat], jnp.zeros((D, MLA_NOPE), F32), w_dkv[:, lat:],
                          jnp.zeros((D, pad), F32)], axis=1)
    wq = jnp.pad(w_q_b.reshape(MLA_Q_LORA, N_HEADS, MLA_NOPE + MLA_ROPE), ((0, 0), (0, 0), (0, pad)))
    kvr = w_kv_b.reshape(MLA_KV_LORA, N_HEADS, 2 * HALF)
    k_part, v_part = kvr[..., :HALF], kvr[..., HALF:]
    zero = jnp.zeros_like(v_part)
    wk = jnp.concatenate([k_part, zero], axis=-1)
    even = (jnp.arange(N_HEADS) % 2 == 0)[None, :, None]
    wv = jnp.where(even, jnp.concatenate([v_part, zero], -1), jnp.concatenate([zero, v_part], -1))
    wkv = jnp.concatenate([wk.reshape(MLA_KV_LORA, -1), wv.reshape(MLA_KV_LORA, -1)], axis=1)
    return wd.astype(BF16), wq.reshape(MLA_Q_LORA, -1).astype(BF16), wkv.astype(BF16)


GQA_STACKS = ((((0, 0), (1, 1)), 0), (((0, 0), (1, 1)), 1))
MLA_STACKS = ((((0, 0),), 0), (((1, 0),), 1))
NA_STACKS = ((((0, 0),), 0), (((0, 0),), 1))


def kernel(x, c, ctx, c_ctx,
           l0_ada_w, l0_ada_b, l0_mix_pre, l0_mix_post, l0_ffn_pre, l0_ffn_post,
           l0_w_qkv, l0_q_norm, l0_k_norm, l0_w_o, l0_ffn_w_gu, l0_ffn_w_down,
           l1_ada_w, l1_ada_b, l1_mix_pre, l1_mix_post, l1_ffn_pre, l1_ffn_post,
           l1_w_qkv, l1_sink, l1_w_o, l1_router, l1_moe_w_gu, l1_moe_w_down,
           l2_ada_w, l2_ada_b, l2_mix_pre, l2_mix_post, l2_ffn_pre, l2_ffn_post,
           l2_w_dkv, l2_q_a_norm, l2_w_q_b, l2_kv_a_norm, l2_w_kv_b, l2_w_o, l2_ffn_w_gu, l2_ffn_w_down,
           l3_ada_w, l3_ada_b, l3_mix_pre, l3_mix_post, l3_ffn_pre, l3_ffn_post,
           l3_w_qkv, l3_rpb, l3_w_o, l3_router, l3_moe_w_gu, l3_moe_w_down):
    vec = lambda a: a.reshape(1, -1)
    tile2 = lambda a: jnp.tile(a, 2).reshape(1, LANES)
    bf = lambda a: a.astype(BF16)
    pad_router = lambda w: jnp.pad(w, ((0, 0), (0, LANES - N_EXPERTS)))

    x_lat, x_ctx = x.reshape(N_LAT, D), ctx.reshape(N_CTX, D)
    c_all = jnp.concatenate([c, c_ctx[None], jnp.zeros((SUBLANES - B - 1, D), F32)], axis=0)
    mods = [_ada(c_all, w, b).reshape(SUBLANES, 1, 6 * D)
            for w, b in ((l0_ada_w, l0_ada_b), (l1_ada_w, l1_ada_b), (l2_ada_w, l2_ada_b), (l3_ada_w, l3_ada_b))]
    cos64, sin64 = _rope_tables(HEAD_DIM, 0, 0)
    cos32, sin32 = _rope_tables(MLA_ROPE, MLA_NOPE, LANES - MLA_NOPE - MLA_ROPE)
    ones = jnp.ones((1, LANES), F32)
    idx = np.arange(LANES)
    bd = jnp.asarray(idx[:, None] // HALF == idx[None, :] // HALF, BF16)

    q, k4, v4 = _qkv(x_lat, mods[0], vec(l0_mix_pre), bf(l0_w_qkv), cos64, sin64,
                     tile2(l0_q_norm), tile2(l0_k_norm), bd, shared_kv=True, head_norm=True, rope=True,
                     v_t=True, q_scale=HEAD_DIM ** -0.5 * LOG2E, x_ctx=x_ctx)
    o = _attention_t(q, k4, v4, stacks=GQA_STACKS, tq=TQ, subs=4)
    oc = _attention_t(q, k4, v4, stacks=GQA_STACKS, tq=CTX, ctx_queries=True)
    xs = _oproj_ffn(o, oc, bf(l0_w_o), x_lat, mods[0], vec(l0_mix_post), vec(l0_ffn_pre),
                    bf(l0_ffn_w_gu), bf(l0_ffn_w_down), vec(l0_ffn_post), x_ctx=x_ctx)

    q, k4, v4 = _qkv(xs, mods[1], vec(l1_mix_pre), bf(l1_w_qkv), cos64, sin64,
                     ones, ones, bd, shared_kv=True, head_norm=False, rope=True,
                     v_t=False, q_scale=HEAD_DIM ** -0.5)
    o = _attention(q, k4, v4, stacks=GQA_STACKS, mode="window", tq=TQ, subs=4, sink=l1_sink)
    oc = _attention(q, k4, v4, stacks=GQA_STACKS, mode="window", tq=CTX, sink=l1_sink, ctx_queries=True)
    xs, h, route = _oproj_route(o, oc, bf(l1_w_o), xs, mods[1], vec(l1_mix_post), vec(l1_ffn_pre),
                                pad_router(l1_router))
    xs = _moe(h, route, l1_moe_w_gu, l1_moe_w_down, xs, mods[1], vec(l1_ffn_post), N_TOK)

    wd, wq, wkv = _mla_weights(l2_w_dkv, l2_w_q_b, l2_w_kv_b)
    q, k4, v4 = _mla_proj(xs, mods[2], vec(l2_mix_pre), wd, vec(l2_q_a_norm), vec(l2_kv_a_norm),
                          wq, wkv, cos32, sin32)
    o = _attention_t(q, k4, v4, stacks=MLA_STACKS, tq=2 * TQ, subs=4)
    oc = _attention_t(q, k4, v4, stacks=MLA_STACKS, tq=CTX, ctx_queries=True)
    xs = _oproj_ffn(o, oc, bf(l2_w_o), xs, mods[2], vec(l2_mix_post), vec(l2_ffn_pre),
                    bf(l2_ffn_w_gu), bf(l2_ffn_w_down), vec(l2_ffn_post))

    q, k4, v4 = _qkv(xs, mods[3], vec(l3_mix_pre), bf(l3_w_qkv), cos64, sin64,
                     ones, ones, bd, shared_kv=False, head_norm=False, rope=False,
                     v_t=False, q_scale=HEAD_DIM ** -0.5)
    o = _attention(q, k4, v4, stacks=NA_STACKS, mode="neighbourhood", tq=NA_QROWS * GRID_W, gp=4,
                   table=_na_table(l3_rpb))
    xl, h, route = _oproj_route(o, None, bf(l3_w_o), xs, mods[3], vec(l3_mix_post), vec(l3_ffn_pre),
                                pad_router(l3_router))
    xl = _moe(h, route, l3_moe_w_gu, l3_moe_w_down, xl, mods[3], vec(l3_ffn_post), N_LAT)
    return xl.reshape(B, S, D)
```

```python
import functools

import numpy as np
import jax
import jax.numpy as jnp
from jax import lax
from jax.experimental import pallas as pl
from jax.experimental.pallas import tpu as pltpu

D = 1024
B = 4
S = 4096
CTX = 256
GRID_W = 64
N_LAT = B * S
N_CTX = B * CTX
N_TOK = N_LAT + N_CTX
HEAD_DIM = 64
N_HEADS = 16
N_KV_HEADS = 4
WINDOW = 128
MLA_Q_LORA = 384
MLA_KV_LORA = 256
MLA_NOPE = 64
MLA_ROPE = 32
NA_ROWS = 8
NA_COLS = 16
D_FF = 2816
N_EXPERTS = 8
D_FF_EXPERT = 1408
ROPE_THETA = 10000.0
EPS = 1e-6
NEG = -1e30
LOG2E = 1.4426950408889634

LANES = 128
SUBLANES = 8
HALF = LANES // 2
TM = 512
TQ = 256
TK = 512
TME = 256
TMR = 1024
ADA_TN = 1536
DMA_UNROLL = 8
NA_QROWS = 4
NA_BAND = NA_QROWS + NA_ROWS - 1
VMEM_LIMIT = 56 * 1024 * 1024

F32 = jnp.float32
BF16 = jnp.bfloat16

SH_M, SC_M, G_M, SH_F, SC_F, G_F = (slice(i * D, (i + 1) * D) for i in range(6))


def _params(*sem):
    return pltpu.CompilerParams(dimension_semantics=sem, vmem_limit_bytes=VMEM_LIMIT)


def _mod_row(i):
    r0 = i * TM
    return jnp.where(r0 >= N_LAT, B, r0 // S)


def _rms(x, gain):
    return x * lax.rsqrt(jnp.mean(x * x, axis=-1, keepdims=True) + EPS) * gain


def _lane(shape):
    return lax.broadcasted_iota(jnp.int32, shape, len(shape) - 1)


def _ada_kernel(c_ref, w_ref, b_ref, o_ref):
    a = jax.nn.silu(c_ref[...])
    o_ref[...] = jnp.dot(a, w_ref[...], preferred_element_type=F32,
                         precision=lax.Precision.HIGHEST) + b_ref[...]


def _ada(c_all, w, b):
    tn = ADA_TN
    return pl.pallas_call(
        _ada_kernel,
        out_shape=jax.ShapeDtypeStruct((SUBLANES, 6 * D), F32),
        grid=(6 * D // tn,),
        in_specs=[pl.BlockSpec((SUBLANES, D), lambda j: (0, 0)),
                  pl.BlockSpec((D, tn), lambda j: (0, j)),
                  pl.BlockSpec((1, tn), lambda j: (0, j))],
        out_specs=pl.BlockSpec((SUBLANES, tn), lambda j: (0, j)),
        compiler_params=_params("arbitrary"),
        name="ada",
    )(c_all, w, b.reshape(1, 6 * D))


def _rope_chunk(t, cos, sin, half):
    lane = _lane(t.shape)
    rot = jnp.where((lane % (2 * half)) < half,
                    pltpu.roll(t, LANES - half, 1), pltpu.roll(t, half, 1))
    return t * cos + rot * sin


def _head_rms(t, gain, bd):
    t2 = t * t
    hi = t2.astype(BF16)
    lo = (t2 - hi.astype(F32)).astype(BF16)
    ss = (jnp.dot(hi, bd, preferred_element_type=F32)
          + jnp.dot(lo, bd, preferred_element_type=F32))
    return t * lax.rsqrt(ss * (1.0 / HEAD_DIM) + EPS) * gain


def _with_ones(v, lane, parity):
    if parity == 0:
        return jnp.where(lane < HALF, v, jnp.where(lane == LANES - 1, 1.0, 0.0))
    return jnp.where(lane >= HALF, v, jnp.where(lane == 0, 1.0, 0.0))


def _half_only(k, lane, parity):
    return jnp.where(lane < HALF, k, 0.0) if parity == 0 else jnp.where(lane >= HALF, k, 0.0)


def _put_v(v_ref, j, parity, v, v_t):
    v_ref[j, parity] = (v.T if v_t else v).astype(BF16)


def _rows_of(lat, ctx):
    n_lat = N_LAT // TM
    if ctx is None:
        return [pl.BlockSpec((TM, D), lambda i: (i, 0))], [lat]
    return ([pl.BlockSpec((TM, D), lambda i: (jnp.minimum(i, n_lat - 1), 0)),
             pl.BlockSpec((TM, D), lambda i: (jnp.maximum(i - n_lat, 0), 0))], [lat, ctx])


def _pick_rows(lat_ref, ctx_ref):
    if ctx_ref is None:
        return lat_ref[...]
    return jnp.where(pl.program_id(0) < N_LAT // TM, lat_ref[...], ctx_ref[...])


def _qkv_kernel(*refs, shared_kv, head_norm, rope, v_t, q_scale, split_x):
    refs = list(refs)
    x = _pick_rows(refs.pop(0), refs.pop(0) if split_x else None)
    (mod_ref, g_ref, w_ref, cos_ref, sin_ref, qg_ref, kg_ref, bd_ref, q_ref, k_ref, v_ref) = refs
    h = _rms(x, g_ref[...]) * (1.0 + mod_ref[:, SC_M]) + mod_ref[:, SH_M]
    acc = jnp.dot(h.astype(BF16), w_ref[...], preferred_element_type=F32)
    lane = _lane((TM, LANES))
    n_q = N_HEADS * HEAD_DIM // LANES
    n_kv = (N_KV_HEADS if shared_kv else N_HEADS) * HEAD_DIM // LANES

    def finish(t, gain_ref):
        if head_norm:
            t = _head_rms(t, gain_ref[...], bd_ref[...])
        if rope:
            t = _rope_chunk(t, cos_ref[...], sin_ref[...], HEAD_DIM // 2)
        return t

    for c in range(n_q):
        t = finish(acc[:, c * LANES:(c + 1) * LANES], qg_ref)
        q_ref[:, c * LANES:(c + 1) * LANES] = (t * q_scale).astype(BF16)
    for c in range(n_kv):
        k = finish(acc[:, (n_q + c) * LANES:(n_q + c + 1) * LANES], kg_ref)
        v = acc[:, (n_q + n_kv + c) * LANES:(n_q + n_kv + c + 1) * LANES]
        if shared_kv:
            k_sw = pltpu.roll(k, HALF, 1)
            v_sw = pltpu.roll(v, HALF, 1)
            for half in range(2):
                j = 2 * c + half
                for parity in range(2):
                    ksrc = k if parity == half else k_sw
                    vsrc = v if parity == half else v_sw
                    k_ref[j, parity] = _half_only(ksrc, lane, parity).astype(BF16)
                    _put_v(v_ref, j, parity, _with_ones(vsrc, lane, parity), v_t)
        else:
            for parity in range(2):
                k_ref[c, parity] = _half_only(k, lane, parity).astype(BF16)
                _put_v(v_ref, c, parity, _with_ones(v, lane, parity), v_t)


def _kv_out(groups, transposed):
    if transposed:
        return (pl.BlockSpec((groups, 2, LANES, TM), lambda i: (0, 0, 0, i)),
                jax.ShapeDtypeStruct((groups, 2, LANES, N_TOK), BF16))
    return (pl.BlockSpec((groups, 2, TM, LANES), lambda i: (0, 0, i, 0)),
            jax.ShapeDtypeStruct((groups, 2, N_TOK, LANES), BF16))


def _qkv(x, mod, gain, w, cos, sin, qg, kg, bd, *, shared_kv, head_norm, rope, v_t, q_scale,
         x_ctx=None):
    n_out = w.shape[1]
    x_specs, x_args = _rows_of(x, x_ctx)
    groups = N_KV_HEADS if shared_kv else N_HEADS // 2
    row = lambda i: (i, 0)
    const = lambda i: (0, 0)
    kv_spec, kv_shape = _kv_out(groups, False)
    v_spec, v_shape = _kv_out(groups, v_t)
    return pl.pallas_call(
        functools.partial(_qkv_kernel, shared_kv=shared_kv, head_norm=head_norm, rope=rope,
                          v_t=v_t, q_scale=q_scale, split_x=x_ctx is not None),
        out_shape=(jax.ShapeDtypeStruct((N_TOK, N_HEADS * HEAD_DIM), BF16), kv_shape, v_shape),
        grid=(N_TOK // TM,),
        in_specs=x_specs + [
                  pl.BlockSpec((None, 1, 6 * D), lambda i: (_mod_row(i), 0, 0)),
                  pl.BlockSpec((1, D), const),
                  pl.BlockSpec((D, n_out), const),
                  pl.BlockSpec((TM, LANES), _rope_row),
                  pl.BlockSpec((TM, LANES), _rope_row),
                  pl.BlockSpec((1, LANES), const),
                  pl.BlockSpec((1, LANES), const),
                  pl.BlockSpec((LANES, LANES), const)],
        out_specs=(pl.BlockSpec((TM, N_HEADS * HEAD_DIM), row), kv_spec, v_spec),
        compiler_params=_params("arbitrary"),
        name="qkv",
    )(*x_args, mod, gain, w, cos, sin, qg, kg, bd)


def _mla_kernel(x_ref, mod_ref, g_ref, wd_ref, qan_ref, kvan_ref, wq_ref, wkv_ref,
                cos_ref, sin_ref, q_ref, k_ref, v_ref):
    h = _rms(x_ref[...], g_ref[...]) * (1.0 + mod_ref[:, SC_M]) + mod_ref[:, SH_M]
    lat = jnp.dot(h.astype(BF16), wd_ref[...], preferred_element_type=F32)
    c_q = _rms(lat[:, :MLA_Q_LORA], qan_ref[...])
    c_kv = _rms(lat[:, MLA_Q_LORA:MLA_Q_LORA + MLA_KV_LORA], kvan_ref[...])
    cos = cos_ref[...]
    sin = sin_ref[...]
    k_rope = _rope_chunk(lat[:, MLA_Q_LORA + MLA_KV_LORA:], cos, sin, MLA_ROPE // 2)
    q = jnp.dot(c_q.astype(BF16), wq_ref[...], preferred_element_type=F32)
    kv = jnp.dot(c_kv.astype(BF16), wkv_ref[...], preferred_element_type=F32)
    scale = (MLA_NOPE + MLA_ROPE) ** -0.5 * LOG2E
    lane = _lane((TM, LANES))
    for hd in range(N_HEADS):
        sl = slice(hd * LANES, (hd + 1) * LANES)
        q_ref[:, sl] = (_rope_chunk(q[:, sl], cos, sin, MLA_ROPE // 2) * scale).astype(BF16)
        k_ref[hd // 2, hd % 2] = (kv[:, sl] + k_rope).astype(BF16)
        v = kv[:, N_HEADS * LANES + hd * LANES:N_HEADS * LANES + (hd + 1) * LANES]
        _put_v(v_ref, hd // 2, hd % 2, _with_ones(v, lane, hd % 2), True)


def _mla_proj(x, mod, gain, wd, qan, kvan, wq, wkv, cos, sin):
    row = lambda i: (i, 0)
    const = lambda i: (0, 0)
    kv_spec, kv_shape = _kv_out(N_HEADS // 2, False)
    v_spec, v_shape = _kv_out(N_HEADS // 2, True)
    return pl.pallas_call(
        _mla_kernel,
        out_shape=(jax.ShapeDtypeStruct((N_TOK, N_HEADS * LANES), BF16), kv_shape, v_shape),
        grid=(N_TOK // TM,),
        in_specs=[pl.BlockSpec((TM, D), row),
                  pl.BlockSpec((None, 1, 6 * D), lambda i: (_mod_row(i), 0, 0)),
                  pl.BlockSpec((1, D), const),
                  pl.BlockSpec(wd.shape, const),
                  pl.BlockSpec((1, MLA_Q_LORA), const),
                  pl.BlockSpec((1, MLA_KV_LORA), const),
                  pl.BlockSpec(wq.shape, const),
                  pl.BlockSpec(wkv.shape, const),
                  pl.BlockSpec((TM, LANES), _rope_row),
                  pl.BlockSpec((TM, LANES), _rope_row)],
        out_specs=(pl.BlockSpec((TM, N_HEADS * LANES), row), kv_spec, v_spec),
        compiler_params=_params("arbitrary"),
        name="mla_proj",
    )(x, mod, gain, wd, qan, kvan, wq, wkv, cos, sin)


def _attn_kernel(*refs, stacks, tq, subs, gp, mode, with_lat):
    refs = list(refs)
    sink_ref = refs.pop(0) if mode == "window" else None
    q_ref, kc_ref, vc_ref = refs[:3]
    refs = refs[3:]
    if with_lat:
        kl_ref, vl_ref = refs[:2]
        refs = refs[2:]
    tab_ref = refs.pop(0) if mode == "neighbourhood" else None
    o_ref = refs[0]
    s_refs = refs[1:]
    g = pl.program_id(1)
    t = pl.program_id(2)
    qw = q_ref.shape[1] // gp
    ow = o_ref.shape[1] // gp
    m_rows = tq * len(stacks[0][0])

    chunks, masks = [], []
    for sub in range(subs):
        ch, mask = [("ctx", 0, CTX)], None
        if with_lat and mode == "window":
            span = tq + 2 * WINDOW
            q0 = (t * subs + sub) * tq
            start = pl.multiple_of(jnp.clip(q0 - WINDOW, 0, S - span), WINDOW)
            qi = q0 + lax.broadcasted_iota(jnp.int32, (m_rows, span), 0) % tq
            ki = start + lax.broadcasted_iota(jnp.int32, (m_rows, span), 1)
            mask = jnp.abs(qi - ki) <= WINDOW
            ch.append(("lat", start, span))
        elif with_lat:
            start = pl.multiple_of(_na_band_start(t * subs + sub) * GRID_W, GRID_W)
            ch.append(("lat", start, NA_BAND * GRID_W))
        chunks.append(ch)
        masks.append(mask)
    cols = [[sum(w for _, _, w in ch[:c]) for c in range(len(ch))] for ch in chunks]

    items = [(gi, sub, qcs, kv) for gi in range(gp) for sub in range(subs) for qcs, kv in stacks]
    qs, sinks = [], []
    for gi, sub, qcs, kv in items:
        qs.append(jnp.concatenate(
            [q_ref[sub * tq:(sub + 1) * tq, gi * qw + qc * LANES:gi * qw + (qc + 1) * LANES]
             for qc, _ in qcs], axis=0))
        sink_col = None
        if mode == "window":
            hd = [(g * gp + gi) * 2 * len(qcs) + 2 * qc + kv for qc, _ in qcs]
            row = lax.broadcasted_iota(jnp.int32, (m_rows, 1), 0)
            sink_col = jnp.where(row < tq, sink_ref[hd[0]], sink_ref[hd[-1]])
        sinks.append(sink_col)

    def score_stage(it):
        gi, sub, _, kv = items[it]
        s_ref = s_refs[it % len(s_refs)]
        m_fold, m_col = None, sinks[it]
        for (kind, st, width), col in zip(chunks[sub], cols[sub]):
            k = kc_ref[gi, kv] if kind == "ctx" else kl_ref[gi, kv, pl.ds(st, width), :]
            s = lax.dot_general(qs[it], k, (((1,), (1,)), ((), ())), preferred_element_type=F32)
            if kind == "lat" and tab_ref is not None:
                s = s + tab_ref[2 * gi + kv]
            if kind == "lat" and masks[sub] is not None:
                s = jnp.where(masks[sub], s, NEG)
            s_ref[0:m_rows, col:col + width] = s
            if width % LANES == 0:
                for j in range(width // LANES):
                    slab = s[:, j * LANES:(j + 1) * LANES]
                    m_fold = slab if m_fold is None else jnp.maximum(m_fold, slab)
            else:
                mc = jnp.max(s, axis=-1, keepdims=True)
                m_col = mc if m_col is None else jnp.maximum(m_col, mc)
        m = jnp.max(m_fold, axis=-1, keepdims=True)
        return m if m_col is None else jnp.maximum(m, m_col)

    def prob_stage(it, m):
        sub = items[it][1]
        s_ref = s_refs[it % len(s_refs)]
        return [jnp.exp(s_ref[0:m_rows, col:col + width] - m).astype(BF16)
                for (_, _, width), col in zip(chunks[sub], cols[sub])]

    def value_stage(it, ps, m):
        gi, sub, _, kv = items[it]
        acc = jnp.zeros((m_rows, LANES), F32)
        for (kind, st, width), p in zip(chunks[sub], ps):
            v = vc_ref[gi, kv] if kind == "ctx" else vl_ref[gi, kv, pl.ds(st, width), :]
            acc = acc + jnp.dot(p, v, preferred_element_type=F32)
        den = acc[:, LANES - 1:LANES] if kv == 0 else acc[:, 0:1]
        if sinks[it] is not None:
            den = den + jnp.exp(sinks[it] - m)
        lane = _lane(acc.shape)
        return jnp.where(lane < HALF if kv == 0 else lane >= HALF, acc / den, 0.0)

    n = len(items)
    ms, ps, outs = [None] * n, [None] * n, {}
    for stage in range(n + 2):
        if 0 <= stage - 1 < n:
            ps[stage - 1] = prob_stage(stage - 1, ms[stage - 1])
        if stage < n:
            ms[stage] = score_stage(stage)
        if 0 <= stage - 2 < n:
            gi, sub, qcs, _ = items[stage - 2]
            o = value_stage(stage - 2, ps[stage - 2], ms[stage - 2])
            for i, (_, oc) in enumerate(qcs):
                key = (sub, gi * (ow // LANES) + oc)
                part = o[i * tq:(i + 1) * tq]
                outs[key] = part if key not in outs else outs[key] + part
    for (sub, oc), o in outs.items():
        o_ref[sub * tq:(sub + 1) * tq, oc * LANES:(oc + 1) * LANES] = o.astype(o_ref.dtype)


def _attention(q, k4, v4, *, stacks, mode, tq, subs=1, gp=1, sink=None, table=None,
               ctx_queries=False):
    groups = k4.shape[0] // gp
    qw = q.shape[1] // groups
    ow = N_HEADS * HEAD_DIM // groups
    with_lat = not ctx_queries
    tqs = tq * subs
    nq = S // tqs if with_lat else CTX // tqs
    row0 = 0 if with_lat else N_LAT // tqs

    q_spec = pl.BlockSpec((tqs, qw), lambda b, g, t: (row0 + b * nq + t, g))
    ctx_spec = pl.BlockSpec((gp, 2, CTX, LANES), lambda b, g, t: (g, 0, N_LAT // CTX + b, 0))
    lat_spec = pl.BlockSpec((gp, 2, S, LANES), lambda b, g, t: (g, 0, b, 0))
    in_specs = [q_spec, ctx_spec, ctx_spec]
    args = [q, k4, v4]
    n_keys = CTX
    if with_lat:
        in_specs += [lat_spec, lat_spec]
        args += [k4, v4]
        n_keys += {"window": tq + 2 * WINDOW, "neighbourhood": NA_BAND * GRID_W}[mode]
    if mode == "window":
        in_specs = [pl.BlockSpec(memory_space=pltpu.SMEM)] + in_specs
        args = [sink] + args
    if mode == "neighbourhood":
        in_specs.append(pl.BlockSpec((None, 2 * gp, tq, NA_BAND * GRID_W),
                                     lambda b, g, t: (_na_variant(t), g, 0, 0)))
        args.append(table)
    m_rows = tq * max(len(qcs) for qcs, _ in stacks)
    return pl.pallas_call(
        functools.partial(_attn_kernel, stacks=stacks, tq=tq, subs=subs, gp=gp, mode=mode,
                          with_lat=with_lat),
        out_shape=jax.ShapeDtypeStruct((N_LAT if with_lat else N_CTX, N_HEADS * HEAD_DIM), BF16),
        grid=(B, groups, nq),
        in_specs=in_specs,
        out_specs=pl.BlockSpec((tqs, ow), lambda b, g, t: (b * nq + t, g)),
        scratch_shapes=[pltpu.VMEM((m_rows, n_keys), F32) for _ in range(2)],
        compiler_params=_params("arbitrary", "arbitrary", "arbitrary"),
        name="attn_" + mode + ("" if with_lat else "_ctx"),
    )(*args)


def _attn_t_kernel(*refs, stacks, tq, subs, with_lat):
    refs = list(refs)
    q_ref, kc_ref, vtc_ref = refs[:3]
    refs = refs[3:]
    if with_lat:
        kl_ref, vtl_ref = refs[:2]
        refs = refs[2:]
    o_ref = refs[0]
    s_refs = refs[1:]
    chunks = [("ctx", 0, CTX)]
    if with_lat:
        chunks += [("lat", c * TK, TK) for c in range(S // TK)]
    rows = [sum(w for _, _, w in chunks[:c]) for c in range(len(chunks))]
    items = [(sub, qcs, kv) for sub in range(subs) for qcs, kv in stacks]
    qs = [jnp.concatenate([q_ref[sub * tq:(sub + 1) * tq, qc * LANES:(qc + 1) * LANES]
                           for qc, _ in qcs], axis=0) for sub, qcs, _ in items]
    cols = qs[0].shape[0]

    def score_chunk(it, c, m8):
        kind, st, width = chunks[c]
        kv = items[it][2]
        k = kc_ref[kv] if kind == "ctx" else kl_ref[kv, st:st + width, :]
        s = lax.dot_general(k, qs[it], (((1,), (1,)), ((), ())), preferred_element_type=F32)
        s_refs[it % len(s_refs)][rows[c]:rows[c] + width, 0:cols] = s
        c8 = jnp.max(s.reshape(width // 8, 8, cols), axis=0)
        return c8 if m8 is None else jnp.maximum(m8, c8)

    def prob_chunk(it, c, m):
        width = chunks[c][2]
        s = s_refs[it % len(s_refs)][rows[c]:rows[c] + width, 0:cols]
        return jnp.exp2(s - m).astype(BF16)

    def value_chunk(it, c, p, acc):
        kind, st, width = chunks[c]
        kv = items[it][2]
        vt = vtc_ref[kv] if kind == "ctx" else vtl_ref[kv, :, st:st + width]
        return acc + jnp.dot(vt, p, preferred_element_type=F32)

    n = len(items)
    ms, ps, accs = [None] * n, [[] for _ in range(n)], []
    for stage in range(n + 2):
        m8 = None
        acc = jnp.zeros((LANES, cols), F32)
        for c in range(len(chunks)):
            if 0 <= stage - 1 < n:
                ps[stage - 1].append(prob_chunk(stage - 1, c, ms[stage - 1]))
            if stage < n:
                m8 = score_chunk(stage, c, m8)
            if 0 <= stage - 2 < n:
                acc = value_chunk(stage - 2, c, ps[stage - 2][c], acc)
        if stage < n:
            ms[stage] = jnp.max(m8, axis=0, keepdims=True)
        if 0 <= stage - 2 < n:
            accs.append(acc)

    outs = {}
    for (sub, qcs, kv), acc in zip(items, accs):
        den = acc[LANES - 1:LANES, :] if kv == 0 else acc[0:1, :]
        o = (acc / den).T
        lane = _lane(o.shape)
        o = jnp.where(lane < HALF if kv == 0 else lane >= HALF, o, 0.0)
        for i, (_, oc) in enumerate(qcs):
            part = o[i * tq:(i + 1) * tq]
            outs[sub, oc] = part if (sub, oc) not in outs else outs[sub, oc] + part
    for (sub, oc), o in outs.items():
        o_ref[sub * tq:(sub + 1) * tq, oc * LANES:(oc + 1) * LANES] = o.astype(o_ref.dtype)


def _attention_t(q, k4, vt4, *, stacks, tq, subs=1, ctx_queries=False):
    groups = k4.shape[0]
    qw = q.shape[1] // groups
    ow = N_HEADS * HEAD_DIM // groups
    with_lat = not ctx_queries
    tqs = tq * subs
    nq = S // tqs if with_lat else CTX // tqs
    row0 = 0 if with_lat else N_LAT // tqs
    in_specs = [pl.BlockSpec((tqs, qw), lambda b, g, t: (row0 + b * nq + t, g)),
                pl.BlockSpec((None, 2, CTX, LANES), lambda b, g, t: (g, 0, N_LAT // CTX + b, 0)),
                pl.BlockSpec((None, 2, LANES, CTX), lambda b, g, t: (g, 0, 0, N_LAT // CTX + b))]
    args = [q, k4, vt4]
    n_keys = CTX
    if with_lat:
        in_specs += [pl.BlockSpec((None, 2, S, LANES), lambda b, g, t: (g, 0, b, 0)),
                     pl.BlockSpec((None, 2, LANES, S), lambda b, g, t: (g, 0, 0, b))]
        args += [k4, vt4]
        n_keys += S
    m_rows = tq * max(len(qcs) for qcs, _ in stacks)
    return pl.pallas_call(
        functools.partial(_attn_t_kernel, stacks=stacks, tq=tq, subs=subs, with_lat=with_lat),
        out_shape=jax.ShapeDtypeStruct((N_LAT if with_lat else N_CTX, N_HEADS * HEAD_DIM), BF16),
        grid=(B, groups, nq),
        in_specs=in_specs,
        out_specs=pl.BlockSpec((tqs, ow), lambda b, g, t: (b * nq + t, g)),
        scratch_shapes=[pltpu.VMEM((n_keys, m_rows), F32) for _ in range(2)],
        compiler_params=_params("arbitrary", "arbitrary", "arbitrary"),
        name="attn_global" + ("" if with_lat else "_ctx"),
    )(*args)


def _na_band_start(t):
    return jnp.clip(t * NA_QROWS - NA_ROWS // 2, 0, S // GRID_W - NA_BAND)


def _na_tiles():
    rows = S // GRID_W
    tiles = [(t * NA_QROWS, min(max(t * NA_QROWS - NA_ROWS // 2, 0), rows - NA_BAND))
             for t in range(rows // NA_QROWS)]
    lo = [rb for rb in tiles if rb[0] - NA_ROWS // 2 < 0]
    hi = [rb for rb in tiles if rb[0] - NA_ROWS // 2 > rows - NA_BAND]
    return lo + [tiles[len(lo)]] + hi, len(lo), len(hi)


def _na_variant(t):
    _, n_lo, n_hi = _na_tiles()
    first_hi = S // GRID_W // NA_QROWS - n_hi
    return jnp.where(t < n_lo, t, jnp.where(t >= first_hi, t - first_hi + n_lo + 1, n_lo))


def _bias_cols_kernel(rpb_ref, sel_ref, o_ref):
    o_ref[...] = jnp.dot(rpb_ref[...], sel_ref[...], preferred_element_type=F32,
                         precision=lax.Precision.HIGHEST)


def _na_table(rpb):
    rows = S // GRID_W
    n_dr, n_dc = 2 * NA_ROWS - 1, 2 * NA_COLS - 1
    c = np.arange(GRID_W)[:, None]
    kc = np.arange(GRID_W)[None, :]
    cs = np.clip(c - NA_COLS // 2, 0, GRID_W - NA_COLS)
    col_ok = (kc >= cs) & (kc < cs + NA_COLS)
    sel = (np.arange(LANES)[:, None, None] == (kc - c + NA_COLS - 1)[None]) & col_ok[None]
    sel = jnp.asarray(sel.reshape(LANES, GRID_W * GRID_W), F32)
    n_rows = -(-N_HEADS * n_dr // SUBLANES) * SUBLANES
    rpb2 = jnp.pad(rpb.reshape(N_HEADS * n_dr, n_dc), ((0, n_rows - N_HEADS * n_dr), (0, LANES - n_dc)))
    cols = pl.pallas_call(
        _bias_cols_kernel,
        out_shape=jax.ShapeDtypeStruct((n_rows, GRID_W * GRID_W), F32),
        name="na_bias_cols",
    )(rpb2, sel)
    cols = cols[:N_HEADS * n_dr].reshape(N_HEADS, n_dr, GRID_W, GRID_W)
    cols = jnp.where(col_ok[None, None], cols, NEG)
    masked = jnp.full((N_HEADS, GRID_W, GRID_W), NEG, F32)
    variants = []
    for r0, bs in _na_tiles()[0]:
        q_rows = []
        for r in range(r0, r0 + NA_QROWS):
            rs = min(max(r - NA_ROWS // 2, 0), rows - NA_ROWS)
            q_rows.append(jnp.concatenate(
                [cols[:, kr - r + NA_ROWS - 1] if rs <= kr < rs + NA_ROWS else masked
                 for kr in range(bs, bs + NA_BAND)], axis=2))
        variants.append(jnp.concatenate(q_rows, axis=1))
    return jnp.stack(variants)


def _router_route(h, wr_ref):
    w = wr_ref[...]
    h_hi, w_hi = h.astype(BF16), w.astype(BF16)
    h_lo = (h - h_hi.astype(F32)).astype(BF16)
    w_lo = (w - w_hi.astype(F32)).astype(BF16)
    logits = (jnp.dot(h_hi, w_hi, preferred_element_type=F32)
              + (jnp.dot(h_hi, w_lo, preferred_element_type=F32)
                 + jnp.dot(h_lo, w_hi, preferred_element_type=F32)))
    lane = _lane(logits.shape).astype(F32)
    lg = jnp.where(lane < N_EXPERTS, logits, -jnp.inf)
    m1 = jnp.max(lg, axis=-1, keepdims=True)
    i1 = jnp.min(jnp.where(lg == m1, lane, float(LANES)), axis=-1, keepdims=True)
    lg2 = jnp.where(lane == i1, -jnp.inf, lg)
    m2 = jnp.max(lg2, axis=-1, keepdims=True)
    i2 = jnp.min(jnp.where(lg2 == m2, lane, float(LANES)), axis=-1, keepdims=True)
    e = jnp.exp(m2 - m1)
    return jnp.where(lane == 0, i1, jnp.where(lane == 1, i2, jnp.where(
        lane == 2, 1.0 / (1.0 + e), jnp.where(lane == 3, e / (1.0 + e), 0.0))))


def _mix_out(o, x, wo_ref, mod_ref, post_ref, pre_ref):
    y = jnp.dot(o, wo_ref[...], preferred_element_type=F32)
    x1 = x + mod_ref[:, G_M] * _rms(y, post_ref[...])
    return x1, _rms(x1, pre_ref[...]) * (1.0 + mod_ref[:, SC_F]) + mod_ref[:, SH_F]


def _swiglu(h, gu_ref, dw_ref):
    gu = jnp.dot(h, gu_ref[...], preferred_element_type=F32)
    half = gu.shape[1] // 2
    act = (jax.nn.silu(gu[:, :half]) * gu[:, half:]).astype(BF16)
    return jnp.dot(act, dw_ref[...], preferred_element_type=F32)


def _oproj_route_kernel(*refs, with_ctx):
    refs = list(refs)
    o = _pick_rows(refs.pop(0), refs.pop(0) if with_ctx else None)
    x_ref, wo_ref, mod_ref, post_ref, pre_ref, wr_ref, xo_ref, h_ref, route_ref = refs
    x1, h = _mix_out(o, x_ref[...], wo_ref, mod_ref, post_ref, pre_ref)
    xo_ref[...] = x1
    h_ref[...] = h
    route_ref[...] = _router_route(h, wr_ref)


def _oproj_route(o, o_ctx, wo, x, mod, post, pre, w_router):
    with_ctx = o_ctx is not None
    n_rows = N_TOK if with_ctx else N_LAT
    row = lambda i: (i, 0)
    const = lambda i: (0, 0)
    o_specs, o_args = _rows_of(o, o_ctx)
    return pl.pallas_call(
        functools.partial(_oproj_route_kernel, with_ctx=with_ctx),
        out_shape=(jax.ShapeDtypeStruct((n_rows, D), F32), jax.ShapeDtypeStruct((n_rows, D), F32),
                   jax.ShapeDtypeStruct((n_rows, LANES), F32)),
        grid=(n_rows // TM,),
        in_specs=o_specs + [
            pl.BlockSpec((TM, D), row),
            pl.BlockSpec((D, D), const),
            pl.BlockSpec((None, 1, 6 * D), lambda i: (_mod_row(i), 0, 0)),
            pl.BlockSpec((1, D), const), pl.BlockSpec((1, D), const),
            pl.BlockSpec((D, LANES), const)],
        out_specs=(pl.BlockSpec((TM, D), row), pl.BlockSpec((TM, D), row),
                   pl.BlockSpec((TM, LANES), row)),
        compiler_params=_params("arbitrary"),
        name="oproj_route",
    )(*o_args, x, wo, mod, post, pre, w_router)


def _oproj_ffn_kernel(*refs, split_x):
    refs = list(refs)
    o = _pick_rows(refs.pop(0), refs.pop(0))
    x = _pick_rows(refs.pop(0), refs.pop(0) if split_x else None)
    wo_ref, mod_ref, post_ref, pre_ref, gu_ref, dw_ref, fpost_ref, o_ref = refs
    x1, h = _mix_out(o, x, wo_ref, mod_ref, post_ref, pre_ref)
    y = _swiglu(h.astype(BF16), gu_ref, dw_ref)
    o_ref[...] = x1 + mod_ref[:, G_F] * _rms(y, fpost_ref[...])


def _oproj_ffn(o, o_ctx, wo, x, mod, post, pre, w_gu, w_down, fpost, x_ctx=None):
    row = lambda i: (i, 0)
    const = lambda i: (0, 0)
    resident = pl.Buffered(1)
    o_specs, o_args = _rows_of(o, o_ctx)
    x_specs, x_args = _rows_of(x, x_ctx)
    return pl.pallas_call(
        functools.partial(_oproj_ffn_kernel, split_x=x_ctx is not None),
        out_shape=jax.ShapeDtypeStruct((N_TOK, D), F32),
        grid=(N_TOK // TM,),
        in_specs=o_specs + x_specs + [
            pl.BlockSpec((D, D), const, pipeline_mode=resident),
            pl.BlockSpec((None, 1, 6 * D), lambda i: (_mod_row(i), 0, 0)),
            pl.BlockSpec((1, D), const), pl.BlockSpec((1, D), const),
            pl.BlockSpec((D, 2 * D_FF), const, pipeline_mode=resident),
            pl.BlockSpec((D_FF, D), const, pipeline_mode=resident),
            pl.BlockSpec((1, D), const)],
        out_specs=pl.BlockSpec((TM, D), row),
        compiler_params=_params("arbitrary"),
        name="oproj_ffn",
    )(*o_args, *x_args, wo, mod, post, pre, w_gu, w_down, fpost)


def _routing(route, n_rows):
    i1 = route[:, 0].astype(jnp.int32)
    i2 = route[:, 1].astype(jnp.int32)
    e = jnp.arange(N_EXPERTS, dtype=jnp.int32)
    hit1 = i1[:, None] == e
    hit2 = i2[:, None] == e
    sel = hit1.astype(jnp.int32) + hit2.astype(jnp.int32)
    cum = jnp.cumsum(sel, axis=0)
    tiles = (cum[-1] + TME - 1) // TME
    tile_end = jnp.cumsum(tiles)
    slot = ((tile_end - tiles) * TME)[None] + cum - sel
    slots = jnp.stack([jnp.sum(jnp.where(hit1, slot, 0), axis=1),
                       jnp.sum(jnp.where(hit2, slot, 0), axis=1)], axis=1).reshape(-1)
    j = jnp.arange(_n_slot_tiles(n_rows), dtype=jnp.int32)
    tile_expert = jnp.minimum(jnp.sum(j[:, None] >= tile_end[None], axis=1), N_EXPERTS - 1)
    return slots.astype(jnp.int32), tile_expert.astype(jnp.int32), tile_end.astype(jnp.int32)


def _n_slot_tiles(n_rows):
    return 2 * n_rows // TME + N_EXPERTS


def _start_rows(slots_ref, n, copy):
    def start(r, carry):
        for k in range(2):
            copy(r, k, slots_ref[2 * r + k]).start(priority=k)
        return carry

    lax.fori_loop(0, n, start, 0, unroll=DMA_UNROLL)


def _wait_rows(n, copy):
    def wait(r, carry):
        for k in range(2):
            copy(r, k, 0).wait()
        return carry

    lax.fori_loop(0, n, wait, 0, unroll=DMA_UNROLL)


def _dispatch_kernel(tend_ref, slots_ref, h_ref, xs_ref, zero_ref, zsem, sems, *, n_tiles):
    i = pl.program_id(0)

    @pl.when(i == 0)
    def _():
        zero_ref[...] = jnp.zeros_like(zero_ref)
        tiles = []
        for e in range(N_EXPERTS):
            first = tend_ref[e - 1] if e else 0
            tiles.append((tend_ref[e] > first, tend_ref[e] - 1))
        for u in range(N_EXPERTS):
            tile = tend_ref[N_EXPERTS - 1] + u
            tiles.append((tile < n_tiles, tile))
        copies = [(ok, pltpu.make_async_copy(
            zero_ref, xs_ref.at[pl.ds(pl.multiple_of(jnp.maximum(tile, 0) * TME, TME), TME)], zsem))
            for ok, tile in tiles]
        for ok, cp in copies:
            pl.when(ok)(cp.start)
        for ok, cp in copies:
            pl.when(ok)(cp.wait)

    def copy(step):
        return lambda r, k, s: pltpu.make_async_copy(
            h_ref.at[pl.ds(step * TMR + r, 1)], xs_ref.at[pl.ds(s, 1)], sems.at[step % 2])

    _start_rows(slots_ref, TMR, copy(i))

    @pl.when(i > 0)
    def _():
        _wait_rows(TMR, copy(i - 1))

    @pl.when(i == pl.num_programs(0) - 1)
    def _():
        _wait_rows(TMR, copy(i))


def _dispatch(tile_end, slots, h, n_rows):
    n_tiles = _n_slot_tiles(n_rows)
    return pl.pallas_call(
        functools.partial(_dispatch_kernel, n_tiles=n_tiles),
        out_shape=jax.ShapeDtypeStruct((n_tiles * TME, D), F32),
        grid_spec=pltpu.PrefetchScalarGridSpec(
            num_scalar_prefetch=1,
            grid=(n_rows // TMR,),
            in_specs=[pl.BlockSpec((2 * TMR,), lambda i, te: (i,), memory_space=pltpu.SMEM),
                      pl.BlockSpec(memory_space=pl.ANY)],
            out_specs=pl.BlockSpec(memory_space=pl.ANY),
            scratch_shapes=[pltpu.VMEM((TME, D), F32), pltpu.SemaphoreType.DMA(()),
                            pltpu.SemaphoreType.DMA((2,))]),
        compiler_params=_params("arbitrary"),
        name="moe_dispatch",
    )(tile_end, slots, h)


def _expert_kernel(te_ref, tend_ref, x_ref, gu_ref, dw_ref, o_ref, gub_ref, db_ref):
    j = pl.program_id(0)
    valid = j < tend_ref[N_EXPERTS - 1]

    @pl.when(jnp.logical_or(j == 0, te_ref[j] != te_ref[jnp.maximum(j - 1, 0)]))
    def _():
        gub_ref[...] = gu_ref[...].astype(BF16)
        db_ref[...] = dw_ref[...].astype(BF16)

    @pl.when(valid)
    def _():
        o_ref[...] = _swiglu(x_ref[...].astype(BF16), gub_ref, db_ref)

    @pl.when(jnp.logical_not(valid))
    def _():
        o_ref[...] = jnp.zeros_like(o_ref)


def _experts(tile_expert, tile_end, xs, w_gu, w_down):
    n_tiles = xs.shape[0] // TME
    return pl.pallas_call(
        _expert_kernel,
        out_shape=jax.ShapeDtypeStruct(xs.shape, F32),
        grid_spec=pltpu.PrefetchScalarGridSpec(
            num_scalar_prefetch=2,
            grid=(n_tiles,),
            in_specs=[pl.BlockSpec((TME, D), lambda j, te, tend: (j, 0)),
                      pl.BlockSpec((None, D, 2 * D_FF_EXPERT), lambda j, te, tend: (te[j], 0, 0)),
                      pl.BlockSpec((None, D_FF_EXPERT, D), lambda j, te, tend: (te[j], 0, 0))],
            out_specs=pl.BlockSpec((TME, D), lambda j, te, tend: (j, 0)),
            scratch_shapes=[pltpu.VMEM((D, 2 * D_FF_EXPERT), BF16),
                            pltpu.VMEM((D_FF_EXPERT, D), BF16)]),
        compiler_params=_params("arbitrary"),
        name="moe_experts",
    )(tile_expert, tile_end, xs, w_gu, w_down)


def _combine_kernel(slots_ref, next_ref, route_ref, x_ref, mod_ref, post_ref, ys_ref, o_ref,
                    buf_ref, sems):
    i = pl.program_id(0)

    def copy(step):
        b = step % 2
        return lambda r, k, s: pltpu.make_async_copy(
            ys_ref.at[pl.ds(s, 1)], buf_ref.at[b, k, pl.ds(r, 1)], sems.at[b])

    @pl.when(i == 0)
    def _():
        _start_rows(slots_ref, TMR, copy(i))

    @pl.when(i < pl.num_programs(0) - 1)
    def _():
        _start_rows(next_ref, TMR, copy(i + 1))

    _wait_rows(TMR, copy(i))
    b = i % 2
    y = route_ref[:, 2:3] * buf_ref[b, 0] + route_ref[:, 3:4] * buf_ref[b, 1]
    o_ref[...] = x_ref[...] + mod_ref[:, G_F] * _rms(y, post_ref[...])


def _combine(slots, route, x, mod, post, ys, n_rows):
    n_steps = n_rows // TMR
    return pl.pallas_call(
        _combine_kernel,
        out_shape=jax.ShapeDtypeStruct((n_rows, D), F32),
        grid=(n_steps,),
        in_specs=[pl.BlockSpec((2 * TMR,), lambda i: (i,), memory_space=pltpu.SMEM),
                  pl.BlockSpec((2 * TMR,), lambda i: (jnp.minimum(i + 1, n_steps - 1),),
                               memory_space=pltpu.SMEM),
                  pl.BlockSpec((TMR, LANES), lambda i: (i, 0)),
                  pl.BlockSpec((TMR, D), lambda i: (i, 0)),
                  pl.BlockSpec((None, 1, 6 * D), lambda i: (_mod_row(i * TMR // TM), 0, 0)),
                  pl.BlockSpec((1, D), lambda i: (0, 0)),
                  pl.BlockSpec(memory_space=pl.ANY)],
        out_specs=pl.BlockSpec((TMR, D), lambda i: (i, 0)),
        scratch_shapes=[pltpu.VMEM((2, 2, TMR, D), F32), pltpu.SemaphoreType.DMA((2,))],
        compiler_params=_params("arbitrary"),
        name="moe_combine",
    )(slots, slots, route, x, mod, post, ys)


def _moe(h, route, w_gu, w_down, x, mod, post, n_rows):
    slots, tile_expert, tile_end = _routing(route, n_rows)
    xs = _dispatch(tile_end, slots, h, n_rows)
    ys = _experts(tile_expert, tile_end, xs, w_gu, w_down)
    return _combine(slots, route, x, mod, post, ys, n_rows)


def _rope_tables(rot_dim, lead, tail):
    t = jnp.arange(S, dtype=jnp.int32)
    row = (t // GRID_W).astype(F32)
    col = (t % GRID_W).astype(F32)
    quarter = rot_dim // 4
    inv = ROPE_THETA ** (-jnp.arange(quarter, dtype=F32) / quarter)
    ang = jnp.concatenate([row[:, None] * inv, col[:, None] * inv], axis=-1)
    cos, sin = jnp.cos(ang), jnp.sin(ang)
    reps = (LANES - lead - tail) // rot_dim
    cos_l = jnp.concatenate([jnp.ones((S, lead), F32)] + [cos, cos] * reps + [jnp.ones((S, tail), F32)], axis=1)
    sin_l = jnp.concatenate([jnp.zeros((S, lead), F32)] + [-sin, sin] * reps + [jnp.zeros((S, tail), F32)], axis=1)
    cos_all = jnp.concatenate([cos_l, jnp.ones((TM, LANES), F32)], axis=0)
    sin_all = jnp.concatenate([sin_l, jnp.zeros((TM, LANES), F32)], axis=0)
    return cos_all, sin_all


def _rope_row(i):
    return (jnp.where(i < N_LAT // TM, i % (S // TM), S // TM), 0)


def _mla_weights(w_dkv, w_q_b, w_kv_b):
    lat = MLA_Q_LORA + MLA_KV_LORA
    pad = LANES - MLA_NOPE - MLA_ROPE
    wd = jnp.concatenate([w_dkv[:, :lat], jnp.zeros((D, MLA_NOPE), F32), w_dkv[:, lat:],
                          jnp.zeros((D, pad), F32)], axis=1)
    wq = jnp.pad(w_q_b.reshape(MLA_Q_LORA, N_HEADS, MLA_NOPE + MLA_ROPE), ((0, 0), (0, 0), (0, pad)))
    kvr = w_kv_b.reshape(MLA_KV_LORA, N_HEADS, 2 * HALF)
    k_part, v_part = kvr[..., :HALF], kvr[..., HALF:]
    zero = jnp.zeros_like(v_part)
    wk = jnp.concatenate([k_part, zero], axis=-1)
    even = (jnp.arange(N_HEADS) % 2 == 0)[None, :, None]
    wv = jnp.where(even, jnp.concatenate([v_part, zero], -1), jnp.concatenate([zero, v_part], -1))
    wkv = jnp.concatenate([wk.reshape(MLA_KV_LORA, -1), wv.reshape(MLA_KV_LORA, -1)], axis=1)
    return wd.astype(BF16), wq.reshape(MLA_Q_LORA, -1).astype(BF16), wkv.astype(BF16)


GQA_STACKS = ((((0, 0), (1, 1)), 0), (((0, 0), (1, 1)), 1))
MLA_STACKS = ((((0, 0),), 0), (((1, 0),), 1))
NA_STACKS = ((((0, 0),), 0), (((0, 0),), 1))


def kernel(x, c, ctx, c_ctx,
           l0_ada_w, l0_ada_b, l0_mix_pre, l0_mix_post, l0_ffn_pre, l0_ffn_post,
           l0_w_qkv, l0_q_norm, l0_k_norm, l0_w_o, l0_ffn_w_gu, l0_ffn_w_down,
           l1_ada_w, l1_ada_b, l1_mix_pre, l1_mix_post, l1_ffn_pre, l1_ffn_post,
           l1_w_qkv, l1_sink, l1_w_o, l1_router, l1_moe_w_gu, l1_moe_w_down,
           l2_ada_w, l2_ada_b, l2_mix_pre, l2_mix_post, l2_ffn_pre, l2_ffn_post,
           l2_w_dkv, l2_q_a_norm, l2_w_q_b, l2_kv_a_norm, l2_w_kv_b, l2_w_o, l2_ffn_w_gu, l2_ffn_w_down,
           l3_ada_w, l3_ada_b, l3_mix_pre, l3_mix_post, l3_ffn_pre, l3_ffn_post,
           l3_w_qkv, l3_rpb, l3_w_o, l3_router, l3_moe_w_gu, l3_moe_w_down):
    vec = lambda a: a.reshape(1, -1)
    tile2 = lambda a: jnp.tile(a, 2).reshape(1, LANES)
    bf = lambda a: a.astype(BF16)
    pad_router = lambda w: jnp.pad(w, ((0, 0), (0, LANES - N_EXPERTS)))

    x_lat, x_ctx = x.reshape(N_LAT, D), ctx.reshape(N_CTX, D)
    c_all = jnp.concatenate([c, c_ctx[None], jnp.zeros((SUBLANES - B - 1, D), F32)], axis=0)
    mods = [_ada(c_all, w, b).reshape(SUBLANES, 1, 6 * D)
            for w, b in ((l0_ada_w, l0_ada_b), (l1_ada_w, l1_ada_b), (l2_ada_w, l2_ada_b), (l3_ada_w, l3_ada_b))]
    cos64, sin64 = _rope_tables(HEAD_DIM, 0, 0)
    cos32, sin32 = _rope_tables(MLA_ROPE, MLA_NOPE, LANES - MLA_NOPE - MLA_ROPE)
    ones = jnp.ones((1, LANES), F32)
    idx = np.arange(LANES)
    bd = jnp.asarray(idx[:, None] // HALF == idx[None, :] // HALF, BF16)

    q, k4, v4 = _qkv(x_lat, mods[0], vec(l0_mix_pre), bf(l0_w_qkv), cos64, sin64,
                     tile2(l0_q_norm), tile2(l0_k_norm), bd, shared_kv=True, head_norm=True, rope=True,
                     v_t=True, q_scale=HEAD_DIM ** -0.5 * LOG2E, x_ctx=x_ctx)
    o = _attention_t(q, k4, v4, stacks=GQA_STACKS, tq=TQ, subs=4)
    oc = _attention_t(q, k4, v4, stacks=GQA_STACKS, tq=CTX, ctx_queries=True)
    xs = _oproj_ffn(o, oc, bf(l0_w_o), x_lat, mods[0], vec(l0_mix_post), vec(l0_ffn_pre),
                    bf(l0_ffn_w_gu), bf(l0_ffn_w_down), vec(l0_ffn_post), x_ctx=x_ctx)

    q, k4, v4 = _qkv(xs, mods[1], vec(l1_mix_pre), bf(l1_w_qkv), cos64, sin64,
                     ones, ones, bd, shared_kv=True, head_norm=False, rope=True,
                     v_t=False, q_scale=HEAD_DIM ** -0.5)
    o = _attention(q, k4, v4, stacks=GQA_STACKS, mode="window", tq=TQ, subs=4, sink=l1_sink)
    oc = _attention(q, k4, v4, stacks=GQA_STACKS, mode="window", tq=CTX, sink=l1_sink, ctx_queries=True)
    xs, h, route = _oproj_route(o, oc, bf(l1_w_o), xs, mods[1], vec(l1_mix_post), vec(l1_ffn_pre),
                                pad_router(l1_router))
    xs = _moe(h, route, l1_moe_w_gu, l1_moe_w_down, xs, mods[1], vec(l1_ffn_post), N_TOK)

    wd, wq, wkv = _mla_weights(l2_w_dkv, l2_w_q_b, l2_w_kv_b)
    q, k4, v4 = _mla_proj(xs, mods[2], vec(l2_mix_pre), wd, vec(l2_q_a_norm), vec(l2_kv_a_norm),
                          wq, wkv, cos32, sin32)
    o = _attention_t(q, k4, v4, stacks=MLA_STACKS, tq=2 * TQ, subs=4)
    oc = _attention_t(q, k4, v4, stacks=MLA_STACKS, tq=CTX, ctx_queries=True)
    xs = _oproj_ffn(o, oc, bf(l2_w_o), xs, mods[2], vec(l2_mix_post), vec(l2_ffn_pre),
                    bf(l2_ffn_w_gu), bf(l2_ffn_w_down), vec(l2_ffn_post))

    q, k4, v4 = _qkv(xs, mods[3], vec(l3_mix_pre), bf(l3_w_qkv), cos64, sin64,
                     ones, ones, bd, shared_kv=False, head_norm=False, rope=False,
                     v_t=False, q_scale=HEAD_DIM ** -0.5)
    o = _attention(q, k4, v4, stacks=NA_STACKS, mode="neighbourhood", tq=NA_QROWS * GRID_W, gp=4,
                   table=_na_table(l3_rpb))
    xl, h, route = _oproj_route(o, None, bf(l3_w_o), xs, mods[3], vec(l3_mix_post), vec(l3_ffn_pre),
                                pad_router(l3_router))
    xl = _moe(h, route, l3_moe_w_gu, l3_moe_w_down, xl, mods[3], vec(l3_ffn_post), N_LAT)
    return xl.reshape(B, S, D)
```
